```python
import jax, jax.numpy as jnp
from jax import lax
import numpy as np

D_MODEL = 1024
BATCH = 2
SEQ = 8192
DEPTH = 1

MLA_HEADS = 8
MLA_Q_LORA = 256
MLA_KV_LORA = 256
MLA_NOPE = 64
MLA_ROPE = 32
MLA_V = 64
ROPE_THETA = 10000.0
DIL_PAIRS = ((128, 1), (512, 4), (2048, 16))
DIL_GROUPS = 3
DIL_HEADS_PER_GROUP = 4
DIL_HEAD_DIM = 128
DIL_HEADS = DIL_GROUPS * DIL_HEADS_PER_GROUP
MEM_LEN = 256
MEM_HEADS = 4
MEM_HEAD_DIM = 128
D_FF = 2816
N_BRANCHES = 3
Q_BLOCK = 128
EPS = 1e-5
ALPHA = (2 * DEPTH) ** 0.25
BETA = (8 * DEPTH) ** -0.25

MLA_OUT = MLA_HEADS * MLA_V
DIL_OUT = DIL_HEADS_PER_GROUP * DIL_HEAD_DIM
DIL_QKV = 3 * DIL_HEADS * DIL_HEAD_DIM
MEM_OUT = MEM_HEADS * MEM_HEAD_DIM
GATE_COLS = N_BRANCHES * D_MODEL
IN_SPLITS = (MLA_Q_LORA, MLA_KV_LORA, MLA_ROPE, DIL_QKV, MEM_OUT, GATE_COLS)
IN_COLS = MLA_Q_LORA + MLA_KV_LORA + MLA_ROPE + DIL_QKV + MEM_OUT + GATE_COLS

kernel_name = "hybrid_gated_mla_dilated_mem_encoder"


def layer_norm(x, g, b):
    xf = x.astype(jnp.float32)
    mu = jnp.mean(xf, -1, keepdims=True)
    var = jnp.mean(jnp.square(xf - mu), -1, keepdims=True)
    y = (xf - mu) * lax.rsqrt(var + EPS) * g.astype(jnp.float32) + b.astype(jnp.float32)
    return y.astype(x.dtype)


def rms_norm(x, g):
    xf = x.astype(jnp.float32)
    y = xf * lax.rsqrt(jnp.mean(jnp.square(xf), -1, keepdims=True) + EPS) * g.astype(jnp.float32)
    return y.astype(x.dtype)


def swiglu(x, w_gate, w_up, w_down):
    return (jax.nn.silu(x @ w_gate) * (x @ w_up)) @ w_down


def rope_tables(seq_len):
    pos = jnp.arange(seq_len, dtype=jnp.float32)
    inv = 1.0 / (ROPE_THETA ** (jnp.arange(0, MLA_ROPE, 2, dtype=jnp.float32) / MLA_ROPE))
    ang = pos[:, None] * inv[None, :]
    return jnp.cos(ang), jnp.sin(ang)


def apply_rope(x, cos, sin):
    cos = cos.astype(x.dtype)
    sin = sin.astype(x.dtype)
    x1, x2 = jnp.split(x, 2, axis=-1)
    return jnp.concatenate([x1 * cos - x2 * sin, x1 * sin + x2 * cos], axis=-1)


def mla_attention(c_q, c_kv, k_rope, q_norm_g, kv_norm_g, w_uq, w_ukv, cos, sin):
    B, S, _ = c_q.shape
    q = (rms_norm(c_q, q_norm_g) @ w_uq).reshape(B, S, MLA_HEADS, MLA_NOPE + MLA_ROPE)
    q_nope, q_pe = q[..., :MLA_NOPE], q[..., MLA_NOPE:]
    q_pe = apply_rope(q_pe, cos[:, None, :], sin[:, None, :])
    kv = (rms_norm(c_kv, kv_norm_g) @ w_ukv).reshape(B, S, MLA_HEADS, MLA_NOPE + MLA_V)
    k_nope, v = kv[..., :MLA_NOPE], kv[..., MLA_NOPE:]
    k_pe = apply_rope(k_rope, cos, sin)
    scale = (MLA_NOPE + MLA_ROPE) ** -0.5
    nb = S // Q_BLOCK
    qn = q_nope.reshape(B, nb, Q_BLOCK, MLA_HEADS, MLA_NOPE).transpose(1, 0, 2, 3, 4)
    qp = q_pe.reshape(B, nb, Q_BLOCK, MLA_HEADS, MLA_ROPE).transpose(1, 0, 2, 3, 4)

    def block(args):
        qn_b, qp_b = args
        s = (jnp.einsum('bqhd,bkhd->bhqk', qn_b, k_nope)
             + jnp.einsum('bqhr,bkr->bhqk', qp_b, k_pe)).astype(jnp.float32) * scale
        p = jax.nn.softmax(s, axis=-1)
        return jnp.einsum('bhqk,bkhd->bqhd', p.astype(v.dtype), v)

    o = lax.map(block, (qn, qp))
    return o.transpose(1, 0, 2, 3, 4).reshape(B, S, MLA_OUT)


def dilated_group(q, k, v, slopes, window, dilation):
    B, S, Hg, Dh = q.shape
    half = window // 2
    n_side = half // dilation
    offs = jnp.arange(-n_side, n_side + 1, dtype=jnp.int32) * dilation
    kp = jnp.pad(k, ((0, 0), (half, half), (0, 0), (0, 0)))
    vp = jnp.pad(v, ((0, 0), (half, half), (0, 0), (0, 0)))
    alibi = -slopes[:, None] * jnp.abs(offs).astype(jnp.float32)[None, :]
    scale = Dh ** -0.5
    nb = S // Q_BLOCK
    qb = q.reshape(B, nb, Q_BLOCK, Hg, Dh).transpose(1, 0, 2, 3, 4)

    def block(args):
        i, q_blk = args
        pos = i * Q_BLOCK + jnp.arange(Q_BLOCK, dtype=jnp.int32)
        kpos = pos[:, None] + offs[None, :]
        valid = (kpos >= 0) & (kpos < S)
        kg = jnp.take(kp, kpos + half, axis=1)
        vg = jnp.take(vp, kpos + half, axis=1)
        s = jnp.einsum('bqhd,bqjhd->bhqj', q_blk, kg).astype(jnp.float32) * scale + alibi[None, :, None, :]
        s = jnp.where(valid[None, None], s, -jnp.inf)
        m = jnp.max(s, axis=-1, keepdims=True)
        e = jnp.exp(s - m)
        den = jnp.sum(e, axis=-1, keepdims=True)
        o = jnp.einsum('bhqj,bqjhd->bqhd', (e / den).astype(v.dtype), vg)
        lse = (m + jnp.log(den))[..., 0].transpose(0, 2, 1)
        return o, lse

    o, lse = lax.map(block, (jnp.arange(nb, dtype=jnp.int32), qb))
    o = o.transpose(1, 0, 2, 3, 4).reshape(B, S, Hg, Dh)
    lse = lse.transpose(1, 0, 2, 3).reshape(B, S, Hg)
    return o, lse


def dilated_attention(dil_cols):
    B, S, _ = dil_cols.shape
    qkv = dil_cols.reshape(B, S, 3, DIL_GROUPS, DIL_HEADS_PER_GROUP, DIL_HEAD_DIM)
    slopes = (2.0 ** (-8.0 * jnp.arange(1, DIL_HEADS + 1, dtype=jnp.float32) / DIL_HEADS)).reshape(
        DIL_GROUPS, DIL_HEADS_PER_GROUP)
    outs, lses = [], []
    for g, (window, dilation) in enumerate(DIL_PAIRS):
        o, lse = dilated_group(qkv[:, :, 0, g], qkv[:, :, 1, g], qkv[:, :, 2, g], slopes[g], window, dilation)
        outs.append(o)
        lses.append(lse)
    w = jax.nn.softmax(jnp.stack(lses, 0), axis=0)
    o = jnp.sum(w[..., None].astype(dil_cols.dtype) * jnp.stack(outs, 0), axis=0)
    return o.reshape(B, S, DIL_OUT)


def memory_attention(q_cols, mem, w_mem_kv):
    B, S, _ = q_cols.shape
    M = mem.shape[1]
    q = q_cols.reshape(B, S, MEM_HEADS, MEM_HEAD_DIM)
    kv = (mem @ w_mem_kv).reshape(B, M, 2, MEM_HEADS, MEM_HEAD_DIM)
    k, v = kv[:, :, 0], kv[:, :, 1]
    s = jnp.einsum('bshd,bmhd->bhsm', q, k).astype(jnp.float32) * (MEM_HEAD_DIM ** -0.5)
    p = jax.nn.softmax(s, axis=-1)
    return jnp.einsum('bhsm,bmhd->bshd', p.astype(v.dtype), v).reshape(B, S, MEM_OUT)


def token_mixing(u, mem, w_in, q_norm_g, kv_norm_g, w_uq, w_ukv, w_mem_kv,
                 w_br_mla, w_br_dil, w_br_mem, w_o, cos, sin):
    B, S, D = u.shape
    proj = u @ w_in
    c_q, c_kv, k_rope, dil_cols, memq_cols, gate_cols = jnp.split(
        proj, [int(i) for i in np.cumsum(IN_SPLITS)[:-1]], axis=-1)
    y_a = mla_attention(c_q, c_kv, k_rope, q_norm_g, kv_norm_g, w_uq, w_ukv, cos, sin) @ w_br_mla
    y_b = dilated_attention(dil_cols) @ w_br_dil
    y_c = memory_attention(memq_cols, mem, w_mem_kv) @ w_br_mem
    gates = jax.nn.sigmoid(gate_cols.astype(jnp.float32)).astype(u.dtype).reshape(B, S, N_BRANCHES, D)
    merged = gates[:, :, 0] * y_a + gates[:, :, 1] * y_b + gates[:, :, 2] * y_c
    return merged @ w_o


def setup_inputs(seed: int = 0) -> dict:
    key = jax.random.key(seed)
    ks = jax.random.split(key, 24)
    f32 = jnp.float32

    def nrm(k, shape, fan_in, gain=1.0):
        return jax.random.normal(k, shape, f32) * (fan_in ** -0.5) * gain

    def gain(k, shape):
        return 1.0 + 0.01 * jax.random.normal(k, shape, f32)

    def bias(k, shape):
        return 0.01 * jax.random.normal(k, shape, f32)

    L, D = DEPTH, D_MODEL
    return {
        "x": jax.random.normal(ks[0], (BATCH, SEQ, D), f32),
        "mem": jax.random.normal(ks[1], (BATCH, MEM_LEN, D), f32),
        "w_in": nrm(ks[2], (L, D, IN_COLS), D),
        "mla_q_norm": gain(ks[3], (L, MLA_Q_LORA)),
        "mla_kv_norm": gain(ks[4], (L, MLA_KV_LORA)),
        "w_uq": nrm(ks[5], (L, MLA_Q_LORA, MLA_HEADS * (MLA_NOPE + MLA_ROPE)), MLA_Q_LORA),
        "w_ukv": nrm(ks[6], (L, MLA_KV_LORA, MLA_HEADS * (MLA_NOPE + MLA_V)), MLA_KV_LORA),
        "w_mem_kv": nrm(ks[7], (L, D, 2 * MEM_OUT), D),
        "w_br_mla": nrm(ks[8], (L, MLA_OUT, D), MLA_OUT),
        "w_br_dil": nrm(ks[9], (L, DIL_OUT, D), DIL_OUT),
        "w_br_mem": nrm(ks[10], (L, MEM_OUT, D), MEM_OUT),
        "w_o": nrm(ks[11], (L, D, D), D, BETA),
        "ffn1_w_gate": nrm(ks[12], (L, D, D_FF), D),
        "ffn1_w_up": nrm(ks[13], (L, D, D_FF), D),
        "ffn1_w_down": nrm(ks[14], (L, D_FF, D), D_FF, BETA),
        "ffn2_w_gate": nrm(ks[15], (L, D, D_FF), D),
        "ffn2_w_up": nrm(ks[16], (L, D, D_FF), D),
        "ffn2_w_down": nrm(ks[17], (L, D_FF, D), D_FF, BETA),
        "ln1_g": gain(ks[18], (L, D)),
        "ln1_b": bias(ks[19], (L, D)),
        "ln2_g": gain(ks[20], (L, D)),
        "ln2_b": bias(ks[21], (L, D)),
        "ln3_g": gain(ks[22], (L, D)),
        "ln3_b": bias(ks[23], (L, D)),
    }


def reference(x, mem, w_in, mla_q_norm, mla_kv_norm, w_uq, w_ukv, w_mem_kv,
              w_br_mla, w_br_dil, w_br_mem, w_o,
              ffn1_w_gate, ffn1_w_up, ffn1_w_down, ffn2_w_gate, ffn2_w_up, ffn2_w_down,
              ln1_g, ln1_b, ln2_g, ln2_b, ln3_g, ln3_b):
    cos, sin = rope_tables(x.shape[1])
    h = x
    for l in range(DEPTH):
        h = layer_norm(ALPHA * h + 0.5 * swiglu(h, ffn1_w_gate[l], ffn1_w_up[l], ffn1_w_down[l]),
                       ln1_g[l], ln1_b[l])
        mix = token_mixing(h, mem, w_in[l], mla_q_norm[l], mla_kv_norm[l], w_uq[l], w_ukv[l], w_mem_kv[l],
                           w_br_mla[l], w_br_dil[l], w_br_mem[l], w_o[l], cos, sin)
        h = layer_norm(ALPHA * h + mix, ln2_g[l], ln2_b[l])
        h = layer_norm(ALPHA * h + 0.5 * swiglu(h, ffn2_w_gate[l], ffn2_w_up[l], ffn2_w_down[l]),
                       ln3_g[l], ln3_b[l])
    return h
```

```python
import functools

import jax
import jax.numpy as jnp
from jax import lax
from jax.experimental import pallas as pl
from jax.experimental.pallas import tpu as pltpu

F32 = jnp.float32
BF16 = jnp.bfloat16

D_MODEL = 1024
DEPTH = 1
MLA_HEADS = 8
MLA_Q_LORA = 256
MLA_KV_LORA = 256
MLA_NOPE = 64
MLA_ROPE = 32
MLA_V = 64
ROPE_THETA = 10000.0
DIL_PAIRS = ((128, 1), (512, 4), (2048, 16))
DIL_GROUPS = 3
DIL_HEADS_PER_GROUP = 4
DIL_HEAD_DIM = 128
DIL_HEADS = DIL_GROUPS * DIL_HEADS_PER_GROUP
MEM_HEADS = 4
MEM_HEAD_DIM = 128
EPS = 1e-5
ALPHA = (2 * DEPTH) ** 0.25

LANES = 128
DIL_OUT = DIL_HEADS_PER_GROUP * DIL_HEAD_DIM
DIL_QKV = 3 * DIL_HEADS * DIL_HEAD_DIM
MEM_OUT = MEM_HEADS * MEM_HEAD_DIM
MLA_OUT = MLA_HEADS * MLA_V
DIL_HALF_SPAN = 64
DIL_HALO = 64
DIL_SUB = 128
VMEM_LIMIT = 56 * 1024 * 1024

_NT = (((1,), (1,)), ((), ()))


def _dot(a, b):
    return jnp.dot(a, b, preferred_element_type=F32)


def _dot_nt(a, b):
    return lax.dot_general(a, b, _NT, preferred_element_type=F32)


def _layer_norm(y, g, b):
    mu = jnp.mean(y, axis=-1, keepdims=True)
    yc = y - mu
    var = jnp.mean(yc * yc, axis=-1, keepdims=True)
    return yc * lax.rsqrt(var + EPS) * g + b


def _rms_norm(y, g):
    return y * lax.rsqrt(jnp.mean(y * y, axis=-1, keepdims=True) + EPS) * g


def _ffn_ln_body(x_ref, wg_ref, wu_ref, wd_ref, g_ref, b_ref, o_ref, acc_ref, *, n_ff_steps):
    j = pl.program_id(1)

    @pl.when(j == 0)
    def _():
        acc_ref[...] = jnp.zeros_like(acc_ref)

    xb = x_ref[...].astype(BF16)
    gate = _dot(xb, wg_ref[...])
    up = _dot(xb, wu_ref[...])
    act = (gate * jax.nn.sigmoid(gate) * up).astype(BF16)
    acc_ref[...] += _dot(act, wd_ref[...])

    @pl.when(j == n_ff_steps - 1)
    def _():
        y = ALPHA * x_ref[...] + 0.5 * acc_ref[...]
        o_ref[...] = _layer_norm(y, g_ref[...], b_ref[...])


def _ffn_ln(x2d, wg, wu, wd, g, b, *, tm, tf):
    t, d = x2d.shape
    f = wg.shape[1]
    n_ff_steps = f // tf
    return pl.pallas_call(
        functools.partial(_ffn_ln_body, n_ff_steps=n_ff_steps),
        grid=(t // tm, n_ff_steps),
        in_specs=[
            pl.BlockSpec((tm, d), lambda i, j: (i, 0)),
            pl.BlockSpec((d, tf), lambda i, j: (0, j)),
            pl.BlockSpec((d, tf), lambda i, j: (0, j)),
            pl.BlockSpec((tf, d), lambda i, j: (j, 0)),
            pl.BlockSpec((1, d), lambda i, j: (0, 0)),
            pl.BlockSpec((1, d), lambda i, j: (0, 0)),
        ],
        out_specs=pl.BlockSpec((tm, d), lambda i, j: (i, 0)),
        out_shape=jax.ShapeDtypeStruct((t, d), F32),
        scratch_shapes=[pltpu.VMEM((tm, d), F32)],
        compiler_params=pltpu.CompilerParams(
            dimension_semantics=("parallel", "arbitrary"), vmem_limit_bytes=VMEM_LIMIT),
        name="ffn_ln",
    )(x2d, wg, wu, wd, g, b)


def _proj_body(h_ref, wcq_ref, wckv_ref, wkra_ref, wkrb_ref, gq_ref, gkv_ref,
               wqa_ref, wqb_ref, wka_ref, wv_ref, wdil_ref,
               cq_ref, sq_ref, ck_ref, sk_ref,
               q_ref, k_ref, v_ref, dil_ref):
    hb = h_ref[...].astype(BF16)
    cqn = _rms_norm(_dot(hb, wcq_ref[...]), gq_ref[...]).astype(BF16)
    ckvn = _rms_norm(_dot(hb, wckv_ref[...]), gkv_ref[...]).astype(BF16)
    k_rope = _dot(hb, wkra_ref[...]) * ck_ref[...] + _dot(hb, wkrb_ref[...]) * sk_ref[...]
    qa = _dot(cqn, wqa_ref[...])
    qb = _dot(cqn, wqb_ref[...])
    ka = _dot(ckvn, wka_ref[...])
    cq_t = cq_ref[...]
    sq_t = sq_ref[...]
    for h in range(MLA_HEADS):
        sl = slice(h * LANES, (h + 1) * LANES)
        q_ref[0, h] = (qa[:, sl] * cq_t + qb[:, sl] * sq_t).astype(BF16)
        k_ref[0, h] = (ka[:, sl] + k_rope).astype(BF16)
    v = _dot(ckvn, wv_ref[...])
    for hp in range(MLA_HEADS // 2):
        v_ref[0, hp] = v[:, hp * LANES:(hp + 1) * LANES].astype(BF16)
    n_q = DIL_HEADS * DIL_HEAD_DIM
    dil_scale = DIL_HEAD_DIM ** -0.5
    dil_ref[:, :n_q] = (_dot(hb, wdil_ref[:, :n_q]) * dil_scale).astype(BF16)
    dil_ref[:, n_q:2 * n_q] = _dot(hb, wdil_ref[:, n_q:2 * n_q]).astype(BF16)
    dil_ref[:, 2 * n_q:] = _dot(hb, wdil_ref[:, 2 * n_q:]).astype(BF16)


def _proj(h2d, weights, tables, *, batch, seq, tm):
    t, d = h2d.shape
    n_s = seq // tm
    hl = MLA_HEADS * LANES

    def full(a):
        return pl.BlockSpec(a.shape, lambda i: (0,) * a.ndim)

    tab_spec = pl.BlockSpec((tm, LANES), lambda i: (i % n_s, 0))
    head_map = lambda i: (i // n_s, 0, i % n_s, 0)
    return pl.pallas_call(
        _proj_body,
        grid=(t // tm,),
        in_specs=[pl.BlockSpec((tm, d), lambda i: (i, 0))]
        + [full(w) for w in weights] + [tab_spec] * 4,
        out_specs=[
            pl.BlockSpec((1, MLA_HEADS, tm, LANES), head_map),
            pl.BlockSpec((1, MLA_HEADS, tm, LANES), head_map),
            pl.BlockSpec((1, MLA_HEADS // 2, tm, LANES), head_map),
            pl.BlockSpec((tm, DIL_QKV), lambda i: (i, 0)),
        ],
        out_shape=[
            jax.ShapeDtypeStruct((batch, MLA_HEADS, seq, LANES), BF16),
            jax.ShapeDtypeStruct((batch, MLA_HEADS, seq, LANES), BF16),
            jax.ShapeDtypeStruct((batch, MLA_HEADS // 2, seq, LANES), BF16),
            jax.ShapeDtypeStruct((t, DIL_QKV), BF16),
        ],
        compiler_params=pltpu.CompilerParams(
            dimension_semantics=("parallel",), vmem_limit_bytes=VMEM_LIMIT),
        name="proj",
    )(h2d, *weights, *tables)


def _mla_attn_body(q_ref, k_ref, v_ref, o_ref, *, tk, n_kchunks):
    tq = q_ref.shape[2]
    q0 = q_ref[0, 0]
    q1 = q_ref[0, 1]

    def update(q, kc, vc, m, l, acc):
        s = _dot_nt(q, kc)
        m_new = jnp.maximum(m, jnp.max(s, axis=1, keepdims=True))
        a = jnp.exp(m - m_new)
        p = jnp.exp(s - m_new)
        l = a * l + jnp.sum(p, axis=1, keepdims=True)
        acc = a * acc + _dot(p.astype(BF16), vc)
        return m_new, l, acc

    def step(j, carry):
        m0, l0, a0, m1, l1, a1 = carry
        off = pl.multiple_of(j * tk, tk)
        vc = v_ref[0, 0, pl.ds(off, tk), :]
        m0, l0, a0 = update(q0, k_ref[0, 0, pl.ds(off, tk), :], vc, m0, l0, a0)
        m1, l1, a1 = update(q1, k_ref[0, 1, pl.ds(off, tk), :], vc, m1, l1, a1)
        return m0, l0, a0, m1, l1, a1

    m_init = jnp.full((tq, 1), -jnp.inf, F32)
    l_init = jnp.zeros((tq, 1), F32)
    a_init = jnp.zeros((tq, LANES), F32)
    _, l0, a0, _, l1, a1 = lax.fori_loop(
        0, n_kchunks, step, (m_init, l_init, a_init, m_init, l_init, a_init))
    lane = lax.broadcasted_iota(jnp.int32, (tq, LANES), 1)
    o_ref[0] = jnp.where(lane < MLA_V, a0 / l0, a1 / l1).astype(o_ref.dtype)


def _mla_attn(q, k, v, *, tq, tk):
    batch, heads, seq, _ = q.shape
    return pl.pallas_call(
        functools.partial(_mla_attn_body, tk=tk, n_kchunks=seq // tk),
        grid=(batch, heads // 2, seq // tq),
        in_specs=[
            pl.BlockSpec((1, 2, tq, LANES), lambda b, hp, i: (b, hp, i, 0)),
            pl.BlockSpec((1, 2, seq, LANES), lambda b, hp, i: (b, hp, 0, 0)),
            pl.BlockSpec((1, 1, seq, LANES), lambda b, hp, i: (b, hp, 0, 0)),
        ],
        out_specs=pl.BlockSpec((1, tq, LANES), lambda b, hp, i: (b, i, hp)),
        out_shape=jax.ShapeDtypeStruct((batch, seq, MLA_OUT), BF16),
        compiler_params=pltpu.CompilerParams(
            dimension_semantics=("parallel", "parallel", "parallel"), vmem_limit_bytes=VMEM_LIMIT),
        name="mla_attn",
    )(q, k, v)


def _dil_attn_body(slopes_ref, q_ref, kl_ref, km_ref, kr_ref, vl_ref, vm_ref, vr_ref,
                   o_ref, lse_ref, kbuf, vbuf, *, group, dilation, tq, sub_len):
    i = pl.program_id(2)
    kbuf[0:DIL_HALO] = kl_ref[0]
    kbuf[DIL_HALO:DIL_HALO + tq] = km_ref[0]
    kbuf[DIL_HALO + tq:] = kr_ref[0]
    vbuf[0:DIL_HALO] = vl_ref[0]
    vbuf[DIL_HALO:DIL_HALO + tq] = vm_ref[0]
    vbuf[DIL_HALO + tq:] = vr_ref[0]

    win = DIL_SUB + 2 * DIL_HALO
    row = lax.broadcasted_iota(jnp.int32, (DIL_SUB, win), 0)
    col = lax.broadcasted_iota(jnp.int32, (DIL_SUB, win), 1)
    steps = jnp.abs(col - DIL_HALO - row)
    in_band = steps <= DIL_HALF_SPAN
    dist = (steps * dilation).astype(F32)
    for sub in range(tq // DIL_SUB):
        key_pos = i * tq + (sub * DIL_SUB - DIL_HALO) + col
        valid = in_band & (key_pos >= 0) & (key_pos < sub_len)
        rows = slice(sub * DIL_SUB, (sub + 1) * DIL_SUB)
        wrows = slice(sub * DIL_SUB, sub * DIL_SUB + win)
        for h in range(DIL_HEADS_PER_GROUP):
            cols = slice(h * DIL_HEAD_DIM, (h + 1) * DIL_HEAD_DIM)
            slope = slopes_ref[group * DIL_HEADS_PER_GROUP + h]
            s = _dot_nt(q_ref[0, rows, cols], kbuf[wrows, cols]) - slope * dist
            s = jnp.where(valid, s, -jnp.inf)
            m = jnp.max(s, axis=1, keepdims=True)
            e = jnp.exp(s - m)
            den = jnp.sum(e, axis=1, keepdims=True)
            o_ref[0, rows, cols] = _dot(e.astype(BF16), vbuf[wrows, cols]) / den
            lse_ref[0, rows, cols] = jnp.broadcast_to(m + jnp.log(den), (DIL_SUB, DIL_HEAD_DIM))


def _dil_attn(dil, slopes, *, batch, seq, group, dilation):
    sub_len = seq // dilation
    tq = min(512, sub_len)
    n_q = sub_len // tq
    halo_per_tile = tq // DIL_HALO
    n_halo = sub_len // DIL_HALO
    col_blocks = DIL_QKV // DIL_OUT
    dil_v = dil.reshape(batch, sub_len, dilation * DIL_QKV)
    q_col = group
    k_col = DIL_GROUPS + group
    v_col = 2 * DIL_GROUPS + group

    def main(c):
        return pl.BlockSpec((1, tq, DIL_OUT), lambda b, r, i: (b, i, r * col_blocks + c))

    def left(c):
        return pl.BlockSpec(
            (1, DIL_HALO, DIL_OUT),
            lambda b, r, i: (b, jnp.maximum(i * halo_per_tile - 1, 0), r * col_blocks + c))

    def right(c):
        return pl.BlockSpec(
            (1, DIL_HALO, DIL_OUT),
            lambda b, r, i: (b, jnp.minimum((i + 1) * halo_per_tile, n_halo - 1), r * col_blocks + c))

    out_spec = pl.BlockSpec((1, tq, DIL_OUT), lambda b, r, i: (b, i, r))
    out_shape = jax.ShapeDtypeStruct((batch, sub_len, dilation * DIL_OUT), F32)
    o, lse = pl.pallas_call(
        functools.partial(_dil_attn_body, group=group, dilation=dilation, tq=tq, sub_len=sub_len),
        grid=(batch, dilation, n_q),
        in_specs=[pl.BlockSpec(memory_space=pltpu.SMEM),
                  main(q_col), left(k_col), main(k_col), right(k_col),
                  left(v_col), main(v_col), right(v_col)],
        out_specs=[out_spec, out_spec],
        out_shape=[out_shape, out_shape],
        scratch_shapes=[pltpu.VMEM((tq + 2 * DIL_HALO, DIL_OUT), BF16),
                        pltpu.VMEM((tq + 2 * DIL_HALO, DIL_OUT), BF16)],
        compiler_params=pltpu.CompilerParams(
            dimension_semantics=("parallel", "parallel", "parallel"), vmem_limit_bytes=VMEM_LIMIT),
        name=f"dil_attn_g{group}",
    )(slopes, dil_v, dil_v, dil_v, dil_v, dil_v, dil_v, dil_v)
    return o.reshape(batch * seq, DIL_OUT), lse.reshape(batch * seq, DIL_OUT)


def _mem_kv_body(mem_ref, w_ref, o_ref):
    o_ref[0] = _dot(mem_ref[0].astype(BF16), w_ref[...]).astype(BF16)


def _mem_kv(mem, w):
    batch, m_len, d = mem.shape
    n = w.shape[1]
    return pl.pallas_call(
        _mem_kv_body,
        grid=(batch,),
        in_specs=[pl.BlockSpec((1, m_len, d), lambda b: (b, 0, 0)),
                  pl.BlockSpec((d, n), lambda b: (0, 0))],
        out_specs=pl.BlockSpec((1, m_len, n), lambda b: (b, 0, 0)),
        out_shape=jax.ShapeDtypeStruct((batch, m_len, n), BF16),
        compiler_params=pltpu.CompilerParams(dimension_semantics=("parallel",)),
        name="mem_kv",
    )(mem, w)


def _merge_body(h_ref, omla_ref, od0_ref, od1_ref, od2_ref, ls0_ref, ls1_ref, ls2_ref, kvm_ref,
                wgate_ref, wmq_ref, wba_ref, wbb_ref, wbc_ref, wo_ref, g_ref, b_ref, out_ref):
    h = h_ref[...]
    hb = h.astype(BF16)

    ls0, ls1, ls2 = ls0_ref[...], ls1_ref[...], ls2_ref[...]
    mx = jnp.maximum(ls0, jnp.maximum(ls1, ls2))
    e0, e1, e2 = jnp.exp(ls0 - mx), jnp.exp(ls1 - mx), jnp.exp(ls2 - mx)
    o_dil = (e0 * od0_ref[...] + e1 * od1_ref[...] + e2 * od2_ref[...]) / (e0 + e1 + e2)

    mq = (_dot(hb, wmq_ref[...]) * (MEM_HEAD_DIM ** -0.5)).astype(BF16)
    heads = []
    for hh in range(MEM_HEADS):
        kc = slice(hh * MEM_HEAD_DIM, (hh + 1) * MEM_HEAD_DIM)
        vc = slice(MEM_OUT + hh * MEM_HEAD_DIM, MEM_OUT + (hh + 1) * MEM_HEAD_DIM)
        s = _dot_nt(mq[:, kc], kvm_ref[0, :, kc])
        p = jnp.exp(s - jnp.max(s, axis=1, keepdims=True))
        heads.append(_dot(p.astype(BF16), kvm_ref[0, :, vc]) / jnp.sum(p, axis=1, keepdims=True))
    o_mem = jnp.concatenate(heads, axis=1)

    y_a = _dot(omla_ref[...], wba_ref[...])
    y_b = _dot(o_dil.astype(BF16), wbb_ref[...])
    y_c = _dot(o_mem.astype(BF16), wbc_ref[...])
    d = h.shape[1]
    merged = (jax.nn.sigmoid(_dot(hb, wgate_ref[:, :d])) * y_a
              + jax.nn.sigmoid(_dot(hb, wgate_ref[:, d:2 * d])) * y_b
              + jax.nn.sigmoid(_dot(hb, wgate_ref[:, 2 * d:])) * y_c)
    mix = _dot(merged.astype(BF16), wo_ref[...])
    out_ref[...] = _layer_norm(ALPHA * h + mix, g_ref[...], b_ref[...])


def _merge(h2d, o_mla, o_dil, lse_dil, kv_mem, weights, g, b, *, seq, tm):
    t, d = h2d.shape
    n_s = seq // tm

    def rows(width):
        return pl.BlockSpec((tm, width), lambda i: (i, 0))

    def full(a):
        return pl.BlockSpec(a.shape, lambda i: (0,) * a.ndim)

    m_len, kv_cols = kv_mem.shape[1:]
    return pl.pallas_call(
        _merge_body,
        grid=(t // tm,),
        in_specs=[rows(d), rows(MLA_OUT)] + [rows(DIL_OUT)] * 6
        + [pl.BlockSpec((1, m_len, kv_cols), lambda i: (i // n_s, 0, 0))]
        + [full(w) for w in weights] + [full(g), full(b)],
        out_specs=rows(d),
        out_shape=jax.ShapeDtypeStruct((t, d), F32),
        compiler_params=pltpu.CompilerParams(
            dimension_semantics=("parallel",), vmem_limit_bytes=VMEM_LIMIT),
        name="merge",
    )(h2d, o_mla, *o_dil, *lse_dil, kv_mem, *weights, g, b)


def _swap_halves(w):
    half = w.shape[-1] // 2
    return jnp.concatenate([w[..., half:], w[..., :half]], axis=-1)


def _prep_mla_weights(w_in, w_uq, w_ukv):
    d = w_in.shape[0]
    o_q, o_kv, o_kr = 0, MLA_Q_LORA, MLA_Q_LORA + MLA_KV_LORA
    w_cq = w_in[:, o_q:o_q + MLA_Q_LORA]
    w_ckv = w_in[:, o_kv:o_kv + MLA_KV_LORA]
    w_kr = w_in[:, o_kr:o_kr + MLA_ROPE]
    pad_tail = LANES - MLA_NOPE - MLA_ROPE
    z_nope = jnp.zeros((d, MLA_NOPE), F32)
    z_tail = jnp.zeros((d, pad_tail), F32)
    w_kra = jnp.concatenate([z_nope, w_kr, z_tail], axis=1)
    w_krb = jnp.concatenate([z_nope, _swap_halves(w_kr), z_tail], axis=1)

    uq = w_uq.reshape(MLA_Q_LORA, MLA_HEADS, MLA_NOPE + MLA_ROPE)
    uq_nope, uq_pe = uq[..., :MLA_NOPE], uq[..., MLA_NOPE:]
    zq_nope = jnp.zeros_like(uq_nope)
    zq_tail = jnp.zeros((MLA_Q_LORA, MLA_HEADS, pad_tail), F32)
    w_qa = jnp.concatenate([uq_nope, uq_pe, zq_tail], axis=-1).reshape(MLA_Q_LORA, MLA_HEADS * LANES)
    w_qb = jnp.concatenate([zq_nope, _swap_halves(uq_pe), zq_tail], axis=-1).reshape(
        MLA_Q_LORA, MLA_HEADS * LANES)

    ukv = w_ukv.reshape(MLA_KV_LORA, MLA_HEADS, MLA_NOPE + MLA_V)
    uk, uv = ukv[..., :MLA_NOPE], ukv[..., MLA_NOPE:]
    w_ka = jnp.concatenate(
        [uk, jnp.zeros((MLA_KV_LORA, MLA_HEADS, LANES - MLA_NOPE), F32)], axis=-1).reshape(
            MLA_KV_LORA, MLA_HEADS * LANES)
    w_v = uv.reshape(MLA_KV_LORA, MLA_HEADS * MLA_V)
    return [w.astype(BF16) for w in (w_cq, w_ckv, w_kra, w_krb)], [w.astype(BF16) for w in (w_qa, w_qb, w_ka, w_v)]


def _rope_tables(seq):
    pos = jnp.arange(seq, dtype=F32)
    inv = 1.0 / (ROPE_THETA ** (jnp.arange(0, MLA_ROPE, 2, dtype=F32) / MLA_ROPE))
    ang = pos[:, None] * inv[None, :]
    cos, sin = jnp.cos(ang), jnp.sin(ang)
    ones = jnp.ones((seq, MLA_NOPE), F32)
    z_nope = jnp.zeros((seq, MLA_NOPE), F32)
    z_tail = jnp.zeros((seq, LANES - MLA_NOPE - MLA_ROPE), F32)
    q_scale = (MLA_NOPE + MLA_ROPE) ** -0.5
    c_q = jnp.concatenate([ones, cos, cos, z_tail], axis=1) * q_scale
    s_q = jnp.concatenate([z_nope, -sin, sin, z_tail], axis=1) * q_scale
    c_k = jnp.concatenate([z_nope, cos, cos, z_tail], axis=1)
    s_k = jnp.concatenate([z_nope, -sin, sin, z_tail], axis=1)
    return c_q, s_q, c_k, s_k


def kernel(x, mem, w_in, mla_q_norm, mla_kv_norm, w_uq, w_ukv, w_mem_kv, w_br_mla, w_br_dil, w_br_mem, w_o,
           ffn1_w_gate, ffn1_w_up, ffn1_w_down, ffn2_w_gate, ffn2_w_up, ffn2_w_down,
           ln1_g, ln1_b, ln2_g, ln2_b, ln3_g, ln3_b):
    batch, seq, d = x.shape
    t = batch * seq
    d_ff = ffn1_w_gate.shape[-1]
    tm_ffn = min(512, t)
    tf = d_ff // 2
    tm_proj = min(512, seq)
    tm_merge = min(256, seq)
    tables = _rope_tables(seq)
    slopes = 2.0 ** (-8.0 * jnp.arange(1, DIL_HEADS + 1, dtype=F32) / DIL_HEADS)

    h = x.reshape(t, d)
    for l in range(DEPTH):
        bf = lambda w: w[l].astype(BF16)
        row = lambda v: v[l].reshape(1, -1)
        h = _ffn_ln(h, bf(ffn1_w_gate), bf(ffn1_w_up), bf(ffn1_w_down), row(ln1_g), row(ln1_b),
                    tm=tm_ffn, tf=tf)

        w_in_l = w_in[l]
        o_dil_cols = MLA_Q_LORA + MLA_KV_LORA + MLA_ROPE
        o_memq = o_dil_cols + DIL_QKV
        o_gate = o_memq + MEM_OUT
        w_c, w_u = _prep_mla_weights(w_in_l, w_uq[l], w_ukv[l])
        proj_weights = w_c + [row(mla_q_norm), row(mla_kv_norm)] + w_u + [
            w_in_l[:, o_dil_cols:o_memq].astype(BF16)]
        q, k, v, dil = _proj(h, proj_weights, tables, batch=batch, seq=seq, tm=tm_proj)

        o_mla = _mla_attn(q, k, v, tq=min(512, seq), tk=min(1024, seq)).reshape(t, MLA_OUT)
        dil_parts = [_dil_attn(dil, slopes, batch=batch, seq=seq, group=g, dilation=dl)
                     for g, (_, dl) in enumerate(DIL_PAIRS)]
        kv_mem = _mem_kv(mem, bf(w_mem_kv))

        merge_weights = [w_in_l[:, o_gate:].astype(BF16), w_in_l[:, o_memq:o_gate].astype(BF16),
                         bf(w_br_mla), bf(w_br_dil), bf(w_br_mem), bf(w_o)]
        h = _merge(h, o_mla, [p[0] for p in dil_parts], [p[1] for p in dil_parts], kv_mem,
                   merge_weights, row(ln2_g), row(ln2_b), seq=seq, tm=tm_merge)

        h = _ffn_ln(h, bf(ffn2_w_gate), bf(ffn2_w_up), bf(ffn2_w_down), row(ln3_g), row(ln3_b),
                    tm=tm_ffn, tf=tf)
    return h.reshape(batch, seq, d)
```

```python
import functools
import math

import jax
import jax.numpy as jnp
from jax import lax
from jax.experimental import pallas as pl
from jax.experimental.pallas import tpu as pltpu

F32 = jnp.float32
BF16 = jnp.bfloat16

D_MODEL = 1024
DEPTH = 1
MLA_HEADS = 8
MLA_Q_LORA = 256
MLA_KV_LORA = 256
MLA_NOPE = 64
MLA_ROPE = 32
MLA_V = 64
ROPE_THETA = 10000.0
DIL_PAIRS = ((128, 1), (512, 4), (2048, 16))
DIL_GROUPS = 3
DIL_HEADS_PER_GROUP = 4
DIL_HEAD_DIM = 128
DIL_HEADS = DIL_GROUPS * DIL_HEADS_PER_GROUP
MEM_HEADS = 4
MEM_HEAD_DIM = 128
EPS = 1e-5
ALPHA = (2 * DEPTH) ** 0.25

LANES = 128
DIL_OUT = DIL_HEADS_PER_GROUP * DIL_HEAD_DIM
DIL_QKV = 3 * DIL_HEADS * DIL_HEAD_DIM
MEM_OUT = MEM_HEADS * MEM_HEAD_DIM
MLA_OUT = MLA_HEADS * MLA_V
DIL_HALF_SPAN = 64
DIL_HALO = 64
DIL_SUB = 128
DIL_CHUNK_TOKENS = 2048
VMEM_LIMIT = 56 * 1024 * 1024

assert all(w // 2 // d == DIL_HALF_SPAN for w, d in DIL_PAIRS)

_NT = (((1,), (1,)), ((), ()))


def _dot(a, b):
    return jnp.dot(a, b, preferred_element_type=F32)


def _dot_nt(a, b):
    return lax.dot_general(a, b, _NT, preferred_element_type=F32)


def _layer_norm(y, g, b):
    mu = jnp.mean(y, axis=-1, keepdims=True)
    yc = y - mu
    var = jnp.mean(yc * yc, axis=-1, keepdims=True)
    return yc * lax.rsqrt(var + EPS) * g + b


def _rms_norm(y, g):
    return y * lax.rsqrt(jnp.mean(y * y, axis=-1, keepdims=True) + EPS) * g


def _ffn_ln_body(x_ref, wg_ref, wu_ref, wd_ref, g_ref, b_ref, o_ref, acc_ref, *, n_ff_steps):
    j = pl.program_id(1)

    @pl.when(j == 0)
    def _():
        acc_ref[...] = jnp.zeros_like(acc_ref)

    xb = x_ref[...].astype(BF16)
    gate = _dot(xb, wg_ref[...])
    up = _dot(xb, wu_ref[...])
    act = (gate * jax.nn.sigmoid(gate) * up).astype(BF16)
    acc_ref[...] += _dot(act, wd_ref[...])

    @pl.when(j == n_ff_steps - 1)
    def _():
        y = ALPHA * x_ref[...] + 0.5 * acc_ref[...]
        o_ref[...] = _layer_norm(y, g_ref[...], b_ref[...])


def _ffn_ln(x2d, wg, wu, wd, g, b, *, tm, tf):
    t, d = x2d.shape
    f = wg.shape[1]
    n_ff_steps = f // tf
    return pl.pallas_call(
        functools.partial(_ffn_ln_body, n_ff_steps=n_ff_steps),
        grid=(t // tm, n_ff_steps),
        in_specs=[
            pl.BlockSpec((tm, d), lambda i, j: (i, 0)),
            pl.BlockSpec((d, tf), lambda i, j: (0, j)),
            pl.BlockSpec((d, tf), lambda i, j: (0, j)),
            pl.BlockSpec((tf, d), lambda i, j: (j, 0)),
            pl.BlockSpec((1, d), lambda i, j: (0, 0)),
            pl.BlockSpec((1, d), lambda i, j: (0, 0)),
        ],
        out_specs=pl.BlockSpec((tm, d), lambda i, j: (i, 0)),
        out_shape=jax.ShapeDtypeStruct((t, d), F32),
        scratch_shapes=[pltpu.VMEM((tm, d), F32)],
        compiler_params=pltpu.CompilerParams(
            dimension_semantics=("parallel", "arbitrary"), vmem_limit_bytes=VMEM_LIMIT),
        name="ffn_ln",
    )(x2d, wg, wu, wd, g, b)


def _proj_body(*refs):
    n_hc = D_MODEL // LANES
    h_refs = refs[:n_hc]
    (wcq_ref, wckv_ref, wkra_ref, wkrb_ref, gq_ref, gkv_ref,
     wqa_ref, wqb_ref, wka_ref, wv_ref, wdil_ref,
     cq_ref, sq_ref, ck_ref, sk_ref,
     q_ref, k_ref, v_ref, dil0_ref, dil1_ref, dil2_ref, hperm_ref) = refs[n_hc:]
    hb = jnp.concatenate([hc[...] for hc in h_refs], axis=1).astype(BF16)
    cqn = _rms_norm(_dot(hb, wcq_ref[...]), gq_ref[...]).astype(BF16)
    ckvn = _rms_norm(_dot(hb, wckv_ref[...]), gkv_ref[...]).astype(BF16)
    k_rope = _dot(hb, wkra_ref[...]) * ck_ref[...] + _dot(hb, wkrb_ref[...]) * sk_ref[...]
    qa = _dot(cqn, wqa_ref[...])
    qb = _dot(cqn, wqb_ref[...])
    ka = _dot(ckvn, wka_ref[...])
    cq_t = cq_ref[...]
    sq_t = sq_ref[...]
    for h in range(MLA_HEADS):
        sl = slice(h * LANES, (h + 1) * LANES)
        q_ref[0, h] = (qa[:, sl] * cq_t + qb[:, sl] * sq_t).astype(BF16)
        k_ref[0, h] = (ka[:, sl] + k_rope).astype(BF16)
    v = _dot(ckvn, wv_ref[...])
    for hp in range(MLA_HEADS // 2):
        v_ref[0, hp] = v[:, hp * LANES:(hp + 1) * LANES].astype(BF16)

    dil_scale = DIL_HEAD_DIM ** -0.5
    tm = h_refs[0].shape[0]
    group_cols = 3 * DIL_OUT
    for g, (dil_ref, (_, dilation)) in enumerate(zip((dil0_ref, dil1_ref, dil2_ref), DIL_PAIRS)):
        rows_per = tm // dilation
        if dilation == 1:
            hp = hb
        else:
            for r in range(dilation):
                for c, hc in enumerate(h_refs):
                    hperm_ref[r * rows_per:(r + 1) * rows_per, c * LANES:(c + 1) * LANES] = hc[
                        pl.ds(r, rows_per, stride=dilation), :].astype(BF16)
            hp = hperm_ref[...]
        c0 = g * group_cols
        qg = (_dot(hp, wdil_ref[:, c0:c0 + DIL_OUT]) * dil_scale).astype(BF16)
        kvg = _dot(hp, wdil_ref[:, c0 + DIL_OUT:c0 + group_cols]).astype(BF16)
        for r in range(dilation):
            rs = slice(r * rows_per, (r + 1) * rows_per)
            dil_ref[0, r, :, :DIL_OUT] = qg[rs]
            dil_ref[0, r, :, DIL_OUT:] = kvg[rs]


def _proj(h2d, weights, tables, *, batch, seq, tm):
    t, d = h2d.shape
    n_s = seq // tm

    def full(a):
        return pl.BlockSpec(a.shape, lambda i: (0,) * a.ndim)

    tab_spec = pl.BlockSpec((tm, LANES), lambda i: (i % n_s, 0))
    head_map = lambda i: (i // n_s, 0, i % n_s, 0)
    dil_specs = [pl.BlockSpec((1, dl, tm // dl, 3 * DIL_OUT), head_map) for _, dl in DIL_PAIRS]
    dil_shapes = [jax.ShapeDtypeStruct((batch, dl, seq // dl, 3 * DIL_OUT), BF16) for _, dl in DIL_PAIRS]
    return pl.pallas_call(
        _proj_body,
        grid=(t // tm,),
        in_specs=[pl.BlockSpec((tm, LANES), functools.partial(lambda c, i: (i, c), c))
                  for c in range(d // LANES)]
        + [full(w) for w in weights] + [tab_spec] * 4,
        out_specs=[
            pl.BlockSpec((1, MLA_HEADS, tm, LANES), head_map),
            pl.BlockSpec((1, MLA_HEADS, tm, LANES), head_map),
            pl.BlockSpec((1, MLA_HEADS // 2, tm, LANES), head_map),
        ] + dil_specs,
        out_shape=[
            jax.ShapeDtypeStruct((batch, MLA_HEADS, seq, LANES), BF16),
            jax.ShapeDtypeStruct((batch, MLA_HEADS, seq, LANES), BF16),
            jax.ShapeDtypeStruct((batch, MLA_HEADS // 2, seq, LANES), BF16),
        ] + dil_shapes,
        scratch_shapes=[pltpu.VMEM((tm, d), BF16)],
        compiler_params=pltpu.CompilerParams(
            dimension_semantics=("parallel",), vmem_limit_bytes=VMEM_LIMIT),
        name="proj",
    )(*([h2d] * (d // LANES)), *weights, *tables)


def _mla_attn_body(q_ref, k_ref, v_ref, o_ref, *, tk, n_kchunks):
    tq = q_ref.shape[2]
    q0 = q_ref[0, 0]
    q1 = q_ref[0, 1]

    def update(q, kc, vc, m, l, acc):
        s = _dot_nt(q, kc)
        m_new = jnp.maximum(m, jnp.max(s, axis=1, keepdims=True))
        a = jnp.exp2(m - m_new)
        p = jnp.exp2(s - m_new)
        l = a * l + jnp.sum(p, axis=1, keepdims=True)
        acc = a * acc + _dot(p.astype(BF16), vc)
        return m_new, l, acc

    def step(j, carry):
        m0, l0, a0, m1, l1, a1 = carry
        off = pl.multiple_of(j * tk, tk)
        vc = v_ref[0, 0, pl.ds(off, tk), :]
        m0, l0, a0 = update(q0, k_ref[0, 0, pl.ds(off, tk), :], vc, m0, l0, a0)
        m1, l1, a1 = update(q1, k_ref[0, 1, pl.ds(off, tk), :], vc, m1, l1, a1)
        return m0, l0, a0, m1, l1, a1

    m_init = jnp.full((tq, 1), -jnp.inf, F32)
    l_init = jnp.zeros((tq, 1), F32)
    a_init = jnp.zeros((tq, LANES), F32)
    _, l0, a0, _, l1, a1 = lax.fori_loop(
        0, n_kchunks, step, (m_init, l_init, a_init, m_init, l_init, a_init))
    lane = lax.broadcasted_iota(jnp.int32, (tq, LANES), 1)
    o_ref[0] = jnp.where(lane < MLA_V, a0 / l0, a1 / l1).astype(o_ref.dtype)


def _mla_attn(q, k, v, *, tq, tk):
    batch, heads, seq, _ = q.shape
    return pl.pallas_call(
        functools.partial(_mla_attn_body, tk=tk, n_kchunks=seq // tk),
        grid=(batch, heads // 2, seq // tq),
        in_specs=[
            pl.BlockSpec((1, 2, tq, LANES), lambda b, hp, i: (b, hp, i, 0)),
            pl.BlockSpec((1, 2, seq, LANES), lambda b, hp, i: (b, hp, 0, 0)),
            pl.BlockSpec((1, 1, seq, LANES), lambda b, hp, i: (b, hp, 0, 0)),
        ],
        out_specs=pl.BlockSpec((1, tq, LANES), lambda b, hp, i: (b, i, hp)),
        out_shape=jax.ShapeDtypeStruct((batch, seq, MLA_OUT), BF16),
        compiler_params=pltpu.CompilerParams(
            dimension_semantics=("parallel", "parallel", "parallel"), vmem_limit_bytes=VMEM_LIMIT),
        name="mla_attn",
    )(q, k, v)


def _dil_attn_body(slopes_ref, q_ref, kl_ref, km_ref, kr_ref, vl_ref, vm_ref, vr_ref,
                   o_ref, lse_ref, kbuf, vbuf, *, group, dilation, tq, sub_len):
    i = pl.program_id(1)
    r = pl.program_id(2)
    kbuf[0:DIL_HALO] = kl_ref[0, 0]
    kbuf[DIL_HALO:DIL_HALO + tq] = km_ref[0, 0]
    kbuf[DIL_HALO + tq:] = kr_ref[0, 0]
    vbuf[0:DIL_HALO] = vl_ref[0, 0]
    vbuf[DIL_HALO:DIL_HALO + tq] = vm_ref[0, 0]
    vbuf[DIL_HALO + tq:] = vr_ref[0, 0]

    win = DIL_SUB + 2 * DIL_HALO
    row = lax.broadcasted_iota(jnp.int32, (DIL_SUB, win), 0)
    col = lax.broadcasted_iota(jnp.int32, (DIL_SUB, win), 1)
    steps = jnp.abs(col - DIL_HALO - row)
    in_band = steps <= DIL_HALF_SPAN
    dist = (steps * dilation).astype(F32)
    for sub in range(tq // DIL_SUB):
        key_pos = i * tq + (sub * DIL_SUB - DIL_HALO) + col
        valid = in_band & (key_pos >= 0) & (key_pos < sub_len)
        rows = slice(sub * DIL_SUB, (sub + 1) * DIL_SUB)
        wrows = slice(sub * DIL_SUB, sub * DIL_SUB + win)
        if dilation == 1:
            out_rows = pl.ds(sub * DIL_SUB, DIL_SUB)
        else:
            out_rows = pl.ds(sub * DIL_SUB * dilation + r, DIL_SUB, stride=dilation)
        for h in range(DIL_HEADS_PER_GROUP):
            cols = slice(h * DIL_HEAD_DIM, (h + 1) * DIL_HEAD_DIM)
            slope = slopes_ref[group * DIL_HEADS_PER_GROUP + h]
            s = _dot_nt(q_ref[0, 0, rows, cols], kbuf[wrows, cols]) - slope * dist
            s = jnp.where(valid, s, -jnp.inf)
            m = jnp.max(s, axis=1, keepdims=True)
            e = jnp.exp(s - m)
            den = jnp.sum(e, axis=1, keepdims=True)
            o_ref[0, h, out_rows, :] = _dot(e.astype(BF16), vbuf[wrows, cols]) / den
            lse_ref[0, h, out_rows, :] = jnp.broadcast_to(m + jnp.log(den), (DIL_SUB, DIL_HEAD_DIM))


def _dil_attn(dil_g, slopes, *, group, dilation):
    batch, _, sub_len, _ = dil_g.shape
    seq = sub_len * dilation
    tq = min(512, sub_len, DIL_CHUNK_TOKENS // dilation)
    n_q = sub_len // tq
    halo_per_tile = tq // DIL_HALO
    n_halo = sub_len // DIL_HALO
    q_col, k_col, v_col = 0, 1, 2

    def main(c):
        return pl.BlockSpec((1, 1, tq, DIL_OUT), lambda b, i, r: (b, r, i, c))

    def left(c):
        return pl.BlockSpec(
            (1, 1, DIL_HALO, DIL_OUT),
            lambda b, i, r: (b, r, jnp.maximum(i * halo_per_tile - 1, 0), c))

    def right(c):
        return pl.BlockSpec(
            (1, 1, DIL_HALO, DIL_OUT),
            lambda b, i, r: (b, r, jnp.minimum((i + 1) * halo_per_tile, n_halo - 1), c))

    out_spec = pl.BlockSpec((1, DIL_HEADS_PER_GROUP, tq * dilation, DIL_HEAD_DIM),
                            lambda b, i, r: (b, 0, i, 0))
    out_shape = jax.ShapeDtypeStruct((batch, DIL_HEADS_PER_GROUP, seq, DIL_HEAD_DIM), F32)
    o, lse = pl.pallas_call(
        functools.partial(_dil_attn_body, group=group, dilation=dilation, tq=tq, sub_len=sub_len),
        grid=(batch, n_q, dilation),
        in_specs=[pl.BlockSpec(memory_space=pltpu.SMEM),
                  main(q_col), left(k_col), main(k_col), right(k_col),
                  left(v_col), main(v_col), right(v_col)],
        out_specs=[out_spec, out_spec],
        out_shape=[out_shape, out_shape],
        scratch_shapes=[pltpu.VMEM((tq + 2 * DIL_HALO, DIL_OUT), BF16),
                        pltpu.VMEM((tq + 2 * DIL_HALO, DIL_OUT), BF16)],
        compiler_params=pltpu.CompilerParams(
            dimension_semantics=("parallel", "parallel", "arbitrary"), vmem_limit_bytes=VMEM_LIMIT),
        name=f"dil_attn_g{group}",
    )(slopes, dil_g, dil_g, dil_g, dil_g, dil_g, dil_g, dil_g)
    return o, lse


def _mem_kv_body(mem_ref, w_ref, o_ref):
    o_ref[0] = _dot(mem_ref[0].astype(BF16), w_ref[...]).astype(BF16)


def _mem_kv(mem, w):
    batch, m_len, d = mem.shape
    n = w.shape[1]
    return pl.pallas_call(
        _mem_kv_body,
        grid=(batch,),
        in_specs=[pl.BlockSpec((1, m_len, d), lambda b: (b, 0, 0)),
                  pl.BlockSpec((d, n), lambda b: (0, 0))],
        out_specs=pl.BlockSpec((1, m_len, n), lambda b: (b, 0, 0)),
        out_shape=jax.ShapeDtypeStruct((batch, m_len, n), BF16),
        compiler_params=pltpu.CompilerParams(dimension_semantics=("parallel",)),
        name="mem_kv",
    )(mem, w)


def _merge_body(h_ref, omla_ref, od0_ref, od1_ref, od2_ref, ls0_ref, ls1_ref, ls2_ref, kvm_ref,
                wgate_ref, wmq_ref, wba_ref, wbb_ref, wbc_ref, wo_ref, g_ref, b_ref, out_ref):
    h = h_ref[...]
    hb = h.astype(BF16)

    def heads_to_lanes(ref):
        return jnp.concatenate([ref[0, hh] for hh in range(DIL_HEADS_PER_GROUP)], axis=1)

    ls0, ls1, ls2 = heads_to_lanes(ls0_ref), heads_to_lanes(ls1_ref), heads_to_lanes(ls2_ref)
    mx = jnp.maximum(ls0, jnp.maximum(ls1, ls2))
    e0, e1, e2 = jnp.exp(ls0 - mx), jnp.exp(ls1 - mx), jnp.exp(ls2 - mx)
    o_dil = (e0 * heads_to_lanes(od0_ref) + e1 * heads_to_lanes(od1_ref)
             + e2 * heads_to_lanes(od2_ref)) / (e0 + e1 + e2)

    mq = (_dot(hb, wmq_ref[...]) * (MEM_HEAD_DIM ** -0.5)).astype(BF16)
    heads = []
    for hh in range(MEM_HEADS):
        kc = slice(hh * MEM_HEAD_DIM, (hh + 1) * MEM_HEAD_DIM)
        vc = slice(MEM_OUT + hh * MEM_HEAD_DIM, MEM_OUT + (hh + 1) * MEM_HEAD_DIM)
        s = _dot_nt(mq[:, kc], kvm_ref[0, :, kc])
        p = jnp.exp(s - jnp.max(s, axis=1, keepdims=True))
        heads.append(_dot(p.astype(BF16), kvm_ref[0, :, vc]) / jnp.sum(p, axis=1, keepdims=True))
    o_mem = jnp.concatenate(heads, axis=1)

    y_a = _dot(omla_ref[...], wba_ref[...])
    y_b = _dot(o_dil.astype(BF16), wbb_ref[...])
    y_c = _dot(o_mem.astype(BF16), wbc_ref[...])
    d = h.shape[1]
    merged = (jax.nn.sigmoid(_dot(hb, wgate_ref[:, :d])) * y_a
              + jax.nn.sigmoid(_dot(hb, wgate_ref[:, d:2 * d])) * y_b
              + jax.nn.sigmoid(_dot(hb, wgate_ref[:, 2 * d:])) * y_c)
    mix = _dot(merged.astype(BF16), wo_ref[...])
    out_ref[...] = _layer_norm(ALPHA * h + mix, g_ref[...], b_ref[...])


def _merge(h2d, o_mla, o_dil, lse_dil, kv_mem, weights, g, b, *, seq, tm):
    t, d = h2d.shape
    n_s = seq // tm

    def rows(width):
        return pl.BlockSpec((tm, width), lambda i: (i, 0))

    def full(a):
        return pl.BlockSpec(a.shape, lambda i: (0,) * a.ndim)

    m_len, kv_cols = kv_mem.shape[1:]
    return pl.pallas_call(
        _merge_body,
        grid=(t // tm,),
        in_specs=[rows(d), rows(MLA_OUT)]
        + [pl.BlockSpec((1, DIL_HEADS_PER_GROUP, tm, DIL_HEAD_DIM),
                        lambda i: (i // n_s, 0, i % n_s, 0))] * 6
        + [pl.BlockSpec((1, m_len, kv_cols), lambda i: (i // n_s, 0, 0))]
        + [full(w) for w in weights] + [full(g), full(b)],
        out_specs=rows(d),
        out_shape=jax.ShapeDtypeStruct((t, d), F32),
        compiler_params=pltpu.CompilerParams(
            dimension_semantics=("parallel",), vmem_limit_bytes=VMEM_LIMIT),
        name="merge",
    )(h2d, o_mla, *o_dil, *lse_dil, kv_mem, *weights, g, b)


def _swap_halves(w):
    half = w.shape[-1] // 2
    return jnp.concatenate([w[..., half:], w[..., :half]], axis=-1)


def _prep_mla_weights(w_in, w_uq, w_ukv):
    d = w_in.shape[0]
    o_q, o_kv, o_kr = 0, MLA_Q_LORA, MLA_Q_LORA + MLA_KV_LORA
    w_cq = w_in[:, o_q:o_q + MLA_Q_LORA]
    w_ckv = w_in[:, o_kv:o_kv + MLA_KV_LORA]
    w_kr = w_in[:, o_kr:o_kr + MLA_ROPE]
    pad_tail = LANES - MLA_NOPE - MLA_ROPE
    z_nope = jnp.zeros((d, MLA_NOPE), F32)
    z_tail = jnp.zeros((d, pad_tail), F32)
    w_kra = jnp.concatenate([z_nope, w_kr, z_tail], axis=1)
    w_krb = jnp.concatenate([z_nope, _swap_halves(w_kr), z_tail], axis=1)

    uq = w_uq.reshape(MLA_Q_LORA, MLA_HEADS, MLA_NOPE + MLA_ROPE)
    uq_nope, uq_pe = uq[..., :MLA_NOPE], uq[..., MLA_NOPE:]
    zq_nope = jnp.zeros_like(uq_nope)
    zq_tail = jnp.zeros((MLA_Q_LORA, MLA_HEADS, pad_tail), F32)
    w_qa = jnp.concatenate([uq_nope, uq_pe, zq_tail], axis=-1).reshape(MLA_Q_LORA, MLA_HEADS * LANES)
    w_qb = jnp.concatenate([zq_nope, _swap_halves(uq_pe), zq_tail], axis=-1).reshape(
        MLA_Q_LORA, MLA_HEADS * LANES)

    ukv = w_ukv.reshape(MLA_KV_LORA, MLA_HEADS, MLA_NOPE + MLA_V)
    uk, uv = ukv[..., :MLA_NOPE], ukv[..., MLA_NOPE:]
    w_ka = jnp.concatenate(
        [uk, jnp.zeros((MLA_KV_LORA, MLA_HEADS, LANES - MLA_NOPE), F32)], axis=-1).reshape(
            MLA_KV_LORA, MLA_HEADS * LANES)
    w_v = uv.reshape(MLA_KV_LORA, MLA_HEADS * MLA_V)
    return [w.astype(BF16) for w in (w_cq, w_ckv, w_kra, w_krb)], [w.astype(BF16) for w in (w_qa, w_qb, w_ka, w_v)]


def _prep_dil_weights(w_dil):
    d = w_dil.shape[0]
    w = w_dil.reshape(d, 3, DIL_GROUPS, DIL_OUT)
    return jnp.transpose(w, (0, 2, 1, 3)).reshape(d, DIL_QKV).astype(BF16)


def _rope_tables(seq):
    pos = jnp.arange(seq, dtype=F32)
    inv = 1.0 / (ROPE_THETA ** (jnp.arange(0, MLA_ROPE, 2, dtype=F32) / MLA_ROPE))
    ang = pos[:, None] * inv[None, :]
    cos, sin = jnp.cos(ang), jnp.sin(ang)
    ones = jnp.ones((seq, MLA_NOPE), F32)
    z_nope = jnp.zeros((seq, MLA_NOPE), F32)
    z_tail = jnp.zeros((seq, LANES - MLA_NOPE - MLA_ROPE), F32)
    q_scale = (MLA_NOPE + MLA_ROPE) ** -0.5 * math.log2(math.e)
    c_q = jnp.concatenate([ones, cos, cos, z_tail], axis=1) * q_scale
    s_q = jnp.concatenate([z_nope, -sin, sin, z_tail], axis=1) * q_scale
    c_k = jnp.concatenate([z_nope, cos, cos, z_tail], axis=1)
    s_k = jnp.concatenate([z_nope, -sin, sin, z_tail], axis=1)
    return c_q, s_q, c_k, s_k


def kernel(x, mem, w_in, mla_q_norm, mla_kv_norm, w_uq, w_ukv, w_mem_kv, w_br_mla, w_br_dil, w_br_mem, w_o,
           ffn1_w_gate, ffn1_w_up, ffn1_w_down, ffn2_w_gate, ffn2_w_up, ffn2_w_down,
           ln1_g, ln1_b, ln2_g, ln2_b, ln3_g, ln3_b):
    batch, seq, d = x.shape
    t = batch * seq
    d_ff = ffn1_w_gate.shape[-1]
    tm_ffn = min(512, t)
    tf = d_ff // 2
    tm_proj = min(512, seq)
    tm_merge = min(256, seq)
    tables = _rope_tables(seq)
    slopes = 2.0 ** (-8.0 * jnp.arange(1, DIL_HEADS + 1, dtype=F32) / DIL_HEADS)

    h = x.reshape(t, d)
    for l in range(DEPTH):
        bf = lambda w: w[l].astype(BF16)
        row = lambda v: v[l].reshape(1, -1)
        h = _ffn_ln(h, bf(ffn1_w_gate), bf(ffn1_w_up), bf(ffn1_w_down), row(ln1_g), row(ln1_b),
                    tm=tm_ffn, tf=tf)

        w_in_l = w_in[l]
        o_dil_cols = MLA_Q_LORA + MLA_KV_LORA + MLA_ROPE
        o_memq = o_dil_cols + DIL_QKV
        o_gate = o_memq + MEM_OUT
        w_c, w_u = _prep_mla_weights(w_in_l, w_uq[l], w_ukv[l])
        proj_weights = w_c + [row(mla_q_norm), row(mla_kv_norm)] + w_u + [
            _prep_dil_weights(w_in_l[:, o_dil_cols:o_memq])]
        q, k, v, *dil = _proj(h, proj_weights, tables, batch=batch, seq=seq, tm=tm_proj)

        o_mla = _mla_attn(q, k, v, tq=min(512, seq), tk=min(1024, seq)).reshape(t, MLA_OUT)
        dil_parts = [_dil_attn(dil[g], slopes, group=g, dilation=dl)
                     for g, (_, dl) in enumerate(DIL_PAIRS)]
        kv_mem = _mem_kv(mem, bf(w_mem_kv))

        merge_weights = [w_in_l[:, o_gate:].astype(BF16), w_in_l[:, o_memq:o_gate].astype(BF16),
                         bf(w_br_mla), bf(w_br_dil), bf(w_br_mem), bf(w_o)]
        h = _merge(h, o_mla, [p[0] for p in dil_parts], [p[1] for p in dil_parts], kv_mem,
                   merge_weights, row(ln2_g), row(ln2_b), seq=seq, tm=tm_merge)

        h = _ffn_ln(h, bf(ffn2_w_gate), bf(ffn2_w_up), bf(ffn2_w_down), row(ln3_g), row(ln3_b),
                    tm=tm_ffn, tf=tf)
    return h.reshape(batch, seq, d)
```

```python
import functools
import math

import jax
import jax.numpy as jnp
from jax import lax
from jax.experimental import pallas as pl
from jax.experimental.pallas import tpu as pltpu

F32 = jnp.float32
BF16 = jnp.bfloat16

D_MODEL = 1024
DEPTH = 1
MLA_HEADS = 8
MLA_Q_LORA = 256
MLA_KV_LORA = 256
MLA_NOPE = 64
MLA_ROPE = 32
MLA_V = 64
ROPE_THETA = 10000.0
DIL_PAIRS = ((128, 1), (512, 4), (2048, 16))
DIL_GROUPS = 3
DIL_HEADS_PER_GROUP = 4
DIL_HEAD_DIM = 128
DIL_HEADS = DIL_GROUPS * DIL_HEADS_PER_GROUP
MEM_HEADS = 4
MEM_HEAD_DIM = 128
EPS = 1e-5
ALPHA = (2 * DEPTH) ** 0.25

LANES = 128
DIL_OUT = DIL_HEADS_PER_GROUP * DIL_HEAD_DIM
DIL_QKV = 3 * DIL_HEADS * DIL_HEAD_DIM
MEM_OUT = MEM_HEADS * MEM_HEAD_DIM
MLA_OUT = MLA_HEADS * MLA_V
DIL_HALF_SPAN = 64
DIL_HALO = 64
DIL_SUB = 128
DIL_CHUNK_TOKENS = 2048
VMEM_LIMIT = 56 * 1024 * 1024

assert all(w // 2 // d == DIL_HALF_SPAN for w, d in DIL_PAIRS)

_NT = (((1,), (1,)), ((), ()))


def _dot(a, b):
    return jnp.dot(a, b, preferred_element_type=F32)


def _dot_nt(a, b):
    return lax.dot_general(a, b, _NT, preferred_element_type=F32)


def _layer_norm(y, g, b):
    mu = jnp.mean(y, axis=-1, keepdims=True)
    yc = y - mu
    var = jnp.mean(yc * yc, axis=-1, keepdims=True)
    return yc * lax.rsqrt(var + EPS) * g + b


def _rms_norm(y, g):
    return y * lax.rsqrt(jnp.mean(y * y, axis=-1, keepdims=True) + EPS) * g


def _ffn_ln_body(x_ref, wg_ref, wu_ref, wd_ref, g_ref, b_ref, o_ref, acc_ref, *, n_ff_steps):
    j = pl.program_id(1)

    @pl.when(j == 0)
    def _():
        acc_ref[...] = jnp.zeros_like(acc_ref)

    xb = x_ref[...].astype(BF16)
    gate = _dot(xb, wg_ref[...])
    up = _dot(xb, wu_ref[...])
    act = (gate * jax.nn.sigmoid(gate) * up).astype(BF16)
    acc_ref[...] += _dot(act, wd_ref[...])

    @pl.when(j == n_ff_steps - 1)
    def _():
        y = ALPHA * x_ref[...] + 0.5 * acc_ref[...]
        o_ref[...] = _layer_norm(y, g_ref[...], b_ref[...])


def _ffn_ln(x2d, wg, wu, wd, g, b, *, tm, tf):
    t, d = x2d.shape
    f = wg.shape[1]
    n_ff_steps = f // tf
    return pl.pallas_call(
        functools.partial(_ffn_ln_body, n_ff_steps=n_ff_steps),
        grid=(t // tm, n_ff_steps),
        in_specs=[
            pl.BlockSpec((tm, d), lambda i, j: (i, 0)),
            pl.BlockSpec((d, tf), lambda i, j: (0, j)),
            pl.BlockSpec((d, tf), lambda i, j: (0, j)),
            pl.BlockSpec((tf, d), lambda i, j: (j, 0)),
            pl.BlockSpec((1, d), lambda i, j: (0, 0)),
            pl.BlockSpec((1, d), lambda i, j: (0, 0)),
        ],
        out_specs=pl.BlockSpec((tm, d), lambda i, j: (i, 0)),
        out_shape=jax.ShapeDtypeStruct((t, d), F32),
        scratch_shapes=[pltpu.VMEM((tm, d), F32)],
        compiler_params=pltpu.CompilerParams(
            dimension_semantics=("parallel", "arbitrary"), vmem_limit_bytes=VMEM_LIMIT),
        name="ffn_ln",
    )(x2d, wg, wu, wd, g, b)


def _proj_body(*refs):
    n_hc = D_MODEL // LANES
    h_refs = refs[:n_hc]
    (wcq_ref, wckv_ref, wkra_ref, wkrb_ref, gq_ref, gkv_ref,
     wqa_ref, wqb_ref, wka_ref, wv_ref, wdil_ref,
     cq_ref, sq_ref, ck_ref, sk_ref,
     q_ref, k_ref, v_ref, dil0_ref, dil1_ref, dil2_ref, hperm_ref) = refs[n_hc:]
    hb = jnp.concatenate([hc[...] for hc in h_refs], axis=1).astype(BF16)
    cqn = _rms_norm(_dot(hb, wcq_ref[...]), gq_ref[...]).astype(BF16)
    ckvn = _rms_norm(_dot(hb, wckv_ref[...]), gkv_ref[...]).astype(BF16)
    k_rope = _dot(hb, wkra_ref[...]) * ck_ref[...] + _dot(hb, wkrb_ref[...]) * sk_ref[...]
    qa = _dot(cqn, wqa_ref[...])
    qb = _dot(cqn, wqb_ref[...])
    ka = _dot(ckvn, wka_ref[...])
    cq_t = cq_ref[...]
    sq_t = sq_ref[...]
    for h in range(MLA_HEADS):
        sl = slice(h * LANES, (h + 1) * LANES)
        q_ref[0, h] = (qa[:, sl] * cq_t + qb[:, sl] * sq_t).astype(BF16)
        k_ref[0, h] = (ka[:, sl] + k_rope).astype(BF16)
    v_ref[0] = _dot_nt(wv_ref[...], ckvn).astype(BF16)

    dil_scale = DIL_HEAD_DIM ** -0.5
    tm = h_refs[0].shape[0]
    group_cols = 3 * DIL_OUT
    for g, (dil_ref, (_, dilation)) in enumerate(zip((dil0_ref, dil1_ref, dil2_ref), DIL_PAIRS)):
        rows_per = tm // dilation
        if dilation == 1:
            hp = hb
        else:
            for r in range(dilation):
                for c, hc in enumerate(h_refs):
                    hperm_ref[r * rows_per:(r + 1) * rows_per, c * LANES:(c + 1) * LANES] = hc[
                        pl.ds(r, rows_per, stride=dilation), :].astype(BF16)
            hp = hperm_ref[...]
        c0 = g * group_cols
        qg = (_dot(hp, wdil_ref[:, c0:c0 + DIL_OUT]) * dil_scale).astype(BF16)
        kvg = _dot(hp, wdil_ref[:, c0 + DIL_OUT:c0 + group_cols]).astype(BF16)
        for r in range(dilation):
            rs = slice(r * rows_per, (r + 1) * rows_per)
            dil_ref[0, r, :, :DIL_OUT] = qg[rs]
            dil_ref[0, r, :, DIL_OUT:] = kvg[rs]


def _proj(h2d, weights, tables, *, batch, seq, tm):
    t, d = h2d.shape
    n_s = seq // tm

    def full(a):
        return pl.BlockSpec(a.shape, lambda i: (0,) * a.ndim)

    tab_spec = pl.BlockSpec((tm, LANES), lambda i: (i % n_s, 0))
    head_map = lambda i: (i // n_s, 0, i % n_s, 0)
    dil_specs = [pl.BlockSpec((1, dl, tm // dl, 3 * DIL_OUT), head_map) for _, dl in DIL_PAIRS]
    dil_shapes = [jax.ShapeDtypeStruct((batch, dl, seq // dl, 3 * DIL_OUT), BF16) for _, dl in DIL_PAIRS]
    return pl.pallas_call(
        _proj_body,
        grid=(t // tm,),
        in_specs=[pl.BlockSpec((tm, LANES), functools.partial(lambda c, i: (i, c), c))
                  for c in range(d // LANES)]
        + [full(w) for w in weights] + [tab_spec] * 4,
        out_specs=[
            pl.BlockSpec((1, MLA_HEADS, tm, LANES), head_map),
            pl.BlockSpec((1, MLA_HEADS, tm, LANES), head_map),
            pl.BlockSpec((1, MLA_OUT, tm), lambda i: (i // n_s, 0, i % n_s)),
        ] + dil_specs,
        out_shape=[
            jax.ShapeDtypeStruct((batch, MLA_HEADS, seq, LANES), BF16),
            jax.ShapeDtypeStruct((batch, MLA_HEADS, seq, LANES), BF16),
            jax.ShapeDtypeStruct((batch, MLA_OUT, seq), BF16),
        ] + dil_shapes,
        scratch_shapes=[pltpu.VMEM((tm, d), BF16)],
        compiler_params=pltpu.CompilerParams(
            dimension_semantics=("parallel",), vmem_limit_bytes=VMEM_LIMIT),
        name="proj",
    )(*([h2d] * (d // LANES)), *weights, *tables)


def _mla_attn_body(q_ref, k_ref, vt_ref, o_ref, st_ref, *, tk, n_kchunks):
    tq = q_ref.shape[2]
    heads = (0, 1)

    def scores(chunk, buf):
        off = pl.multiple_of(chunk * tk, tk)
        for h in heads:
            st_ref[buf, h] = _dot_nt(k_ref[0, h, pl.ds(off, tk), :], q_ref[0, h])

    def softmax_values(chunk, buf, carry):
        off = pl.multiple_of(chunk * tk, tk)
        out = []
        for h in heads:
            m, l, acc = carry[h]
            st = st_ref[buf, h]
            m_new = jnp.maximum(m, jnp.max(st, axis=0, keepdims=True))
            a = jnp.exp2(m - m_new)
            pt = jnp.exp2(st - m_new)
            l = a * l + jnp.sum(pt, axis=0, keepdims=True)
            vtc = vt_ref[0, h * MLA_V:(h + 1) * MLA_V, pl.ds(off, tk)]
            acc = a * acc + _dot(vtc, pt.astype(BF16))
            out.append((m_new, l, acc))
        return tuple(out)

    def pair_step(jj, carry):
        c = 2 * jj
        scores(c + 1, 1)
        carry = softmax_values(c, 0, carry)
        scores(jnp.minimum(c + 2, n_kchunks - 1), 0)
        return softmax_values(c + 1, 1, carry)

    init = (jnp.full((1, tq), -jnp.inf, F32), jnp.zeros((1, tq), F32), jnp.zeros((MLA_V, tq), F32))
    scores(0, 0)
    (_, l0, a0), (_, l1, a1) = lax.fori_loop(0, n_kchunks // 2, pair_step, (init, init))
    out_t = jnp.concatenate([a0 / l0, a1 / l1], axis=0)
    o_ref[0] = out_t.T.astype(o_ref.dtype)


def _mla_attn(q, k, v, *, tq, tk):
    batch, heads, seq, _ = q.shape
    assert (seq // tk) % 2 == 0
    return pl.pallas_call(
        functools.partial(_mla_attn_body, tk=tk, n_kchunks=seq // tk),
        grid=(batch, heads // 2, seq // tq),
        in_specs=[
            pl.BlockSpec((1, 2, tq, LANES), lambda b, hp, i: (b, hp, i, 0)),
            pl.BlockSpec((1, 2, seq, LANES), lambda b, hp, i: (b, hp, 0, 0)),
            pl.BlockSpec((1, 2 * MLA_V, seq), lambda b, hp, i: (b, hp, 0)),
        ],
        out_specs=pl.BlockSpec((1, tq, LANES), lambda b, hp, i: (b, i, hp)),
        out_shape=jax.ShapeDtypeStruct((batch, seq, MLA_OUT), BF16),
        scratch_shapes=[pltpu.VMEM((2, 2, tk, tq), F32)],
        compiler_params=pltpu.CompilerParams(
            dimension_semantics=("parallel", "parallel", "parallel"), vmem_limit_bytes=VMEM_LIMIT),
        name="mla_attn",
    )(q, k, v)


def _dil_attn_body(slopes_ref, q_ref, kl_ref, km_ref, kr_ref, vl_ref, vm_ref, vr_ref,
                   o_ref, lse_ref, kbuf, vbuf, *, group, dilation, tq, sub_len):
    i = pl.program_id(1)
    r = pl.program_id(2)
    kbuf[0:DIL_HALO] = kl_ref[0, 0]
    kbuf[DIL_HALO:DIL_HALO + tq] = km_ref[0, 0]
    kbuf[DIL_HALO + tq:] = kr_ref[0, 0]
    vbuf[0:DIL_HALO] = vl_ref[0, 0]
    vbuf[DIL_HALO:DIL_HALO + tq] = vm_ref[0, 0]
    vbuf[DIL_HALO + tq:] = vr_ref[0, 0]

    win = DIL_SUB + 2 * DIL_HALO
    row = lax.broadcasted_iota(jnp.int32, (DIL_SUB, win), 0)
    col = lax.broadcasted_iota(jnp.int32, (DIL_SUB, win), 1)
    steps = jnp.abs(col - DIL_HALO - row)
    in_band = steps <= DIL_HALF_SPAN
    dist = (steps * dilation).astype(F32)
    for sub in range(tq // DIL_SUB):
        key_pos = i * tq + (sub * DIL_SUB - DIL_HALO) + col
        valid = in_band & (key_pos >= 0) & (key_pos < sub_len)
        rows = slice(sub * DIL_SUB, (sub + 1) * DIL_SUB)
        wrows = slice(sub * DIL_SUB, sub * DIL_SUB + win)
        if dilation == 1:
            out_rows = pl.ds(sub * DIL_SUB, DIL_SUB)
        else:
            out_rows = pl.ds(sub * DIL_SUB * dilation + r, DIL_SUB, stride=dilation)
        for h in range(DIL_HEADS_PER_GROUP):
            cols = slice(h * DIL_HEAD_DIM, (h + 1) * DIL_HEAD_DIM)
            slope = slopes_ref[group * DIL_HEADS_PER_GROUP + h]
            s = _dot_nt(q_ref[0, 0, rows, cols], kbuf[wrows, cols]) - slope * dist
            s = jnp.where(valid, s, -jnp.inf)
            m = jnp.max(s, axis=1, keepdims=True)
            e = jnp.exp(s - m)
            den = jnp.sum(e, axis=1, keepdims=True)
            o_ref[0, h, out_rows, :] = _dot(e.astype(BF16), vbuf[wrows, cols]) / den
            lse_ref[0, h, out_rows, :] = jnp.broadcast_to(m + jnp.log(den), (DIL_SUB, DIL_HEAD_DIM))


def _dil_attn(dil_g, slopes, *, group, dilation):
    batch, _, sub_len, _ = dil_g.shape
    seq = sub_len * dilation
    tq = min(512, sub_len, DIL_CHUNK_TOKENS // dilation)
    n_q = sub_len // tq
    halo_per_tile = tq // DIL_HALO
    n_halo = sub_len // DIL_HALO
    q_col, k_col, v_col = 0, 1, 2

    def main(c):
        return pl.BlockSpec((1, 1, tq, DIL_OUT), lambda b, i, r: (b, r, i, c))

    def left(c):
        return pl.BlockSpec(
            (1, 1, DIL_HALO, DIL_OUT),
            lambda b, i, r: (b, r, jnp.maximum(i * halo_per_tile - 1, 0), c))

    def right(c):
        return pl.BlockSpec(
            (1, 1, DIL_HALO, DIL_OUT),
            lambda b, i, r: (b, r, jnp.minimum((i + 1) * halo_per_tile, n_halo - 1), c))

    out_spec = pl.BlockSpec((1, DIL_HEADS_PER_GROUP, tq * dilation, DIL_HEAD_DIM),
                            lambda b, i, r: (b, 0, i, 0))
    out_shape = jax.ShapeDtypeStruct((batch, DIL_HEADS_PER_GROUP, seq, DIL_HEAD_DIM), F32)
    o, lse = pl.pallas_call(
        functools.partial(_dil_attn_body, group=group, dilation=dilation, tq=tq, sub_len=sub_len),
        grid=(batch, n_q, dilation),
        in_specs=[pl.BlockSpec(memory_space=pltpu.SMEM),
                  main(q_col), left(k_col), main(k_col), right(k_col),
                  left(v_col), main(v_col), right(v_col)],
        out_specs=[out_spec, out_spec],
        out_shape=[out_shape, out_shape],
        scratch_shapes=[pltpu.VMEM((tq + 2 * DIL_HALO, DIL_OUT), BF16),
                        pltpu.VMEM((tq + 2 * DIL_HALO, DIL_OUT), BF16)],
        compiler_params=pltpu.CompilerParams(
            dimension_semantics=("parallel", "parallel", "arbitrary"), vmem_limit_bytes=VMEM_LIMIT),
        name=f"dil_attn_g{group}",
    )(slopes, dil_g, dil_g, dil_g, dil_g, dil_g, dil_g, dil_g)
    return o, lse


def _mem_kv_body(mem_ref, w_ref, o_ref):
    o_ref[0] = _dot(mem_ref[0].astype(BF16), w_ref[...]).astype(BF16)


def _mem_kv(mem, w):
    batch, m_len, d = mem.shape
    n = w.shape[1]
    return pl.pallas_call(
        _mem_kv_body,
        grid=(batch,),
        in_specs=[pl.BlockSpec((1, m_len, d), lambda b: (b, 0, 0)),
                  pl.BlockSpec((d, n), lambda b: (0, 0))],
        out_specs=pl.BlockSpec((1, m_len, n), lambda b: (b, 0, 0)),
        out_shape=jax.ShapeDtypeStruct((batch, m_len, n), BF16),
        compiler_params=pltpu.CompilerParams(dimension_semantics=("parallel",)),
        name="mem_kv",
    )(mem, w)


def _merge_body(h_ref, omla_ref, od0_ref, od1_ref, od2_ref, ls0_ref, ls1_ref, ls2_ref, kvm_ref,
                wgate_ref, wmq_ref, wba_ref, wbb_ref, wbc_ref, wo_ref, g_ref, b_ref, out_ref):
    h = h_ref[...]
    hb = h.astype(BF16)

    def heads_to_lanes(ref):
        return jnp.concatenate([ref[0, hh] for hh in range(DIL_HEADS_PER_GROUP)], axis=1)

    ls0, ls1, ls2 = heads_to_lanes(ls0_ref), heads_to_lanes(ls1_ref), heads_to_lanes(ls2_ref)
    mx = jnp.maximum(ls0, jnp.maximum(ls1, ls2))
    e0, e1, e2 = jnp.exp(ls0 - mx), jnp.exp(ls1 - mx), jnp.exp(ls2 - mx)
    o_dil = (e0 * heads_to_lanes(od0_ref) + e1 * heads_to_lanes(od1_ref)
             + e2 * heads_to_lanes(od2_ref)) / (e0 + e1 + e2)

    mq = (_dot(hb, wmq_ref[...]) * (MEM_HEAD_DIM ** -0.5)).astype(BF16)
    heads = []
    for hh in range(MEM_HEADS):
        kc = slice(hh * MEM_HEAD_DIM, (hh + 1) * MEM_HEAD_DIM)
        vc = slice(MEM_OUT + hh * MEM_HEAD_DIM, MEM_OUT + (hh + 1) * MEM_HEAD_DIM)
        s = _dot_nt(mq[:, kc], kvm_ref[0, :, kc])
        p = jnp.exp(s - jnp.max(s, axis=1, keepdims=True))
        heads.append(_dot(p.astype(BF16), kvm_ref[0, :, vc]) / jnp.sum(p, axis=1, keepdims=True))
    o_mem = jnp.concatenate(heads, axis=1)

    y_a = _dot(omla_ref[...], wba_ref[...])
    y_b = _dot(o_dil.astype(BF16), wbb_ref[...])
    y_c = _dot(o_mem.astype(BF16), wbc_ref[...])
    d = h.shape[1]
    merged = (jax.nn.sigmoid(_dot(hb, wgate_ref[:, :d])) * y_a
              + jax.nn.sigmoid(_dot(hb, wgate_ref[:, d:2 * d])) * y_b
              + jax.nn.sigmoid(_dot(hb, wgate_ref[:, 2 * d:])) * y_c)
    mix = _dot(merged.astype(BF16), wo_ref[...])
    out_ref[...] = _layer_norm(ALPHA * h + mix, g_ref[...], b_ref[...])


def _merge(h2d, o_mla, o_dil, lse_dil, kv_mem, weights, g, b, *, seq, tm):
    t, d = h2d.shape
    n_s = seq // tm

    def rows(width):
        return pl.BlockSpec((tm, width), lambda i: (i, 0))

    def full(a):
        return pl.BlockSpec(a.shape, lambda i: (0,) * a.ndim)

    m_len, kv_cols = kv_mem.shape[1:]
    return pl.pallas_call(
        _merge_body,
        grid=(t // tm,),
        in_specs=[rows(d), rows(MLA_OUT)]
        + [pl.BlockSpec((1, DIL_HEADS_PER_GROUP, tm, DIL_HEAD_DIM),
                        lambda i: (i // n_s, 0, i % n_s, 0))] * 6
        + [pl.BlockSpec((1, m_len, kv_cols), lambda i: (i // n_s, 0, 0))]
        + [full(w) for w in weights] + [full(g), full(b)],
        out_specs=rows(d),
        out_shape=jax.ShapeDtypeStruct((t, d), F32),
        compiler_params=pltpu.CompilerParams(
            dimension_semantics=("parallel",), vmem_limit_bytes=VMEM_LIMIT),
        name="merge",
    )(h2d, o_mla, *o_dil, *lse_dil, kv_mem, *weights, g, b)


def _swap_halves(w):
    half = w.shape[-1] // 2
    return jnp.concatenate([w[..., half:], w[..., :half]], axis=-1)


def _prep_mla_weights(w_in, w_uq, w_ukv):
    d = w_in.shape[0]
    o_q, o_kv, o_kr = 0, MLA_Q_LORA, MLA_Q_LORA + MLA_KV_LORA
    w_cq = w_in[:, o_q:o_q + MLA_Q_LORA]
    w_ckv = w_in[:, o_kv:o_kv + MLA_KV_LORA]
    w_kr = w_in[:, o_kr:o_kr + MLA_ROPE]
    pad_tail = LANES - MLA_NOPE - MLA_ROPE
    z_nope = jnp.zeros((d, MLA_NOPE), F32)
    z_tail = jnp.zeros((d, pad_tail), F32)
    w_kra = jnp.concatenate([z_nope, w_kr, z_tail], axis=1)
    w_krb = jnp.concatenate([z_nope, _swap_halves(w_kr), z_tail], axis=1)

    uq = w_uq.reshape(MLA_Q_LORA, MLA_HEADS, MLA_NOPE + MLA_ROPE)
    uq_nope, uq_pe = uq[..., :MLA_NOPE], uq[..., MLA_NOPE:]
    zq_nope = jnp.zeros_like(uq_nope)
    zq_tail = jnp.zeros((MLA_Q_LORA, MLA_HEADS, pad_tail), F32)
    w_qa = jnp.concatenate([uq_nope, uq_pe, zq_tail], axis=-1).reshape(MLA_Q_LORA, MLA_HEADS * LANES)
    w_qb = jnp.concatenate([zq_nope, _swap_halves(uq_pe), zq_tail], axis=-1).reshape(
        MLA_Q_LORA, MLA_HEADS * LANES)

    ukv = w_ukv.reshape(MLA_KV_LORA, MLA_HEADS, MLA_NOPE + MLA_V)
    uk, uv = ukv[..., :MLA_NOPE], ukv[..., MLA_NOPE:]
    w_ka = jnp.concatenate(
        [uk, jnp.zeros((MLA_KV_LORA, MLA_HEADS, LANES - MLA_NOPE), F32)], axis=-1).reshape(
            MLA_KV_LORA, MLA_HEADS * LANES)
    w_v = uv.reshape(MLA_KV_LORA, MLA_HEADS * MLA_V).T
    return [w.astype(BF16) for w in (w_cq, w_ckv, w_kra, w_krb)], [w.astype(BF16) for w in (w_qa, w_qb, w_ka, w_v)]


def _prep_dil_weights(w_dil):
    d = w_dil.shape[0]
    w = w_dil.reshape(d, 3, DIL_GROUPS, DIL_OUT)
    return jnp.transpose(w, (0, 2, 1, 3)).reshape(d, DIL_QKV).astype(BF16)


def _rope_tables(seq):
    pos = jnp.arange(seq, dtype=F32)
    inv = 1.0 / (ROPE_THETA ** (jnp.arange(0, MLA_ROPE, 2, dtype=F32) / MLA_ROPE))
    ang = pos[:, None] * inv[None, :]
    cos, sin = jnp.cos(ang), jnp.sin(ang)
    ones = jnp.ones((seq, MLA_NOPE), F32)
    z_nope = jnp.zeros((seq, MLA_NOPE), F32)
    z_tail = jnp.zeros((seq, LANES - MLA_NOPE - MLA_ROPE), F32)
    q_scale = (MLA_NOPE + MLA_ROPE) ** -0.5 * math.log2(math.e)
    c_q = jnp.concatenate([ones, cos, cos, z_tail], axis=1) * q_scale
    s_q = jnp.concatenate([z_nope, -sin, sin, z_tail], axis=1) * q_scale
    c_k = jnp.concatenate([z_nope, cos, cos, z_tail], axis=1)
    s_k = jnp.concatenate([z_nope, -sin, sin, z_tail], axis=1)
    return c_q, s_q, c_k, s_k


def kernel(x, mem, w_in, mla_q_norm, mla_kv_norm, w_uq, w_ukv, w_mem_kv, w_br_mla, w_br_dil, w_br_mem, w_o,
           ffn1_w_gate, ffn1_w_up, ffn1_w_down, ffn2_w_gate, ffn2_w_up, ffn2_w_down,
           ln1_g, ln1_b, ln2_g, ln2_b, ln3_g, ln3_b):
    batch, seq, d = x.shape
    t = batch * seq
    d_ff = ffn1_w_gate.shape[-1]
    tm_ffn = min(512, t)
    tf = d_ff // 2
    tm_proj = min(512, seq)
    tm_merge = min(256, seq)
    tables = _rope_tables(seq)
    slopes = 2.0 ** (-8.0 * jnp.arange(1, DIL_HEADS + 1, dtype=F32) / DIL_HEADS)

    h = x.reshape(t, d)
    for l in range(DEPTH):
        bf = lambda w: w[l].astype(BF16)
        row = lambda v: v[l].reshape(1, -1)
        h = _ffn_ln(h, bf(ffn1_w_gate), bf(ffn1_w_up), bf(ffn1_w_down), row(ln1_g), row(ln1_b),
                    tm=tm_ffn, tf=tf)

        w_in_l = w_in[l]
        o_dil_cols = MLA_Q_LORA + MLA_KV_LORA + MLA_ROPE
        o_memq = o_dil_cols + DIL_QKV
        o_gate = o_memq + MEM_OUT
        w_c, w_u = _prep_mla_weights(w_in_l, w_uq[l], w_ukv[l])
        proj_weights = w_c + [row(mla_q_norm), row(mla_kv_norm)] + w_u + [
            _prep_dil_weights(w_in_l[:, o_dil_cols:o_memq])]
        q, k, v, *dil = _proj(h, proj_weights, tables, batch=batch, seq=seq, tm=tm_proj)

        o_mla = _mla_attn(q, k, v, tq=min(512, seq), tk=min(1024, seq)).reshape(t, MLA_OUT)
        dil_parts = [_dil_attn(dil[g], slopes, group=g, dilation=dl)
                     for g, (_, dl) in enumerate(DIL_PAIRS)]
        kv_mem = _mem_kv(mem, bf(w_mem_kv))

        merge_weights = [w_in_l[:, o_gate:].astype(BF16), w_in_l[:, o_memq:o_gate].astype(BF16),
                         bf(w_br_mla), bf(w_br_dil), bf(w_br_mem), bf(w_o)]
        h = _merge(h, o_mla, [p[0] for p in dil_parts], [p[1] for p in dil_parts], kv_mem,
                   merge_weights, row(ln2_g), row(ln2_b), seq=seq, tm=tm_merge)

        h = _ffn_ln(h, bf(ffn2_w_gate), bf(ffn2_w_up), bf(ffn2_w_down), row(ln3_g), row(ln3_b),
                    tm=tm_ffn, tf=tf)
    return h.reshape(batch, seq, d)
```

```python
import functools
import math

import jax
import jax.numpy as jnp
from jax import lax
from jax.experimental import pallas as pl
from jax.experimental.pallas import tpu as pltpu

F32 = jnp.float32
BF16 = jnp.bfloat16

D_MODEL = 1024
DEPTH = 1
MLA_HEADS = 8
MLA_Q_LORA = 256
MLA_KV_LORA = 256
MLA_NOPE = 64
MLA_ROPE = 32
MLA_V = 64
ROPE_THETA = 10000.0
DIL_PAIRS = ((128, 1), (512, 4), (2048, 16))
DIL_GROUPS = 3
DIL_HEADS_PER_GROUP = 4
DIL_HEAD_DIM = 128
DIL_HEADS = DIL_GROUPS * DIL_HEADS_PER_GROUP
MEM_HEADS = 4
MEM_HEAD_DIM = 128
EPS = 1e-5
ALPHA = (2 * DEPTH) ** 0.25

LANES = 128
DIL_OUT = DIL_HEADS_PER_GROUP * DIL_HEAD_DIM
DIL_QKV = 3 * DIL_HEADS * DIL_HEAD_DIM
MEM_OUT = MEM_HEADS * MEM_HEAD_DIM
MLA_OUT = MLA_HEADS * MLA_V
MLA_DEN_ROWS = 16
MLA_PIECE = 256
DIL_HALF_SPAN = 64
DIL_HALO = 64
DIL_SUB = 128
DIL_CHUNK_TOKENS = 2048
VMEM_LIMIT = 56 * 1024 * 1024

assert all(w // 2 // d == DIL_HALF_SPAN for w, d in DIL_PAIRS)

_NT = (((1,), (1,)), ((), ()))


def _dot(a, b):
    return jnp.dot(a, b, preferred_element_type=F32)


def _dot_nt(a, b):
    return lax.dot_general(a, b, _NT, preferred_element_type=F32)


def _layer_norm(y, g, b):
    mu = jnp.mean(y, axis=-1, keepdims=True)
    yc = y - mu
    var = jnp.mean(yc * yc, axis=-1, keepdims=True)
    return yc * lax.rsqrt(var + EPS) * g + b


def _rms_norm(y, g):
    return y * lax.rsqrt(jnp.mean(y * y, axis=-1, keepdims=True) + EPS) * g


def _ffn_ln_body(x_ref, wg_ref, wu_ref, wd_ref, g_ref, b_ref, o_ref, acc_ref, *, n_ff_steps):
    j = pl.program_id(1)

    @pl.when(j == 0)
    def _():
        acc_ref[...] = jnp.zeros_like(acc_ref)

    xb = x_ref[...].astype(BF16)
    gate = _dot(xb, wg_ref[...])
    up = _dot(xb, wu_ref[...])
    act = (gate * jax.nn.sigmoid(gate) * up).astype(BF16)
    acc_ref[...] += _dot(act, wd_ref[...])

    @pl.when(j == n_ff_steps - 1)
    def _():
        y = ALPHA * x_ref[...] + 0.5 * acc_ref[...]
        o_ref[...] = _layer_norm(y, g_ref[...], b_ref[...])


def _ffn_ln(x2d, wg, wu, wd, g, b, *, tm, tf):
    t, d = x2d.shape
    f = wg.shape[1]
    n_ff_steps = f // tf
    return pl.pallas_call(
        functools.partial(_ffn_ln_body, n_ff_steps=n_ff_steps),
        grid=(t // tm, n_ff_steps),
        in_specs=[
            pl.BlockSpec((tm, d), lambda i, j: (i, 0)),
            pl.BlockSpec((d, tf), lambda i, j: (0, j)),
            pl.BlockSpec((d, tf), lambda i, j: (0, j)),
            pl.BlockSpec((tf, d), lambda i, j: (j, 0)),
            pl.BlockSpec((1, d), lambda i, j: (0, 0)),
            pl.BlockSpec((1, d), lambda i, j: (0, 0)),
        ],
        out_specs=pl.BlockSpec((tm, d), lambda i, j: (i, 0)),
        out_shape=jax.ShapeDtypeStruct((t, d), F32),
        scratch_shapes=[pltpu.VMEM((tm, d), F32)],
        compiler_params=pltpu.CompilerParams(
            dimension_semantics=("parallel", "arbitrary"), vmem_limit_bytes=VMEM_LIMIT),
        name="ffn_ln",
    )(x2d, wg, wu, wd, g, b)


def _proj_body(*refs):
    n_hc = D_MODEL // LANES
    h_refs = refs[:n_hc]
    (wcq_ref, wckv_ref, wkra_ref, wkrb_ref, gq_ref, gkv_ref,
     wqa_ref, wqb_ref, wka_ref, wv_ref, wdil_ref,
     cq_ref, sq_ref, ck_ref, sk_ref,
     q_ref, k_ref, v_ref, dil0_ref, dil1_ref, dil2_ref, hperm_ref) = refs[n_hc:]
    hb = jnp.concatenate([hc[...] for hc in h_refs], axis=1).astype(BF16)
    cqn = _rms_norm(_dot(hb, wcq_ref[...]), gq_ref[...]).astype(BF16)
    ckvn = _rms_norm(_dot(hb, wckv_ref[...]), gkv_ref[...]).astype(BF16)
    k_rope = _dot(hb, wkra_ref[...]) * ck_ref[...] + _dot(hb, wkrb_ref[...]) * sk_ref[...]
    qa = _dot(cqn, wqa_ref[...])
    qb = _dot(cqn, wqb_ref[...])
    ka = _dot(ckvn, wka_ref[...])
    cq_t = cq_ref[...]
    sq_t = sq_ref[...]
    for h in range(MLA_HEADS):
        sl = slice(h * LANES, (h + 1) * LANES)
        q_ref[0, h] = (qa[:, sl] * cq_t + qb[:, sl] * sq_t).astype(BF16)
        k_ref[0, h] = (ka[:, sl] + k_rope).astype(BF16)
    v_ref[0] = _dot_nt(wv_ref[...], ckvn).astype(BF16)

    dil_scale = DIL_HEAD_DIM ** -0.5
    tm = h_refs[0].shape[0]
    group_cols = 3 * DIL_OUT
    for g, (dil_ref, (_, dilation)) in enumerate(zip((dil0_ref, dil1_ref, dil2_ref), DIL_PAIRS)):
        rows_per = tm // dilation
        if dilation == 1:
            hp = hb
        else:
            for r in range(dilation):
                for c, hc in enumerate(h_refs):
                    hperm_ref[r * rows_per:(r + 1) * rows_per, c * LANES:(c + 1) * LANES] = hc[
                        pl.ds(r, rows_per, stride=dilation), :].astype(BF16)
            hp = hperm_ref[...]
        c0 = g * group_cols
        qg = (_dot(hp, wdil_ref[:, c0:c0 + DIL_OUT]) * dil_scale).astype(BF16)
        kvg = _dot(hp, wdil_ref[:, c0 + DIL_OUT:c0 + group_cols]).astype(BF16)
        for r in range(dilation):
            rs = slice(r * rows_per, (r + 1) * rows_per)
            dil_ref[0, r, :, :DIL_OUT] = qg[rs]
            dil_ref[0, r, :, DIL_OUT:] = kvg[rs]


def _proj(h2d, weights, tables, *, batch, seq, tm):
    t, d = h2d.shape
    n_s = seq // tm

    def full(a):
        return pl.BlockSpec(a.shape, lambda i: (0,) * a.ndim)

    tab_spec = pl.BlockSpec((tm, LANES), lambda i: (i % n_s, 0))
    head_map = lambda i: (i // n_s, 0, i % n_s, 0)
    dil_specs = [pl.BlockSpec((1, dl, tm // dl, 3 * DIL_OUT), head_map) for _, dl in DIL_PAIRS]
    dil_shapes = [jax.ShapeDtypeStruct((batch, dl, seq // dl, 3 * DIL_OUT), BF16) for _, dl in DIL_PAIRS]
    return pl.pallas_call(
        _proj_body,
        grid=(t // tm,),
        in_specs=[pl.BlockSpec((tm, LANES), functools.partial(lambda c, i: (i, c), c))
                  for c in range(d // LANES)]
        + [full(w) for w in weights] + [tab_spec] * 4,
        out_specs=[
            pl.BlockSpec((1, MLA_HEADS, tm, LANES), head_map),
            pl.BlockSpec((1, MLA_HEADS, tm, LANES), head_map),
            pl.BlockSpec((1, MLA_OUT, tm), lambda i: (i // n_s, 0, i % n_s)),
        ] + dil_specs,
        out_shape=[
            jax.ShapeDtypeStruct((batch, MLA_HEADS, seq, LANES), BF16),
            jax.ShapeDtypeStruct((batch, MLA_HEADS, seq, LANES), BF16),
            jax.ShapeDtypeStruct((batch, MLA_OUT, seq), BF16),
        ] + dil_shapes,
        scratch_shapes=[pltpu.VMEM((tm, d), BF16)],
        compiler_params=pltpu.CompilerParams(
            dimension_semantics=("parallel",), vmem_limit_bytes=VMEM_LIMIT),
        name="proj",
    )(*([h2d] * (d // LANES)), *weights, *tables)


def _mla_attn_body(q_ref, k_ref, vt_ref, o_ref, st_ref, *, tk, n_kchunks):
    tq = q_ref.shape[2]
    heads = (0, 1)
    n_pieces = tk // MLA_PIECE

    def scores_piece(chunk, buf, p, maxes):
        off = pl.multiple_of(chunk * tk, tk) + p * MLA_PIECE
        rows = slice(p * MLA_PIECE, (p + 1) * MLA_PIECE)
        out = []
        for h in heads:
            st = _dot_nt(k_ref[0, h, pl.ds(off, MLA_PIECE), :], q_ref[0, h])
            st_ref[buf, h, rows] = st
            out.append(jnp.maximum(maxes[h], jnp.max(st, axis=0, keepdims=True)))
        return tuple(out)

    ones_rows = jnp.ones((MLA_DEN_ROWS, MLA_PIECE), BF16)

    def values_piece(chunk, buf, p, m_new, accs):
        off = pl.multiple_of(chunk * tk, tk) + p * MLA_PIECE
        rows = slice(p * MLA_PIECE, (p + 1) * MLA_PIECE)
        out = []
        for h in heads:
            pt = jnp.exp2(st_ref[buf, h, rows] - m_new[h]).astype(BF16)
            vtc = jnp.concatenate(
                [vt_ref[0, h * MLA_V:(h + 1) * MLA_V, pl.ds(off, MLA_PIECE)], ones_rows], axis=0)
            out.append(accs[h] + _dot(vtc, pt))
        return tuple(out)

    neg_inf = jnp.full((1, tq), -jnp.inf, F32)

    def half(next_chunk, next_buf, chunk, buf, chunk_max, state):
        m_new = tuple(jnp.maximum(state[h][0], chunk_max[h]) for h in heads)
        accs = tuple(jnp.exp2(state[h][0] - m_new[h]) * state[h][1] for h in heads)
        next_max = (neg_inf, neg_inf)
        for p in range(n_pieces):
            next_max = scores_piece(next_chunk, next_buf, p, next_max)
            accs = values_piece(chunk, buf, p, m_new, accs)
        return next_max, tuple((m_new[h], accs[h]) for h in heads)

    def pair_step(jj, loop_carry):
        max_a, state = loop_carry
        c = 2 * jj
        max_b, state = half(c + 1, 1, c, 0, max_a, state)
        return half(jnp.minimum(c + 2, n_kchunks - 1), 0, c + 1, 1, max_b, state)

    max_0 = (neg_inf, neg_inf)
    for p in range(n_pieces):
        max_0 = scores_piece(0, 0, p, max_0)
    init = (neg_inf, jnp.zeros((MLA_V + MLA_DEN_ROWS, tq), F32))
    _, ((_, a0), (_, a1)) = lax.fori_loop(0, n_kchunks // 2, pair_step, (max_0, (init, init)))
    out_t = jnp.concatenate([a[:MLA_V] / a[MLA_V:MLA_V + 1] for a in (a0, a1)], axis=0)
    o_ref[0] = out_t.T.astype(o_ref.dtype)


def _mla_attn(q, k, v, *, tq, tk):
    batch, heads, seq, _ = q.shape
    assert (seq // tk) % 2 == 0
    return pl.pallas_call(
        functools.partial(_mla_attn_body, tk=tk, n_kchunks=seq // tk),
        grid=(batch, heads // 2, seq // tq),
        in_specs=[
            pl.BlockSpec((1, 2, tq, LANES), lambda b, hp, i: (b, hp, i, 0)),
            pl.BlockSpec((1, 2, seq, LANES), lambda b, hp, i: (b, hp, 0, 0)),
            pl.BlockSpec((1, 2 * MLA_V, seq), lambda b, hp, i: (b, hp, 0)),
        ],
        out_specs=pl.BlockSpec((1, tq, LANES), lambda b, hp, i: (b, i, hp)),
        out_shape=jax.ShapeDtypeStruct((batch, seq, MLA_OUT), BF16),
        scratch_shapes=[pltpu.VMEM((2, 2, tk, tq), F32)],
        compiler_params=pltpu.CompilerParams(
            dimension_semantics=("parallel", "parallel", "parallel"), vmem_limit_bytes=VMEM_LIMIT),
        name="mla_attn",
    )(q, k, v)


def _dil_attn_body(slopes_ref, q_ref, kl_ref, km_ref, kr_ref, vl_ref, vm_ref, vr_ref,
                   o_ref, lse_ref, kbuf, vbuf, *, group, dilation, tq, sub_len):
    i = pl.program_id(1)
    r = pl.program_id(2)
    kbuf[0:DIL_HALO] = kl_ref[0, 0]
    kbuf[DIL_HALO:DIL_HALO + tq] = km_ref[0, 0]
    kbuf[DIL_HALO + tq:] = kr_ref[0, 0]
    vbuf[0:DIL_HALO] = vl_ref[0, 0]
    vbuf[DIL_HALO:DIL_HALO + tq] = vm_ref[0, 0]
    vbuf[DIL_HALO + tq:] = vr_ref[0, 0]

    win = DIL_SUB + 2 * DIL_HALO
    row = lax.broadcasted_iota(jnp.int32, (DIL_SUB, win), 0)
    col = lax.broadcasted_iota(jnp.int32, (DIL_SUB, win), 1)
    steps = jnp.abs(col - DIL_HALO - row)
    in_band = steps <= DIL_HALF_SPAN
    dist = (steps * dilation).astype(F32)
    for sub in range(tq // DIL_SUB):
        key_pos = i * tq + (sub * DIL_SUB - DIL_HALO) + col
        valid = in_band & (key_pos >= 0) & (key_pos < sub_len)
        rows = slice(sub * DIL_SUB, (sub + 1) * DIL_SUB)
        wrows = slice(sub * DIL_SUB, sub * DIL_SUB + win)
        if dilation == 1:
            out_rows = pl.ds(sub * DIL_SUB, DIL_SUB)
        else:
            out_rows = pl.ds(sub * DIL_SUB * dilation + r, DIL_SUB, stride=dilation)
        for h in range(DIL_HEADS_PER_GROUP):
            cols = slice(h * DIL_HEAD_DIM, (h + 1) * DIL_HEAD_DIM)
            slope = slopes_ref[group * DIL_HEADS_PER_GROUP + h]
            s = _dot_nt(q_ref[0, 0, rows, cols], kbuf[wrows, cols]) - slope * dist
            s = jnp.where(valid, s, -jnp.inf)
            m = jnp.max(s, axis=1, keepdims=True)
            e = jnp.exp(s - m)
            den = jnp.sum(e, axis=1, keepdims=True)
            o_ref[0, h, out_rows, :] = _dot(e.astype(BF16), vbuf[wrows, cols]) / den
            lse_ref[0, h, out_rows, :] = jnp.broadcast_to(m + jnp.log(den), (DIL_SUB, DIL_HEAD_DIM))


def _dil_attn(dil_g, slopes, *, group, dilation):
    batch, _, sub_len, _ = dil_g.shape
    seq = sub_len * dilation
    tq = min(512, sub_len, DIL_CHUNK_TOKENS // dilation)
    n_q = sub_len // tq
    halo_per_tile = tq // DIL_HALO
    n_halo = sub_len // DIL_HALO
    q_col, k_col, v_col = 0, 1, 2

    def main(c):
        return pl.BlockSpec((1, 1, tq, DIL_OUT), lambda b, i, r: (b, r, i, c))

    def left(c):
        return pl.BlockSpec(
            (1, 1, DIL_HALO, DIL_OUT),
            lambda b, i, r: (b, r, jnp.maximum(i * halo_per_tile - 1, 0), c))

    def right(c):
        return pl.BlockSpec(
            (1, 1, DIL_HALO, DIL_OUT),
            lambda b, i, r: (b, r, jnp.minimum((i + 1) * halo_per_tile, n_halo - 1), c))

    out_spec = pl.BlockSpec((1, DIL_HEADS_PER_GROUP, tq * dilation, DIL_HEAD_DIM),
                            lambda b, i, r: (b, 0, i, 0))
    out_shape = jax.ShapeDtypeStruct((batch, DIL_HEADS_PER_GROUP, seq, DIL_HEAD_DIM), F32)
    o, lse = pl.pallas_call(
        functools.partial(_dil_attn_body, group=group, dilation=dilation, tq=tq, sub_len=sub_len),
        grid=(batch, n_q, dilation),
        in_specs=[pl.BlockSpec(memory_space=pltpu.SMEM),
                  main(q_col), left(k_col), main(k_col), right(k_col),
                  left(v_col), main(v_col), right(v_col)],
        out_specs=[out_spec, out_spec],
        out_shape=[out_shape, out_shape],
        scratch_shapes=[pltpu.VMEM((tq + 2 * DIL_HALO, DIL_OUT), BF16),
                        pltpu.VMEM((tq + 2 * DIL_HALO, DIL_OUT), BF16)],
        compiler_params=pltpu.CompilerParams(
            dimension_semantics=("parallel", "parallel", "arbitrary"), vmem_limit_bytes=VMEM_LIMIT),
        name=f"dil_attn_g{group}",
    )(slopes, dil_g, dil_g, dil_g, dil_g, dil_g, dil_g, dil_g)
    return o, lse


def _mem_kv_body(mem_ref, w_ref, o_ref):
    o_ref[0] = _dot(mem_ref[0].astype(BF16), w_ref[...]).astype(BF16)


def _mem_kv(mem, w):
    batch, m_len, d = mem.shape
    n = w.shape[1]
    return pl.pallas_call(
        _mem_kv_body,
        grid=(batch,),
        in_specs=[pl.BlockSpec((1, m_len, d), lambda b: (b, 0, 0)),
                  pl.BlockSpec((d, n), lambda b: (0, 0))],
        out_specs=pl.BlockSpec((1, m_len, n), lambda b: (b, 0, 0)),
        out_shape=jax.ShapeDtypeStruct((batch, m_len, n), BF16),
        compiler_params=pltpu.CompilerParams(dimension_semantics=("parallel",)),
        name="mem_kv",
    )(mem, w)


def _merge_body(h_ref, omla_ref, od0_ref, od1_ref, od2_ref, ls0_ref, ls1_ref, ls2_ref, kvm_ref,
                wgate_ref, wmq_ref, wba_ref, wbb_ref, wbc_ref, wo_ref, g_ref, b_ref, out_ref):
    h = h_ref[...]
    hb = h.astype(BF16)

    def heads_to_lanes(ref):
        return jnp.concatenate([ref[0, hh] for hh in range(DIL_HEADS_PER_GROUP)], axis=1)

    ls0, ls1, ls2 = heads_to_lanes(ls0_ref), heads_to_lanes(ls1_ref), heads_to_lanes(ls2_ref)
    mx = jnp.maximum(ls0, jnp.maximum(ls1, ls2))
    e0, e1, e2 = jnp.exp(ls0 - mx), jnp.exp(ls1 - mx), jnp.exp(ls2 - mx)
    o_dil = (e0 * heads_to_lanes(od0_ref) + e1 * heads_to_lanes(od1_ref)
             + e2 * heads_to_lanes(od2_ref)) / (e0 + e1 + e2)

    mq = (_dot(hb, wmq_ref[...]) * (MEM_HEAD_DIM ** -0.5)).astype(BF16)
    heads = []
    for hh in range(MEM_HEADS):
        kc = slice(hh * MEM_HEAD_DIM, (hh + 1) * MEM_HEAD_DIM)
        vc = slice(MEM_OUT + hh * MEM_HEAD_DIM, MEM_OUT + (hh + 1) * MEM_HEAD_DIM)
        s = _dot_nt(mq[:, kc], kvm_ref[0, :, kc])
        p = jnp.exp(s - jnp.max(s, axis=1, keepdims=True))
        heads.append(_dot(p.astype(BF16), kvm_ref[0, :, vc]) / jnp.sum(p, axis=1, keepdims=True))
    o_mem = jnp.concatenate(heads, axis=1)

    y_a = _dot(omla_ref[...], wba_ref[...])
    y_b = _dot(o_dil.astype(BF16), wbb_ref[...])
    y_c = _dot(o_mem.astype(BF16), wbc_ref[...])
    d = h.shape[1]
    merged = (jax.nn.sigmoid(_dot(hb, wgate_ref[:, :d])) * y_a
              + jax.nn.sigmoid(_dot(hb, wgate_ref[:, d:2 * d])) * y_b
              + jax.nn.sigmoid(_dot(hb, wgate_ref[:, 2 * d:])) * y_c)
    mix = _dot(merged.astype(BF16), wo_ref[...])
    out_ref[...] = _layer_norm(ALPHA * h + mix, g_ref[...], b_ref[...])


def _merge(h2d, o_mla, o_dil, lse_dil, kv_mem, weights, g, b, *, seq, tm):
    t, d = h2d.shape
    n_s = seq // tm

    def rows(width):
        return pl.BlockSpec((tm, width), lambda i: (i, 0))

    def full(a):
        return pl.BlockSpec(a.shape, lambda i: (0,) * a.ndim)

    m_len, kv_cols = kv_mem.shape[1:]
    return pl.pallas_call(
        _merge_body,
        grid=(t // tm,),
        in_specs=[rows(d), rows(MLA_OUT)]
        + [pl.BlockSpec((1, DIL_HEADS_PER_GROUP, tm, DIL_HEAD_DIM),
                        lambda i: (i // n_s, 0, i % n_s, 0))] * 6
        + [pl.BlockSpec((1, m_len, kv_cols), lambda i: (i // n_s, 0, 0))]
        + [full(w) for w in weights] + [full(g), full(b)],
        out_specs=rows(d),
        out_shape=jax.ShapeDtypeStruct((t, d), F32),
        compiler_params=pltpu.CompilerParams(
            dimension_semantics=("parallel",), vmem_limit_bytes=VMEM_LIMIT),
        name="merge",
    )(h2d, o_mla, *o_dil, *lse_dil, kv_mem, *weights, g, b)


def _swap_halves(w):
    half = w.shape[-1] // 2
    return jnp.concatenate([w[..., half:], w[..., :half]], axis=-1)


def _prep_mla_weights(w_in, w_uq, w_ukv):
    d = w_in.shape[0]
    o_q, o_kv, o_kr = 0, MLA_Q_LORA, MLA_Q_LORA + MLA_KV_LORA
    w_cq = w_in[:, o_q:o_q + MLA_Q_LORA]
    w_ckv = w_in[:, o_kv:o_kv + MLA_KV_LORA]
    w_kr = w_in[:, o_kr:o_kr + MLA_ROPE]
    pad_tail = LANES - MLA_NOPE - MLA_ROPE
    z_nope = jnp.zeros((d, MLA_NOPE), F32)
    z_tail = jnp.zeros((d, pad_tail), F32)
    w_kra = jnp.concatenate([z_nope, w_kr, z_tail], axis=1)
    w_krb = jnp.concatenate([z_nope, _swap_halves(w_kr), z_tail], axis=1)

    uq = w_uq.reshape(MLA_Q_LORA, MLA_HEADS, MLA_NOPE + MLA_ROPE)
    uq_nope, uq_pe = uq[..., :MLA_NOPE], uq[..., MLA_NOPE:]
    zq_nope = jnp.zeros_like(uq_nope)
    zq_tail = jnp.zeros((MLA_Q_LORA, MLA_HEADS, pad_tail), F32)
    w_qa = jnp.concatenate([uq_nope, uq_pe, zq_tail], axis=-1).reshape(MLA_Q_LORA, MLA_HEADS * LANES)
    w_qb = jnp.concatenate([zq_nope, _swap_halves(uq_pe), zq_tail], axis=-1).reshape(
        MLA_Q_LORA, MLA_HEADS * LANES)

    ukv = w_ukv.reshape(MLA_KV_LORA, MLA_HEADS, MLA_NOPE + MLA_V)
    uk, uv = ukv[..., :MLA_NOPE], ukv[..., MLA_NOPE:]
    w_ka = jnp.concatenate(
        [uk, jnp.zeros((MLA_KV_LORA, MLA_HEADS, LANES - MLA_NOPE), F32)], axis=-1).reshape(
            MLA_KV_LORA, MLA_HEADS * LANES)
    w_v = uv.reshape(MLA_KV_LORA, MLA_HEADS * MLA_V).T
    return [w.astype(BF16) for w in (w_cq, w_ckv, w_kra, w_krb)], [w.astype(BF16) for w in (w_qa, w_qb, w_ka, w_v)]


def _prep_dil_weights(w_dil):
    d = w_dil.shape[0]
    w = w_dil.reshape(d, 3, DIL_GROUPS, DIL_OUT)
    return jnp.transpose(w, (0, 2, 1, 3)).reshape(d, DIL_QKV).astype(BF16)


def _rope_tables(seq):
    pos = jnp.arange(seq, dtype=F32)
    inv = 1.0 / (ROPE_THETA ** (jnp.arange(0, MLA_ROPE, 2, dtype=F32) / MLA_ROPE))
    ang = pos[:, None] * inv[None, :]
    cos, sin = jnp.cos(ang), jnp.sin(ang)
    ones = jnp.ones((seq, MLA_NOPE), F32)
    z_nope = jnp.zeros((seq, MLA_NOPE), F32)
    z_tail = jnp.zeros((seq, LANES - MLA_NOPE - MLA_ROPE), F32)
    q_scale = (MLA_NOPE + MLA_ROPE) ** -0.5 * math.log2(math.e)
    c_q = jnp.concatenate([ones, cos, cos, z_tail], axis=1) * q_scale
    s_q = jnp.concatenate([z_nope, -sin, sin, z_tail], axis=1) * q_scale
    c_k = jnp.concatenate([z_nope, cos, cos, z_tail], axis=1)
    s_k = jnp.concatenate([z_nope, -sin, sin, z_tail], axis=1)
    return c_q, s_q, c_k, s_k


def kernel(x, mem, w_in, mla_q_norm, mla_kv_norm, w_uq, w_ukv, w_mem_kv, w_br_mla, w_br_dil, w_br_mem, w_o,
           ffn1_w_gate, ffn1_w_up, ffn1_w_down, ffn2_w_gate, ffn2_w_up, ffn2_w_down,
           ln1_g, ln1_b, ln2_g, ln2_b, ln3_g, ln3_b):
    batch, seq, d = x.shape
    t = batch * seq
    d_ff = ffn1_w_gate.shape[-1]
    tm_ffn = min(512, t)
    tf = d_ff // 2
    tm_proj = min(512, seq)
    tm_merge = min(256, seq)
    tables = _rope_tables(seq)
    slopes = 2.0 ** (-8.0 * jnp.arange(1, DIL_HEADS + 1, dtype=F32) / DIL_HEADS)

    h = x.reshape(t, d)
    for l in range(DEPTH):
        bf = lambda w: w[l].astype(BF16)
        row = lambda v: v[l].reshape(1, -1)
        h = _ffn_ln(h, bf(ffn1_w_gate), bf(ffn1_w_up), bf(ffn1_w_down), row(ln1_g), row(ln1_b),
                    tm=tm_ffn, tf=tf)

        w_in_l = w_in[l]
        o_dil_cols = MLA_Q_LORA + MLA_KV_LORA + MLA_ROPE
        o_memq = o_dil_cols + DIL_QKV
        o_gate = o_memq + MEM_OUT
        w_c, w_u = _prep_mla_weights(w_in_l, w_uq[l], w_ukv[l])
        proj_weights = w_c + [row(mla_q_norm), row(mla_kv_norm)] + w_u + [
            _prep_dil_weights(w_in_l[:, o_dil_cols:o_memq])]
        q, k, v, *dil = _proj(h, proj_weights, tables, batch=batch, seq=seq, tm=tm_proj)

        o_mla = _mla_attn(q, k, v, tq=min(512, seq), tk=min(1024, seq)).reshape(t, MLA_OUT)
        dil_parts = [_dil_attn(dil[g], slopes, group=g, dilation=dl)
                     for g, (_, dl) in enumerate(DIL_PAIRS)]
        kv_mem = _mem_kv(mem, bf(w_mem_kv))

        merge_weights = [w_in_l[:, o_gate:].astype(BF16), w_in_l[:, o_memq:o_gate].astype(BF16),
                         bf(w_br_mla), bf(w_br_dil), bf(w_br_mem), bf(w_o)]
        h = _merge(h, o_mla, [p[0] for p in dil_parts], [p[1] for p in dil_parts], kv_mem,
                   merge_weights, row(ln2_g), row(ln2_b), seq=seq, tm=tm_merge)

        h = _ffn_ln(h, bf(ffn2_w_gate), bf(ffn2_w_up), bf(ffn2_w_down), row(ln3_g), row(ln3_b),
                    tm=tm_ffn, tf=tf)
    return h.reshape(batch, seq, d)
```

```python
import functools
import math

import jax
import jax.numpy as jnp
from jax import lax
from jax.experimental import pallas as pl
from jax.experimental.pallas import tpu as pltpu

F32 = jnp.float32
BF16 = jnp.bfloat16

D_MODEL = 1024
DEPTH = 1
MLA_HEADS = 8
MLA_Q_LORA = 256
MLA_KV_LORA = 256
MLA_NOPE = 64
MLA_ROPE = 32
MLA_V = 64
ROPE_THETA = 10000.0
DIL_PAIRS = ((128, 1), (512, 4), (2048, 16))
DIL_GROUPS = 3
DIL_HEADS_PER_GROUP = 4
DIL_HEAD_DIM = 128
DIL_HEADS = DIL_GROUPS * DIL_HEADS_PER_GROUP
MEM_HEADS = 4
MEM_HEAD_DIM = 128
EPS = 1e-5
ALPHA = (2 * DEPTH) ** 0.25

LANES = 128
DIL_OUT = DIL_HEADS_PER_GROUP * DIL_HEAD_DIM
DIL_QKV = 3 * DIL_HEADS * DIL_HEAD_DIM
MEM_OUT = MEM_HEADS * MEM_HEAD_DIM
MLA_OUT = MLA_HEADS * MLA_V
MLA_DEN_ROWS = 16
MLA_PIECE = 256
DIL_HALF_SPAN = 64
DIL_HALO = 64
DIL_SUB = 128
DIL_CHUNK_TOKENS = 2048
DIL_ROWS_PER_STEP = 512
VMEM_LIMIT = 56 * 1024 * 1024

assert all(w // 2 // d == DIL_HALF_SPAN for w, d in DIL_PAIRS)

_NT = (((1,), (1,)), ((), ()))


def _dot(a, b):
    return jnp.dot(a, b, preferred_element_type=F32)


def _dot_nt(a, b):
    return lax.dot_general(a, b, _NT, preferred_element_type=F32)


def _layer_norm(y, g, b):
    mu = jnp.mean(y, axis=-1, keepdims=True)
    yc = y - mu
    var = jnp.mean(yc * yc, axis=-1, keepdims=True)
    return yc * lax.rsqrt(var + EPS) * g + b


def _rms_norm(y, g):
    return y * lax.rsqrt(jnp.mean(y * y, axis=-1, keepdims=True) + EPS) * g


def _ffn_ln_body(x_ref, wg_ref, wu_ref, wd_ref, g_ref, b_ref, o_ref, acc_ref, *, n_ff_steps):
    j = pl.program_id(1)

    @pl.when(j == 0)
    def _():
        acc_ref[...] = jnp.zeros_like(acc_ref)

    xb = x_ref[...].astype(BF16)
    gate = _dot(xb, wg_ref[...])
    up = _dot(xb, wu_ref[...])
    act = (gate * jax.nn.sigmoid(gate) * up).astype(BF16)
    acc_ref[...] += _dot(act, wd_ref[...])

    @pl.when(j == n_ff_steps - 1)
    def _():
        y = ALPHA * x_ref[...] + 0.5 * acc_ref[...]
        o_ref[...] = _layer_norm(y, g_ref[...], b_ref[...])


def _ffn_ln(x2d, wg, wu, wd, g, b, *, tm, tf):
    t, d = x2d.shape
    f = wg.shape[1]
    n_ff_steps = f // tf
    return pl.pallas_call(
        functools.partial(_ffn_ln_body, n_ff_steps=n_ff_steps),
        grid=(t // tm, n_ff_steps),
        in_specs=[
            pl.BlockSpec((tm, d), lambda i, j: (i, 0)),
            pl.BlockSpec((d, tf), lambda i, j: (0, j)),
            pl.BlockSpec((d, tf), lambda i, j: (0, j)),
            pl.BlockSpec((tf, d), lambda i, j: (j, 0)),
            pl.BlockSpec((1, d), lambda i, j: (0, 0)),
            pl.BlockSpec((1, d), lambda i, j: (0, 0)),
        ],
        out_specs=pl.BlockSpec((tm, d), lambda i, j: (i, 0)),
        out_shape=jax.ShapeDtypeStruct((t, d), F32),
        scratch_shapes=[pltpu.VMEM((tm, d), F32)],
        compiler_params=pltpu.CompilerParams(
            dimension_semantics=("parallel", "arbitrary"), vmem_limit_bytes=VMEM_LIMIT),
        name="ffn_ln",
    )(x2d, wg, wu, wd, g, b)


def _proj_body(*refs):
    n_hc = D_MODEL // LANES
    h_refs = refs[:n_hc]
    (wcq_ref, wckv_ref, wkra_ref, wkrb_ref, gq_ref, gkv_ref,
     wqa_ref, wqb_ref, wka_ref, wv_ref, wdil_ref,
     cq_ref, sq_ref, ck_ref, sk_ref,
     q_ref, k_ref, v_ref, dil0_ref, dil1_ref, dil2_ref, hperm_ref) = refs[n_hc:]
    hb = jnp.concatenate([hc[...] for hc in h_refs], axis=1).astype(BF16)
    cqn = _rms_norm(_dot(hb, wcq_ref[...]), gq_ref[...]).astype(BF16)
    ckvn = _rms_norm(_dot(hb, wckv_ref[...]), gkv_ref[...]).astype(BF16)
    k_rope = _dot(hb, wkra_ref[...]) * ck_ref[...] + _dot(hb, wkrb_ref[...]) * sk_ref[...]
    ka = _dot(ckvn, wka_ref[...])
    for h in range(MLA_HEADS):
        sl = slice(h * LANES, (h + 1) * LANES)
        k_ref[0, h] = (ka[:, sl] + k_rope).astype(BF16)
    qa_t = _dot_nt(wqa_ref[...], cqn)
    qb_t = _dot_nt(wqb_ref[...], cqn)
    cq_t = cq_ref[...]
    sq_t = sq_ref[...]
    for h in range(MLA_HEADS):
        sl = slice(h * LANES, (h + 1) * LANES)
        q_ref[0, sl] = (qa_t[sl] * cq_t + qb_t[sl] * sq_t).astype(BF16)
    v_ref[0] = _dot_nt(wv_ref[...], ckvn).astype(BF16)

    dil_scale = DIL_HEAD_DIM ** -0.5
    tm = h_refs[0].shape[0]
    group_cols = 3 * DIL_OUT
    for g, (dil_ref, (_, dilation)) in enumerate(zip((dil0_ref, dil1_ref, dil2_ref), DIL_PAIRS)):
        rows_per = tm // dilation
        if dilation == 1:
            hp = hb
        else:
            for r in range(dilation):
                for c, hc in enumerate(h_refs):
                    hperm_ref[r * rows_per:(r + 1) * rows_per, c * LANES:(c + 1) * LANES] = hc[
                        pl.ds(r, rows_per, stride=dilation), :].astype(BF16)
            hp = hperm_ref[...]
        c0 = g * group_cols
        qg = (_dot(hp, wdil_ref[:, c0:c0 + DIL_OUT]) * dil_scale).astype(BF16)
        kvg = _dot(hp, wdil_ref[:, c0 + DIL_OUT:c0 + group_cols]).astype(BF16)
        for r in range(dilation):
            rs = slice(r * rows_per, (r + 1) * rows_per)
            dil_ref[0, r, :, :DIL_OUT] = qg[rs]
            dil_ref[0, r, :, DIL_OUT:] = kvg[rs]


def _proj(h2d, weights, tables, *, batch, seq, tm):
    t, d = h2d.shape
    n_s = seq // tm

    def full(a):
        return pl.BlockSpec(a.shape, lambda i: (0,) * a.ndim)

    tab_spec = pl.BlockSpec((tm, LANES), lambda i: (i % n_s, 0))
    tab_t_spec = pl.BlockSpec((LANES, tm), lambda i: (0, i % n_s))
    head_map = lambda i: (i // n_s, 0, i % n_s, 0)
    feat_map = lambda i: (i // n_s, 0, i % n_s)
    dil_specs = [pl.BlockSpec((1, dl, tm // dl, 3 * DIL_OUT), head_map) for _, dl in DIL_PAIRS]
    dil_shapes = [jax.ShapeDtypeStruct((batch, dl, seq // dl, 3 * DIL_OUT), BF16) for _, dl in DIL_PAIRS]
    return pl.pallas_call(
        _proj_body,
        grid=(t // tm,),
        in_specs=[pl.BlockSpec((tm, LANES), functools.partial(lambda c, i: (i, c), c))
                  for c in range(d // LANES)]
        + [full(w) for w in weights] + [tab_t_spec] * 2 + [tab_spec] * 2,
        out_specs=[
            pl.BlockSpec((1, MLA_HEADS * LANES, tm), feat_map),
            pl.BlockSpec((1, MLA_HEADS, tm, LANES), head_map),
            pl.BlockSpec((1, MLA_OUT, tm), feat_map),
        ] + dil_specs,
        out_shape=[
            jax.ShapeDtypeStruct((batch, MLA_HEADS * LANES, seq), BF16),
            jax.ShapeDtypeStruct((batch, MLA_HEADS, seq, LANES), BF16),
            jax.ShapeDtypeStruct((batch, MLA_OUT, seq), BF16),
        ] + dil_shapes,
        scratch_shapes=[pltpu.VMEM((tm, d), BF16)],
        compiler_params=pltpu.CompilerParams(
            dimension_semantics=("parallel",), vmem_limit_bytes=VMEM_LIMIT),
        name="proj",
    )(*([h2d] * (d // LANES)), *weights, *tables)


def _mla_attn_body(q_ref, k_ref, vt_ref, o_ref, st_ref, *, tk, n_kchunks):
    tq = q_ref.shape[2]
    heads = (0, 1)
    n_pieces = tk // MLA_PIECE
    q_t = [q_ref[0, h * LANES:(h + 1) * LANES, :] for h in heads]

    def key_offset(chunk, p):
        base = chunk * tk
        if not isinstance(chunk, int):
            base = pl.multiple_of(base, tk)
        return base + p * MLA_PIECE

    def scores_piece(chunk, buf, p, maxes):
        off = key_offset(chunk, p)
        rows = slice(p * MLA_PIECE, (p + 1) * MLA_PIECE)
        out = []
        for h in heads:
            st = _dot(k_ref[0, h, pl.ds(off, MLA_PIECE), :], q_t[h])
            st_ref[buf, h, rows] = st
            out.append(jnp.maximum(maxes[h], jnp.max(st, axis=0, keepdims=True)))
        return tuple(out)

    ones_rows = jnp.ones((MLA_DEN_ROWS, MLA_PIECE), BF16)

    def values_piece(chunk, buf, p, m_new, accs):
        off = key_offset(chunk, p)
        rows = slice(p * MLA_PIECE, (p + 1) * MLA_PIECE)
        out = []
        for h in heads:
            pt = jnp.exp2(st_ref[buf, h, rows] - m_new[h]).astype(BF16)
            vtc = jnp.concatenate(
                [vt_ref[0, h * MLA_V:(h + 1) * MLA_V, pl.ds(off, MLA_PIECE)], ones_rows], axis=0)
            out.append(accs[h] + _dot(vtc, pt))
        return tuple(out)

    neg_inf = jnp.full((1, tq), -jnp.inf, F32)

    def half(next_chunk, next_buf, chunk, buf, chunk_max, state):
        m_new = tuple(jnp.maximum(state[h][0], chunk_max[h]) for h in heads)
        accs = tuple(jnp.exp2(state[h][0] - m_new[h]) * state[h][1] for h in heads)
        next_max = (neg_inf, neg_inf)
        for p in range(n_pieces):
            if next_chunk is not None:
                next_max = scores_piece(next_chunk, next_buf, p, next_max)
            accs = values_piece(chunk, buf, p, m_new, accs)
        return next_max, tuple((m_new[h], accs[h]) for h in heads)

    def pair_step(jj, loop_carry):
        max_a, state = loop_carry
        c = 2 * jj
        max_b, state = half(c + 1, 1, c, 0, max_a, state)
        return half(c + 2, 0, c + 1, 1, max_b, state)

    max_0 = (neg_inf, neg_inf)
    for p in range(n_pieces):
        max_0 = scores_piece(0, 0, p, max_0)
    init = (neg_inf, jnp.zeros((MLA_V + MLA_DEN_ROWS, tq), F32))
    max_a, state = lax.fori_loop(0, n_kchunks // 2 - 1, pair_step, (max_0, (init, init)))
    max_b, state = half(n_kchunks - 1, 1, n_kchunks - 2, 0, max_a, state)
    _, ((_, a0), (_, a1)) = half(None, None, n_kchunks - 1, 1, max_b, state)
    out_t = jnp.concatenate([a[:MLA_V] / a[MLA_V:MLA_V + 1] for a in (a0, a1)], axis=0)
    o_ref[0] = out_t.T.astype(o_ref.dtype)


def _mla_attn(q, k, v, *, tq, tk):
    batch, heads, seq, _ = k.shape
    assert (seq // tk) % 2 == 0
    return pl.pallas_call(
        functools.partial(_mla_attn_body, tk=tk, n_kchunks=seq // tk),
        grid=(batch, heads // 2, seq // tq),
        in_specs=[
            pl.BlockSpec((1, 2 * LANES, tq), lambda b, hp, i: (b, hp, i)),
            pl.BlockSpec((1, 2, seq, LANES), lambda b, hp, i: (b, hp, 0, 0)),
            pl.BlockSpec((1, 2 * MLA_V, seq), lambda b, hp, i: (b, hp, 0)),
        ],
        out_specs=pl.BlockSpec((1, tq, LANES), lambda b, hp, i: (b, i, hp)),
        out_shape=jax.ShapeDtypeStruct((batch, seq, MLA_OUT), BF16),
        scratch_shapes=[pltpu.VMEM((2, 2, tk, tq), F32)],
        compiler_params=pltpu.CompilerParams(
            dimension_semantics=("parallel", "parallel", "parallel"), vmem_limit_bytes=VMEM_LIMIT),
        name="mla_attn",
    )(q, k, v)


def _dil_attn_body(slopes_ref, q_ref, kl_ref, km_ref, kr_ref, vl_ref, vm_ref, vr_ref,
                   o_ref, lse_ref, kbuf, vbuf, *, group, dilation, tq, sub_len):
    i = pl.program_id(1)
    n_res = q_ref.shape[1]
    first_res = pl.program_id(2) * n_res
    kbuf[:, 0:DIL_HALO] = kl_ref[0]
    kbuf[:, DIL_HALO:DIL_HALO + tq] = km_ref[0]
    kbuf[:, DIL_HALO + tq:] = kr_ref[0]
    vbuf[:, 0:DIL_HALO] = vl_ref[0]
    vbuf[:, DIL_HALO:DIL_HALO + tq] = vm_ref[0]
    vbuf[:, DIL_HALO + tq:] = vr_ref[0]

    win = DIL_SUB + 2 * DIL_HALO
    row = lax.broadcasted_iota(jnp.int32, (DIL_SUB, win), 0)
    col = lax.broadcasted_iota(jnp.int32, (DIL_SUB, win), 1)
    steps = jnp.abs(col - DIL_HALO - row)
    in_band = steps <= DIL_HALF_SPAN
    dist = (steps * dilation).astype(F32)
    for sub in range(tq // DIL_SUB):
        key_pos = i * tq + (sub * DIL_SUB - DIL_HALO) + col
        valid = in_band & (key_pos >= 0) & (key_pos < sub_len)
        rows = slice(sub * DIL_SUB, (sub + 1) * DIL_SUB)
        wrows = slice(sub * DIL_SUB, sub * DIL_SUB + win)
        for rr in range(n_res):
            if dilation == 1:
                out_rows = pl.ds(sub * DIL_SUB, DIL_SUB)
            else:
                out_rows = pl.ds(sub * DIL_SUB * dilation + first_res + rr, DIL_SUB, stride=dilation)
            for h in range(DIL_HEADS_PER_GROUP):
                cols = slice(h * DIL_HEAD_DIM, (h + 1) * DIL_HEAD_DIM)
                slope = slopes_ref[group * DIL_HEADS_PER_GROUP + h]
                s = _dot_nt(q_ref[0, rr, rows, cols], kbuf[rr, wrows, cols]) - slope * dist
                s = jnp.where(valid, s, -jnp.inf)
                m = jnp.max(s, axis=1, keepdims=True)
                e = jnp.exp(s - m)
                den = jnp.sum(e, axis=1, keepdims=True)
                o_ref[0, h, out_rows, :] = _dot(e.astype(BF16), vbuf[rr, wrows, cols]) / den
                lse_ref[0, h, out_rows, :] = jnp.broadcast_to(m + jnp.log(den), (DIL_SUB, DIL_HEAD_DIM))


def _dil_attn(dil_g, slopes, *, group, dilation):
    batch, _, sub_len, _ = dil_g.shape
    seq = sub_len * dilation
    tq = min(512, sub_len, DIL_CHUNK_TOKENS // dilation)
    n_q = sub_len // tq
    n_res = min(dilation, DIL_ROWS_PER_STEP // tq)
    halo_per_tile = tq // DIL_HALO
    n_halo = sub_len // DIL_HALO
    q_col, k_col, v_col = 0, 1, 2

    def main(c):
        return pl.BlockSpec((1, n_res, tq, DIL_OUT), lambda b, i, r: (b, r, i, c))

    def left(c):
        return pl.BlockSpec(
            (1, n_res, DIL_HALO, DIL_OUT),
            lambda b, i, r: (b, r, jnp.maximum(i * halo_per_tile - 1, 0), c))

    def right(c):
        return pl.BlockSpec(
            (1, n_res, DIL_HALO, DIL_OUT),
            lambda b, i, r: (b, r, jnp.minimum((i + 1) * halo_per_tile, n_halo - 1), c))

    out_spec = pl.BlockSpec((1, DIL_HEADS_PER_GROUP, tq * dilation, DIL_HEAD_DIM),
                            lambda b, i, r: (b, 0, i, 0))
    out_shape = jax.ShapeDtypeStruct((batch, DIL_HEADS_PER_GROUP, seq, DIL_HEAD_DIM), F32)
    o, lse = pl.pallas_call(
        functools.partial(_dil_attn_body, group=group, dilation=dilation, tq=tq, sub_len=sub_len),
        grid=(batch, n_q, dilation // n_res),
        in_specs=[pl.BlockSpec(memory_space=pltpu.SMEM),
                  main(q_col), left(k_col), main(k_col), right(k_col),
                  left(v_col), main(v_col), right(v_col)],
        out_specs=[out_spec, out_spec],
        out_shape=[out_shape, out_shape],
        scratch_shapes=[pltpu.VMEM((n_res, tq + 2 * DIL_HALO, DIL_OUT), BF16),
                        pltpu.VMEM((n_res, tq + 2 * DIL_HALO, DIL_OUT), BF16)],
        compiler_params=pltpu.CompilerParams(
            dimension_semantics=("parallel", "parallel", "arbitrary"), vmem_limit_bytes=VMEM_LIMIT),
        name=f"dil_attn_g{group}",
    )(slopes, dil_g, dil_g, dil_g, dil_g, dil_g, dil_g, dil_g)
    return o, lse


def _mem_kv_body(mem_ref, w_ref, o_ref):
    o_ref[0] = _dot(mem_ref[0].astype(BF16), w_ref[...]).astype(BF16)


def _mem_kv(mem, w):
    batch, m_len, d = mem.shape
    n = w.shape[1]
    return pl.pallas_call(
        _mem_kv_body,
        grid=(batch,),
        in_specs=[pl.BlockSpec((1, m_len, d), lambda b: (b, 0, 0)),
                  pl.BlockSpec((d, n), lambda b: (0, 0))],
        out_specs=pl.BlockSpec((1, m_len, n), lambda b: (b, 0, 0)),
        out_shape=jax.ShapeDtypeStruct((batch, m_len, n), BF16),
        compiler_params=pltpu.CompilerParams(dimension_semantics=("parallel",)),
        name="mem_kv",
    )(mem, w)


def _merge_body(h_ref, omla_ref, od0_ref, od1_ref, od2_ref, ls0_ref, ls1_ref, ls2_ref, kvm_ref,
                wgate_ref, wmq_ref, wba_ref, wbb_ref, wbc_ref, wo_ref, g_ref, b_ref, out_ref):
    h = h_ref[...]
    hb = h.astype(BF16)

    def heads_to_lanes(ref):
        return jnp.concatenate([ref[0, hh] for hh in range(DIL_HEADS_PER_GROUP)], axis=1)

    ls0, ls1, ls2 = heads_to_lanes(ls0_ref), heads_to_lanes(ls1_ref), heads_to_lanes(ls2_ref)
    mx = jnp.maximum(ls0, jnp.maximum(ls1, ls2))
    e0, e1, e2 = jnp.exp(ls0 - mx), jnp.exp(ls1 - mx), jnp.exp(ls2 - mx)
    o_dil = (e0 * heads_to_lanes(od0_ref) + e1 * heads_to_lanes(od1_ref)
             + e2 * heads_to_lanes(od2_ref)) / (e0 + e1 + e2)

    mq = (_dot(hb, wmq_ref[...]) * (MEM_HEAD_DIM ** -0.5)).astype(BF16)
    heads = []
    for hh in range(MEM_HEADS):
        kc = slice(hh * MEM_HEAD_DIM, (hh + 1) * MEM_HEAD_DIM)
        vc = slice(MEM_OUT + hh * MEM_HEAD_DIM, MEM_OUT + (hh + 1) * MEM_HEAD_DIM)
        s = _dot_nt(mq[:, kc], kvm_ref[0, :, kc])
        p = jnp.exp(s - jnp.max(s, axis=1, keepdims=True))
        heads.append(_dot(p.astype(BF16), kvm_ref[0, :, vc]) / jnp.sum(p, axis=1, keepdims=True))
    o_mem = jnp.concatenate(heads, axis=1)

    y_a = _dot(omla_ref[...], wba_ref[...])
    y_b = _dot(o_dil.astype(BF16), wbb_ref[...])
    y_c = _dot(o_mem.astype(BF16), wbc_ref[...])
    d = h.shape[1]
    merged = (jax.nn.sigmoid(_dot(hb, wgate_ref[:, :d])) * y_a
              + jax.nn.sigmoid(_dot(hb, wgate_ref[:, d:2 * d])) * y_b
              + jax.nn.sigmoid(_dot(hb, wgate_ref[:, 2 * d:])) * y_c)
    mix = _dot(merged.astype(BF16), wo_ref[...])
    out_ref[...] = _layer_norm(ALPHA * h + mix, g_ref[...], b_ref[...])


def _merge(h2d, o_mla, o_dil, lse_dil, kv_mem, weights, g, b, *, seq, tm):
    t, d = h2d.shape
    n_s = seq // tm

    def rows(width):
        return pl.BlockSpec((tm, width), lambda i: (i, 0))

    def full(a):
        return pl.BlockSpec(a.shape, lambda i: (0,) * a.ndim)

    m_len, kv_cols = kv_mem.shape[1:]
    return pl.pallas_call(
        _merge_body,
        grid=(t // tm,),
        in_specs=[rows(d), rows(MLA_OUT)]
        + [pl.BlockSpec((1, DIL_HEADS_PER_GROUP, tm, DIL_HEAD_DIM),
                        lambda i: (i // n_s, 0, i % n_s, 0))] * 6
        + [pl.BlockSpec((1, m_len, kv_cols), lambda i: (i // n_s, 0, 0))]
        + [full(w) for w in weights] + [full(g), full(b)],
        out_specs=rows(d),
        out_shape=jax.ShapeDtypeStruct((t, d), F32),
        compiler_params=pltpu.CompilerParams(
            dimension_semantics=("parallel",), vmem_limit_bytes=VMEM_LIMIT),
        name="merge",
    )(h2d, o_mla, *o_dil, *lse_dil, kv_mem, *weights, g, b)


def _swap_halves(w):
    half = w.shape[-1] // 2
    return jnp.concatenate([w[..., half:], w[..., :half]], axis=-1)


def _prep_mla_weights(w_in, w_uq, w_ukv):
    d = w_in.shape[0]
    o_q, o_kv, o_kr = 0, MLA_Q_LORA, MLA_Q_LORA + MLA_KV_LORA
    w_cq = w_in[:, o_q:o_q + MLA_Q_LORA]
    w_ckv = w_in[:, o_kv:o_kv + MLA_KV_LORA]
    w_kr = w_in[:, o_kr:o_kr + MLA_ROPE]
    pad_tail = LANES - MLA_NOPE - MLA_ROPE
    z_nope = jnp.zeros((d, MLA_NOPE), F32)
    z_tail = jnp.zeros((d, pad_tail), F32)
    w_kra = jnp.concatenate([z_nope, w_kr, z_tail], axis=1)
    w_krb = jnp.concatenate([z_nope, _swap_halves(w_kr), z_tail], axis=1)

    uq = w_uq.reshape(MLA_Q_LORA, MLA_HEADS, MLA_NOPE + MLA_ROPE)
    uq_nope, uq_pe = uq[..., :MLA_NOPE], uq[..., MLA_NOPE:]
    zq_nope = jnp.zeros_like(uq_nope)
    zq_tail = jnp.zeros((MLA_Q_LORA, MLA_HEADS, pad_tail), F32)
    w_qa = jnp.concatenate([uq_nope, uq_pe, zq_tail], axis=-1).reshape(MLA_Q_LORA, MLA_HEADS * LANES).T
    w_qb = jnp.concatenate([zq_nope, _swap_halves(uq_pe), zq_tail], axis=-1).reshape(
        MLA_Q_LORA, MLA_HEADS * LANES).T

    ukv = w_ukv.reshape(MLA_KV_LORA, MLA_HEADS, MLA_NOPE + MLA_V)
    uk, uv = ukv[..., :MLA_NOPE], ukv[..., MLA_NOPE:]
    w_ka = jnp.concatenate(
        [uk, jnp.zeros((MLA_KV_LORA, MLA_HEADS, LANES - MLA_NOPE), F32)], axis=-1).reshape(
            MLA_KV_LORA, MLA_HEADS * LANES)
    w_v = uv.reshape(MLA_KV_LORA, MLA_HEADS * MLA_V).T
    return [w.astype(BF16) for w in (w_cq, w_ckv, w_kra, w_krb)], [w.astype(BF16) for w in (w_qa, w_qb, w_ka, w_v)]


def _prep_dil_weights(w_dil):
    d = w_dil.shape[0]
    w = w_dil.reshape(d, 3, DIL_GROUPS, DIL_OUT)
    return jnp.transpose(w, (0, 2, 1, 3)).reshape(d, DIL_QKV).astype(BF16)


def _rope_tables(seq):
    pos = jnp.arange(seq, dtype=F32)
    inv = 1.0 / (ROPE_THETA ** (jnp.arange(0, MLA_ROPE, 2, dtype=F32) / MLA_ROPE))
    ang = pos[:, None] * inv[None, :]
    cos, sin = jnp.cos(ang), jnp.sin(ang)
    ones = jnp.ones((seq, MLA_NOPE), F32)
    z_nope = jnp.zeros((seq, MLA_NOPE), F32)
    z_tail = jnp.zeros((seq, LANES - MLA_NOPE - MLA_ROPE), F32)
    q_scale = (MLA_NOPE + MLA_ROPE) ** -0.5 * math.log2(math.e)
    c_q = (jnp.concatenate([ones, cos, cos, z_tail], axis=1) * q_scale).T
    s_q = (jnp.concatenate([z_nope, -sin, sin, z_tail], axis=1) * q_scale).T
    c_k = jnp.concatenate([z_nope, cos, cos, z_tail], axis=1)
    s_k = jnp.concatenate([z_nope, -sin, sin, z_tail], axis=1)
    return c_q, s_q, c_k, s_k


def kernel(x, mem, w_in, mla_q_norm, mla_kv_norm, w_uq, w_ukv, w_mem_kv, w_br_mla, w_br_dil, w_br_mem, w_o,
           ffn1_w_gate, ffn1_w_up, ffn1_w_down, ffn2_w_gate, ffn2_w_up, ffn2_w_down,
           ln1_g, ln1_b, ln2_g, ln2_b, ln3_g, ln3_b):
    batch, seq, d = x.shape
    t = batch * seq
    d_ff = ffn1_w_gate.shape[-1]
    tm_ffn = min(512, t)
    tf = d_ff // 2
    tm_proj = min(512, seq)
    tm_merge = min(256, seq)
    tables = _rope_tables(seq)
    slopes = 2.0 ** (-8.0 * jnp.arange(1, DIL_HEADS + 1, dtype=F32) / DIL_HEADS)

    h = x.reshape(t, d)
    for l in range(DEPTH):
        bf = lambda w: w[l].astype(BF16)
        row = lambda v: v[l].reshape(1, -1)
        h = _ffn_ln(h, bf(ffn1_w_gate), bf(ffn1_w_up), bf(ffn1_w_down), row(ln1_g), row(ln1_b),
                    tm=tm_ffn, tf=tf)

        w_in_l = w_in[l]
        o_dil_cols = MLA_Q_LORA + MLA_KV_LORA + MLA_ROPE
        o_memq = o_dil_cols + DIL_QKV
        o_gate = o_memq + MEM_OUT
        w_c, w_u = _prep_mla_weights(w_in_l, w_uq[l], w_ukv[l])
        proj_weights = w_c + [row(mla_q_norm), row(mla_kv_norm)] + w_u + [
            _prep_dil_weights(w_in_l[:, o_dil_cols:o_memq])]
        q, k, v, *dil = _proj(h, proj_weights, tables, batch=batch, seq=seq, tm=tm_proj)

        o_mla = _mla_attn(q, k, v, tq=min(512, seq), tk=min(1024, seq)).reshape(t, MLA_OUT)
        dil_parts = [_dil_attn(dil[g], slopes, group=g, dilation=dl)
                     for g, (_, dl) in enumerate(DIL_PAIRS)]
        kv_mem = _mem_kv(mem, bf(w_mem_kv))

        merge_weights = [w_in_l[:, o_gate:].astype(BF16), w_in_l[:, o_memq:o_gate].astype(BF16),
                         bf(w_br_mla), bf(w_br_dil), bf(w_br_mem), bf(w_o)]
        h = _merge(h, o_mla, [p[0] for p in dil_parts], [p[1] for p in dil_parts], kv_mem,
                   merge_weights, row(ln2_g), row(ln2_b), seq=seq, tm=tm_merge)

        h = _ffn_ln(h, bf(ffn2_w_gate), bf(ffn2_w_up), bf(ffn2_w_down), row(ln3_g), row(ln3_b),
                    tm=tm_ffn, tf=tf)
    return h.reshape(batch, seq, d)
```

```python
import functools
import math

import jax
import jax.numpy as jnp
from jax import lax
from jax.experimental import pallas as pl
from jax.experimental.pallas import tpu as pltpu

F32 = jnp.float32
BF16 = jnp.bfloat16

D_MODEL = 1024
DEPTH = 1
MLA_HEADS = 8
MLA_Q_LORA = 256
MLA_KV_LORA = 256
MLA_NOPE = 64
MLA_ROPE = 32
MLA_V = 64
ROPE_THETA = 10000.0
DIL_PAIRS = ((128, 1), (512, 4), (2048, 16))
DIL_GROUPS = 3
DIL_HEADS_PER_GROUP = 4
DIL_HEAD_DIM = 128
DIL_HEADS = DIL_GROUPS * DIL_HEADS_PER_GROUP
MEM_HEADS = 4
MEM_HEAD_DIM = 128
EPS = 1e-5
ALPHA = (2 * DEPTH) ** 0.25

LANES = 128
DIL_OUT = DIL_HEADS_PER_GROUP * DIL_HEAD_DIM
DIL_QKV = 3 * DIL_HEADS * DIL_HEAD_DIM
MEM_OUT = MEM_HEADS * MEM_HEAD_DIM
MLA_OUT = MLA_HEADS * MLA_V
MLA_DEN_ROWS = 16
MLA_PIECE = 256
DIL_HALF_SPAN = 64
DIL_HALO = 64
DIL_SUB = 128
DIL_CHUNK_TOKENS = 2048
DIL_ROWS_PER_STEP = 512
VMEM_LIMIT = 56 * 1024 * 1024

assert all(w // 2 // d == DIL_HALF_SPAN for w, d in DIL_PAIRS)

_NT = (((1,), (1,)), ((), ()))


def _dot(a, b):
    return jnp.dot(a, b, preferred_element_type=F32)


def _dot_nt(a, b):
    return lax.dot_general(a, b, _NT, preferred_element_type=F32)


def _layer_norm(y, g, b):
    mu = jnp.mean(y, axis=-1, keepdims=True)
    yc = y - mu
    var = jnp.mean(yc * yc, axis=-1, keepdims=True)
    return yc * lax.rsqrt(var + EPS) * g + b


def _rms_norm(y, g):
    return y * lax.rsqrt(jnp.mean(y * y, axis=-1, keepdims=True) + EPS) * g


def _ffn_ln_body(x_ref, wg_ref, wu_ref, wd_ref, g_ref, b_ref, o_ref, acc_ref, *, n_ff_steps):
    j = pl.program_id(1)

    @pl.when(j == 0)
    def _():
        acc_ref[...] = jnp.zeros_like(acc_ref)

    xb = x_ref[...].astype(BF16)
    gate = _dot(xb, wg_ref[...])
    up = _dot(xb, wu_ref[...])
    act = (gate * jax.nn.sigmoid(gate) * up).astype(BF16)
    acc_ref[...] += _dot(act, wd_ref[...])

    @pl.when(j == n_ff_steps - 1)
    def _():
        y = ALPHA * x_ref[...] + 0.5 * acc_ref[...]
        o_ref[...] = _layer_norm(y, g_ref[...], b_ref[...])


def _ffn_ln(x2d, wg, wu, wd, g, b, *, tm, tf):
    t, d = x2d.shape
    f = wg.shape[1]
    n_ff_steps = f // tf
    return pl.pallas_call(
        functools.partial(_ffn_ln_body, n_ff_steps=n_ff_steps),
        grid=(t // tm, n_ff_steps),
        in_specs=[
            pl.BlockSpec((tm, d), lambda i, j: (i, 0)),
            pl.BlockSpec((d, tf), lambda i, j: (0, j)),
            pl.BlockSpec((d, tf), lambda i, j: (0, j)),
            pl.BlockSpec((tf, d), lambda i, j: (j, 0)),
            pl.BlockSpec((1, d), lambda i, j: (0, 0)),
            pl.BlockSpec((1, d), lambda i, j: (0, 0)),
        ],
        out_specs=pl.BlockSpec((tm, d), lambda i, j: (i, 0)),
        out_shape=jax.ShapeDtypeStruct((t, d), F32),
        scratch_shapes=[pltpu.VMEM((tm, d), F32)],
        compiler_params=pltpu.CompilerParams(
            dimension_semantics=("parallel", "arbitrary"), vmem_limit_bytes=VMEM_LIMIT),
        name="ffn_ln",
    )(x2d, wg, wu, wd, g, b)


def _proj_body(*refs):
    n_hc = D_MODEL // LANES
    h_refs = refs[:n_hc]
    (wcq_ref, wckv_ref, wkra_ref, wkrb_ref, gq_ref, gkv_ref,
     wqa_ref, wqb_ref, wka_ref, wv_ref, wdil_ref,
     cq_ref, sq_ref, ck_ref, sk_ref,
     q_ref, k_ref, v_ref, dil0_ref, dil1_ref, dil2_ref, hperm_ref) = refs[n_hc:]
    hb = jnp.concatenate([hc[...] for hc in h_refs], axis=1).astype(BF16)
    cqn = _rms_norm(_dot(hb, wcq_ref[...]), gq_ref[...]).astype(BF16)
    ckvn = _rms_norm(_dot(hb, wckv_ref[...]), gkv_ref[...]).astype(BF16)
    k_rope = _dot(hb, wkra_ref[...]) * ck_ref[...] + _dot(hb, wkrb_ref[...]) * sk_ref[...]
    ka = _dot(ckvn, wka_ref[...])
    for h in range(MLA_HEADS):
        sl = slice(h * LANES, (h + 1) * LANES)
        k_ref[0, h] = (ka[:, sl] + k_rope).astype(BF16)
    qa_t = _dot_nt(wqa_ref[...], cqn)
    qb_t = _dot_nt(wqb_ref[...], cqn)
    cq_t = cq_ref[...]
    sq_t = sq_ref[...]
    for h in range(MLA_HEADS):
        sl = slice(h * LANES, (h + 1) * LANES)
        q_ref[0, sl] = (qa_t[sl] * cq_t + qb_t[sl] * sq_t).astype(BF16)
    v_ref[0] = _dot_nt(wv_ref[...], ckvn).astype(BF16)

    dil_scale = DIL_HEAD_DIM ** -0.5
    tm = h_refs[0].shape[0]
    group_cols = 3 * DIL_OUT
    for g, (dil_ref, (_, dilation)) in enumerate(zip((dil0_ref, dil1_ref, dil2_ref), DIL_PAIRS)):
        rows_per = tm // dilation
        if dilation == 1:
            hp = hb
        else:
            for r in range(dilation):
                for c, hc in enumerate(h_refs):
                    hperm_ref[r * rows_per:(r + 1) * rows_per, c * LANES:(c + 1) * LANES] = hc[
                        pl.ds(r, rows_per, stride=dilation), :].astype(BF16)
            hp = hperm_ref[...]
        c0 = g * group_cols
        qg = (_dot(hp, wdil_ref[:, c0:c0 + DIL_OUT]) * dil_scale).astype(BF16)
        kvg = _dot(hp, wdil_ref[:, c0 + DIL_OUT:c0 + group_cols]).astype(BF16)
        for r in range(dilation):
            rs = slice(r * rows_per, (r + 1) * rows_per)
            dil_ref[0, r, :, :DIL_OUT] = qg[rs]
            dil_ref[0, r, :, DIL_OUT:] = kvg[rs]


def _proj(h2d, weights, tables, *, batch, seq, tm):
    t, d = h2d.shape
    n_s = seq // tm

    def full(a):
        return pl.BlockSpec(a.shape, lambda i: (0,) * a.ndim)

    tab_spec = pl.BlockSpec((tm, LANES), lambda i: (i % n_s, 0))
    tab_t_spec = pl.BlockSpec((LANES, tm), lambda i: (0, i % n_s))
    head_map = lambda i: (i // n_s, 0, i % n_s, 0)
    feat_map = lambda i: (i // n_s, 0, i % n_s)
    dil_specs = [pl.BlockSpec((1, dl, tm // dl, 3 * DIL_OUT), head_map) for _, dl in DIL_PAIRS]
    dil_shapes = [jax.ShapeDtypeStruct((batch, dl, seq // dl, 3 * DIL_OUT), BF16) for _, dl in DIL_PAIRS]
    return pl.pallas_call(
        _proj_body,
        grid=(t // tm,),
        in_specs=[pl.BlockSpec((tm, LANES), functools.partial(lambda c, i: (i, c), c))
                  for c in range(d // LANES)]
        + [full(w) for w in weights] + [tab_t_spec] * 2 + [tab_spec] * 2,
        out_specs=[
            pl.BlockSpec((1, MLA_HEADS * LANES, tm), feat_map),
            pl.BlockSpec((1, MLA_HEADS, tm, LANES), head_map),
            pl.BlockSpec((1, MLA_OUT, tm), feat_map),
        ] + dil_specs,
        out_shape=[
            jax.ShapeDtypeStruct((batch, MLA_HEADS * LANES, seq), BF16),
            jax.ShapeDtypeStruct((batch, MLA_HEADS, seq, LANES), BF16),
            jax.ShapeDtypeStruct((batch, MLA_OUT, seq), BF16),
        ] + dil_shapes,
        scratch_shapes=[pltpu.VMEM((tm, d), BF16)],
        compiler_params=pltpu.CompilerParams(
            dimension_semantics=("parallel",), vmem_limit_bytes=VMEM_LIMIT),
        name="proj",
    )(*([h2d] * (d // LANES)), *weights, *tables)


def _mla_attn_body(q_ref, k_ref, vt_ref, o_ref, st_ref, m_ref, acc_ref, *, tq, tk):
    seq = k_ref.shape[2]
    n_tiles = seq // tq
    n_units = (seq // tk) * n_tiles
    heads = (0, 1)
    n_pieces = tk // MLA_PIECE

    def offsets(unit):
        if isinstance(unit, int):
            return (unit // n_tiles) * tk, (unit % n_tiles) * tq
        return (pl.multiple_of(lax.div(unit, n_tiles) * tk, tk),
                pl.multiple_of(lax.rem(unit, n_tiles) * tq, tq))

    def scores_piece(unit, buf, p, maxes):
        k_off, q_off = offsets(unit)
        rows = slice(p * MLA_PIECE, (p + 1) * MLA_PIECE)
        out = []
        for h in heads:
            st = _dot(k_ref[0, h, pl.ds(k_off + p * MLA_PIECE, MLA_PIECE), :],
                      q_ref[0, h * LANES:(h + 1) * LANES, pl.ds(q_off, tq)])
            st_ref[buf, h, rows] = st
            out.append(jnp.maximum(maxes[h], jnp.max(st, axis=0, keepdims=True)))
        return tuple(out)

    ones_rows = jnp.ones((MLA_DEN_ROWS, MLA_PIECE), BF16)

    def values_piece(k_off, buf, p, m_new, accs):
        rows = slice(p * MLA_PIECE, (p + 1) * MLA_PIECE)
        out = []
        for h in heads:
            pt = jnp.exp2(st_ref[buf, h, rows] - m_new[h]).astype(BF16)
            vtc = jnp.concatenate(
                [vt_ref[0, h * MLA_V:(h + 1) * MLA_V, pl.ds(k_off + p * MLA_PIECE, MLA_PIECE)], ones_rows],
                axis=0)
            out.append(accs[h] + _dot(vtc, pt))
        return tuple(out)

    neg_inf = jnp.full((1, tq), -jnp.inf, F32)

    def half(next_unit, next_buf, unit, buf, unit_max):
        k_off, q_off = offsets(unit)
        cols = pl.ds(q_off, tq)
        m_old = tuple(m_ref[h, :, cols] for h in heads)
        m_new = tuple(jnp.maximum(m_old[h], unit_max[h]) for h in heads)
        accs = tuple(jnp.exp2(m_old[h] - m_new[h]) * acc_ref[h, :, cols] for h in heads)
        next_max = (neg_inf, neg_inf)
        for p in range(n_pieces):
            if next_unit is not None:
                next_max = scores_piece(next_unit, next_buf, p, next_max)
            accs = values_piece(k_off, buf, p, m_new, accs)
        for h in heads:
            m_ref[h, :, cols] = m_new[h]
            acc_ref[h, :, cols] = accs[h]
        return next_max

    def pair_step(jj, max_a):
        u = 2 * jj
        max_b = half(u + 1, 1, u, 0, max_a)
        return half(u + 2, 0, u + 1, 1, max_b)

    m_ref[...] = jnp.full(m_ref.shape, -jnp.inf, F32)
    acc_ref[...] = jnp.zeros(acc_ref.shape, F32)
    max_0 = (neg_inf, neg_inf)
    for p in range(n_pieces):
        max_0 = scores_piece(0, 0, p, max_0)
    max_a = lax.fori_loop(0, n_units // 2 - 1, pair_step, max_0)
    max_b = half(n_units - 1, 1, n_units - 2, 0, max_a)
    half(None, None, n_units - 1, 1, max_b)

    @pl.loop(0, n_tiles)
    def _(tile):
        cols = pl.ds(pl.multiple_of(tile * tq, tq), tq)
        out_t = jnp.concatenate(
            [acc_ref[h, :MLA_V, cols] / acc_ref[h, MLA_V:MLA_V + 1, cols] for h in heads], axis=0)
        o_ref[0, cols, :] = out_t.T.astype(o_ref.dtype)


def _mla_attn(q, k, v, *, tq, tk):
    batch, heads, seq, _ = k.shape
    assert ((seq // tk) * (seq // tq)) % 2 == 0
    return pl.pallas_call(
        functools.partial(_mla_attn_body, tq=tq, tk=tk),
        grid=(batch, heads // 2),
        in_specs=[
            pl.BlockSpec((1, 2 * LANES, seq), lambda b, hp: (b, hp, 0)),
            pl.BlockSpec((1, 2, seq, LANES), lambda b, hp: (b, hp, 0, 0)),
            pl.BlockSpec((1, 2 * MLA_V, seq), lambda b, hp: (b, hp, 0)),
        ],
        out_specs=pl.BlockSpec((1, seq, LANES), lambda b, hp: (b, 0, hp)),
        out_shape=jax.ShapeDtypeStruct((batch, seq, MLA_OUT), BF16),
        scratch_shapes=[pltpu.VMEM((2, 2, tk, tq), F32),
                        pltpu.VMEM((2, 1, seq), F32),
                        pltpu.VMEM((2, MLA_V + MLA_DEN_ROWS, seq), F32)],
        compiler_params=pltpu.CompilerParams(
            dimension_semantics=("parallel", "parallel"), vmem_limit_bytes=VMEM_LIMIT),
        name="mla_attn",
    )(q, k, v)


def _dil_attn_body(slopes_ref, q_ref, kl_ref, km_ref, kr_ref, vl_ref, vm_ref, vr_ref,
                   o_ref, lse_ref, kbuf, vbuf, *, group, dilation, tq, sub_len):
    i = pl.program_id(1)
    n_res = q_ref.shape[1]
    first_res = pl.program_id(2) * n_res
    kbuf[:, 0:DIL_HALO] = kl_ref[0]
    kbuf[:, DIL_HALO:DIL_HALO + tq] = km_ref[0]
    kbuf[:, DIL_HALO + tq:] = kr_ref[0]
    vbuf[:, 0:DIL_HALO] = vl_ref[0]
    vbuf[:, DIL_HALO:DIL_HALO + tq] = vm_ref[0]
    vbuf[:, DIL_HALO + tq:] = vr_ref[0]

    win = DIL_SUB + 2 * DIL_HALO
    row = lax.broadcasted_iota(jnp.int32, (DIL_SUB, win), 0)
    col = lax.broadcasted_iota(jnp.int32, (DIL_SUB, win), 1)
    steps = jnp.abs(col - DIL_HALO - row)
    in_band = steps <= DIL_HALF_SPAN
    dist = (steps * dilation).astype(F32)
    for sub in range(tq // DIL_SUB):
        key_pos = i * tq + (sub * DIL_SUB - DIL_HALO) + col
        valid = in_band & (key_pos >= 0) & (key_pos < sub_len)
        rows = slice(sub * DIL_SUB, (sub + 1) * DIL_SUB)
        wrows = slice(sub * DIL_SUB, sub * DIL_SUB + win)
        for rr in range(n_res):
            if dilation == 1:
                out_rows = pl.ds(sub * DIL_SUB, DIL_SUB)
            else:
                out_rows = pl.ds(sub * DIL_SUB * dilation + first_res + rr, DIL_SUB, stride=dilation)
            for h in range(DIL_HEADS_PER_GROUP):
                cols = slice(h * DIL_HEAD_DIM, (h + 1) * DIL_HEAD_DIM)
                slope = slopes_ref[group * DIL_HEADS_PER_GROUP + h]
                s = _dot_nt(q_ref[0, rr, rows, cols], kbuf[rr, wrows, cols]) - slope * dist
                s = jnp.where(valid, s, -jnp.inf)
                m = jnp.max(s, axis=1, keepdims=True)
                e = jnp.exp(s - m)
                den = jnp.sum(e, axis=1, keepdims=True)
                o_ref[0, h, out_rows, :] = _dot(e.astype(BF16), vbuf[rr, wrows, cols]) / den
                lse_ref[0, h, out_rows, :] = jnp.broadcast_to(m + jnp.log(den), (DIL_SUB, DIL_HEAD_DIM))


def _dil_attn(dil_g, slopes, *, group, dilation):
    batch, _, sub_len, _ = dil_g.shape
    seq = sub_len * dilation
    tq = min(512, sub_len, DIL_CHUNK_TOKENS // dilation)
    n_q = sub_len // tq
    n_res = min(dilation, DIL_ROWS_PER_STEP // tq)
    halo_per_tile = tq // DIL_HALO
    n_halo = sub_len // DIL_HALO
    q_col, k_col, v_col = 0, 1, 2

    def main(c):
        return pl.BlockSpec((1, n_res, tq, DIL_OUT), lambda b, i, r: (b, r, i, c))

    def left(c):
        return pl.BlockSpec(
            (1, n_res, DIL_HALO, DIL_OUT),
            lambda b, i, r: (b, r, jnp.maximum(i * halo_per_tile - 1, 0), c))

    def right(c):
        return pl.BlockSpec(
            (1, n_res, DIL_HALO, DIL_OUT),
            lambda b, i, r: (b, r, jnp.minimum((i + 1) * halo_per_tile, n_halo - 1), c))

    out_spec = pl.BlockSpec((1, DIL_HEADS_PER_GROUP, tq * dilation, DIL_HEAD_DIM),
                            lambda b, i, r: (b, 0, i, 0))
    out_shape = jax.ShapeDtypeStruct((batch, DIL_HEADS_PER_GROUP, seq, DIL_HEAD_DIM), F32)
    o, lse = pl.pallas_call(
        functools.partial(_dil_attn_body, group=group, dilation=dilation, tq=tq, sub_len=sub_len),
        grid=(batch, n_q, dilation // n_res),
        in_specs=[pl.BlockSpec(memory_space=pltpu.SMEM),
                  main(q_col), left(k_col), main(k_col), right(k_col),
                  left(v_col), main(v_col), right(v_col)],
        out_specs=[out_spec, out_spec],
        out_shape=[out_shape, out_shape],
        scratch_shapes=[pltpu.VMEM((n_res, tq + 2 * DIL_HALO, DIL_OUT), BF16),
                        pltpu.VMEM((n_res, tq + 2 * DIL_HALO, DIL_OUT), BF16)],
        compiler_params=pltpu.CompilerParams(
            dimension_semantics=("parallel", "parallel", "arbitrary"), vmem_limit_bytes=VMEM_LIMIT),
        name=f"dil_attn_g{group}",
    )(slopes, dil_g, dil_g, dil_g, dil_g, dil_g, dil_g, dil_g)
    return o, lse


def _mem_kv_body(mem_ref, w_ref, o_ref):
    o_ref[0] = _dot(mem_ref[0].astype(BF16), w_ref[...]).astype(BF16)


def _mem_kv(mem, w):
    batch, m_len, d = mem.shape
    n = w.shape[1]
    return pl.pallas_call(
        _mem_kv_body,
        grid=(batch,),
        in_specs=[pl.BlockSpec((1, m_len, d), lambda b: (b, 0, 0)),
                  pl.BlockSpec((d, n), lambda b: (0, 0))],
        out_specs=pl.BlockSpec((1, m_len, n), lambda b: (b, 0, 0)),
        out_shape=jax.ShapeDtypeStruct((batch, m_len, n), BF16),
        compiler_params=pltpu.CompilerParams(dimension_semantics=("parallel",)),
        name="mem_kv",
    )(mem, w)


def _merge_body(h_ref, omla_ref, od0_ref, od1_ref, od2_ref, ls0_ref, ls1_ref, ls2_ref, kvm_ref,
                wgate_ref, wmq_ref, wba_ref, wbb_ref, wbc_ref, wo_ref, g_ref, b_ref, out_ref):
    h = h_ref[...]
    hb = h.astype(BF16)

    def heads_to_lanes(ref):
        return jnp.concatenate([ref[0, hh] for hh in range(DIL_HEADS_PER_GROUP)], axis=1)

    ls0, ls1, ls2 = heads_to_lanes(ls0_ref), heads_to_lanes(ls1_ref), heads_to_lanes(ls2_ref)
    mx = jnp.maximum(ls0, jnp.maximum(ls1, ls2))
    e0, e1, e2 = jnp.exp(ls0 - mx), jnp.exp(ls1 - mx), jnp.exp(ls2 - mx)
    o_dil = (e0 * heads_to_lanes(od0_ref) + e1 * heads_to_lanes(od1_ref)
             + e2 * heads_to_lanes(od2_ref)) / (e0 + e1 + e2)

    mq = (_dot(hb, wmq_ref[...]) * (MEM_HEAD_DIM ** -0.5)).astype(BF16)
    heads = []
    for hh in range(MEM_HEADS):
        kc = slice(hh * MEM_HEAD_DIM, (hh + 1) * MEM_HEAD_DIM)
        vc = slice(MEM_OUT + hh * MEM_HEAD_DIM, MEM_OUT + (hh + 1) * MEM_HEAD_DIM)
        s = _dot_nt(mq[:, kc], kvm_ref[0, :, kc])
        p = jnp.exp(s - jnp.max(s, axis=1, keepdims=True))
        heads.append(_dot(p.astype(BF16), kvm_ref[0, :, vc]) / jnp.sum(p, axis=1, keepdims=True))
    o_mem = jnp.concatenate(heads, axis=1)

    y_a = _dot(omla_ref[...], wba_ref[...])
    y_b = _dot(o_dil.astype(BF16), wbb_ref[...])
    y_c = _dot(o_mem.astype(BF16), wbc_ref[...])
    d = h.shape[1]
    merged = (jax.nn.sigmoid(_dot(hb, wgate_ref[:, :d])) * y_a
              + jax.nn.sigmoid(_dot(hb, wgate_ref[:, d:2 * d])) * y_b
              + jax.nn.sigmoid(_dot(hb, wgate_ref[:, 2 * d:])) * y_c)
    mix = _dot(merged.astype(BF16), wo_ref[...])
    out_ref[...] = _layer_norm(ALPHA * h + mix, g_ref[...], b_ref[...])


def _merge(h2d, o_mla, o_dil, lse_dil, kv_mem, weights, g, b, *, seq, tm):
    t, d = h2d.shape
    n_s = seq // tm

    def rows(width):
        return pl.BlockSpec((tm, width), lambda i: (i, 0))

    def full(a):
        return pl.BlockSpec(a.shape, lambda i: (0,) * a.ndim)

    m_len, kv_cols = kv_mem.shape[1:]
    return pl.pallas_call(
        _merge_body,
        grid=(t // tm,),
        in_specs=[rows(d), rows(MLA_OUT)]
        + [pl.BlockSpec((1, DIL_HEADS_PER_GROUP, tm, DIL_HEAD_DIM),
                        lambda i: (i // n_s, 0, i % n_s, 0))] * 6
        + [pl.BlockSpec((1, m_len, kv_cols), lambda i: (i // n_s, 0, 0))]
        + [full(w) for w in weights] + [full(g), full(b)],
        out_specs=rows(d),
        out_shape=jax.ShapeDtypeStruct((t, d), F32),
        compiler_params=pltpu.CompilerParams(
            dimension_semantics=("parallel",), vmem_limit_bytes=VMEM_LIMIT),
        name="merge",
    )(h2d, o_mla, *o_dil, *lse_dil, kv_mem, *weights, g, b)


def _swap_halves(w):
    half = w.shape[-1] // 2
    return jnp.concatenate([w[..., half:], w[..., :half]], axis=-1)


def _prep_mla_weights(w_in, w_uq, w_ukv):
    d = w_in.shape[0]
    o_q, o_kv, o_kr = 0, MLA_Q_LORA, MLA_Q_LORA + MLA_KV_LORA
    w_cq = w_in[:, o_q:o_q + MLA_Q_LORA]
    w_ckv = w_in[:, o_kv:o_kv + MLA_KV_LORA]
    w_kr = w_in[:, o_kr:o_kr + MLA_ROPE]
    pad_tail = LANES - MLA_NOPE - MLA_ROPE
    z_nope = jnp.zeros((d, MLA_NOPE), F32)
    z_tail = jnp.zeros((d, pad_tail), F32)
    w_kra = jnp.concatenate([z_nope, w_kr, z_tail], axis=1)
    w_krb = jnp.concatenate([z_nope, _swap_halves(w_kr), z_tail], axis=1)

    uq = w_uq.reshape(MLA_Q_LORA, MLA_HEADS, MLA_NOPE + MLA_ROPE)
    uq_nope, uq_pe = uq[..., :MLA_NOPE], uq[..., MLA_NOPE:]
    zq_nope = jnp.zeros_like(uq_nope)
    zq_tail = jnp.zeros((MLA_Q_LORA, MLA_HEADS, pad_tail), F32)
    w_qa = jnp.concatenate([uq_nope, uq_pe, zq_tail], axis=-1).reshape(MLA_Q_LORA, MLA_HEADS * LANES).T
    w_qb = jnp.concatenate([zq_nope, _swap_halves(uq_pe), zq_tail], axis=-1).reshape(
        MLA_Q_LORA, MLA_HEADS * LANES).T

    ukv = w_ukv.reshape(MLA_KV_LORA, MLA_HEADS, MLA_NOPE + MLA_V)
    uk, uv = ukv[..., :MLA_NOPE], ukv[..., MLA_NOPE:]
    w_ka = jnp.concatenate(
        [uk, jnp.zeros((MLA_KV_LORA, MLA_HEADS, LANES - MLA_NOPE), F32)], axis=-1).reshape(
            MLA_KV_LORA, MLA_HEADS * LANES)
    w_v = uv.reshape(MLA_KV_LORA, MLA_HEADS * MLA_V).T
    return [w.astype(BF16) for w in (w_cq, w_ckv, w_kra, w_krb)], [w.astype(BF16) for w in (w_qa, w_qb, w_ka, w_v)]


def _prep_dil_weights(w_dil):
    d = w_dil.shape[0]
    w = w_dil.reshape(d, 3, DIL_GROUPS, DIL_OUT)
    return jnp.transpose(w, (0, 2, 1, 3)).reshape(d, DIL_QKV).astype(BF16)


def _rope_tables(seq):
    pos = jnp.arange(seq, dtype=F32)
    inv = 1.0 / (ROPE_THETA ** (jnp.arange(0, MLA_ROPE, 2, dtype=F32) / MLA_ROPE))
    ang = pos[:, None] * inv[None, :]
    cos, sin = jnp.cos(ang), jnp.sin(ang)
    ones = jnp.ones((seq, MLA_NOPE), F32)
    z_nope = jnp.zeros((seq, MLA_NOPE), F32)
    z_tail = jnp.zeros((seq, LANES - MLA_NOPE - MLA_ROPE), F32)
    q_scale = (MLA_NOPE + MLA_ROPE) ** -0.5 * math.log2(math.e)
    c_q = (jnp.concatenate([ones, cos, cos, z_tail], axis=1) * q_scale).T
    s_q = (jnp.concatenate([z_nope, -sin, sin, z_tail], axis=1) * q_scale).T
    c_k = jnp.concatenate([z_nope, cos, cos, z_tail], axis=1)
    s_k = jnp.concatenate([z_nope, -sin, sin, z_tail], axis=1)
    return c_q, s_q, c_k, s_k


def kernel(x, mem, w_in, mla_q_norm, mla_kv_norm, w_uq, w_ukv, w_mem_kv, w_br_mla, w_br_dil, w_br_mem, w_o,
           ffn1_w_gate, ffn1_w_up, ffn1_w_down, ffn2_w_gate, ffn2_w_up, ffn2_w_down,
           ln1_g, ln1_b, ln2_g, ln2_b, ln3_g, ln3_b):
    batch, seq, d = x.shape
    t = batch * seq
    d_ff = ffn1_w_gate.shape[-1]
    tm_ffn = min(512, t)
    tf = d_ff // 2
    tm_proj = min(512, seq)
    tm_merge = min(256, seq)
    tables = _rope_tables(seq)
    slopes = 2.0 ** (-8.0 * jnp.arange(1, DIL_HEADS + 1, dtype=F32) / DIL_HEADS)

    h = x.reshape(t, d)
    for l in range(DEPTH):
        bf = lambda w: w[l].astype(BF16)
        row = lambda v: v[l].reshape(1, -1)
        h = _ffn_ln(h, bf(ffn1_w_gate), bf(ffn1_w_up), bf(ffn1_w_down), row(ln1_g), row(ln1_b),
                    tm=tm_ffn, tf=tf)

        w_in_l = w_in[l]
        o_dil_cols = MLA_Q_LORA + MLA_KV_LORA + MLA_ROPE
        o_memq = o_dil_cols + DIL_QKV
        o_gate = o_memq + MEM_OUT
        w_c, w_u = _prep_mla_weights(w_in_l, w_uq[l], w_ukv[l])
        proj_weights = w_c + [row(mla_q_norm), row(mla_kv_norm)] + w_u + [
            _prep_dil_weights(w_in_l[:, o_dil_cols:o_memq])]
        q, k, v, *dil = _proj(h, proj_weights, tables, batch=batch, seq=seq, tm=tm_proj)

        o_mla = _mla_attn(q, k, v, tq=min(512, seq), tk=min(1024, seq)).reshape(t, MLA_OUT)
        dil_parts = [_dil_attn(dil[g], slopes, group=g, dilation=dl)
                     for g, (_, dl) in enumerate(DIL_PAIRS)]
        kv_mem = _mem_kv(mem, bf(w_mem_kv))

        merge_weights = [w_in_l[:, o_gate:].astype(BF16), w_in_l[:, o_memq:o_gate].astype(BF16),
                         bf(w_br_mla), bf(w_br_dil), bf(w_br_mem), bf(w_o)]
        h = _merge(h, o_mla, [p[0] for p in dil_parts], [p[1] for p in dil_parts], kv_mem,
                   merge_weights, row(ln2_g), row(ln2_b), seq=seq, tm=tm_merge)

        h = _ffn_ln(h, bf(ffn2_w_gate), bf(ffn2_w_up), bf(ffn2_w_down), row(ln3_g), row(ln3_b),
                    tm=tm_ffn, tf=tf)
    return h.reshape(batch, seq, d)
```

```python
import functools
import math

import jax
import jax.numpy as jnp
from jax import lax
from jax.experimental import pallas as pl
from jax.experimental.pallas import tpu as pltpu

F32 = jnp.float32
BF16 = jnp.bfloat16

D_MODEL = 1024
DEPTH = 1
MLA_HEADS = 8
MLA_Q_LORA = 256
MLA_KV_LORA = 256
MLA_NOPE = 64
MLA_ROPE = 32
MLA_V = 64
ROPE_THETA = 10000.0
DIL_PAIRS = ((128, 1), (512, 4), (2048, 16))
DIL_GROUPS = 3
DIL_HEADS_PER_GROUP = 4
DIL_HEAD_DIM = 128
DIL_HEADS = DIL_GROUPS * DIL_HEADS_PER_GROUP
MEM_HEADS = 4
MEM_HEAD_DIM = 128
EPS = 1e-5
ALPHA = (2 * DEPTH) ** 0.25

LANES = 128
DIL_OUT = DIL_HEADS_PER_GROUP * DIL_HEAD_DIM
DIL_QKV = 3 * DIL_HEADS * DIL_HEAD_DIM
MEM_OUT = MEM_HEADS * MEM_HEAD_DIM
MLA_OUT = MLA_HEADS * MLA_V
MLA_DEN_ROWS = 16
MLA_PIECE = 256
DIL_HALF_SPAN = 64
DIL_HALO = 64
DIL_SUB = 128
DIL_CHUNK_TOKENS = 2048
DIL_ROWS_PER_STEP = 512
VMEM_LIMIT = 56 * 1024 * 1024

assert all(w // 2 // d == DIL_HALF_SPAN for w, d in DIL_PAIRS)

_NT = (((1,), (1,)), ((), ()))


def _dot(a, b):
    return jnp.dot(a, b, preferred_element_type=F32)


def _dot_nt(a, b):
    return lax.dot_general(a, b, _NT, preferred_element_type=F32)


def _dot_tt(w, x):
    return lax.dot_general(w, x, (((0,), (1,)), ((), ())), preferred_element_type=F32)


def _layer_norm(y, g, b):
    mu = jnp.mean(y, axis=-1, keepdims=True)
    yc = y - mu
    var = jnp.mean(yc * yc, axis=-1, keepdims=True)
    return yc * lax.rsqrt(var + EPS) * g + b


def _rms_norm(y, g):
    return y * lax.rsqrt(jnp.mean(y * y, axis=-1, keepdims=True) + EPS) * g


def _ffn_ln_body(x_ref, wg_ref, wu_ref, wd_ref, g_ref, b_ref, o_ref, acc_ref, *, n_ff_steps):
    j = pl.program_id(1)

    @pl.when(j == 0)
    def _():
        acc_ref[...] = jnp.zeros_like(acc_ref)

    xb = x_ref[...].astype(BF16)
    gate = _dot(xb, wg_ref[...])
    up = _dot(xb, wu_ref[...])
    act = (gate * jax.nn.sigmoid(gate) * up).astype(BF16)
    acc_ref[...] += _dot(act, wd_ref[...])

    @pl.when(j == n_ff_steps - 1)
    def _():
        y = ALPHA * x_ref[...] + 0.5 * acc_ref[...]
        o_ref[...] = _layer_norm(y, g_ref[...], b_ref[...])


def _ffn_ln(x2d, wg, wu, wd, g, b, *, tm, tf):
    t, d = x2d.shape
    f = wg.shape[1]
    n_ff_steps = f // tf
    return pl.pallas_call(
        functools.partial(_ffn_ln_body, n_ff_steps=n_ff_steps),
        grid=(t // tm, n_ff_steps),
        in_specs=[
            pl.BlockSpec((tm, d), lambda i, j: (i, 0)),
            pl.BlockSpec((d, tf), lambda i, j: (0, j)),
            pl.BlockSpec((d, tf), lambda i, j: (0, j)),
            pl.BlockSpec((tf, d), lambda i, j: (j, 0)),
            pl.BlockSpec((1, d), lambda i, j: (0, 0)),
            pl.BlockSpec((1, d), lambda i, j: (0, 0)),
        ],
        out_specs=pl.BlockSpec((tm, d), lambda i, j: (i, 0)),
        out_shape=jax.ShapeDtypeStruct((t, d), F32),
        scratch_shapes=[pltpu.VMEM((tm, d), F32)],
        compiler_params=pltpu.CompilerParams(
            dimension_semantics=("parallel", "arbitrary"), vmem_limit_bytes=VMEM_LIMIT),
        name="ffn_ln",
    )(x2d, wg, wu, wd, g, b)


def _proj_body(*refs):
    n_hc = D_MODEL // LANES
    h_refs = refs[:n_hc]
    (wcq_ref, wckv_ref, wkra_ref, wkrb_ref, gq_ref, gkv_ref,
     wqa_ref, wqb_ref, wka_ref, wv_ref, wdil_ref,
     cq_ref, sq_ref, ck_ref, sk_ref,
     q_ref, k_ref, v_ref, dil0_ref, dil1_ref, dil2_ref, hperm_ref) = refs[n_hc:]
    hb = jnp.concatenate([hc[...] for hc in h_refs], axis=1).astype(BF16)
    cqn = _rms_norm(_dot(hb, wcq_ref[...]), gq_ref[...]).astype(BF16)
    ckvn = _rms_norm(_dot(hb, wckv_ref[...]), gkv_ref[...]).astype(BF16)
    k_rope = _dot(hb, wkra_ref[...]) * ck_ref[...] + _dot(hb, wkrb_ref[...]) * sk_ref[...]
    ka = _dot(ckvn, wka_ref[...])
    for h in range(MLA_HEADS):
        sl = slice(h * LANES, (h + 1) * LANES)
        k_ref[0, h] = (ka[:, sl] + k_rope).astype(BF16)
    qa_t = _dot_tt(wqa_ref[...], cqn)
    qb_t = _dot_tt(wqb_ref[...], cqn)
    cq_t = cq_ref[...]
    sq_t = sq_ref[...]
    for h in range(MLA_HEADS):
        sl = slice(h * LANES, (h + 1) * LANES)
        q_ref[0, sl] = (qa_t[sl] * cq_t + qb_t[sl] * sq_t).astype(BF16)
    v_ref[0] = _dot_tt(wv_ref[...], ckvn).astype(BF16)

    dil_scale = DIL_HEAD_DIM ** -0.5
    tm = h_refs[0].shape[0]
    for g, (dil_ref, (_, dilation)) in enumerate(zip((dil0_ref, dil1_ref, dil2_ref), DIL_PAIRS)):
        rows_per = tm // dilation
        if dilation == 1:
            hp = hb
        else:
            for r in range(dilation):
                for c, hc in enumerate(h_refs):
                    hperm_ref[r * rows_per:(r + 1) * rows_per, c * LANES:(c + 1) * LANES] = hc[
                        pl.ds(r, rows_per, stride=dilation), :].astype(BF16)
            hp = hperm_ref[...]
        for part in range(3):
            c0 = (part * DIL_GROUPS + g) * DIL_OUT
            res = _dot(hp, wdil_ref[:, c0:c0 + DIL_OUT])
            if part == 0:
                res = res * dil_scale
            res = res.astype(BF16)
            for r in range(dilation):
                dil_ref[0, r, :, part * DIL_OUT:(part + 1) * DIL_OUT] = res[r * rows_per:(r + 1) * rows_per]


def _proj(h2d, weights, tables, *, batch, seq, tm):
    t, d = h2d.shape
    n_s = seq // tm

    def full(a):
        return pl.BlockSpec(a.shape, lambda i: (0,) * a.ndim)

    tab_spec = pl.BlockSpec((tm, LANES), lambda i: (i % n_s, 0))
    tab_t_spec = pl.BlockSpec((LANES, tm), lambda i: (0, i % n_s))
    head_map = lambda i: (i // n_s, 0, i % n_s, 0)
    feat_map = lambda i: (i // n_s, 0, i % n_s)
    dil_specs = [pl.BlockSpec((1, dl, tm // dl, 3 * DIL_OUT), head_map) for _, dl in DIL_PAIRS]
    dil_shapes = [jax.ShapeDtypeStruct((batch, dl, seq // dl, 3 * DIL_OUT), BF16) for _, dl in DIL_PAIRS]
    return pl.pallas_call(
        _proj_body,
        grid=(t // tm,),
        in_specs=[pl.BlockSpec((tm, LANES), functools.partial(lambda c, i: (i, c), c))
                  for c in range(d // LANES)]
        + [full(w) for w in weights] + [tab_t_spec] * 2 + [tab_spec] * 2,
        out_specs=[
            pl.BlockSpec((1, MLA_HEADS * LANES, tm), feat_map),
            pl.BlockSpec((1, MLA_HEADS, tm, LANES), head_map),
            pl.BlockSpec((1, MLA_OUT, tm), feat_map),
        ] + dil_specs,
        out_shape=[
            jax.ShapeDtypeStruct((batch, MLA_HEADS * LANES, seq), BF16),
            jax.ShapeDtypeStruct((batch, MLA_HEADS, seq, LANES), BF16),
            jax.ShapeDtypeStruct((batch, MLA_OUT, seq), BF16),
        ] + dil_shapes,
        scratch_shapes=[pltpu.VMEM((tm, d), BF16)],
        compiler_params=pltpu.CompilerParams(
            dimension_semantics=("parallel",), vmem_limit_bytes=VMEM_LIMIT),
        name="proj",
    )(*([h2d] * (d // LANES)), *weights, *tables)


def _mla_attn_body(q_ref, k_ref, vt_ref, o_ref, st_ref, m_ref, acc_ref, *, tq, tk):
    seq = k_ref.shape[2]
    n_tiles = seq // tq
    n_units = (seq // tk) * n_tiles
    heads = (0, 1)
    n_pieces = tk // MLA_PIECE

    def offsets(unit):
        if isinstance(unit, int):
            return (unit // n_tiles) * tk, (unit % n_tiles) * tq
        return (pl.multiple_of(lax.div(unit, n_tiles) * tk, tk),
                pl.multiple_of(lax.rem(unit, n_tiles) * tq, tq))

    def scores_piece(unit, buf, p, maxes):
        k_off, q_off = offsets(unit)
        rows = slice(p * MLA_PIECE, (p + 1) * MLA_PIECE)
        out = []
        for h in heads:
            st = _dot(k_ref[0, h, pl.ds(k_off + p * MLA_PIECE, MLA_PIECE), :],
                      q_ref[0, h * LANES:(h + 1) * LANES, pl.ds(q_off, tq)])
            st_ref[buf, h, rows] = st
            out.append(jnp.maximum(maxes[h], jnp.max(st, axis=0, keepdims=True)))
        return tuple(out)

    ones_rows = jnp.ones((MLA_DEN_ROWS, MLA_PIECE), BF16)

    def values_piece(k_off, buf, p, m_new, accs):
        rows = slice(p * MLA_PIECE, (p + 1) * MLA_PIECE)
        out = []
        for h in heads:
            pt = jnp.exp2(st_ref[buf, h, rows] - m_new[h]).astype(BF16)
            vtc = jnp.concatenate(
                [vt_ref[0, h * MLA_V:(h + 1) * MLA_V, pl.ds(k_off + p * MLA_PIECE, MLA_PIECE)], ones_rows],
                axis=0)
            out.append(accs[h] + _dot(vtc, pt))
        return tuple(out)

    neg_inf = jnp.full((1, tq), -jnp.inf, F32)

    def half(next_unit, next_buf, unit, buf, unit_max):
        k_off, q_off = offsets(unit)
        cols = pl.ds(q_off, tq)
        m_old = tuple(m_ref[h, :, cols] for h in heads)
        m_new = tuple(jnp.maximum(m_old[h], unit_max[h]) for h in heads)
        accs = tuple(jnp.exp2(m_old[h] - m_new[h]) * acc_ref[h, :, cols] for h in heads)
        next_max = (neg_inf, neg_inf)
        for p in range(n_pieces):
            if next_unit is not None:
                next_max = scores_piece(next_unit, next_buf, p, next_max)
            accs = values_piece(k_off, buf, p, m_new, accs)
        for h in heads:
            m_ref[h, :, cols] = m_new[h]
            acc_ref[h, :, cols] = accs[h]
        return next_max

    def pair_step(jj, max_a):
        u = 2 * jj
        max_b = half(u + 1, 1, u, 0, max_a)
        return half(u + 2, 0, u + 1, 1, max_b)

    m_ref[...] = jnp.full(m_ref.shape, -jnp.inf, F32)
    acc_ref[...] = jnp.zeros(acc_ref.shape, F32)
    max_0 = (neg_inf, neg_inf)
    for p in range(n_pieces):
        max_0 = scores_piece(0, 0, p, max_0)
    max_a = lax.fori_loop(0, n_units // 2 - 1, pair_step, max_0)
    max_b = half(n_units - 1, 1, n_units - 2, 0, max_a)
    half(None, None, n_units - 1, 1, max_b)

    @pl.loop(0, n_tiles)
    def _(tile):
        cols = pl.ds(pl.multiple_of(tile * tq, tq), tq)
        out_t = jnp.concatenate(
            [acc_ref[h, :MLA_V, cols] / acc_ref[h, MLA_V:MLA_V + 1, cols] for h in heads], axis=0)
        o_ref[0, cols, :] = out_t.T.astype(o_ref.dtype)


def _mla_attn(q, k, v, *, tq, tk):
    batch, heads, seq, _ = k.shape
    assert ((seq // tk) * (seq // tq)) % 2 == 0
    return pl.pallas_call(
        functools.partial(_mla_attn_body, tq=tq, tk=tk),
        grid=(batch, heads // 2),
        in_specs=[
            pl.BlockSpec((1, 2 * LANES, seq), lambda b, hp: (b, hp, 0)),
            pl.BlockSpec((1, 2, seq, LANES), lambda b, hp: (b, hp, 0, 0)),
            pl.BlockSpec((1, 2 * MLA_V, seq), lambda b, hp: (b, hp, 0)),
        ],
        out_specs=pl.BlockSpec((1, seq, LANES), lambda b, hp: (b, 0, hp)),
        out_shape=jax.ShapeDtypeStruct((batch, seq, MLA_OUT), BF16),
        scratch_shapes=[pltpu.VMEM((2, 2, tk, tq), F32),
                        pltpu.VMEM((2, 1, seq), F32),
                        pltpu.VMEM((2, MLA_V + MLA_DEN_ROWS, seq), F32)],
        compiler_params=pltpu.CompilerParams(
            dimension_semantics=("parallel", "parallel"), vmem_limit_bytes=VMEM_LIMIT),
        name="mla_attn",
    )(q, k, v)


def _dil_attn_body(slopes_ref, q_ref, kl_ref, km_ref, kr_ref, vl_ref, vm_ref, vr_ref,
                   o_ref, lse_ref, kbuf, vbuf, *, group, dilation, tq, sub_len):
    i = pl.program_id(1)
    n_res = q_ref.shape[1]
    first_res = pl.program_id(2) * n_res
    kbuf[:, 0:DIL_HALO] = kl_ref[0]
    kbuf[:, DIL_HALO:DIL_HALO + tq] = km_ref[0]
    kbuf[:, DIL_HALO + tq:] = kr_ref[0]
    vbuf[:, 0:DIL_HALO] = vl_ref[0]
    vbuf[:, DIL_HALO:DIL_HALO + tq] = vm_ref[0]
    vbuf[:, DIL_HALO + tq:] = vr_ref[0]

    win = DIL_SUB + 2 * DIL_HALO
    row = lax.broadcasted_iota(jnp.int32, (DIL_SUB, win), 0)
    col = lax.broadcasted_iota(jnp.int32, (DIL_SUB, win), 1)
    steps = jnp.abs(col - DIL_HALO - row)
    in_band = steps <= DIL_HALF_SPAN
    dist = (steps * dilation).astype(F32)
    for sub in range(tq // DIL_SUB):
        key_pos = i * tq + (sub * DIL_SUB - DIL_HALO) + col
        valid = in_band & (key_pos >= 0) & (key_pos < sub_len)
        rows = slice(sub * DIL_SUB, (sub + 1) * DIL_SUB)
        wrows = slice(sub * DIL_SUB, sub * DIL_SUB + win)
        for rr in range(n_res):
            if dilation == 1:
                out_rows = pl.ds(sub * DIL_SUB, DIL_SUB)
            else:
                out_rows = pl.ds(sub * DIL_SUB * dilation + first_res + rr, DIL_SUB, stride=dilation)
            for h in range(DIL_HEADS_PER_GROUP):
                cols = slice(h * DIL_HEAD_DIM, (h + 1) * DIL_HEAD_DIM)
                slope = slopes_ref[group * DIL_HEADS_PER_GROUP + h]
                s = _dot_nt(q_ref[0, rr, rows, cols], kbuf[rr, wrows, cols]) - slope * dist
                s = jnp.where(valid, s, -jnp.inf)
                m = jnp.max(s, axis=1, keepdims=True)
                e = jnp.exp(s - m)
                den = jnp.sum(e, axis=1, keepdims=True)
                o_ref[0, h, out_rows, :] = _dot(e.astype(BF16), vbuf[rr, wrows, cols]) / den
                lse_ref[0, h, out_rows, :] = jnp.broadcast_to(m + jnp.log(den), (DIL_SUB, DIL_HEAD_DIM))


def _dil_attn(dil_g, slopes, *, group, dilation):
    batch, _, sub_len, _ = dil_g.shape
    seq = sub_len * dilation
    tq = min(512, sub_len, DIL_CHUNK_TOKENS // dilation)
    n_q = sub_len // tq
    n_res = min(dilation, DIL_ROWS_PER_STEP // tq)
    halo_per_tile = tq // DIL_HALO
    n_halo = sub_len // DIL_HALO
    q_col, k_col, v_col = 0, 1, 2

    def main(c):
        return pl.BlockSpec((1, n_res, tq, DIL_OUT), lambda b, i, r: (b, r, i, c))

    def left(c):
        return pl.BlockSpec(
            (1, n_res, DIL_HALO, DIL_OUT),
            lambda b, i, r: (b, r, jnp.maximum(i * halo_per_tile - 1, 0), c))

    def right(c):
        return pl.BlockSpec(
            (1, n_res, DIL_HALO, DIL_OUT),
            lambda b, i, r: (b, r, jnp.minimum((i + 1) * halo_per_tile, n_halo - 1), c))

    out_spec = pl.BlockSpec((1, DIL_HEADS_PER_GROUP, tq * dilation, DIL_HEAD_DIM),
                            lambda b, i, r: (b, 0, i, 0))
    out_shape = jax.ShapeDtypeStruct((batch, DIL_HEADS_PER_GROUP, seq, DIL_HEAD_DIM), F32)
    o, lse = pl.pallas_call(
        functools.partial(_dil_attn_body, group=group, dilation=dilation, tq=tq, sub_len=sub_len),
        grid=(batch, n_q, dilation // n_res),
        in_specs=[pl.BlockSpec(memory_space=pltpu.SMEM),
                  main(q_col), left(k_col), main(k_col), right(k_col),
                  left(v_col), main(v_col), right(v_col)],
        out_specs=[out_spec, out_spec],
        out_shape=[out_shape, out_shape],
        scratch_shapes=[pltpu.VMEM((n_res, tq + 2 * DIL_HALO, DIL_OUT), BF16),
                        pltpu.VMEM((n_res, tq + 2 * DIL_HALO, DIL_OUT), BF16)],
        compiler_params=pltpu.CompilerParams(
            dimension_semantics=("parallel", "parallel", "arbitrary"), vmem_limit_bytes=VMEM_LIMIT),
        name=f"dil_attn_g{group}",
    )(slopes, dil_g, dil_g, dil_g, dil_g, dil_g, dil_g, dil_g)
    return o, lse


def _mem_kv_body(mem_ref, w_ref, o_ref):
    o_ref[0] = _dot(mem_ref[0].astype(BF16), w_ref[...]).astype(BF16)


def _mem_kv(mem, w):
    batch, m_len, d = mem.shape
    n = w.shape[1]
    return pl.pallas_call(
        _mem_kv_body,
        grid=(batch,),
        in_specs=[pl.BlockSpec((1, m_len, d), lambda b: (b, 0, 0)),
                  pl.BlockSpec((d, n), lambda b: (0, 0))],
        out_specs=pl.BlockSpec((1, m_len, n), lambda b: (b, 0, 0)),
        out_shape=jax.ShapeDtypeStruct((batch, m_len, n), BF16),
        compiler_params=pltpu.CompilerParams(dimension_semantics=("parallel",)),
        name="mem_kv",
    )(mem, w)


def _merge_body(h_ref, omla_ref, od0_ref, od1_ref, od2_ref, ls0_ref, ls1_ref, ls2_ref, kvm_ref,
                wgate_ref, wmq_ref, wba_ref, wbb_ref, wbc_ref, wo_ref, g_ref, b_ref, out_ref, *, n_sub):
    sub_rows = h_ref.shape[0] // n_sub
    for sub in range(n_sub):
        rows = slice(sub * sub_rows, (sub + 1) * sub_rows)
        _merge_rows(rows, h_ref, omla_ref, od0_ref, od1_ref, od2_ref, ls0_ref, ls1_ref, ls2_ref, kvm_ref,
                    wgate_ref, wmq_ref, wba_ref, wbb_ref, wbc_ref, wo_ref, g_ref, b_ref, out_ref)


def _merge_rows(rows, h_ref, omla_ref, od0_ref, od1_ref, od2_ref, ls0_ref, ls1_ref, ls2_ref, kvm_ref,
                wgate_ref, wmq_ref, wba_ref, wbb_ref, wbc_ref, wo_ref, g_ref, b_ref, out_ref):
    h = h_ref[rows, :]
    hb = h.astype(BF16)

    def heads_to_lanes(ref):
        return jnp.concatenate([ref[0, hh, rows, :] for hh in range(DIL_HEADS_PER_GROUP)], axis=1)

    ls0, ls1, ls2 = heads_to_lanes(ls0_ref), heads_to_lanes(ls1_ref), heads_to_lanes(ls2_ref)
    mx = jnp.maximum(ls0, jnp.maximum(ls1, ls2))
    e0, e1, e2 = jnp.exp(ls0 - mx), jnp.exp(ls1 - mx), jnp.exp(ls2 - mx)
    o_dil = (e0 * heads_to_lanes(od0_ref) + e1 * heads_to_lanes(od1_ref)
             + e2 * heads_to_lanes(od2_ref)) / (e0 + e1 + e2)

    mq = (_dot(hb, wmq_ref[...]) * (MEM_HEAD_DIM ** -0.5)).astype(BF16)
    heads = []
    for hh in range(MEM_HEADS):
        kc = slice(hh * MEM_HEAD_DIM, (hh + 1) * MEM_HEAD_DIM)
        vc = slice(MEM_OUT + hh * MEM_HEAD_DIM, MEM_OUT + (hh + 1) * MEM_HEAD_DIM)
        s = _dot_nt(mq[:, kc], kvm_ref[0, :, kc])
        p = jnp.exp(s - jnp.max(s, axis=1, keepdims=True))
        heads.append(_dot(p.astype(BF16), kvm_ref[0, :, vc]) / jnp.sum(p, axis=1, keepdims=True))
    o_mem = jnp.concatenate(heads, axis=1)

    y_a = _dot(omla_ref[rows, :], wba_ref[...])
    y_b = _dot(o_dil.astype(BF16), wbb_ref[...])
    y_c = _dot(o_mem.astype(BF16), wbc_ref[...])
    d = h.shape[1]
    merged = (jax.nn.sigmoid(_dot(hb, wgate_ref[:, :d])) * y_a
              + jax.nn.sigmoid(_dot(hb, wgate_ref[:, d:2 * d])) * y_b
              + jax.nn.sigmoid(_dot(hb, wgate_ref[:, 2 * d:])) * y_c)
    mix = _dot(merged.astype(BF16), wo_ref[...])
    out_ref[rows, :] = _layer_norm(ALPHA * h + mix, g_ref[...], b_ref[...])


def _merge(h2d, o_mla, o_dil, lse_dil, kv_mem, weights, g, b, *, seq, tm, n_sub):
    t, d = h2d.shape
    n_s = seq // tm

    def rows(width):
        return pl.BlockSpec((tm, width), lambda i: (i, 0))

    def full(a):
        return pl.BlockSpec(a.shape, lambda i: (0,) * a.ndim, pipeline_mode=pl.Buffered(1))

    m_len, kv_cols = kv_mem.shape[1:]
    return pl.pallas_call(
        functools.partial(_merge_body, n_sub=n_sub),
        grid=(t // tm,),
        in_specs=[rows(d), rows(MLA_OUT)]
        + [pl.BlockSpec((1, DIL_HEADS_PER_GROUP, tm, DIL_HEAD_DIM),
                        lambda i: (i // n_s, 0, i % n_s, 0))] * 6
        + [pl.BlockSpec((1, m_len, kv_cols), lambda i: (i // n_s, 0, 0))]
        + [full(w) for w in weights] + [full(g), full(b)],
        out_specs=rows(d),
        out_shape=jax.ShapeDtypeStruct((t, d), F32),
        compiler_params=pltpu.CompilerParams(
            dimension_semantics=("parallel",), vmem_limit_bytes=VMEM_LIMIT),
        name="merge",
    )(h2d, o_mla, *o_dil, *lse_dil, kv_mem, *weights, g, b)


def _swap_halves(w):
    half = w.shape[-1] // 2
    return jnp.concatenate([w[..., half:], w[..., :half]], axis=-1)


def _prep_mla_weights(w_in, w_uq, w_ukv):
    d = w_in.shape[0]
    o_q, o_kv, o_kr = 0, MLA_Q_LORA, MLA_Q_LORA + MLA_KV_LORA
    w_cq = w_in[:, o_q:o_q + MLA_Q_LORA]
    w_ckv = w_in[:, o_kv:o_kv + MLA_KV_LORA]
    w_kr = w_in[:, o_kr:o_kr + MLA_ROPE]
    pad_tail = LANES - MLA_NOPE - MLA_ROPE
    z_nope = jnp.zeros((d, MLA_NOPE), F32)
    z_tail = jnp.zeros((d, pad_tail), F32)
    w_kra = jnp.concatenate([z_nope, w_kr, z_tail], axis=1)
    w_krb = jnp.concatenate([z_nope, _swap_halves(w_kr), z_tail], axis=1)

    uq = w_uq.reshape(MLA_Q_LORA, MLA_HEADS, MLA_NOPE + MLA_ROPE)
    uq_nope, uq_pe = uq[..., :MLA_NOPE], uq[..., MLA_NOPE:]
    zq_nope = jnp.zeros_like(uq_nope)
    zq_tail = jnp.zeros((MLA_Q_LORA, MLA_HEADS, pad_tail), F32)
    w_qa = jnp.concatenate([uq_nope, uq_pe, zq_tail], axis=-1).reshape(MLA_Q_LORA, MLA_HEADS * LANES)
    w_qb = jnp.concatenate([zq_nope, _swap_halves(uq_pe), zq_tail], axis=-1).reshape(
        MLA_Q_LORA, MLA_HEADS * LANES)

    ukv = w_ukv.reshape(MLA_KV_LORA, MLA_HEADS, MLA_NOPE + MLA_V)
    uk, uv = ukv[..., :MLA_NOPE], ukv[..., MLA_NOPE:]
    w_ka = jnp.concatenate(
        [uk, jnp.zeros((MLA_KV_LORA, MLA_HEADS, LANES - MLA_NOPE), F32)], axis=-1).reshape(
            MLA_KV_LORA, MLA_HEADS * LANES)
    w_v = uv.reshape(MLA_KV_LORA, MLA_HEADS * MLA_V)
    return [w.astype(BF16) for w in (w_cq, w_ckv, w_kra, w_krb)], [w.astype(BF16) for w in (w_qa, w_qb, w_ka, w_v)]


def _rope_tables(seq):
    pos = jnp.arange(seq, dtype=F32)
    inv = 1.0 / (ROPE_THETA ** (jnp.arange(0, MLA_ROPE, 2, dtype=F32) / MLA_ROPE))
    pad_tail = LANES - MLA_NOPE - MLA_ROPE
    q_scale = (MLA_NOPE + MLA_ROPE) ** -0.5 * math.log2(math.e)

    def tables(ang, axis, lead_one):
        cos, sin = jnp.cos(ang), jnp.sin(ang)
        shape = lambda n: (n, seq) if axis == 0 else (seq, n)
        lead = jnp.full(shape(MLA_NOPE), lead_one, F32)
        c = jnp.concatenate([lead, cos, cos, jnp.zeros(shape(pad_tail), F32)], axis=axis)
        s = jnp.concatenate([jnp.zeros(shape(MLA_NOPE), F32), -sin, sin, jnp.zeros(shape(pad_tail), F32)],
                            axis=axis)
        return c, s

    c_q, s_q = tables(inv[:, None] * pos[None, :], 0, 1.0)
    c_k, s_k = tables(pos[:, None] * inv[None, :], 1, 0.0)
    return c_q * q_scale, s_q * q_scale, c_k, s_k


def kernel(x, mem, w_in, mla_q_norm, mla_kv_norm, w_uq, w_ukv, w_mem_kv, w_br_mla, w_br_dil, w_br_mem, w_o,
           ffn1_w_gate, ffn1_w_up, ffn1_w_down, ffn2_w_gate, ffn2_w_up, ffn2_w_down,
           ln1_g, ln1_b, ln2_g, ln2_b, ln3_g, ln3_b):
    batch, seq, d = x.shape
    t = batch * seq
    d_ff = ffn1_w_gate.shape[-1]
    tm_ffn = min(512, t)
    tf = d_ff // 2
    tm_proj = min(512, seq)
    tm_merge = min(512, seq)
    tables = _rope_tables(seq)
    slopes = 2.0 ** (-8.0 * jnp.arange(1, DIL_HEADS + 1, dtype=F32) / DIL_HEADS)

    h = x.reshape(t, d)
    for l in range(DEPTH):
        bf = lambda w: w[l].astype(BF16)
        row = lambda v: v[l].reshape(1, -1)
        h = _ffn_ln(h, bf(ffn1_w_gate), bf(ffn1_w_up), bf(ffn1_w_down), row(ln1_g), row(ln1_b),
                    tm=tm_ffn, tf=tf)

        w_in_l = w_in[l]
        o_dil_cols = MLA_Q_LORA + MLA_KV_LORA + MLA_ROPE
        o_memq = o_dil_cols + DIL_QKV
        o_gate = o_memq + MEM_OUT
        w_c, w_u = _prep_mla_weights(w_in_l, w_uq[l], w_ukv[l])
        proj_weights = w_c + [row(mla_q_norm), row(mla_kv_norm)] + w_u + [
            w_in_l[:, o_dil_cols:o_memq].astype(BF16)]
        q, k, v, *dil = _proj(h, proj_weights, tables, batch=batch, seq=seq, tm=tm_proj)

        o_mla = _mla_attn(q, k, v, tq=min(512, seq), tk=min(1024, seq)).reshape(t, MLA_OUT)
        dil_parts = [_dil_attn(dil[g], slopes, group=g, dilation=dl)
                     for g, (_, dl) in enumerate(DIL_PAIRS)]
        kv_mem = _mem_kv(mem, bf(w_mem_kv))

        merge_weights = [w_in_l[:, o_gate:].astype(BF16), w_in_l[:, o_memq:o_gate].astype(BF16),
                         bf(w_br_mla), bf(w_br_dil), bf(w_br_mem), bf(w_o)]
        h = _merge(h, o_mla, [p[0] for p in dil_parts], [p[1] for p in dil_parts], kv_mem,
                   merge_weights, row(ln2_g), row(ln2_b), seq=seq, tm=tm_merge, n_sub=2)

        h = _ffn_ln(h, bf(ffn2_w_gate), bf(ffn2_w_up), bf(ffn2_w_down), row(ln3_g), row(ln3_b),
                    tm=tm_ffn, tf=tf)
    return h.reshape(batch, seq, d)
```

```python
import functools
import math

import jax
import jax.numpy as jnp
from jax import lax
from jax.experimental import pallas as pl
from jax.experimental.pallas import tpu as pltpu

F32 = jnp.float32
BF16 = jnp.bfloat16

D_MODEL = 1024
DEPTH = 1
MLA_HEADS = 8
MLA_Q_LORA = 256
MLA_KV_LORA = 256
MLA_NOPE = 64
MLA_ROPE = 32
MLA_V = 64
ROPE_THETA = 10000.0
DIL_PAIRS = ((128, 1), (512, 4), (2048, 16))
DIL_GROUPS = 3
DIL_HEADS_PER_GROUP = 4
DIL_HEAD_DIM = 128
DIL_HEADS = DIL_GROUPS * DIL_HEADS_PER_GROUP
MEM_HEADS = 4
MEM_HEAD_DIM = 128
EPS = 1e-5
ALPHA = (2 * DEPTH) ** 0.25

LANES = 128
DIL_OUT = DIL_HEADS_PER_GROUP * DIL_HEAD_DIM
DIL_QKV = 3 * DIL_HEADS * DIL_HEAD_DIM
MEM_OUT = MEM_HEADS * MEM_HEAD_DIM
MLA_OUT = MLA_HEADS * MLA_V
MLA_DEN_ROWS = 16
MLA_PIECE = 256
DIL_HALF_SPAN = 64
DIL_HALO = 64
DIL_SUB = 128
DIL_CHUNK_TOKENS = 2048
DIL_ROWS_PER_STEP = 512
FFN_SUB_TILES = 4
VMEM_LIMIT = 56 * 1024 * 1024

assert all(w // 2 // d == DIL_HALF_SPAN for w, d in DIL_PAIRS)

_NT = (((1,), (1,)), ((), ()))


def _dot(a, b):
    return jnp.dot(a, b, preferred_element_type=F32)


def _dot_nt(a, b):
    return lax.dot_general(a, b, _NT, preferred_element_type=F32)


def _dot_tt(w, x):
    return lax.dot_general(w, x, (((0,), (1,)), ((), ())), preferred_element_type=F32)


def _layer_norm(y, g, b):
    mu = jnp.mean(y, axis=-1, keepdims=True)
    yc = y - mu
    var = jnp.mean(yc * yc, axis=-1, keepdims=True)
    return yc * lax.rsqrt(var + EPS) * g + b


def _rms_norm(y, g):
    return y * lax.rsqrt(jnp.mean(y * y, axis=-1, keepdims=True) + EPS) * g


def _ffn_ln_body(x_ref, wg_ref, wu_ref, wd_ref, g_ref, b_ref, o_ref, *, n_sub):
    sub_rows = x_ref.shape[0] // n_sub
    for sub in range(n_sub):
        rows = slice(sub * sub_rows, (sub + 1) * sub_rows)
        x = x_ref[rows, :]
        xb = x.astype(BF16)
        gate = _dot(xb, wg_ref[...])
        up = _dot(xb, wu_ref[...])
        act = (gate * jax.nn.sigmoid(gate) * up).astype(BF16)
        y = ALPHA * x + 0.5 * _dot(act, wd_ref[...])
        o_ref[rows, :] = _layer_norm(y, g_ref[...], b_ref[...])


def _ffn_ln(x2d, wg, wu, wd, g, b, *, tm, n_sub):
    t, d = x2d.shape

    def full(a):
        return pl.BlockSpec(a.shape, lambda i: (0,) * a.ndim, pipeline_mode=pl.Buffered(1))

    return pl.pallas_call(
        functools.partial(_ffn_ln_body, n_sub=n_sub),
        grid=(t // tm,),
        in_specs=[pl.BlockSpec((tm, d), lambda i: (i, 0))] + [full(a) for a in (wg, wu, wd, g, b)],
        out_specs=pl.BlockSpec((tm, d), lambda i: (i, 0)),
        out_shape=jax.ShapeDtypeStruct((t, d), F32),
        compiler_params=pltpu.CompilerParams(
            dimension_semantics=("parallel",), vmem_limit_bytes=VMEM_LIMIT),
        name="ffn_ln",
    )(x2d, wg, wu, wd, g, b)


def _proj_body(*refs):
    n_hc = D_MODEL // LANES
    h_refs = refs[:n_hc]
    (wcq_ref, wckv_ref, wkra_ref, wkrb_ref, gq_ref, gkv_ref,
     wqa_ref, wqb_ref, wka_ref, wv_ref, wdil_ref,
     cq_ref, sq_ref, ck_ref, sk_ref,
     q_ref, k_ref, v_ref, dil0_ref, dil1_ref, dil2_ref, hperm_ref) = refs[n_hc:]
    hb = jnp.concatenate([hc[...] for hc in h_refs], axis=1).astype(BF16)
    cqn = _rms_norm(_dot(hb, wcq_ref[...]), gq_ref[...]).astype(BF16)
    ckvn = _rms_norm(_dot(hb, wckv_ref[...]), gkv_ref[...]).astype(BF16)
    k_rope = _dot(hb, wkra_ref[...]) * ck_ref[...] + _dot(hb, wkrb_ref[...]) * sk_ref[...]
    ka = _dot(ckvn, wka_ref[...])
    for h in range(MLA_HEADS):
        sl = slice(h * LANES, (h + 1) * LANES)
        k_ref[0, h] = (ka[:, sl] + k_rope).astype(BF16)
    qa_t = _dot_tt(wqa_ref[...], cqn)
    qb_t = _dot_tt(wqb_ref[...], cqn)
    cq_t = cq_ref[...]
    sq_t = sq_ref[...]
    for h in range(MLA_HEADS):
        sl = slice(h * LANES, (h + 1) * LANES)
        q_ref[0, sl] = (qa_t[sl] * cq_t + qb_t[sl] * sq_t).astype(BF16)
    v_ref[0] = _dot_tt(wv_ref[...], ckvn).astype(BF16)

    dil_scale = DIL_HEAD_DIM ** -0.5
    tm = h_refs[0].shape[0]
    for g, (dil_ref, (_, dilation)) in enumerate(zip((dil0_ref, dil1_ref, dil2_ref), DIL_PAIRS)):
        rows_per = tm // dilation
        if dilation == 1:
            hp = hb
        else:
            for r in range(dilation):
                for c, hc in enumerate(h_refs):
                    hperm_ref[r * rows_per:(r + 1) * rows_per, c * LANES:(c + 1) * LANES] = hc[
                        pl.ds(r, rows_per, stride=dilation), :].astype(BF16)
            hp = hperm_ref[...]
        for part in range(3):
            c0 = (part * DIL_GROUPS + g) * DIL_OUT
            res = _dot(hp, wdil_ref[:, c0:c0 + DIL_OUT])
            if part == 0:
                res = res * dil_scale
            res = res.astype(BF16)
            for r in range(dilation):
                dil_ref[0, r, :, part * DIL_OUT:(part + 1) * DIL_OUT] = res[r * rows_per:(r + 1) * rows_per]


def _proj(h2d, weights, tables, *, batch, seq, tm):
    t, d = h2d.shape
    n_s = seq // tm

    def full(a):
        return pl.BlockSpec(a.shape, lambda i: (0,) * a.ndim)

    tab_spec = pl.BlockSpec((tm, LANES), lambda i: (i % n_s, 0))
    tab_t_spec = pl.BlockSpec((LANES, tm), lambda i: (0, i % n_s))
    head_map = lambda i: (i // n_s, 0, i % n_s, 0)
    feat_map = lambda i: (i // n_s, 0, i % n_s)
    dil_specs = [pl.BlockSpec((1, dl, tm // dl, 3 * DIL_OUT), head_map) for _, dl in DIL_PAIRS]
    dil_shapes = [jax.ShapeDtypeStruct((batch, dl, seq // dl, 3 * DIL_OUT), BF16) for _, dl in DIL_PAIRS]
    return pl.pallas_call(
        _proj_body,
        grid=(t // tm,),
        in_specs=[pl.BlockSpec((tm, LANES), functools.partial(lambda c, i: (i, c), c))
                  for c in range(d // LANES)]
        + [full(w) for w in weights] + [tab_t_spec] * 2 + [tab_spec] * 2,
        out_specs=[
            pl.BlockSpec((1, MLA_HEADS * LANES, tm), feat_map),
            pl.BlockSpec((1, MLA_HEADS, tm, LANES), head_map),
            pl.BlockSpec((1, MLA_OUT, tm), feat_map),
        ] + dil_specs,
        out_shape=[
            jax.ShapeDtypeStruct((batch, MLA_HEADS * LANES, seq), BF16),
            jax.ShapeDtypeStruct((batch, MLA_HEADS, seq, LANES), BF16),
            jax.ShapeDtypeStruct((batch, MLA_OUT, seq), BF16),
        ] + dil_shapes,
        scratch_shapes=[pltpu.VMEM((tm, d), BF16)],
        compiler_params=pltpu.CompilerParams(
            dimension_semantics=("parallel",), vmem_limit_bytes=VMEM_LIMIT),
        name="proj",
    )(*([h2d] * (d // LANES)), *weights, *tables)


def _mla_attn_body(q_ref, k_ref, vt_ref, o_ref, st_ref, m_ref, acc_ref, *, tq, tk):
    seq = k_ref.shape[2]
    n_tiles = seq // tq
    n_units = (seq // tk) * n_tiles
    heads = (0, 1)
    n_pieces = tk // MLA_PIECE

    def offsets(unit):
        if isinstance(unit, int):
            return (unit // n_tiles) * tk, (unit % n_tiles) * tq
        return (pl.multiple_of(lax.div(unit, n_tiles) * tk, tk),
                pl.multiple_of(lax.rem(unit, n_tiles) * tq, tq))

    def scores_piece(unit, buf, p, maxes):
        k_off, q_off = offsets(unit)
        rows = slice(p * MLA_PIECE, (p + 1) * MLA_PIECE)
        out = []
        for h in heads:
            st = _dot(k_ref[0, h, pl.ds(k_off + p * MLA_PIECE, MLA_PIECE), :],
                      q_ref[0, h * LANES:(h + 1) * LANES, pl.ds(q_off, tq)])
            st_ref[buf, h, rows] = st
            out.append(jnp.maximum(maxes[h], jnp.max(st, axis=0, keepdims=True)))
        return tuple(out)

    ones_rows = jnp.ones((MLA_DEN_ROWS, MLA_PIECE), BF16)

    def values_piece(k_off, buf, p, m_new, accs):
        rows = slice(p * MLA_PIECE, (p + 1) * MLA_PIECE)
        out = []
        for h in heads:
            pt = jnp.exp2(st_ref[buf, h, rows] - m_new[h]).astype(BF16)
            vtc = jnp.concatenate(
                [vt_ref[0, h * MLA_V:(h + 1) * MLA_V, pl.ds(k_off + p * MLA_PIECE, MLA_PIECE)], ones_rows],
                axis=0)
            out.append(accs[h] + _dot(vtc, pt))
        return tuple(out)

    neg_inf = jnp.full((1, tq), -jnp.inf, F32)

    def half(next_unit, next_buf, unit, buf, unit_max):
        k_off, q_off = offsets(unit)
        cols = pl.ds(q_off, tq)
        m_old = tuple(m_ref[h, :, cols] for h in heads)
        m_new = tuple(jnp.maximum(m_old[h], unit_max[h]) for h in heads)
        accs = tuple(jnp.exp2(m_old[h] - m_new[h]) * acc_ref[h, :, cols] for h in heads)
        next_max = (neg_inf, neg_inf)
        for p in range(n_pieces):
            if next_unit is not None:
                next_max = scores_piece(next_unit, next_buf, p, next_max)
            accs = values_piece(k_off, buf, p, m_new, accs)
        for h in heads:
            m_ref[h, :, cols] = m_new[h]
            acc_ref[h, :, cols] = accs[h]
        return next_max

    def pair_step(jj, max_a):
        u = 2 * jj
        max_b = half(u + 1, 1, u, 0, max_a)
        return half(u + 2, 0, u + 1, 1, max_b)

    m_ref[...] = jnp.full(m_ref.shape, -jnp.inf, F32)
    acc_ref[...] = jnp.zeros(acc_ref.shape, F32)
    max_0 = (neg_inf, neg_inf)
    for p in range(n_pieces):
        max_0 = scores_piece(0, 0, p, max_0)
    max_a = lax.fori_loop(0, n_units // 2 - 1, pair_step, max_0)
    max_b = half(n_units - 1, 1, n_units - 2, 0, max_a)
    half(None, None, n_units - 1, 1, max_b)

    @pl.loop(0, n_tiles)
    def _(tile):
        cols = pl.ds(pl.multiple_of(tile * tq, tq), tq)
        out_t = jnp.concatenate(
            [acc_ref[h, :MLA_V, cols] / acc_ref[h, MLA_V:MLA_V + 1, cols] for h in heads], axis=0)
        o_ref[0, cols, :] = out_t.T.astype(o_ref.dtype)


def _mla_attn(q, k, v, *, tq, tk):
    batch, heads, seq, _ = k.shape
    assert ((seq // tk) * (seq // tq)) % 2 == 0
    return pl.pallas_call(
        functools.partial(_mla_attn_body, tq=tq, tk=tk),
        grid=(batch, heads // 2),
        in_specs=[
            pl.BlockSpec((1, 2 * LANES, seq), lambda b, hp: (b, hp, 0)),
            pl.BlockSpec((1, 2, seq, LANES), lambda b, hp: (b, hp, 0, 0)),
            pl.BlockSpec((1, 2 * MLA_V, seq), lambda b, hp: (b, hp, 0)),
        ],
        out_specs=pl.BlockSpec((1, seq, LANES), lambda b, hp: (b, 0, hp)),
        out_shape=jax.ShapeDtypeStruct((batch, seq, MLA_OUT), BF16),
        scratch_shapes=[pltpu.VMEM((2, 2, tk, tq), F32),
                        pltpu.VMEM((2, 1, seq), F32),
                        pltpu.VMEM((2, MLA_V + MLA_DEN_ROWS, seq), F32)],
        compiler_params=pltpu.CompilerParams(
            dimension_semantics=("parallel", "parallel"), vmem_limit_bytes=VMEM_LIMIT),
        name="mla_attn",
    )(q, k, v)


def _dil_attn_body(slopes_ref, q_ref, kl_ref, km_ref, kr_ref, vl_ref, vm_ref, vr_ref,
                   o_ref, lse_ref, kbuf, vbuf, *, group, dilation, tq, sub_len):
    i = pl.program_id(1)
    n_res = q_ref.shape[1]
    first_res = pl.program_id(2) * n_res
    kbuf[:, 0:DIL_HALO] = kl_ref[0]
    kbuf[:, DIL_HALO:DIL_HALO + tq] = km_ref[0]
    kbuf[:, DIL_HALO + tq:] = kr_ref[0]
    vbuf[:, 0:DIL_HALO] = vl_ref[0]
    vbuf[:, DIL_HALO:DIL_HALO + tq] = vm_ref[0]
    vbuf[:, DIL_HALO + tq:] = vr_ref[0]

    win = DIL_SUB + 2 * DIL_HALO
    row = lax.broadcasted_iota(jnp.int32, (DIL_SUB, win), 0)
    col = lax.broadcasted_iota(jnp.int32, (DIL_SUB, win), 1)
    steps = jnp.abs(col - DIL_HALO - row)
    in_band = steps <= DIL_HALF_SPAN
    dist = (steps * dilation).astype(F32)
    for sub in range(tq // DIL_SUB):
        key_pos = i * tq + (sub * DIL_SUB - DIL_HALO) + col
        valid = in_band & (key_pos >= 0) & (key_pos < sub_len)
        rows = slice(sub * DIL_SUB, (sub + 1) * DIL_SUB)
        wrows = slice(sub * DIL_SUB, sub * DIL_SUB + win)
        for rr in range(n_res):
            if dilation == 1:
                out_rows = pl.ds(sub * DIL_SUB, DIL_SUB)
            else:
                out_rows = pl.ds(sub * DIL_SUB * dilation + first_res + rr, DIL_SUB, stride=dilation)
            for h in range(DIL_HEADS_PER_GROUP):
                cols = slice(h * DIL_HEAD_DIM, (h + 1) * DIL_HEAD_DIM)
                slope = slopes_ref[group * DIL_HEADS_PER_GROUP + h]
                s = _dot_nt(q_ref[0, rr, rows, cols], kbuf[rr, wrows, cols]) - slope * dist
                s = jnp.where(valid, s, -jnp.inf)
                m = jnp.max(s, axis=1, keepdims=True)
                e = jnp.exp(s - m)
                den = jnp.sum(e, axis=1, keepdims=True)
                o_ref[0, h, out_rows, :] = _dot(e.astype(BF16), vbuf[rr, wrows, cols]) / den
                lse_ref[0, h, out_rows, :] = jnp.broadcast_to(m + jnp.log(den), (DIL_SUB, DIL_HEAD_DIM))


def _dil_attn(dil_g, slopes, *, group, dilation):
    batch, _, sub_len, _ = dil_g.shape
    seq = sub_len * dilation
    tq = min(512, sub_len, DIL_CHUNK_TOKENS // dilation)
    n_q = sub_len // tq
    n_res = min(dilation, DIL_ROWS_PER_STEP // tq)
    halo_per_tile = tq // DIL_HALO
    n_halo = sub_len // DIL_HALO
    q_col, k_col, v_col = 0, 1, 2

    def main(c):
        return pl.BlockSpec((1, n_res, tq, DIL_OUT), lambda b, i, r: (b, r, i, c))

    def left(c):
        return pl.BlockSpec(
            (1, n_res, DIL_HALO, DIL_OUT),
            lambda b, i, r: (b, r, jnp.maximum(i * halo_per_tile - 1, 0), c))

    def right(c):
        return pl.BlockSpec(
            (1, n_res, DIL_HALO, DIL_OUT),
            lambda b, i, r: (b, r, jnp.minimum((i + 1) * halo_per_tile, n_halo - 1), c))

    out_spec = pl.BlockSpec((1, DIL_HEADS_PER_GROUP, tq * dilation, DIL_HEAD_DIM),
                            lambda b, i, r: (b, 0, i, 0))
    out_shape = jax.ShapeDtypeStruct((batch, DIL_HEADS_PER_GROUP, seq, DIL_HEAD_DIM), F32)
    o, lse = pl.pallas_call(
        functools.partial(_dil_attn_body, group=group, dilation=dilation, tq=tq, sub_len=sub_len),
        grid=(batch, n_q, dilation // n_res),
        in_specs=[pl.BlockSpec(memory_space=pltpu.SMEM),
                  main(q_col), left(k_col), main(k_col), right(k_col),
                  left(v_col), main(v_col), right(v_col)],
        out_specs=[out_spec, out_spec],
        out_shape=[out_shape, out_shape],
        scratch_shapes=[pltpu.VMEM((n_res, tq + 2 * DIL_HALO, DIL_OUT), BF16),
                        pltpu.VMEM((n_res, tq + 2 * DIL_HALO, DIL_OUT), BF16)],
        compiler_params=pltpu.CompilerParams(
            dimension_semantics=("parallel", "parallel", "arbitrary"), vmem_limit_bytes=VMEM_LIMIT),
        name=f"dil_attn_g{group}",
    )(slopes, dil_g, dil_g, dil_g, dil_g, dil_g, dil_g, dil_g)
    return o, lse


def _mem_kv_body(mem_ref, w_ref, o_ref):
    o_ref[0] = _dot(mem_ref[0].astype(BF16), w_ref[...]).astype(BF16)


def _mem_kv(mem, w):
    batch, m_len, d = mem.shape
    n = w.shape[1]
    return pl.pallas_call(
        _mem_kv_body,
        grid=(batch,),
        in_specs=[pl.BlockSpec((1, m_len, d), lambda b: (b, 0, 0)),
                  pl.BlockSpec((d, n), lambda b: (0, 0))],
        out_specs=pl.BlockSpec((1, m_len, n), lambda b: (b, 0, 0)),
        out_shape=jax.ShapeDtypeStruct((batch, m_len, n), BF16),
        compiler_params=pltpu.CompilerParams(dimension_semantics=("parallel",)),
        name="mem_kv",
    )(mem, w)


def _merge_body(h_ref, omla_ref, od0_ref, od1_ref, od2_ref, ls0_ref, ls1_ref, ls2_ref, kvm_ref,
                wgate_ref, wmq_ref, wba_ref, wbb_ref, wbc_ref, wo_ref, g_ref, b_ref, out_ref, *, n_sub):
    sub_rows = h_ref.shape[0] // n_sub
    for sub in range(n_sub):
        rows = slice(sub * sub_rows, (sub + 1) * sub_rows)
        _merge_rows(rows, h_ref, omla_ref, od0_ref, od1_ref, od2_ref, ls0_ref, ls1_ref, ls2_ref, kvm_ref,
                    wgate_ref, wmq_ref, wba_ref, wbb_ref, wbc_ref, wo_ref, g_ref, b_ref, out_ref)


def _merge_rows(rows, h_ref, omla_ref, od0_ref, od1_ref, od2_ref, ls0_ref, ls1_ref, ls2_ref, kvm_ref,
                wgate_ref, wmq_ref, wba_ref, wbb_ref, wbc_ref, wo_ref, g_ref, b_ref, out_ref):
    h = h_ref[rows, :]
    hb = h.astype(BF16)

    def heads_to_lanes(ref):
        return jnp.concatenate([ref[0, hh, rows, :] for hh in range(DIL_HEADS_PER_GROUP)], axis=1)

    ls0, ls1, ls2 = heads_to_lanes(ls0_ref), heads_to_lanes(ls1_ref), heads_to_lanes(ls2_ref)
    mx = jnp.maximum(ls0, jnp.maximum(ls1, ls2))
    e0, e1, e2 = jnp.exp(ls0 - mx), jnp.exp(ls1 - mx), jnp.exp(ls2 - mx)
    o_dil = (e0 * heads_to_lanes(od0_ref) + e1 * heads_to_lanes(od1_ref)
             + e2 * heads_to_lanes(od2_ref)) / (e0 + e1 + e2)

    mq = (_dot(hb, wmq_ref[...]) * (MEM_HEAD_DIM ** -0.5)).astype(BF16)
    heads = []
    for hh in range(MEM_HEADS):
        kc = slice(hh * MEM_HEAD_DIM, (hh + 1) * MEM_HEAD_DIM)
        vc = slice(MEM_OUT + hh * MEM_HEAD_DIM, MEM_OUT + (hh + 1) * MEM_HEAD_DIM)
        s = _dot_nt(mq[:, kc], kvm_ref[0, :, kc])
        p = jnp.exp(s - jnp.max(s, axis=1, keepdims=True))
        heads.append(_dot(p.astype(BF16), kvm_ref[0, :, vc]) / jnp.sum(p, axis=1, keepdims=True))
    o_mem = jnp.concatenate(heads, axis=1)

    y_a = _dot(omla_ref[rows, :], wba_ref[...])
    y_b = _dot(o_dil.astype(BF16), wbb_ref[...])
    y_c = _dot(o_mem.astype(BF16), wbc_ref[...])
    d = h.shape[1]
    merged = (jax.nn.sigmoid(_dot(hb, wgate_ref[:, :d])) * y_a
              + jax.nn.sigmoid(_dot(hb, wgate_ref[:, d:2 * d])) * y_b
              + jax.nn.sigmoid(_dot(hb, wgate_ref[:, 2 * d:])) * y_c)
    mix = _dot(merged.astype(BF16), wo_ref[...])
    out_ref[rows, :] = _layer_norm(ALPHA * h + mix, g_ref[...], b_ref[...])


def _merge(h2d, o_mla, o_dil, lse_dil, kv_mem, weights, g, b, *, seq, tm, n_sub):
    t, d = h2d.shape
    n_s = seq // tm

    def rows(width):
        return pl.BlockSpec((tm, width), lambda i: (i, 0))

    def full(a):
        return pl.BlockSpec(a.shape, lambda i: (0,) * a.ndim, pipeline_mode=pl.Buffered(1))

    m_len, kv_cols = kv_mem.shape[1:]
    return pl.pallas_call(
        functools.partial(_merge_body, n_sub=n_sub),
        grid=(t // tm,),
        in_specs=[rows(d), rows(MLA_OUT)]
        + [pl.BlockSpec((1, DIL_HEADS_PER_GROUP, tm, DIL_HEAD_DIM),
                        lambda i: (i // n_s, 0, i % n_s, 0))] * 6
        + [pl.BlockSpec((1, m_len, kv_cols), lambda i: (i // n_s, 0, 0))]
        + [full(w) for w in weights] + [full(g), full(b)],
        out_specs=rows(d),
        out_shape=jax.ShapeDtypeStruct((t, d), F32),
        compiler_params=pltpu.CompilerParams(
            dimension_semantics=("parallel",), vmem_limit_bytes=VMEM_LIMIT),
        name="merge",
    )(h2d, o_mla, *o_dil, *lse_dil, kv_mem, *weights, g, b)


def _swap_halves(w):
    half = w.shape[-1] // 2
    return jnp.concatenate([w[..., half:], w[..., :half]], axis=-1)


def _prep_mla_weights(w_in, w_uq, w_ukv):
    d = w_in.shape[0]
    o_q, o_kv, o_kr = 0, MLA_Q_LORA, MLA_Q_LORA + MLA_KV_LORA
    w_cq = w_in[:, o_q:o_q + MLA_Q_LORA]
    w_ckv = w_in[:, o_kv:o_kv + MLA_KV_LORA]
    w_kr = w_in[:, o_kr:o_kr + MLA_ROPE]
    pad_tail = LANES - MLA_NOPE - MLA_ROPE
    z_nope = jnp.zeros((d, MLA_NOPE), F32)
    z_tail = jnp.zeros((d, pad_tail), F32)
    w_kra = jnp.concatenate([z_nope, w_kr, z_tail], axis=1)
    w_krb = jnp.concatenate([z_nope, _swap_halves(w_kr), z_tail], axis=1)

    uq = w_uq.reshape(MLA_Q_LORA, MLA_HEADS, MLA_NOPE + MLA_ROPE)
    uq_nope, uq_pe = uq[..., :MLA_NOPE], uq[..., MLA_NOPE:]
    zq_nope = jnp.zeros_like(uq_nope)
    zq_tail = jnp.zeros((MLA_Q_LORA, MLA_HEADS, pad_tail), F32)
    w_qa = jnp.concatenate([uq_nope, uq_pe, zq_tail], axis=-1).reshape(MLA_Q_LORA, MLA_HEADS * LANES)
    w_qb = jnp.concatenate([zq_nope, _swap_halves(uq_pe), zq_tail], axis=-1).reshape(
        MLA_Q_LORA, MLA_HEADS * LANES)

    ukv = w_ukv.reshape(MLA_KV_LORA, MLA_HEADS, MLA_NOPE + MLA_V)
    uk, uv = ukv[..., :MLA_NOPE], ukv[..., MLA_NOPE:]
    w_ka = jnp.concatenate(
        [uk, jnp.zeros((MLA_KV_LORA, MLA_HEADS, LANES - MLA_NOPE), F32)], axis=-1).reshape(
            MLA_KV_LORA, MLA_HEADS * LANES)
    w_v = uv.reshape(MLA_KV_LORA, MLA_HEADS * MLA_V)
    return [w.astype(BF16) for w in (w_cq, w_ckv, w_kra, w_krb)], [w.astype(BF16) for w in (w_qa, w_qb, w_ka, w_v)]


def _rope_tables(seq):
    pos = jnp.arange(seq, dtype=F32)
    inv = 1.0 / (ROPE_THETA ** (jnp.arange(0, MLA_ROPE, 2, dtype=F32) / MLA_ROPE))
    pad_tail = LANES - MLA_NOPE - MLA_ROPE
    q_scale = (MLA_NOPE + MLA_ROPE) ** -0.5 * math.log2(math.e)

    def tables(ang, axis, lead_one):
        cos, sin = jnp.cos(ang), jnp.sin(ang)
        shape = lambda n: (n, seq) if axis == 0 else (seq, n)
        lead = jnp.full(shape(MLA_NOPE), lead_one, F32)
        c = jnp.concatenate([lead, cos, cos, jnp.zeros(shape(pad_tail), F32)], axis=axis)
        s = jnp.concatenate([jnp.zeros(shape(MLA_NOPE), F32), -sin, sin, jnp.zeros(shape(pad_tail), F32)],
                            axis=axis)
        return c, s

    c_q, s_q = tables(inv[:, None] * pos[None, :], 0, 1.0)
    c_k, s_k = tables(pos[:, None] * inv[None, :], 1, 0.0)
    return c_q * q_scale, s_q * q_scale, c_k, s_k


def kernel(x, mem, w_in, mla_q_norm, mla_kv_norm, w_uq, w_ukv, w_mem_kv, w_br_mla, w_br_dil, w_br_mem, w_o,
           ffn1_w_gate, ffn1_w_up, ffn1_w_down, ffn2_w_gate, ffn2_w_up, ffn2_w_down,
           ln1_g, ln1_b, ln2_g, ln2_b, ln3_g, ln3_b):
    batch, seq, d = x.shape
    t = batch * seq
    tm_ffn = min(1024, t)
    tm_proj = min(512, seq)
    tm_merge = min(512, seq)
    tables = _rope_tables(seq)
    slopes = 2.0 ** (-8.0 * jnp.arange(1, DIL_HEADS + 1, dtype=F32) / DIL_HEADS)

    h = x.reshape(t, d)
    for l in range(DEPTH):
        bf = lambda w: w[l].astype(BF16)
        row = lambda v: v[l].reshape(1, -1)
        h = _ffn_ln(h, bf(ffn1_w_gate), bf(ffn1_w_up), bf(ffn1_w_down), row(ln1_g), row(ln1_b),
                    tm=tm_ffn, n_sub=FFN_SUB_TILES)

        w_in_l = w_in[l]
        o_dil_cols = MLA_Q_LORA + MLA_KV_LORA + MLA_ROPE
        o_memq = o_dil_cols + DIL_QKV
        o_gate = o_memq + MEM_OUT
        w_c, w_u = _prep_mla_weights(w_in_l, w_uq[l], w_ukv[l])
        proj_weights = w_c + [row(mla_q_norm), row(mla_kv_norm)] + w_u + [
            w_in_l[:, o_dil_cols:o_memq].astype(BF16)]
        q, k, v, *dil = _proj(h, proj_weights, tables, batch=batch, seq=seq, tm=tm_proj)

        o_mla = _mla_attn(q, k, v, tq=min(512, seq), tk=min(1024, seq)).reshape(t, MLA_OUT)
        dil_parts = [_dil_attn(dil[g], slopes, group=g, dilation=dl)
                     for g, (_, dl) in enumerate(DIL_PAIRS)]
        kv_mem = _mem_kv(mem, bf(w_mem_kv))

        merge_weights = [w_in_l[:, o_gate:].astype(BF16), w_in_l[:, o_memq:o_gate].astype(BF16),
                         bf(w_br_mla), bf(w_br_dil), bf(w_br_mem), bf(w_o)]
        h = _merge(h, o_mla, [p[0] for p in dil_parts], [p[1] for p in dil_parts], kv_mem,
                   merge_weights, row(ln2_g), row(ln2_b), seq=seq, tm=tm_merge, n_sub=2)

        h = _ffn_ln(h, bf(ffn2_w_gate), bf(ffn2_w_up), bf(ffn2_w_down), row(ln3_g), row(ln3_b),
                    tm=tm_ffn, n_sub=FFN_SUB_TILES)
    return h.reshape(batch, seq, d)
```

```python
import functools
import math

import jax
import jax.numpy as jnp
from jax import lax
from jax.experimental import pallas as pl
from jax.experimental.pallas import tpu as pltpu

F32 = jnp.float32
BF16 = jnp.bfloat16

D_MODEL = 1024
DEPTH = 1
MLA_HEADS = 8
MLA_Q_LORA = 256
MLA_KV_LORA = 256
MLA_NOPE = 64
MLA_ROPE = 32
MLA_V = 64
ROPE_THETA = 10000.0
DIL_PAIRS = ((128, 1), (512, 4), (2048, 16))
DIL_GROUPS = 3
DIL_HEADS_PER_GROUP = 4
DIL_HEAD_DIM = 128
DIL_HEADS = DIL_GROUPS * DIL_HEADS_PER_GROUP
MEM_HEADS = 4
MEM_HEAD_DIM = 128
EPS = 1e-5
ALPHA = (2 * DEPTH) ** 0.25

LOG2_E = math.log2(math.e)
LN_2 = math.log(2.0)
LANES = 128
DIL_OUT = DIL_HEADS_PER_GROUP * DIL_HEAD_DIM
DIL_QKV = 3 * DIL_HEADS * DIL_HEAD_DIM
MEM_OUT = MEM_HEADS * MEM_HEAD_DIM
MLA_OUT = MLA_HEADS * MLA_V
MLA_DEN_ROWS = 16
MLA_PIECE = 256
DIL_HALF_SPAN = 64
DIL_HALO = 64
DIL_SUB = 128
DIL_CHUNK_TOKENS = 2048
DIL_ROWS_PER_STEP = 512
FFN_SUB_TILES = 4
VMEM_LIMIT = 56 * 1024 * 1024

assert all(w // 2 // d == DIL_HALF_SPAN for w, d in DIL_PAIRS)

_NT = (((1,), (1,)), ((), ()))


def _dot(a, b):
    return jnp.dot(a, b, preferred_element_type=F32)


def _dot_nt(a, b):
    return lax.dot_general(a, b, _NT, preferred_element_type=F32)


def _dot_tt(w, x):
    return lax.dot_general(w, x, (((0,), (1,)), ((), ())), preferred_element_type=F32)


def _layer_norm(y, g, b):
    mu = jnp.mean(y, axis=-1, keepdims=True)
    yc = y - mu
    var = jnp.mean(yc * yc, axis=-1, keepdims=True)
    return yc * lax.rsqrt(var + EPS) * g + b


def _rms_norm(y, g):
    return y * lax.rsqrt(jnp.mean(y * y, axis=-1, keepdims=True) + EPS) * g


def _ffn_ln_body(x_ref, wg_ref, wu_ref, wd_ref, g_ref, b_ref, o_ref, *, n_sub):
    sub_rows = x_ref.shape[0] // n_sub
    for sub in range(n_sub):
        rows = slice(sub * sub_rows, (sub + 1) * sub_rows)
        x = x_ref[rows, :]
        xb = x.astype(BF16)
        gate = _dot(xb, wg_ref[...])
        up = _dot(xb, wu_ref[...])
        act = (gate * jax.nn.sigmoid(gate) * up).astype(BF16)
        y = ALPHA * x + 0.5 * _dot(act, wd_ref[...])
        o_ref[rows, :] = _layer_norm(y, g_ref[...], b_ref[...])


def _ffn_ln(x2d, wg, wu, wd, g, b, *, tm, n_sub):
    t, d = x2d.shape

    def full(a):
        return pl.BlockSpec(a.shape, lambda i: (0,) * a.ndim, pipeline_mode=pl.Buffered(1))

    return pl.pallas_call(
        functools.partial(_ffn_ln_body, n_sub=n_sub),
        grid=(t // tm,),
        in_specs=[pl.BlockSpec((tm, d), lambda i: (i, 0))] + [full(a) for a in (wg, wu, wd, g, b)],
        out_specs=pl.BlockSpec((tm, d), lambda i: (i, 0)),
        out_shape=jax.ShapeDtypeStruct((t, d), F32),
        compiler_params=pltpu.CompilerParams(
            dimension_semantics=("parallel",), vmem_limit_bytes=VMEM_LIMIT),
        name="ffn_ln",
    )(x2d, wg, wu, wd, g, b)


def _proj_body(*refs):
    n_hc = D_MODEL // LANES
    h_refs = refs[:n_hc]
    (wcq_ref, wckv_ref, wkra_ref, wkrb_ref, gq_ref, gkv_ref,
     wqa_ref, wqb_ref, wka_ref, wv_ref, wdil_ref,
     cq_ref, sq_ref, ck_ref, sk_ref,
     q_ref, k_ref, v_ref, dil0_ref, dil1_ref, dil2_ref, hperm_ref) = refs[n_hc:]
    hb = jnp.concatenate([hc[...] for hc in h_refs], axis=1).astype(BF16)
    cqn = _rms_norm(_dot(hb, wcq_ref[...]), gq_ref[...]).astype(BF16)
    ckvn = _rms_norm(_dot(hb, wckv_ref[...]), gkv_ref[...]).astype(BF16)
    k_rope = _dot(hb, wkra_ref[...]) * ck_ref[...] + _dot(hb, wkrb_ref[...]) * sk_ref[...]
    ka = _dot(ckvn, wka_ref[...])
    for h in range(MLA_HEADS):
        sl = slice(h * LANES, (h + 1) * LANES)
        k_ref[0, h] = (ka[:, sl] + k_rope).astype(BF16)
    qa_t = _dot_tt(wqa_ref[...], cqn)
    qb_t = _dot_tt(wqb_ref[...], cqn)
    cq_t = cq_ref[...]
    sq_t = sq_ref[...]
    for h in range(MLA_HEADS):
        sl = slice(h * LANES, (h + 1) * LANES)
        q_ref[0, sl] = (qa_t[sl] * cq_t + qb_t[sl] * sq_t).astype(BF16)
    v_ref[0] = _dot_tt(wv_ref[...], ckvn).astype(BF16)

    dil_scale = DIL_HEAD_DIM ** -0.5 * LOG2_E
    tm = h_refs[0].shape[0]
    for g, (dil_ref, (_, dilation)) in enumerate(zip((dil0_ref, dil1_ref, dil2_ref), DIL_PAIRS)):
        rows_per = tm // dilation
        if dilation == 1:
            hp = hb
        else:
            for r in range(dilation):
                for c, hc in enumerate(h_refs):
                    hperm_ref[r * rows_per:(r + 1) * rows_per, c * LANES:(c + 1) * LANES] = hc[
                        pl.ds(r, rows_per, stride=dilation), :].astype(BF16)
            hp = hperm_ref[...]
        for part in range(3):
            c0 = (part * DIL_GROUPS + g) * DIL_OUT
            res = _dot(hp, wdil_ref[:, c0:c0 + DIL_OUT])
            if part == 0:
                res = res * dil_scale
            res = res.astype(BF16)
            for r in range(dilation):
                dil_ref[0, r, :, part * DIL_OUT:(part + 1) * DIL_OUT] = res[r * rows_per:(r + 1) * rows_per]


def _proj(h2d, weights, tables, *, batch, seq, tm):
    t, d = h2d.shape
    n_s = seq // tm

    def full(a):
        return pl.BlockSpec(a.shape, lambda i: (0,) * a.ndim)

    tab_spec = pl.BlockSpec((tm, LANES), lambda i: (i % n_s, 0))
    tab_t_spec = pl.BlockSpec((LANES, tm), lambda i: (0, i % n_s))
    head_map = lambda i: (i // n_s, 0, i % n_s, 0)
    feat_map = lambda i: (i // n_s, 0, i % n_s)
    dil_specs = [pl.BlockSpec((1, dl, tm // dl, 3 * DIL_OUT), head_map) for _, dl in DIL_PAIRS]
    dil_shapes = [jax.ShapeDtypeStruct((batch, dl, seq // dl, 3 * DIL_OUT), BF16) for _, dl in DIL_PAIRS]
    return pl.pallas_call(
        _proj_body,
        grid=(t // tm,),
        in_specs=[pl.BlockSpec((tm, LANES), functools.partial(lambda c, i: (i, c), c))
                  for c in range(d // LANES)]
        + [full(w) for w in weights] + [tab_t_spec] * 2 + [tab_spec] * 2,
        out_specs=[
            pl.BlockSpec((1, MLA_HEADS * LANES, tm), feat_map),
            pl.BlockSpec((1, MLA_HEADS, tm, LANES), head_map),
            pl.BlockSpec((1, MLA_OUT, tm), feat_map),
        ] + dil_specs,
        out_shape=[
            jax.ShapeDtypeStruct((batch, MLA_HEADS * LANES, seq), BF16),
            jax.ShapeDtypeStruct((batch, MLA_HEADS, seq, LANES), BF16),
            jax.ShapeDtypeStruct((batch, MLA_OUT, seq), BF16),
        ] + dil_shapes,
        scratch_shapes=[pltpu.VMEM((tm, d), BF16)],
        compiler_params=pltpu.CompilerParams(
            dimension_semantics=("parallel",), vmem_limit_bytes=VMEM_LIMIT),
        name="proj",
    )(*([h2d] * (d // LANES)), *weights, *tables)


def _mla_attn_body(q_ref, k_ref, vt_ref, o_ref, st_ref, m_ref, acc_ref, *, tq, tk):
    seq = k_ref.shape[2]
    n_tiles = seq // tq
    n_units = (seq // tk) * n_tiles
    heads = (0, 1)
    n_pieces = tk // MLA_PIECE

    def offsets(unit):
        if isinstance(unit, int):
            return (unit // n_tiles) * tk, (unit % n_tiles) * tq
        return (pl.multiple_of(lax.div(unit, n_tiles) * tk, tk),
                pl.multiple_of(lax.rem(unit, n_tiles) * tq, tq))

    def scores_piece(unit, buf, p, maxes):
        k_off, q_off = offsets(unit)
        rows = slice(p * MLA_PIECE, (p + 1) * MLA_PIECE)
        out = []
        for h in heads:
            st = _dot(k_ref[0, h, pl.ds(k_off + p * MLA_PIECE, MLA_PIECE), :],
                      q_ref[0, h * LANES:(h + 1) * LANES, pl.ds(q_off, tq)])
            st_ref[buf, h, rows] = st
            out.append(jnp.maximum(maxes[h], jnp.max(st, axis=0, keepdims=True)))
        return tuple(out)

    ones_rows = jnp.ones((MLA_DEN_ROWS, MLA_PIECE), BF16)

    def values_piece(k_off, buf, p, m_new, accs):
        rows = slice(p * MLA_PIECE, (p + 1) * MLA_PIECE)
        out = []
        for h in heads:
            pt = jnp.exp2(st_ref[buf, h, rows] - m_new[h]).astype(BF16)
            vtc = jnp.concatenate(
                [vt_ref[0, h * MLA_V:(h + 1) * MLA_V, pl.ds(k_off + p * MLA_PIECE, MLA_PIECE)], ones_rows],
                axis=0)
            out.append(accs[h] + _dot(vtc, pt))
        return tuple(out)

    neg_inf = jnp.full((1, tq), -jnp.inf, F32)

    def half(next_unit, next_buf, unit, buf, unit_max):
        k_off, q_off = offsets(unit)
        cols = pl.ds(q_off, tq)
        m_old = tuple(m_ref[h, :, cols] for h in heads)
        m_new = tuple(jnp.maximum(m_old[h], unit_max[h]) for h in heads)
        accs = tuple(jnp.exp2(m_old[h] - m_new[h]) * acc_ref[h, :, cols] for h in heads)
        next_max = (neg_inf, neg_inf)
        for p in range(n_pieces):
            if next_unit is not None:
                next_max = scores_piece(next_unit, next_buf, p, next_max)
            accs = values_piece(k_off, buf, p, m_new, accs)
        for h in heads:
            m_ref[h, :, cols] = m_new[h]
            acc_ref[h, :, cols] = accs[h]
        return next_max

    def pair_step(jj, max_a):
        u = 2 * jj
        max_b = half(u + 1, 1, u, 0, max_a)
        return half(u + 2, 0, u + 1, 1, max_b)

    m_ref[...] = jnp.full(m_ref.shape, -jnp.inf, F32)
    acc_ref[...] = jnp.zeros(acc_ref.shape, F32)
    max_0 = (neg_inf, neg_inf)
    for p in range(n_pieces):
        max_0 = scores_piece(0, 0, p, max_0)
    max_a = lax.fori_loop(0, n_units // 2 - 1, pair_step, max_0)
    max_b = half(n_units - 1, 1, n_units - 2, 0, max_a)
    half(None, None, n_units - 1, 1, max_b)

    @pl.loop(0, n_tiles)
    def _(tile):
        cols = pl.ds(pl.multiple_of(tile * tq, tq), tq)
        out_t = jnp.concatenate(
            [acc_ref[h, :MLA_V, cols] / acc_ref[h, MLA_V:MLA_V + 1, cols] for h in heads], axis=0)
        o_ref[0, cols, :] = out_t.T.astype(o_ref.dtype)


def _mla_attn(q, k, v, *, tq, tk):
    batch, heads, seq, _ = k.shape
    assert ((seq // tk) * (seq // tq)) % 2 == 0
    return pl.pallas_call(
        functools.partial(_mla_attn_body, tq=tq, tk=tk),
        grid=(batch, heads // 2),
        in_specs=[
            pl.BlockSpec((1, 2 * LANES, seq), lambda b, hp: (b, hp, 0)),
            pl.BlockSpec((1, 2, seq, LANES), lambda b, hp: (b, hp, 0, 0)),
            pl.BlockSpec((1, 2 * MLA_V, seq), lambda b, hp: (b, hp, 0)),
        ],
        out_specs=pl.BlockSpec((1, seq, LANES), lambda b, hp: (b, 0, hp)),
        out_shape=jax.ShapeDtypeStruct((batch, seq, MLA_OUT), BF16),
        scratch_shapes=[pltpu.VMEM((2, 2, tk, tq), F32),
                        pltpu.VMEM((2, 1, seq), F32),
                        pltpu.VMEM((2, MLA_V + MLA_DEN_ROWS, seq), F32)],
        compiler_params=pltpu.CompilerParams(
            dimension_semantics=("parallel", "parallel"), vmem_limit_bytes=VMEM_LIMIT),
        name="mla_attn",
    )(q, k, v)


def _dil_attn_body(slopes_ref, q_ref, kl_ref, km_ref, kr_ref, vl_ref, vm_ref, vr_ref,
                   o_ref, lse_ref, kbuf, vbuf, *, group, dilation, tq, sub_len):
    i = pl.program_id(1)
    n_res = q_ref.shape[1]
    first_res = pl.program_id(2) * n_res
    kbuf[:, 0:DIL_HALO] = kl_ref[0]
    kbuf[:, DIL_HALO:DIL_HALO + tq] = km_ref[0]
    kbuf[:, DIL_HALO + tq:] = kr_ref[0]
    vbuf[:, 0:DIL_HALO] = vl_ref[0]
    vbuf[:, DIL_HALO:DIL_HALO + tq] = vm_ref[0]
    vbuf[:, DIL_HALO + tq:] = vr_ref[0]

    win = DIL_SUB + 2 * DIL_HALO
    row = lax.broadcasted_iota(jnp.int32, (DIL_SUB, win), 0)
    col = lax.broadcasted_iota(jnp.int32, (DIL_SUB, win), 1)
    steps = jnp.abs(col - DIL_HALO - row)
    in_band = steps <= DIL_HALF_SPAN
    dist = (steps * dilation).astype(F32)
    bias = [jnp.where(in_band, (-LOG2_E * slopes_ref[group * DIL_HEADS_PER_GROUP + h]) * dist, -jnp.inf)
            for h in range(DIL_HEADS_PER_GROUP)]
    n_sub = tq // DIL_SUB
    for sub in range(n_sub):
        key_pos = i * tq + (sub * DIL_SUB - DIL_HALO) + col
        in_range = None
        if sub == 0:
            in_range = key_pos >= 0
        if sub == n_sub - 1:
            below = key_pos < sub_len
            in_range = below if in_range is None else in_range & below
        rows = slice(sub * DIL_SUB, (sub + 1) * DIL_SUB)
        wrows = slice(sub * DIL_SUB, sub * DIL_SUB + win)
        for rr in range(n_res):
            if dilation == 1:
                out_rows = pl.ds(sub * DIL_SUB, DIL_SUB)
            else:
                out_rows = pl.ds(sub * DIL_SUB * dilation + first_res + rr, DIL_SUB, stride=dilation)
            for h in range(DIL_HEADS_PER_GROUP):
                cols = slice(h * DIL_HEAD_DIM, (h + 1) * DIL_HEAD_DIM)
                s = _dot_nt(q_ref[0, rr, rows, cols], kbuf[rr, wrows, cols]) + bias[h]
                if in_range is not None:
                    s = jnp.where(in_range, s, -jnp.inf)
                m = jnp.max(s, axis=1, keepdims=True)
                e = jnp.exp2(s - m)
                den = jnp.sum(e, axis=1, keepdims=True)
                o_ref[0, h, out_rows, :] = _dot(e.astype(BF16), vbuf[rr, wrows, cols]) / den
                lse_ref[0, h, out_rows, :] = jnp.broadcast_to(
                    (m + jnp.log2(den)) * LN_2, (DIL_SUB, DIL_HEAD_DIM))


def _dil_attn(dil_g, slopes, *, group, dilation):
    batch, _, sub_len, _ = dil_g.shape
    seq = sub_len * dilation
    tq = min(512, sub_len, DIL_CHUNK_TOKENS // dilation)
    n_q = sub_len // tq
    n_res = min(dilation, DIL_ROWS_PER_STEP // tq)
    halo_per_tile = tq // DIL_HALO
    n_halo = sub_len // DIL_HALO
    q_col, k_col, v_col = 0, 1, 2

    def main(c):
        return pl.BlockSpec((1, n_res, tq, DIL_OUT), lambda b, i, r: (b, r, i, c))

    def left(c):
        return pl.BlockSpec(
            (1, n_res, DIL_HALO, DIL_OUT),
            lambda b, i, r: (b, r, jnp.maximum(i * halo_per_tile - 1, 0), c))

    def right(c):
        return pl.BlockSpec(
            (1, n_res, DIL_HALO, DIL_OUT),
            lambda b, i, r: (b, r, jnp.minimum((i + 1) * halo_per_tile, n_halo - 1), c))

    out_spec = pl.BlockSpec((1, DIL_HEADS_PER_GROUP, tq * dilation, DIL_HEAD_DIM),
                            lambda b, i, r: (b, 0, i, 0))
    out_shape = jax.ShapeDtypeStruct((batch, DIL_HEADS_PER_GROUP, seq, DIL_HEAD_DIM), F32)
    o, lse = pl.pallas_call(
        functools.partial(_dil_attn_body, group=group, dilation=dilation, tq=tq, sub_len=sub_len),
        grid=(batch, n_q, dilation // n_res),
        in_specs=[pl.BlockSpec(memory_space=pltpu.SMEM),
                  main(q_col), left(k_col), main(k_col), right(k_col),
                  left(v_col), main(v_col), right(v_col)],
        out_specs=[out_spec, out_spec],
        out_shape=[out_shape, out_shape],
        scratch_shapes=[pltpu.VMEM((n_res, tq + 2 * DIL_HALO, DIL_OUT), BF16),
                        pltpu.VMEM((n_res, tq + 2 * DIL_HALO, DIL_OUT), BF16)],
        compiler_params=pltpu.CompilerParams(
            dimension_semantics=("parallel", "parallel", "arbitrary"), vmem_limit_bytes=VMEM_LIMIT),
        name=f"dil_attn_g{group}",
    )(slopes, dil_g, dil_g, dil_g, dil_g, dil_g, dil_g, dil_g)
    return o, lse


def _mem_kv_body(mem_ref, w_ref, o_ref):
    o_ref[0] = _dot(mem_ref[0].astype(BF16), w_ref[...]).astype(BF16)


def _mem_kv(mem, w):
    batch, m_len, d = mem.shape
    n = w.shape[1]
    return pl.pallas_call(
        _mem_kv_body,
        grid=(batch,),
        in_specs=[pl.BlockSpec((1, m_len, d), lambda b: (b, 0, 0)),
                  pl.BlockSpec((d, n), lambda b: (0, 0))],
        out_specs=pl.BlockSpec((1, m_len, n), lambda b: (b, 0, 0)),
        out_shape=jax.ShapeDtypeStruct((batch, m_len, n), BF16),
        compiler_params=pltpu.CompilerParams(dimension_semantics=("parallel",)),
        name="mem_kv",
    )(mem, w)


def _merge_body(h_ref, omla_ref, od0_ref, od1_ref, od2_ref, ls0_ref, ls1_ref, ls2_ref, kvm_ref,
                wgate_ref, wmq_ref, wba_ref, wbb_ref, wbc_ref, wo_ref, g_ref, b_ref, out_ref, *, n_sub):
    sub_rows = h_ref.shape[0] // n_sub
    for sub in range(n_sub):
        rows = slice(sub * sub_rows, (sub + 1) * sub_rows)
        _merge_rows(rows, h_ref, omla_ref, od0_ref, od1_ref, od2_ref, ls0_ref, ls1_ref, ls2_ref, kvm_ref,
                    wgate_ref, wmq_ref, wba_ref, wbb_ref, wbc_ref, wo_ref, g_ref, b_ref, out_ref)


def _merge_rows(rows, h_ref, omla_ref, od0_ref, od1_ref, od2_ref, ls0_ref, ls1_ref, ls2_ref, kvm_ref,
                wgate_ref, wmq_ref, wba_ref, wbb_ref, wbc_ref, wo_ref, g_ref, b_ref, out_ref):
    h = h_ref[rows, :]
    hb = h.astype(BF16)

    def heads_to_lanes(ref):
        return jnp.concatenate([ref[0, hh, rows, :] for hh in range(DIL_HEADS_PER_GROUP)], axis=1)

    ls0, ls1, ls2 = heads_to_lanes(ls0_ref), heads_to_lanes(ls1_ref), heads_to_lanes(ls2_ref)
    mx = jnp.maximum(ls0, jnp.maximum(ls1, ls2))
    e0, e1, e2 = jnp.exp(ls0 - mx), jnp.exp(ls1 - mx), jnp.exp(ls2 - mx)
    o_dil = (e0 * heads_to_lanes(od0_ref) + e1 * heads_to_lanes(od1_ref)
             + e2 * heads_to_lanes(od2_ref)) / (e0 + e1 + e2)

    mq = (_dot(hb, wmq_ref[...]) * (MEM_HEAD_DIM ** -0.5)).astype(BF16)
    heads = []
    for hh in range(MEM_HEADS):
        kc = slice(hh * MEM_HEAD_DIM, (hh + 1) * MEM_HEAD_DIM)
        vc = slice(MEM_OUT + hh * MEM_HEAD_DIM, MEM_OUT + (hh + 1) * MEM_HEAD_DIM)
        s = _dot_nt(mq[:, kc], kvm_ref[0, :, kc])
        p = jnp.exp(s - jnp.max(s, axis=1, keepdims=True))
        heads.append(_dot(p.astype(BF16), kvm_ref[0, :, vc]) / jnp.sum(p, axis=1, keepdims=True))
    o_mem = jnp.concatenate(heads, axis=1)

    y_a = _dot(omla_ref[rows, :], wba_ref[...])
    y_b = _dot(o_dil.astype(BF16), wbb_ref[...])
    y_c = _dot(o_mem.astype(BF16), wbc_ref[...])
    d = h.shape[1]
    merged = (jax.nn.sigmoid(_dot(hb, wgate_ref[:, :d])) * y_a
              + jax.nn.sigmoid(_dot(hb, wgate_ref[:, d:2 * d])) * y_b
              + jax.nn.sigmoid(_dot(hb, wgate_ref[:, 2 * d:])) * y_c)
    mix = _dot(merged.astype(BF16), wo_ref[...])
    out_ref[rows, :] = _layer_norm(ALPHA * h + mix, g_ref[...], b_ref[...])


def _merge(h2d, o_mla, o_dil, lse_dil, kv_mem, weights, g, b, *, seq, tm, n_sub):
    t, d = h2d.shape
    n_s = seq // tm

    def rows(width):
        return pl.BlockSpec((tm, width), lambda i: (i, 0))

    def full(a):
        return pl.BlockSpec(a.shape, lambda i: (0,) * a.ndim, pipeline_mode=pl.Buffered(1))

    m_len, kv_cols = kv_mem.shape[1:]
    return pl.pallas_call(
        functools.partial(_merge_body, n_sub=n_sub),
        grid=(t // tm,),
        in_specs=[rows(d), rows(MLA_OUT)]
        + [pl.BlockSpec((1, DIL_HEADS_PER_GROUP, tm, DIL_HEAD_DIM),
                        lambda i: (i // n_s, 0, i % n_s, 0))] * 6
        + [pl.BlockSpec((1, m_len, kv_cols), lambda i: (i // n_s, 0, 0))]
        + [full(w) for w in weights] + [full(g), full(b)],
        out_specs=rows(d),
        out_shape=jax.ShapeDtypeStruct((t, d), F32),
        compiler_params=pltpu.CompilerParams(
            dimension_semantics=("parallel",), vmem_limit_bytes=VMEM_LIMIT),
        name="merge",
    )(h2d, o_mla, *o_dil, *lse_dil, kv_mem, *weights, g, b)


def _swap_halves(w):
    half = w.shape[-1] // 2
    return jnp.concatenate([w[..., half:], w[..., :half]], axis=-1)


def _prep_mla_weights(w_in, w_uq, w_ukv):
    d = w_in.shape[0]
    o_q, o_kv, o_kr = 0, MLA_Q_LORA, MLA_Q_LORA + MLA_KV_LORA
    w_cq = w_in[:, o_q:o_q + MLA_Q_LORA]
    w_ckv = w_in[:, o_kv:o_kv + MLA_KV_LORA]
    w_kr = w_in[:, o_kr:o_kr + MLA_ROPE]
    pad_tail = LANES - MLA_NOPE - MLA_ROPE
    z_nope = jnp.zeros((d, MLA_NOPE), F32)
    z_tail = jnp.zeros((d, pad_tail), F32)
    w_kra = jnp.concatenate([z_nope, w_kr, z_tail], axis=1)
    w_krb = jnp.concatenate([z_nope, _swap_halves(w_kr), z_tail], axis=1)

    uq = w_uq.reshape(MLA_Q_LORA, MLA_HEADS, MLA_NOPE + MLA_ROPE)
    uq_nope, uq_pe = uq[..., :MLA_NOPE], uq[..., MLA_NOPE:]
    zq_nope = jnp.zeros_like(uq_nope)
    zq_tail = jnp.zeros((MLA_Q_LORA, MLA_HEADS, pad_tail), F32)
    w_qa = jnp.concatenate([uq_nope, uq_pe, zq_tail], axis=-1).reshape(MLA_Q_LORA, MLA_HEADS * LANES)
    w_qb = jnp.concatenate([zq_nope, _swap_halves(uq_pe), zq_tail], axis=-1).reshape(
        MLA_Q_LORA, MLA_HEADS * LANES)

    ukv = w_ukv.reshape(MLA_KV_LORA, MLA_HEADS, MLA_NOPE + MLA_V)
    uk, uv = ukv[..., :MLA_NOPE], ukv[..., MLA_NOPE:]
    w_ka = jnp.concatenate(
        [uk, jnp.zeros((MLA_KV_LORA, MLA_HEADS, LANES - MLA_NOPE), F32)], axis=-1).reshape(
            MLA_KV_LORA, MLA_HEADS * LANES)
    w_v = uv.reshape(MLA_KV_LORA, MLA_HEADS * MLA_V)
    return [w.astype(BF16) for w in (w_cq, w_ckv, w_kra, w_krb)], [w.astype(BF16) for w in (w_qa, w_qb, w_ka, w_v)]


def _rope_tables(seq):
    pos = jnp.arange(seq, dtype=F32)
    inv = 1.0 / (ROPE_THETA ** (jnp.arange(0, MLA_ROPE, 2, dtype=F32) / MLA_ROPE))
    pad_tail = LANES - MLA_NOPE - MLA_ROPE
    q_scale = (MLA_NOPE + MLA_ROPE) ** -0.5 * LOG2_E
    ang = pos[:, None] * inv[None, :]
    cos_sin = (jnp.cos(ang), jnp.sin(ang))

    def tables(cos_sin, axis, lead_one):
        cos, sin = cos_sin
        shape = lambda n: (n, seq) if axis == 0 else (seq, n)
        lead = jnp.full(shape(MLA_NOPE), lead_one, F32)
        c = jnp.concatenate([lead, cos, cos, jnp.zeros(shape(pad_tail), F32)], axis=axis)
        s = jnp.concatenate([jnp.zeros(shape(MLA_NOPE), F32), -sin, sin, jnp.zeros(shape(pad_tail), F32)],
                            axis=axis)
        return c, s

    c_q, s_q = tables([t.T for t in cos_sin], 0, 1.0)
    c_k, s_k = tables(cos_sin, 1, 0.0)
    return c_q * q_scale, s_q * q_scale, c_k, s_k


def kernel(x, mem, w_in, mla_q_norm, mla_kv_norm, w_uq, w_ukv, w_mem_kv, w_br_mla, w_br_dil, w_br_mem, w_o,
           ffn1_w_gate, ffn1_w_up, ffn1_w_down, ffn2_w_gate, ffn2_w_up, ffn2_w_down,
           ln1_g, ln1_b, ln2_g, ln2_b, ln3_g, ln3_b):
    batch, seq, d = x.shape
    t = batch * seq
    tm_ffn = min(1024, t)
    tm_proj = min(512, seq)
    tm_merge = min(512, seq)
    tables = _rope_tables(seq)
    slopes = 2.0 ** (-8.0 * jnp.arange(1, DIL_HEADS + 1, dtype=F32) / DIL_HEADS)

    h = x.reshape(t, d)
    for l in range(DEPTH):
        bf = lambda w: w[l].astype(BF16)
        row = lambda v: v[l].reshape(1, -1)
        h = _ffn_ln(h, bf(ffn1_w_gate), bf(ffn1_w_up), bf(ffn1_w_down), row(ln1_g), row(ln1_b),
                    tm=tm_ffn, n_sub=FFN_SUB_TILES)

        w_in_l = w_in[l]
        o_dil_cols = MLA_Q_LORA + MLA_KV_LORA + MLA_ROPE
        o_memq = o_dil_cols + DIL_QKV
        o_gate = o_memq + MEM_OUT
        w_c, w_u = _prep_mla_weights(w_in_l, w_uq[l], w_ukv[l])
        proj_weights = w_c + [row(mla_q_norm), row(mla_kv_norm)] + w_u + [
            w_in_l[:, o_dil_cols:o_memq].astype(BF16)]
        q, k, v, *dil = _proj(h, proj_weights, tables, batch=batch, seq=seq, tm=tm_proj)

        o_mla = _mla_attn(q, k, v, tq=min(512, seq), tk=min(1024, seq)).reshape(t, MLA_OUT)
        dil_parts = [_dil_attn(dil[g], slopes, group=g, dilation=dl)
                     for g, (_, dl) in enumerate(DIL_PAIRS)]
        kv_mem = _mem_kv(mem, bf(w_mem_kv))

        merge_weights = [w_in_l[:, o_gate:].astype(BF16), w_in_l[:, o_memq:o_gate].astype(BF16),
                         bf(w_br_mla), bf(w_br_dil), bf(w_br_mem), bf(w_o)]
        h = _merge(h, o_mla, [p[0] for p in dil_parts], [p[1] for p in dil_parts], kv_mem,
                   merge_weights, row(ln2_g), row(ln2_b), seq=seq, tm=tm_merge, n_sub=2)

        h = _ffn_ln(h, bf(ffn2_w_gate), bf(ffn2_w_up), bf(ffn2_w_down), row(ln3_g), row(ln3_b),
                    tm=tm_ffn, n_sub=FFN_SUB_TILES)
    return h.reshape(batch, seq, d)
```

```python
import functools
import math

import jax
import jax.numpy as jnp
from jax import lax
from jax.experimental import pallas as pl
from jax.experimental.pallas import tpu as pltpu

F32 = jnp.float32
BF16 = jnp.bfloat16

D_MODEL = 1024
DEPTH = 1
MLA_HEADS = 8
MLA_Q_LORA = 256
MLA_KV_LORA = 256
MLA_NOPE = 64
MLA_ROPE = 32
MLA_V = 64
ROPE_THETA = 10000.0
DIL_PAIRS = ((128, 1), (512, 4), (2048, 16))
DIL_GROUPS = 3
DIL_HEADS_PER_GROUP = 4
DIL_HEAD_DIM = 128
DIL_HEADS = DIL_GROUPS * DIL_HEADS_PER_GROUP
MEM_HEADS = 4
MEM_HEAD_DIM = 128
EPS = 1e-5
ALPHA = (2 * DEPTH) ** 0.25

LOG2_E = math.log2(math.e)
LANES = 128
DIL_OUT = DIL_HEADS_PER_GROUP * DIL_HEAD_DIM
DIL_QKV = 3 * DIL_HEADS * DIL_HEAD_DIM
MEM_OUT = MEM_HEADS * MEM_HEAD_DIM
MLA_OUT = MLA_HEADS * MLA_V
MLA_DEN_ROWS = 16
MLA_PIECE = 256
DIL_HALF_SPAN = 64
DIL_HALO = 64
DIL_SUB = 128
DIL_CHUNK_TOKENS = 2048
DIL_ROWS_PER_STEP = 512
FFN_SUB_TILES = 4
VMEM_LIMIT = 56 * 1024 * 1024

assert all(w // 2 // d == DIL_HALF_SPAN for w, d in DIL_PAIRS)

_NT = (((1,), (1,)), ((), ()))


def _dot(a, b):
    return jnp.dot(a, b, preferred_element_type=F32)


def _dot_nt(a, b):
    return lax.dot_general(a, b, _NT, preferred_element_type=F32)


def _dot_tt(w, x):
    return lax.dot_general(w, x, (((0,), (1,)), ((), ())), preferred_element_type=F32)


def _layer_norm(y, g, b):
    mu = jnp.mean(y, axis=-1, keepdims=True)
    yc = y - mu
    var = jnp.mean(yc * yc, axis=-1, keepdims=True)
    return yc * lax.rsqrt(var + EPS) * g + b


def _rms_norm(y, g):
    return y * lax.rsqrt(jnp.mean(y * y, axis=-1, keepdims=True) + EPS) * g


def _ffn_ln_body(x_ref, wg_ref, wu_ref, wd_ref, g_ref, b_ref, o_ref, *, n_sub):
    sub_rows = x_ref.shape[0] // n_sub
    for sub in range(n_sub):
        rows = slice(sub * sub_rows, (sub + 1) * sub_rows)
        x = x_ref[rows, :]
        xb = x.astype(BF16)
        gate = _dot(xb, wg_ref[...])
        up = _dot(xb, wu_ref[...])
        act = (gate * jax.nn.sigmoid(gate) * up).astype(BF16)
        y = ALPHA * x + 0.5 * _dot(act, wd_ref[...])
        o_ref[rows, :] = _layer_norm(y, g_ref[...], b_ref[...])


def _ffn_ln(x2d, wg, wu, wd, g, b, *, tm, n_sub):
    t, d = x2d.shape

    def full(a):
        return pl.BlockSpec(a.shape, lambda i: (0,) * a.ndim, pipeline_mode=pl.Buffered(1))

    return pl.pallas_call(
        functools.partial(_ffn_ln_body, n_sub=n_sub),
        grid=(t // tm,),
        in_specs=[pl.BlockSpec((tm, d), lambda i: (i, 0))] + [full(a) for a in (wg, wu, wd, g, b)],
        out_specs=pl.BlockSpec((tm, d), lambda i: (i, 0)),
        out_shape=jax.ShapeDtypeStruct((t, d), F32),
        compiler_params=pltpu.CompilerParams(
            dimension_semantics=("parallel",), vmem_limit_bytes=VMEM_LIMIT),
        name="ffn_ln",
    )(x2d, wg, wu, wd, g, b)


def _proj_body(*refs):
    n_hc = D_MODEL // LANES
    h_refs = refs[:n_hc]
    (wcq_ref, wckv_ref, wkra_ref, wkrb_ref, gq_ref, gkv_ref,
     wqa_ref, wqb_ref, wka_ref, wv_ref, wdil_ref,
     cq_ref, sq_ref, ck_ref, sk_ref,
     q_ref, k_ref, v_ref, dil0_ref, dil1_ref, dil2_ref, hperm_ref) = refs[n_hc:]
    hb = jnp.concatenate([hc[...] for hc in h_refs], axis=1).astype(BF16)
    cqn = _rms_norm(_dot(hb, wcq_ref[...]), gq_ref[...]).astype(BF16)
    ckvn = _rms_norm(_dot(hb, wckv_ref[...]), gkv_ref[...]).astype(BF16)
    k_rope = _dot(hb, wkra_ref[...]) * ck_ref[...] + _dot(hb, wkrb_ref[...]) * sk_ref[...]
    ka = _dot(ckvn, wka_ref[...])
    for h in range(MLA_HEADS):
        sl = slice(h * LANES, (h + 1) * LANES)
        k_ref[0, h] = (ka[:, sl] + k_rope).astype(BF16)
    qa_t = _dot_tt(wqa_ref[...], cqn)
    qb_t = _dot_tt(wqb_ref[...], cqn)
    cq_t = cq_ref[...]
    sq_t = sq_ref[...]
    for h in range(MLA_HEADS):
        sl = slice(h * LANES, (h + 1) * LANES)
        q_ref[0, sl] = (qa_t[sl] * cq_t + qb_t[sl] * sq_t).astype(BF16)
    v_ref[0] = _dot_tt(wv_ref[...], ckvn).astype(BF16)

    dil_scale = DIL_HEAD_DIM ** -0.5
    tm = h_refs[0].shape[0]
    for g, (dil_ref, (_, dilation)) in enumerate(zip((dil0_ref, dil1_ref, dil2_ref), DIL_PAIRS)):
        rows_per = tm // dilation
        if dilation == 1:
            hp = hb
        else:
            for r in range(dilation):
                for c, hc in enumerate(h_refs):
                    hperm_ref[r * rows_per:(r + 1) * rows_per, c * LANES:(c + 1) * LANES] = hc[
                        pl.ds(r, rows_per, stride=dilation), :].astype(BF16)
            hp = hperm_ref[...]
        for part in range(3):
            c0 = (part * DIL_GROUPS + g) * DIL_OUT
            res = _dot(hp, wdil_ref[:, c0:c0 + DIL_OUT])
            if part == 0:
                res = res * dil_scale
            res = res.astype(BF16)
            for r in range(dilation):
                dil_ref[0, r, :, part * DIL_OUT:(part + 1) * DIL_OUT] = res[r * rows_per:(r + 1) * rows_per]


def _proj(h2d, weights, tables, *, batch, seq, tm):
    t, d = h2d.shape
    n_s = seq // tm

    def full(a):
        return pl.BlockSpec(a.shape, lambda i: (0,) * a.ndim)

    tab_spec = pl.BlockSpec((tm, LANES), lambda i: (i % n_s, 0))
    tab_t_spec = pl.BlockSpec((LANES, tm), lambda i: (0, i % n_s))
    head_map = lambda i: (i // n_s, 0, i % n_s, 0)
    feat_map = lambda i: (i // n_s, 0, i % n_s)
    dil_specs = [pl.BlockSpec((1, dl, tm // dl, 3 * DIL_OUT), head_map) for _, dl in DIL_PAIRS]
    dil_shapes = [jax.ShapeDtypeStruct((batch, dl, seq // dl, 3 * DIL_OUT), BF16) for _, dl in DIL_PAIRS]
    return pl.pallas_call(
        _proj_body,
        grid=(t // tm,),
        in_specs=[pl.BlockSpec((tm, LANES), functools.partial(lambda c, i: (i, c), c))
                  for c in range(d // LANES)]
        + [full(w) for w in weights] + [tab_t_spec] * 2 + [tab_spec] * 2,
        out_specs=[
            pl.BlockSpec((1, MLA_HEADS * LANES, tm), feat_map),
            pl.BlockSpec((1, MLA_HEADS, tm, LANES), head_map),
            pl.BlockSpec((1, MLA_OUT, tm), feat_map),
        ] + dil_specs,
        out_shape=[
            jax.ShapeDtypeStruct((batch, MLA_HEADS * LANES, seq), BF16),
            jax.ShapeDtypeStruct((batch, MLA_HEADS, seq, LANES), BF16),
            jax.ShapeDtypeStruct((batch, MLA_OUT, seq), BF16),
        ] + dil_shapes,
        scratch_shapes=[pltpu.VMEM((tm, d), BF16)],
        compiler_params=pltpu.CompilerParams(
            dimension_semantics=("parallel",), vmem_limit_bytes=VMEM_LIMIT),
        name="proj",
    )(*([h2d] * (d // LANES)), *weights, *tables)


def _mla_attn_body(q_ref, k_ref, vt_ref, o_ref, st_ref, m_ref, acc_ref, *, tq, tk):
    seq = k_ref.shape[2]
    n_tiles = seq // tq
    n_units = (seq // tk) * n_tiles
    heads = (0, 1)
    n_pieces = tk // MLA_PIECE

    def offsets(unit):
        if isinstance(unit, int):
            return (unit // n_tiles) * tk, (unit % n_tiles) * tq
        return (pl.multiple_of(lax.div(unit, n_tiles) * tk, tk),
                pl.multiple_of(lax.rem(unit, n_tiles) * tq, tq))

    def scores_piece(unit, buf, p, h, run_max):
        k_off, q_off = offsets(unit)
        n_keys = 2 * MLA_PIECE
        st = _dot(k_ref[0, h, pl.ds(k_off + p * n_keys, n_keys), :],
                  q_ref[0, h * LANES:(h + 1) * LANES, pl.ds(q_off, tq)])
        st_ref[buf, h, p * n_keys:(p + 1) * n_keys] = st
        return jnp.maximum(run_max, jnp.max(st, axis=0, keepdims=True))

    ones_rows = jnp.ones((MLA_DEN_ROWS, MLA_PIECE), BF16)

    def values_piece(k_off, buf, p, m_new, accs):
        rows = slice(p * MLA_PIECE, (p + 1) * MLA_PIECE)
        out = []
        for h in heads:
            pt = jnp.exp2(st_ref[buf, h, rows] - m_new[h]).astype(BF16)
            vtc = jnp.concatenate(
                [vt_ref[0, h * MLA_V:(h + 1) * MLA_V, pl.ds(k_off + p * MLA_PIECE, MLA_PIECE)], ones_rows],
                axis=0)
            out.append(accs[h] + _dot(vtc, pt))
        return tuple(out)

    neg_inf = jnp.full((1, tq), -jnp.inf, F32)

    def half(next_unit, next_buf, unit, buf, unit_max):
        k_off, q_off = offsets(unit)
        cols = pl.ds(q_off, tq)
        m_old = tuple(m_ref[h, :, cols] for h in heads)
        m_new = tuple(jnp.maximum(m_old[h], unit_max[h]) for h in heads)
        accs = tuple(jnp.exp2(m_old[h] - m_new[h]) * acc_ref[h, :, cols] for h in heads)
        next_max = [neg_inf, neg_inf]
        for p in range(n_pieces):
            if next_unit is not None:
                next_max[p % 2] = scores_piece(next_unit, next_buf, p // 2, p % 2, next_max[p % 2])
            accs = values_piece(k_off, buf, p, m_new, accs)
        for h in heads:
            m_ref[h, :, cols] = m_new[h]
            acc_ref[h, :, cols] = accs[h]
        return tuple(next_max)

    def pair_step(jj, max_a):
        u = 2 * jj
        max_b = half(u + 1, 1, u, 0, max_a)
        return half(u + 2, 0, u + 1, 1, max_b)

    m_ref[...] = jnp.full(m_ref.shape, -jnp.inf, F32)
    acc_ref[...] = jnp.zeros(acc_ref.shape, F32)
    max_0 = [neg_inf, neg_inf]
    for p in range(n_pieces):
        max_0[p % 2] = scores_piece(0, 0, p // 2, p % 2, max_0[p % 2])
    max_a = lax.fori_loop(0, n_units // 2 - 1, pair_step, tuple(max_0))
    max_b = half(n_units - 1, 1, n_units - 2, 0, max_a)
    half(None, None, n_units - 1, 1, max_b)

    @pl.loop(0, n_tiles)
    def _(tile):
        cols = pl.ds(pl.multiple_of(tile * tq, tq), tq)
        out_t = jnp.concatenate(
            [acc_ref[h, :MLA_V, cols] / acc_ref[h, MLA_V:MLA_V + 1, cols] for h in heads], axis=0)
        o_ref[0, cols, :] = out_t.T.astype(o_ref.dtype)


def _mla_attn(q, k, v, *, tq, tk):
    batch, heads, seq, _ = k.shape
    assert ((seq // tk) * (seq // tq)) % 2 == 0
    return pl.pallas_call(
        functools.partial(_mla_attn_body, tq=tq, tk=tk),
        grid=(batch, heads // 2),
        in_specs=[
            pl.BlockSpec((1, 2 * LANES, seq), lambda b, hp: (b, hp, 0)),
            pl.BlockSpec((1, 2, seq, LANES), lambda b, hp: (b, hp, 0, 0)),
            pl.BlockSpec((1, 2 * MLA_V, seq), lambda b, hp: (b, hp, 0)),
        ],
        out_specs=pl.BlockSpec((1, seq, LANES), lambda b, hp: (b, 0, hp)),
        out_shape=jax.ShapeDtypeStruct((batch, seq, MLA_OUT), BF16),
        scratch_shapes=[pltpu.VMEM((2, 2, tk, tq), F32),
                        pltpu.VMEM((2, 1, seq), F32),
                        pltpu.VMEM((2, MLA_V + MLA_DEN_ROWS, seq), F32)],
        compiler_params=pltpu.CompilerParams(
            dimension_semantics=("parallel", "parallel"), vmem_limit_bytes=VMEM_LIMIT),
        name="mla_attn",
    )(q, k, v)


def _dil_attn_body(slopes_ref, q_ref, kl_ref, km_ref, kr_ref, vl_ref, vm_ref, vr_ref,
                   o_ref, lse_ref, kbuf, vbuf, *, group, dilation, tq, sub_len):
    i = pl.program_id(1)
    n_res = q_ref.shape[1]
    first_res = pl.program_id(2) * n_res
    kbuf[:, 0:DIL_HALO] = kl_ref[0]
    kbuf[:, DIL_HALO:DIL_HALO + tq] = km_ref[0]
    kbuf[:, DIL_HALO + tq:] = kr_ref[0]
    vbuf[:, 0:DIL_HALO] = vl_ref[0]
    vbuf[:, DIL_HALO:DIL_HALO + tq] = vm_ref[0]
    vbuf[:, DIL_HALO + tq:] = vr_ref[0]

    win = DIL_SUB + 2 * DIL_HALO
    row = lax.broadcasted_iota(jnp.int32, (DIL_SUB, win), 0)
    col = lax.broadcasted_iota(jnp.int32, (DIL_SUB, win), 1)
    steps = jnp.abs(col - DIL_HALO - row)
    in_band = steps <= DIL_HALF_SPAN
    dist = (steps * dilation).astype(F32)
    for sub in range(tq // DIL_SUB):
        key_pos = i * tq + (sub * DIL_SUB - DIL_HALO) + col
        valid = in_band & (key_pos >= 0) & (key_pos < sub_len)
        rows = slice(sub * DIL_SUB, (sub + 1) * DIL_SUB)
        wrows = slice(sub * DIL_SUB, sub * DIL_SUB + win)
        for rr in range(n_res):
            if dilation == 1:
                out_rows = pl.ds(sub * DIL_SUB, DIL_SUB)
            else:
                out_rows = pl.ds(sub * DIL_SUB * dilation + first_res + rr, DIL_SUB, stride=dilation)
            for h in range(DIL_HEADS_PER_GROUP):
                cols = slice(h * DIL_HEAD_DIM, (h + 1) * DIL_HEAD_DIM)
                slope = slopes_ref[group * DIL_HEADS_PER_GROUP + h]
                s = _dot_nt(q_ref[0, rr, rows, cols], kbuf[rr, wrows, cols]) - slope * dist
                s = jnp.where(valid, s, -jnp.inf)
                m = jnp.max(s, axis=1, keepdims=True)
                e = jnp.exp(s - m)
                den = jnp.sum(e, axis=1, keepdims=True)
                o_ref[0, h, out_rows, :] = _dot(e.astype(BF16), vbuf[rr, wrows, cols]) / den
                lse_ref[0, h, out_rows, :] = jnp.broadcast_to(m + jnp.log(den), (DIL_SUB, DIL_HEAD_DIM))


def _dil_attn(dil_g, slopes, *, group, dilation):
    batch, _, sub_len, _ = dil_g.shape
    seq = sub_len * dilation
    tq = min(512, sub_len, DIL_CHUNK_TOKENS // dilation)
    n_q = sub_len // tq
    n_res = min(dilation, DIL_ROWS_PER_STEP // tq)
    halo_per_tile = tq // DIL_HALO
    n_halo = sub_len // DIL_HALO
    q_col, k_col, v_col = 0, 1, 2

    def main(c):
        return pl.BlockSpec((1, n_res, tq, DIL_OUT), lambda b, i, r: (b, r, i, c))

    def left(c):
        return pl.BlockSpec(
            (1, n_res, DIL_HALO, DIL_OUT),
            lambda b, i, r: (b, r, jnp.maximum(i * halo_per_tile - 1, 0), c))

    def right(c):
        return pl.BlockSpec(
            (1, n_res, DIL_HALO, DIL_OUT),
            lambda b, i, r: (b, r, jnp.minimum((i + 1) * halo_per_tile, n_halo - 1), c))

    out_spec = pl.BlockSpec((1, DIL_HEADS_PER_GROUP, tq * dilation, DIL_HEAD_DIM),
                            lambda b, i, r: (b, 0, i, 0))
    out_shape = jax.ShapeDtypeStruct((batch, DIL_HEADS_PER_GROUP, seq, DIL_HEAD_DIM), F32)
    o, lse = pl.pallas_call(
        functools.partial(_dil_attn_body, group=group, dilation=dilation, tq=tq, sub_len=sub_len),
        grid=(batch, n_q, dilation // n_res),
        in_specs=[pl.BlockSpec(memory_space=pltpu.SMEM),
                  main(q_col), left(k_col), main(k_col), right(k_col),
                  left(v_col), main(v_col), right(v_col)],
        out_specs=[out_spec, out_spec],
        out_shape=[out_shape, out_shape],
        scratch_shapes=[pltpu.VMEM((n_res, tq + 2 * DIL_HALO, DIL_OUT), BF16),
                        pltpu.VMEM((n_res, tq + 2 * DIL_HALO, DIL_OUT), BF16)],
        compiler_params=pltpu.CompilerParams(
            dimension_semantics=("parallel", "parallel", "arbitrary"), vmem_limit_bytes=VMEM_LIMIT),
        name=f"dil_attn_g{group}",
    )(slopes, dil_g, dil_g, dil_g, dil_g, dil_g, dil_g, dil_g)
    return o, lse


def _mem_kv_body(mem_ref, w_ref, o_ref):
    o_ref[0] = _dot(mem_ref[0].astype(BF16), w_ref[...]).astype(BF16)


def _mem_kv(mem, w):
    batch, m_len, d = mem.shape
    n = w.shape[1]
    return pl.pallas_call(
        _mem_kv_body,
        grid=(batch,),
        in_specs=[pl.BlockSpec((1, m_len, d), lambda b: (b, 0, 0)),
                  pl.BlockSpec((d, n), lambda b: (0, 0))],
        out_specs=pl.BlockSpec((1, m_len, n), lambda b: (b, 0, 0)),
        out_shape=jax.ShapeDtypeStruct((batch, m_len, n), BF16),
        compiler_params=pltpu.CompilerParams(dimension_semantics=("parallel",)),
        name="mem_kv",
    )(mem, w)


def _merge_body(h_ref, omla_ref, od0_ref, od1_ref, od2_ref, ls0_ref, ls1_ref, ls2_ref, kvm_ref,
                wgate_ref, wmq_ref, wba_ref, wbb_ref, wbc_ref, wo_ref, g_ref, b_ref, out_ref, *, n_sub):
    sub_rows = h_ref.shape[0] // n_sub
    for sub in range(n_sub):
        rows = slice(sub * sub_rows, (sub + 1) * sub_rows)
        _merge_rows(rows, h_ref, omla_ref, od0_ref, od1_ref, od2_ref, ls0_ref, ls1_ref, ls2_ref, kvm_ref,
                    wgate_ref, wmq_ref, wba_ref, wbb_ref, wbc_ref, wo_ref, g_ref, b_ref, out_ref)


def _merge_rows(rows, h_ref, omla_ref, od0_ref, od1_ref, od2_ref, ls0_ref, ls1_ref, ls2_ref, kvm_ref,
                wgate_ref, wmq_ref, wba_ref, wbb_ref, wbc_ref, wo_ref, g_ref, b_ref, out_ref):
    h = h_ref[rows, :]
    hb = h.astype(BF16)

    def heads_to_lanes(ref):
        return jnp.concatenate([ref[0, hh, rows, :] for hh in range(DIL_HEADS_PER_GROUP)], axis=1)

    ls0, ls1, ls2 = heads_to_lanes(ls0_ref), heads_to_lanes(ls1_ref), heads_to_lanes(ls2_ref)
    mx = jnp.maximum(ls0, jnp.maximum(ls1, ls2))
    e0, e1, e2 = jnp.exp(ls0 - mx), jnp.exp(ls1 - mx), jnp.exp(ls2 - mx)
    o_dil = (e0 * heads_to_lanes(od0_ref) + e1 * heads_to_lanes(od1_ref)
             + e2 * heads_to_lanes(od2_ref)) / (e0 + e1 + e2)

    mq = (_dot(hb, wmq_ref[...]) * (MEM_HEAD_DIM ** -0.5)).astype(BF16)
    heads = []
    for hh in range(MEM_HEADS):
        kc = slice(hh * MEM_HEAD_DIM, (hh + 1) * MEM_HEAD_DIM)
        vc = slice(MEM_OUT + hh * MEM_HEAD_DIM, MEM_OUT + (hh + 1) * MEM_HEAD_DIM)
        s = _dot_nt(mq[:, kc], kvm_ref[0, :, kc])
        p = jnp.exp(s - jnp.max(s, axis=1, keepdims=True))
        heads.append(_dot(p.astype(BF16), kvm_ref[0, :, vc]) / jnp.sum(p, axis=1, keepdims=True))
    o_mem = jnp.concatenate(heads, axis=1)

    y_a = _dot(omla_ref[rows, :], wba_ref[...])
    y_b = _dot(o_dil.astype(BF16), wbb_ref[...])
    y_c = _dot(o_mem.astype(BF16), wbc_ref[...])
    d = h.shape[1]
    merged = (jax.nn.sigmoid(_dot(hb, wgate_ref[:, :d])) * y_a
              + jax.nn.sigmoid(_dot(hb, wgate_ref[:, d:2 * d])) * y_b
              + jax.nn.sigmoid(_dot(hb, wgate_ref[:, 2 * d:])) * y_c)
    mix = _dot(merged.astype(BF16), wo_ref[...])
    out_ref[rows, :] = _layer_norm(ALPHA * h + mix, g_ref[...], b_ref[...])


def _merge(h2d, o_mla, o_dil, lse_dil, kv_mem, weights, g, b, *, seq, tm, n_sub):
    t, d = h2d.shape
    n_s = seq // tm

    def rows(width):
        return pl.BlockSpec((tm, width), lambda i: (i, 0))

    def full(a):
        return pl.BlockSpec(a.shape, lambda i: (0,) * a.ndim, pipeline_mode=pl.Buffered(1))

    m_len, kv_cols = kv_mem.shape[1:]
    return pl.pallas_call(
        functools.partial(_merge_body, n_sub=n_sub),
        grid=(t // tm,),
        in_specs=[rows(d), rows(MLA_OUT)]
        + [pl.BlockSpec((1, DIL_HEADS_PER_GROUP, tm, DIL_HEAD_DIM),
                        lambda i: (i // n_s, 0, i % n_s, 0))] * 6
        + [pl.BlockSpec((1, m_len, kv_cols), lambda i: (i // n_s, 0, 0))]
        + [full(w) for w in weights] + [full(g), full(b)],
        out_specs=rows(d),
        out_shape=jax.ShapeDtypeStruct((t, d), F32),
        compiler_params=pltpu.CompilerParams(
            dimension_semantics=("parallel",), vmem_limit_bytes=VMEM_LIMIT),
        name="merge",
    )(h2d, o_mla, *o_dil, *lse_dil, kv_mem, *weights, g, b)


def _swap_halves(w):
    half = w.shape[-1] // 2
    return jnp.concatenate([w[..., half:], w[..., :half]], axis=-1)


def _prep_mla_weights(w_in, w_uq, w_ukv):
    d = w_in.shape[0]
    o_q, o_kv, o_kr = 0, MLA_Q_LORA, MLA_Q_LORA + MLA_KV_LORA
    w_cq = w_in[:, o_q:o_q + MLA_Q_LORA]
    w_ckv = w_in[:, o_kv:o_kv + MLA_KV_LORA]
    w_kr = w_in[:, o_kr:o_kr + MLA_ROPE]
    pad_tail = LANES - MLA_NOPE - MLA_ROPE
    z_nope = jnp.zeros((d, MLA_NOPE), F32)
    z_tail = jnp.zeros((d, pad_tail), F32)
    w_kra = jnp.concatenate([z_nope, w_kr, z_tail], axis=1)
    w_krb = jnp.concatenate([z_nope, _swap_halves(w_kr), z_tail], axis=1)

    uq = w_uq.reshape(MLA_Q_LORA, MLA_HEADS, MLA_NOPE + MLA_ROPE)
    uq_nope, uq_pe = uq[..., :MLA_NOPE], uq[..., MLA_NOPE:]
    zq_nope = jnp.zeros_like(uq_nope)
    zq_tail = jnp.zeros((MLA_Q_LORA, MLA_HEADS, pad_tail), F32)
    w_qa = jnp.concatenate([uq_nope, uq_pe, zq_tail], axis=-1).reshape(MLA_Q_LORA, MLA_HEADS * LANES)
    w_qb = jnp.concatenate([zq_nope, _swap_halves(uq_pe), zq_tail], axis=-1).reshape(
        MLA_Q_LORA, MLA_HEADS * LANES)

    ukv = w_ukv.reshape(MLA_KV_LORA, MLA_HEADS, MLA_NOPE + MLA_V)
    uk, uv = ukv[..., :MLA_NOPE], ukv[..., MLA_NOPE:]
    w_ka = jnp.concatenate(
        [uk, jnp.zeros((MLA_KV_LORA, MLA_HEADS, LANES - MLA_NOPE), F32)], axis=-1).reshape(
            MLA_KV_LORA, MLA_HEADS * LANES)
    w_v = uv.reshape(MLA_KV_LORA, MLA_HEADS * MLA_V)
    return [w.astype(BF16) for w in (w_cq, w_ckv, w_kra, w_krb)], [w.astype(BF16) for w in (w_qa, w_qb, w_ka, w_v)]


def _rope_tables(seq):
    pos = jnp.arange(seq, dtype=F32)
    inv = 1.0 / (ROPE_THETA ** (jnp.arange(0, MLA_ROPE, 2, dtype=F32) / MLA_ROPE))
    pad_tail = LANES - MLA_NOPE - MLA_ROPE
    q_scale = (MLA_NOPE + MLA_ROPE) ** -0.5 * LOG2_E
    ang = pos[:, None] * inv[None, :]
    cos_sin = (jnp.cos(ang), jnp.sin(ang))

    def tables(cos_sin, axis, lead_one):
        cos, sin = cos_sin
        shape = lambda n: (n, seq) if axis == 0 else (seq, n)
        lead = jnp.full(shape(MLA_NOPE), lead_one, F32)
        c = jnp.concatenate([lead, cos, cos, jnp.zeros(shape(pad_tail), F32)], axis=axis)
        s = jnp.concatenate([jnp.zeros(shape(MLA_NOPE), F32), -sin, sin, jnp.zeros(shape(pad_tail), F32)],
                            axis=axis)
        return c, s

    c_q, s_q = tables([t.T for t in cos_sin], 0, 1.0)
    c_k, s_k = tables(cos_sin, 1, 0.0)
    return c_q * q_scale, s_q * q_scale, c_k, s_k


def kernel(x, mem, w_in, mla_q_norm, mla_kv_norm, w_uq, w_ukv, w_mem_kv, w_br_mla, w_br_dil, w_br_mem, w_o,
           ffn1_w_gate, ffn1_w_up, ffn1_w_down, ffn2_w_gate, ffn2_w_up, ffn2_w_down,
           ln1_g, ln1_b, ln2_g, ln2_b, ln3_g, ln3_b):
    batch, seq, d = x.shape
    t = batch * seq
    tm_ffn = min(1024, t)
    tm_proj = min(512, seq)
    tm_merge = min(512, seq)
    tables = _rope_tables(seq)
    slopes = 2.0 ** (-8.0 * jnp.arange(1, DIL_HEADS + 1, dtype=F32) / DIL_HEADS)

    h = x.reshape(t, d)
    for l in range(DEPTH):
        bf = lambda w: w[l].astype(BF16)
        row = lambda v: v[l].reshape(1, -1)
        h = _ffn_ln(h, bf(ffn1_w_gate), bf(ffn1_w_up), bf(ffn1_w_down), row(ln1_g), row(ln1_b),
                    tm=tm_ffn, n_sub=FFN_SUB_TILES)

        w_in_l = w_in[l]
        o_dil_cols = MLA_Q_LORA + MLA_KV_LORA + MLA_ROPE
        o_memq = o_dil_cols + DIL_QKV
        o_gate = o_memq + MEM_OUT
        w_c, w_u = _prep_mla_weights(w_in_l, w_uq[l], w_ukv[l])
        proj_weights = w_c + [row(mla_q_norm), row(mla_kv_norm)] + w_u + [
            w_in_l[:, o_dil_cols:o_memq].astype(BF16)]
        q, k, v, *dil = _proj(h, proj_weights, tables, batch=batch, seq=seq, tm=tm_proj)

        o_mla = _mla_attn(q, k, v, tq=min(512, seq), tk=min(1024, seq)).reshape(t, MLA_OUT)
        dil_parts = [_dil_attn(dil[g], slopes, group=g, dilation=dl)
                     for g, (_, dl) in enumerate(DIL_PAIRS)]
        kv_mem = _mem_kv(mem, bf(w_mem_kv))

        merge_weights = [w_in_l[:, o_gate:].astype(BF16), w_in_l[:, o_memq:o_gate].astype(BF16),
                         bf(w_br_mla), bf(w_br_dil), bf(w_br_mem), bf(w_o)]
        h = _merge(h, o_mla, [p[0] for p in dil_parts], [p[1] for p in dil_parts], kv_mem,
                   merge_weights, row(ln2_g), row(ln2_b), seq=seq, tm=tm_merge, n_sub=2)

        h = _ffn_ln(h, bf(ffn2_w_gate), bf(ffn2_w_up), bf(ffn2_w_down), row(ln3_g), row(ln3_b),
                    tm=tm_ffn, n_sub=FFN_SUB_TILES)
    return h.reshape(batch, seq, d)
```

```python
import functools
import math

import jax
import jax.numpy as jnp
from jax import lax
from jax.experimental import pallas as pl
from jax.experimental.pallas import tpu as pltpu

F32 = jnp.float32
BF16 = jnp.bfloat16

D_MODEL = 1024
DEPTH = 1
MLA_HEADS = 8
MLA_Q_LORA = 256
MLA_KV_LORA = 256
MLA_NOPE = 64
MLA_ROPE = 32
MLA_V = 64
ROPE_THETA = 10000.0
DIL_PAIRS = ((128, 1), (512, 4), (2048, 16))
DIL_GROUPS = 3
DIL_HEADS_PER_GROUP = 4
DIL_HEAD_DIM = 128
DIL_HEADS = DIL_GROUPS * DIL_HEADS_PER_GROUP
MEM_HEADS = 4
MEM_HEAD_DIM = 128
EPS = 1e-5
ALPHA = (2 * DEPTH) ** 0.25

LOG2_E = math.log2(math.e)
LANES = 128
DIL_OUT = DIL_HEADS_PER_GROUP * DIL_HEAD_DIM
DIL_QKV = 3 * DIL_HEADS * DIL_HEAD_DIM
MEM_OUT = MEM_HEADS * MEM_HEAD_DIM
MLA_OUT = MLA_HEADS * MLA_V
MLA_DEN_ROWS = 16
MLA_PIECE = 256
MLA_UNITS_PER_TRIP = 4
DIL_HALF_SPAN = 64
DIL_HALO = 64
DIL_SUB = 128
DIL_CHUNK_TOKENS = 2048
DIL_ROWS_PER_STEP = 512
FFN_SUB_TILES = 4
VMEM_LIMIT = 56 * 1024 * 1024

assert all(w // 2 // d == DIL_HALF_SPAN for w, d in DIL_PAIRS)

_NT = (((1,), (1,)), ((), ()))


def _dot(a, b):
    return jnp.dot(a, b, preferred_element_type=F32)


def _dot_nt(a, b):
    return lax.dot_general(a, b, _NT, preferred_element_type=F32)


def _dot_tt(w, x):
    return lax.dot_general(w, x, (((0,), (1,)), ((), ())), preferred_element_type=F32)


def _layer_norm(y, g, b):
    mu = jnp.mean(y, axis=-1, keepdims=True)
    yc = y - mu
    var = jnp.mean(yc * yc, axis=-1, keepdims=True)
    return yc * lax.rsqrt(var + EPS) * g + b


def _rms_norm(y, g):
    return y * lax.rsqrt(jnp.mean(y * y, axis=-1, keepdims=True) + EPS) * g


def _ffn_ln_body(x_ref, wg_ref, wu_ref, wd_ref, g_ref, b_ref, o_ref, *, n_sub):
    sub_rows = x_ref.shape[0] // n_sub
    for sub in range(n_sub):
        rows = slice(sub * sub_rows, (sub + 1) * sub_rows)
        x = x_ref[rows, :]
        xb = x.astype(BF16)
        gate = _dot(xb, wg_ref[...])
        up = _dot(xb, wu_ref[...])
        act = (gate * jax.nn.sigmoid(gate) * up).astype(BF16)
        y = ALPHA * x + 0.5 * _dot(act, wd_ref[...])
        o_ref[rows, :] = _layer_norm(y, g_ref[...], b_ref[...])


def _ffn_ln(x2d, wg, wu, wd, g, b, *, tm, n_sub):
    t, d = x2d.shape

    def full(a):
        return pl.BlockSpec(a.shape, lambda i: (0,) * a.ndim, pipeline_mode=pl.Buffered(1))

    return pl.pallas_call(
        functools.partial(_ffn_ln_body, n_sub=n_sub),
        grid=(t // tm,),
        in_specs=[pl.BlockSpec((tm, d), lambda i: (i, 0))] + [full(a) for a in (wg, wu, wd, g, b)],
        out_specs=pl.BlockSpec((tm, d), lambda i: (i, 0)),
        out_shape=jax.ShapeDtypeStruct((t, d), F32),
        compiler_params=pltpu.CompilerParams(
            dimension_semantics=("parallel",), vmem_limit_bytes=VMEM_LIMIT),
        name="ffn_ln",
    )(x2d, wg, wu, wd, g, b)


def _proj_body(*refs):
    n_hc = D_MODEL // LANES
    h_refs = refs[:n_hc]
    (wcq_ref, wckv_ref, wkra_ref, wkrb_ref, gq_ref, gkv_ref,
     wqa_ref, wqb_ref, wka_ref, wv_ref, wdil_ref,
     cq_ref, sq_ref, ck_ref, sk_ref,
     q_ref, k_ref, v_ref, dil0_ref, dil1_ref, dil2_ref, hperm_ref) = refs[n_hc:]
    hb = jnp.concatenate([hc[...] for hc in h_refs], axis=1).astype(BF16)
    cqn = _rms_norm(_dot(hb, wcq_ref[...]), gq_ref[...]).astype(BF16)
    ckvn = _rms_norm(_dot(hb, wckv_ref[...]), gkv_ref[...]).astype(BF16)
    k_rope = _dot(hb, wkra_ref[...]) * ck_ref[...] + _dot(hb, wkrb_ref[...]) * sk_ref[...]
    ka = _dot(ckvn, wka_ref[...])
    for h in range(MLA_HEADS):
        sl = slice(h * LANES, (h + 1) * LANES)
        k_ref[0, h] = (ka[:, sl] + k_rope).astype(BF16)
    qa_t = _dot_tt(wqa_ref[...], cqn)
    qb_t = _dot_tt(wqb_ref[...], cqn)
    cq_t = cq_ref[...]
    sq_t = sq_ref[...]
    for h in range(MLA_HEADS):
        sl = slice(h * LANES, (h + 1) * LANES)
        q_ref[0, sl] = (qa_t[sl] * cq_t + qb_t[sl] * sq_t).astype(BF16)
    v_ref[0] = _dot_tt(wv_ref[...], ckvn).astype(BF16)

    dil_scale = DIL_HEAD_DIM ** -0.5
    tm = h_refs[0].shape[0]
    for g, (dil_ref, (_, dilation)) in enumerate(zip((dil0_ref, dil1_ref, dil2_ref), DIL_PAIRS)):
        rows_per = tm // dilation
        if dilation == 1:
            hp = hb
        else:
            for r in range(dilation):
                for c, hc in enumerate(h_refs):
                    hperm_ref[r * rows_per:(r + 1) * rows_per, c * LANES:(c + 1) * LANES] = hc[
                        pl.ds(r, rows_per, stride=dilation), :].astype(BF16)
            hp = hperm_ref[...]
        for part in range(3):
            c0 = (part * DIL_GROUPS + g) * DIL_OUT
            res = _dot(hp, wdil_ref[:, c0:c0 + DIL_OUT])
            if part == 0:
                res = res * dil_scale
            res = res.astype(BF16)
            for r in range(dilation):
                dil_ref[0, r, :, part * DIL_OUT:(part + 1) * DIL_OUT] = res[r * rows_per:(r + 1) * rows_per]


def _proj(h2d, weights, tables, *, batch, seq, tm):
    t, d = h2d.shape
    n_s = seq // tm

    def full(a):
        return pl.BlockSpec(a.shape, lambda i: (0,) * a.ndim)

    tab_spec = pl.BlockSpec((tm, LANES), lambda i: (i % n_s, 0))
    tab_t_spec = pl.BlockSpec((LANES, tm), lambda i: (0, i % n_s))
    head_map = lambda i: (i // n_s, 0, i % n_s, 0)
    feat_map = lambda i: (i // n_s, 0, i % n_s)
    dil_specs = [pl.BlockSpec((1, dl, tm // dl, 3 * DIL_OUT), head_map) for _, dl in DIL_PAIRS]
    dil_shapes = [jax.ShapeDtypeStruct((batch, dl, seq // dl, 3 * DIL_OUT), BF16) for _, dl in DIL_PAIRS]
    return pl.pallas_call(
        _proj_body,
        grid=(t // tm,),
        in_specs=[pl.BlockSpec((tm, LANES), functools.partial(lambda c, i: (i, c), c))
                  for c in range(d // LANES)]
        + [full(w) for w in weights] + [tab_t_spec] * 2 + [tab_spec] * 2,
        out_specs=[
            pl.BlockSpec((1, MLA_HEADS * LANES, tm), feat_map),
            pl.BlockSpec((1, MLA_HEADS, tm, LANES), head_map),
            pl.BlockSpec((1, MLA_OUT, tm), feat_map),
        ] + dil_specs,
        out_shape=[
            jax.ShapeDtypeStruct((batch, MLA_HEADS * LANES, seq), BF16),
            jax.ShapeDtypeStruct((batch, MLA_HEADS, seq, LANES), BF16),
            jax.ShapeDtypeStruct((batch, MLA_OUT, seq), BF16),
        ] + dil_shapes,
        scratch_shapes=[pltpu.VMEM((tm, d), BF16)],
        compiler_params=pltpu.CompilerParams(
            dimension_semantics=("parallel",), vmem_limit_bytes=VMEM_LIMIT),
        name="proj",
    )(*([h2d] * (d // LANES)), *weights, *tables)


def _mla_attn_body(q_ref, k_ref, vt_ref, o_ref, st_ref, m_ref, acc_ref, *, tq, tk):
    seq = k_ref.shape[2]
    n_tiles = seq // tq
    n_units = (seq // tk) * n_tiles
    heads = (0, 1)
    n_pieces = tk // MLA_PIECE

    def offsets(unit):
        if isinstance(unit, int):
            return (unit // n_tiles) * tk, (unit % n_tiles) * tq
        return (pl.multiple_of(lax.div(unit, n_tiles) * tk, tk),
                pl.multiple_of(lax.rem(unit, n_tiles) * tq, tq))

    def scores_piece(unit, buf, p, maxes):
        k_off, q_off = offsets(unit)
        rows = slice(p * MLA_PIECE, (p + 1) * MLA_PIECE)
        out = []
        for h in heads:
            st = _dot(k_ref[0, h, pl.ds(k_off + p * MLA_PIECE, MLA_PIECE), :],
                      q_ref[0, h * LANES:(h + 1) * LANES, pl.ds(q_off, tq)])
            st_ref[buf, h, rows] = st
            out.append(jnp.maximum(maxes[h], jnp.max(st, axis=0, keepdims=True)))
        return tuple(out)

    ones_rows = jnp.ones((MLA_DEN_ROWS, MLA_PIECE), BF16)

    def values_piece(k_off, buf, p, m_new, accs):
        rows = slice(p * MLA_PIECE, (p + 1) * MLA_PIECE)
        out = []
        for h in heads:
            pt = jnp.exp2(st_ref[buf, h, rows] - m_new[h]).astype(BF16)
            vtc = jnp.concatenate(
                [vt_ref[0, h * MLA_V:(h + 1) * MLA_V, pl.ds(k_off + p * MLA_PIECE, MLA_PIECE)], ones_rows],
                axis=0)
            out.append(accs[h] + _dot(vtc, pt))
        return tuple(out)

    neg_inf = jnp.full((1, tq), -jnp.inf, F32)

    def half(next_unit, next_buf, unit, buf, unit_max):
        k_off, q_off = offsets(unit)
        cols = pl.ds(q_off, tq)
        m_old = tuple(m_ref[h, :, cols] for h in heads)
        m_new = tuple(jnp.maximum(m_old[h], unit_max[h]) for h in heads)
        accs = tuple(jnp.exp2(m_old[h] - m_new[h]) * acc_ref[h, :, cols] for h in heads)
        next_max = (neg_inf, neg_inf)
        for p in range(n_pieces):
            if next_unit is not None:
                next_max = scores_piece(next_unit, next_buf, p, next_max)
            accs = values_piece(k_off, buf, p, m_new, accs)
        for h in heads:
            m_ref[h, :, cols] = m_new[h]
            acc_ref[h, :, cols] = accs[h]
        return next_max

    def group_step(jj, unit_max):
        for s in range(MLA_UNITS_PER_TRIP):
            u = MLA_UNITS_PER_TRIP * jj + s
            unit_max = half(u + 1, (s + 1) % 2, u, s % 2, unit_max)
        return unit_max

    m_ref[...] = jnp.full(m_ref.shape, -jnp.inf, F32)
    acc_ref[...] = jnp.zeros(acc_ref.shape, F32)
    unit_max = (neg_inf, neg_inf)
    for p in range(n_pieces):
        unit_max = scores_piece(0, 0, p, unit_max)
    unit_max = lax.fori_loop(0, n_units // MLA_UNITS_PER_TRIP - 1, group_step, unit_max)
    for u in range(n_units - MLA_UNITS_PER_TRIP, n_units):
        last = u == n_units - 1
        unit_max = half(None if last else u + 1, None if last else (u + 1) % 2, u, u % 2, unit_max)

    @pl.loop(0, n_tiles)
    def _(tile):
        cols = pl.ds(pl.multiple_of(tile * tq, tq), tq)
        out_t = jnp.concatenate(
            [acc_ref[h, :MLA_V, cols] / acc_ref[h, MLA_V:MLA_V + 1, cols] for h in heads], axis=0)
        o_ref[0, cols, :] = out_t.T.astype(o_ref.dtype)


def _mla_attn(q, k, v, *, tq, tk):
    batch, heads, seq, _ = k.shape
    assert MLA_UNITS_PER_TRIP % 2 == 0 and ((seq // tk) * (seq // tq)) % MLA_UNITS_PER_TRIP == 0
    return pl.pallas_call(
        functools.partial(_mla_attn_body, tq=tq, tk=tk),
        grid=(batch, heads // 2),
        in_specs=[
            pl.BlockSpec((1, 2 * LANES, seq), lambda b, hp: (b, hp, 0)),
            pl.BlockSpec((1, 2, seq, LANES), lambda b, hp: (b, hp, 0, 0)),
            pl.BlockSpec((1, 2 * MLA_V, seq), lambda b, hp: (b, hp, 0)),
        ],
        out_specs=pl.BlockSpec((1, seq, LANES), lambda b, hp: (b, 0, hp)),
        out_shape=jax.ShapeDtypeStruct((batch, seq, MLA_OUT), BF16),
        scratch_shapes=[pltpu.VMEM((2, 2, tk, tq), F32),
                        pltpu.VMEM((2, 1, seq), F32),
                        pltpu.VMEM((2, MLA_V + MLA_DEN_ROWS, seq), F32)],
        compiler_params=pltpu.CompilerParams(
            dimension_semantics=("parallel", "parallel"), vmem_limit_bytes=VMEM_LIMIT),
        name="mla_attn",
    )(q, k, v)


def _dil_attn_body(slopes_ref, q_ref, kl_ref, km_ref, kr_ref, vl_ref, vm_ref, vr_ref,
                   o_ref, lse_ref, kbuf, vbuf, *, group, dilation, tq, sub_len):
    i = pl.program_id(1)
    n_res = q_ref.shape[1]
    first_res = pl.program_id(2) * n_res
    kbuf[:, 0:DIL_HALO] = kl_ref[0]
    kbuf[:, DIL_HALO:DIL_HALO + tq] = km_ref[0]
    kbuf[:, DIL_HALO + tq:] = kr_ref[0]
    vbuf[:, 0:DIL_HALO] = vl_ref[0]
    vbuf[:, DIL_HALO:DIL_HALO + tq] = vm_ref[0]
    vbuf[:, DIL_HALO + tq:] = vr_ref[0]

    win = DIL_SUB + 2 * DIL_HALO
    row = lax.broadcasted_iota(jnp.int32, (DIL_SUB, win), 0)
    col = lax.broadcasted_iota(jnp.int32, (DIL_SUB, win), 1)
    steps = jnp.abs(col - DIL_HALO - row)
    in_band = steps <= DIL_HALF_SPAN
    dist = (steps * dilation).astype(F32)
    for sub in range(tq // DIL_SUB):
        key_pos = i * tq + (sub * DIL_SUB - DIL_HALO) + col
        valid = in_band & (key_pos >= 0) & (key_pos < sub_len)
        rows = slice(sub * DIL_SUB, (sub + 1) * DIL_SUB)
        wrows = slice(sub * DIL_SUB, sub * DIL_SUB + win)
        for rr in range(n_res):
            if dilation == 1:
                out_rows = pl.ds(sub * DIL_SUB, DIL_SUB)
            else:
                out_rows = pl.ds(sub * DIL_SUB * dilation + first_res + rr, DIL_SUB, stride=dilation)
            for h in range(DIL_HEADS_PER_GROUP):
                cols = slice(h * DIL_HEAD_DIM, (h + 1) * DIL_HEAD_DIM)
                slope = slopes_ref[group * DIL_HEADS_PER_GROUP + h]
                s = _dot_nt(q_ref[0, rr, rows, cols], kbuf[rr, wrows, cols]) - slope * dist
                s = jnp.where(valid, s, -jnp.inf)
                m = jnp.max(s, axis=1, keepdims=True)
                e = jnp.exp(s - m)
                den = jnp.sum(e, axis=1, keepdims=True)
                o_ref[0, h, out_rows, :] = _dot(e.astype(BF16), vbuf[rr, wrows, cols]) / den
                lse_ref[0, h, out_rows, :] = jnp.broadcast_to(m + jnp.log(den), (DIL_SUB, DIL_HEAD_DIM))


def _dil_attn(dil_g, slopes, *, group, dilation):
    batch, _, sub_len, _ = dil_g.shape
    seq = sub_len * dilation
    tq = min(512, sub_len, DIL_CHUNK_TOKENS // dilation)
    n_q = sub_len // tq
    n_res = min(dilation, DIL_ROWS_PER_STEP // tq)
    halo_per_tile = tq // DIL_HALO
    n_halo = sub_len // DIL_HALO
    q_col, k_col, v_col = 0, 1, 2

    def main(c):
        return pl.BlockSpec((1, n_res, tq, DIL_OUT), lambda b, i, r: (b, r, i, c))

    def left(c):
        return pl.BlockSpec(
            (1, n_res, DIL_HALO, DIL_OUT),
            lambda b, i, r: (b, r, jnp.maximum(i * halo_per_tile - 1, 0), c))

    def right(c):
        return pl.BlockSpec(
            (1, n_res, DIL_HALO, DIL_OUT),
            lambda b, i, r: (b, r, jnp.minimum((i + 1) * halo_per_tile, n_halo - 1), c))

    out_spec = pl.BlockSpec((1, DIL_HEADS_PER_GROUP, tq * dilation, DIL_HEAD_DIM),
                            lambda b, i, r: (b, 0, i, 0))
    out_shape = jax.ShapeDtypeStruct((batch, DIL_HEADS_PER_GROUP, seq, DIL_HEAD_DIM), F32)
    o, lse = pl.pallas_call(
        functools.partial(_dil_attn_body, group=group, dilation=dilation, tq=tq, sub_len=sub_len),
        grid=(batch, n_q, dilation // n_res),
        in_specs=[pl.BlockSpec(memory_space=pltpu.SMEM),
                  main(q_col), left(k_col), main(k_col), right(k_col),
                  left(v_col), main(v_col), right(v_col)],
        out_specs=[out_spec, out_spec],
        out_shape=[out_shape, out_shape],
        scratch_shapes=[pltpu.VMEM((n_res, tq + 2 * DIL_HALO, DIL_OUT), BF16),
                        pltpu.VMEM((n_res, tq + 2 * DIL_HALO, DIL_OUT), BF16)],
        compiler_params=pltpu.CompilerParams(
            dimension_semantics=("parallel", "parallel", "arbitrary"), vmem_limit_bytes=VMEM_LIMIT),
        name=f"dil_attn_g{group}",
    )(slopes, dil_g, dil_g, dil_g, dil_g, dil_g, dil_g, dil_g)
    return o, lse


def _mem_kv_body(mem_ref, w_ref, o_ref):
    o_ref[0] = _dot(mem_ref[0].astype(BF16), w_ref[...]).astype(BF16)


def _mem_kv(mem, w):
    batch, m_len, d = mem.shape
    n = w.shape[1]
    return pl.pallas_call(
        _mem_kv_body,
        grid=(batch,),
        in_specs=[pl.BlockSpec((1, m_len, d), lambda b: (b, 0, 0)),
                  pl.BlockSpec((d, n), lambda b: (0, 0))],
        out_specs=pl.BlockSpec((1, m_len, n), lambda b: (b, 0, 0)),
        out_shape=jax.ShapeDtypeStruct((batch, m_len, n), BF16),
        compiler_params=pltpu.CompilerParams(dimension_semantics=("parallel",)),
        name="mem_kv",
    )(mem, w)


def _merge_body(h_ref, omla_ref, od0_ref, od1_ref, od2_ref, ls0_ref, ls1_ref, ls2_ref, kvm_ref,
                wgate_ref, wmq_ref, wba_ref, wbb_ref, wbc_ref, wo_ref, g_ref, b_ref, out_ref, *, n_sub):
    sub_rows = h_ref.shape[0] // n_sub
    for sub in range(n_sub):
        rows = slice(sub * sub_rows, (sub + 1) * sub_rows)
        _merge_rows(rows, h_ref, omla_ref, od0_ref, od1_ref, od2_ref, ls0_ref, ls1_ref, ls2_ref, kvm_ref,
                    wgate_ref, wmq_ref, wba_ref, wbb_ref, wbc_ref, wo_ref, g_ref, b_ref, out_ref)


def _merge_rows(rows, h_ref, omla_ref, od0_ref, od1_ref, od2_ref, ls0_ref, ls1_ref, ls2_ref, kvm_ref,
                wgate_ref, wmq_ref, wba_ref, wbb_ref, wbc_ref, wo_ref, g_ref, b_ref, out_ref):
    h = h_ref[rows, :]
    hb = h.astype(BF16)

    def heads_to_lanes(ref):
        return jnp.concatenate([ref[0, hh, rows, :] for hh in range(DIL_HEADS_PER_GROUP)], axis=1)

    ls0, ls1, ls2 = heads_to_lanes(ls0_ref), heads_to_lanes(ls1_ref), heads_to_lanes(ls2_ref)
    mx = jnp.maximum(ls0, jnp.maximum(ls1, ls2))
    e0, e1, e2 = jnp.exp(ls0 - mx), jnp.exp(ls1 - mx), jnp.exp(ls2 - mx)
    o_dil = (e0 * heads_to_lanes(od0_ref) + e1 * heads_to_lanes(od1_ref)
             + e2 * heads_to_lanes(od2_ref)) / (e0 + e1 + e2)

    mq = (_dot(hb, wmq_ref[...]) * (MEM_HEAD_DIM ** -0.5)).astype(BF16)
    heads = []
    for hh in range(MEM_HEADS):
        kc = slice(hh * MEM_HEAD_DIM, (hh + 1) * MEM_HEAD_DIM)
        vc = slice(MEM_OUT + hh * MEM_HEAD_DIM, MEM_OUT + (hh + 1) * MEM_HEAD_DIM)
        s = _dot_nt(mq[:, kc], kvm_ref[0, :, kc])
        p = jnp.exp(s - jnp.max(s, axis=1, keepdims=True))
        heads.append(_dot(p.astype(BF16), kvm_ref[0, :, vc]) / jnp.sum(p, axis=1, keepdims=True))
    o_mem = jnp.concatenate(heads, axis=1)

    y_a = _dot(omla_ref[rows, :], wba_ref[...])
    y_b = _dot(o_dil.astype(BF16), wbb_ref[...])
    y_c = _dot(o_mem.astype(BF16), wbc_ref[...])
    d = h.shape[1]
    merged = (jax.nn.sigmoid(_dot(hb, wgate_ref[:, :d])) * y_a
              + jax.nn.sigmoid(_dot(hb, wgate_ref[:, d:2 * d])) * y_b
              + jax.nn.sigmoid(_dot(hb, wgate_ref[:, 2 * d:])) * y_c)
    mix = _dot(merged.astype(BF16), wo_ref[...])
    out_ref[rows, :] = _layer_norm(ALPHA * h + mix, g_ref[...], b_ref[...])


def _merge(h2d, o_mla, o_dil, lse_dil, kv_mem, weights, g, b, *, seq, tm, n_sub):
    t, d = h2d.shape
    n_s = seq // tm

    def rows(width):
        return pl.BlockSpec((tm, width), lambda i: (i, 0))

    def full(a):
        return pl.BlockSpec(a.shape, lambda i: (0,) * a.ndim, pipeline_mode=pl.Buffered(1))

    m_len, kv_cols = kv_mem.shape[1:]
    return pl.pallas_call(
        functools.partial(_merge_body, n_sub=n_sub),
        grid=(t // tm,),
        in_specs=[rows(d), rows(MLA_OUT)]
        + [pl.BlockSpec((1, DIL_HEADS_PER_GROUP, tm, DIL_HEAD_DIM),
                        lambda i: (i // n_s, 0, i % n_s, 0))] * 6
        + [pl.BlockSpec((1, m_len, kv_cols), lambda i: (i // n_s, 0, 0))]
        + [full(w) for w in weights] + [full(g), full(b)],
        out_specs=rows(d),
        out_shape=jax.ShapeDtypeStruct((t, d), F32),
        compiler_params=pltpu.CompilerParams(
            dimension_semantics=("parallel",), vmem_limit_bytes=VMEM_LIMIT),
        name="merge",
    )(h2d, o_mla, *o_dil, *lse_dil, kv_mem, *weights, g, b)


def _swap_halves(w):
    half = w.shape[-1] // 2
    return jnp.concatenate([w[..., half:], w[..., :half]], axis=-1)


def _prep_mla_weights(w_in, w_uq, w_ukv):
    d = w_in.shape[0]
    o_q, o_kv, o_kr = 0, MLA_Q_LORA, MLA_Q_LORA + MLA_KV_LORA
    w_cq = w_in[:, o_q:o_q + MLA_Q_LORA]
    w_ckv = w_in[:, o_kv:o_kv + MLA_KV_LORA]
    w_kr = w_in[:, o_kr:o_kr + MLA_ROPE]
    pad_tail = LANES - MLA_NOPE - MLA_ROPE
    z_nope = jnp.zeros((d, MLA_NOPE), F32)
    z_tail = jnp.zeros((d, pad_tail), F32)
    w_kra = jnp.concatenate([z_nope, w_kr, z_tail], axis=1)
    w_krb = jnp.concatenate([z_nope, _swap_halves(w_kr), z_tail], axis=1)

    uq = w_uq.reshape(MLA_Q_LORA, MLA_HEADS, MLA_NOPE + MLA_ROPE)
    uq_nope, uq_pe = uq[..., :MLA_NOPE], uq[..., MLA_NOPE:]
    zq_nope = jnp.zeros_like(uq_nope)
    zq_tail = jnp.zeros((MLA_Q_LORA, MLA_HEADS, pad_tail), F32)
    w_qa = jnp.concatenate([uq_nope, uq_pe, zq_tail], axis=-1).reshape(MLA_Q_LORA, MLA_HEADS * LANES)
    w_qb = jnp.concatenate([zq_nope, _swap_halves(uq_pe), zq_tail], axis=-1).reshape(
        MLA_Q_LORA, MLA_HEADS * LANES)

    ukv = w_ukv.reshape(MLA_KV_LORA, MLA_HEADS, MLA_NOPE + MLA_V)
    uk, uv = ukv[..., :MLA_NOPE], ukv[..., MLA_NOPE:]
    w_ka = jnp.concatenate(
        [uk, jnp.zeros((MLA_KV_LORA, MLA_HEADS, LANES - MLA_NOPE), F32)], axis=-1).reshape(
            MLA_KV_LORA, MLA_HEADS * LANES)
    w_v = uv.reshape(MLA_KV_LORA, MLA_HEADS * MLA_V)
    return [w.astype(BF16) for w in (w_cq, w_ckv, w_kra, w_krb)], [w.astype(BF16) for w in (w_qa, w_qb, w_ka, w_v)]


def _rope_tables(seq):
    pos = jnp.arange(seq, dtype=F32)
    inv = 1.0 / (ROPE_THETA ** (jnp.arange(0, MLA_ROPE, 2, dtype=F32) / MLA_ROPE))
    pad_tail = LANES - MLA_NOPE - MLA_ROPE
    q_scale = (MLA_NOPE + MLA_ROPE) ** -0.5 * LOG2_E
    ang = pos[:, None] * inv[None, :]
    cos_sin = (jnp.cos(ang), jnp.sin(ang))

    def tables(cos_sin, axis, lead_one):
        cos, sin = cos_sin
        shape = lambda n: (n, seq) if axis == 0 else (seq, n)
        lead = jnp.full(shape(MLA_NOPE), lead_one, F32)
        c = jnp.concatenate([lead, cos, cos, jnp.zeros(shape(pad_tail), F32)], axis=axis)
        s = jnp.concatenate([jnp.zeros(shape(MLA_NOPE), F32), -sin, sin, jnp.zeros(shape(pad_tail), F32)],
                            axis=axis)
        return c, s

    c_q, s_q = tables([t.T for t in cos_sin], 0, 1.0)
    c_k, s_k = tables(cos_sin, 1, 0.0)
    return c_q * q_scale, s_q * q_scale, c_k, s_k


def kernel(x, mem, w_in, mla_q_norm, mla_kv_norm, w_uq, w_ukv, w_mem_kv, w_br_mla, w_br_dil, w_br_mem, w_o,
           ffn1_w_gate, ffn1_w_up, ffn1_w_down, ffn2_w_gate, ffn2_w_up, ffn2_w_down,
           ln1_g, ln1_b, ln2_g, ln2_b, ln3_g, ln3_b):
    batch, seq, d = x.shape
    t = batch * seq
    tm_ffn = min(1024, t)
    tm_proj = min(512, seq)
    tm_merge = min(512, seq)
    tables = _rope_tables(seq)
    slopes = 2.0 ** (-8.0 * jnp.arange(1, DIL_HEADS + 1, dtype=F32) / DIL_HEADS)

    h = x.reshape(t, d)
    for l in range(DEPTH):
        bf = lambda w: w[l].astype(BF16)
        row = lambda v: v[l].reshape(1, -1)
        h = _ffn_ln(h, bf(ffn1_w_gate), bf(ffn1_w_up), bf(ffn1_w_down), row(ln1_g), row(ln1_b),
                    tm=tm_ffn, n_sub=FFN_SUB_TILES)

        w_in_l = w_in[l]
        o_dil_cols = MLA_Q_LORA + MLA_KV_LORA + MLA_ROPE
        o_memq = o_dil_cols + DIL_QKV
        o_gate = o_memq + MEM_OUT
        w_c, w_u = _prep_mla_weights(w_in_l, w_uq[l], w_ukv[l])
        proj_weights = w_c + [row(mla_q_norm), row(mla_kv_norm)] + w_u + [
            w_in_l[:, o_dil_cols:o_memq].astype(BF16)]
        q, k, v, *dil = _proj(h, proj_weights, tables, batch=batch, seq=seq, tm=tm_proj)

        o_mla = _mla_attn(q, k, v, tq=min(512, seq), tk=min(1024, seq)).reshape(t, MLA_OUT)
        dil_parts = [_dil_attn(dil[g], slopes, group=g, dilation=dl)
                     for g, (_, dl) in enumerate(DIL_PAIRS)]
        kv_mem = _mem_kv(mem, bf(w_mem_kv))

        merge_weights = [w_in_l[:, o_gate:].astype(BF16), w_in_l[:, o_memq:o_gate].astype(BF16),
                         bf(w_br_mla), bf(w_br_dil), bf(w_br_mem), bf(w_o)]
        h = _merge(h, o_mla, [p[0] for p in dil_parts], [p[1] for p in dil_parts], kv_mem,
                   merge_weights, row(ln2_g), row(ln2_b), seq=seq, tm=tm_merge, n_sub=2)

        h = _ffn_ln(h, bf(ffn2_w_gate), bf(ffn2_w_up), bf(ffn2_w_down), row(ln3_g), row(ln3_b),
                    tm=tm_ffn, n_sub=FFN_SUB_TILES)
    return h.reshape(batch, seq, d)
```

```python
import functools
import math

import jax
import jax.numpy as jnp
from jax import lax
from jax.experimental import pallas as pl
from jax.experimental.pallas import tpu as pltpu

F32 = jnp.float32
BF16 = jnp.bfloat16

D_MODEL = 1024
DEPTH = 1
MLA_HEADS = 8
MLA_Q_LORA = 256
MLA_KV_LORA = 256
MLA_NOPE = 64
MLA_ROPE = 32
MLA_V = 64
ROPE_THETA = 10000.0
DIL_PAIRS = ((128, 1), (512, 4), (2048, 16))
DIL_GROUPS = 3
DIL_HEADS_PER_GROUP = 4
DIL_HEAD_DIM = 128
DIL_HEADS = DIL_GROUPS * DIL_HEADS_PER_GROUP
MEM_HEADS = 4
MEM_HEAD_DIM = 128
EPS = 1e-5
ALPHA = (2 * DEPTH) ** 0.25

LOG2_E = math.log2(math.e)
LANES = 128
DIL_OUT = DIL_HEADS_PER_GROUP * DIL_HEAD_DIM
DIL_QKV = 3 * DIL_HEADS * DIL_HEAD_DIM
MEM_OUT = MEM_HEADS * MEM_HEAD_DIM
MLA_OUT = MLA_HEADS * MLA_V
MLA_DEN_ROWS = 16
MLA_PIECE = 256
MLA_UNITS_PER_TRIP = 8
DIL_HALF_SPAN = 64
DIL_HALO = 64
DIL_SUB = 128
DIL_CHUNK_TOKENS = 2048
DIL_ROWS_PER_STEP = 512
FFN_SUB_TILES = 4
VMEM_LIMIT = 56 * 1024 * 1024

assert all(w // 2 // d == DIL_HALF_SPAN for w, d in DIL_PAIRS)

_NT = (((1,), (1,)), ((), ()))


def _dot(a, b):
    return jnp.dot(a, b, preferred_element_type=F32)


def _dot_nt(a, b):
    return lax.dot_general(a, b, _NT, preferred_element_type=F32)


def _dot_tt(w, x):
    return lax.dot_general(w, x, (((0,), (1,)), ((), ())), preferred_element_type=F32)


def _layer_norm(y, g, b):
    mu = jnp.mean(y, axis=-1, keepdims=True)
    yc = y - mu
    var = jnp.mean(yc * yc, axis=-1, keepdims=True)
    return yc * lax.rsqrt(var + EPS) * g + b


def _rms_norm(y, g):
    return y * lax.rsqrt(jnp.mean(y * y, axis=-1, keepdims=True) + EPS) * g


def _ffn_ln_body(x_ref, wg_ref, wu_ref, wd_ref, g_ref, b_ref, o_ref, *, n_sub):
    sub_rows = x_ref.shape[0] // n_sub
    for sub in range(n_sub):
        rows = slice(sub * sub_rows, (sub + 1) * sub_rows)
        x = x_ref[rows, :]
        xb = x.astype(BF16)
        gate = _dot(xb, wg_ref[...])
        up = _dot(xb, wu_ref[...])
        act = (gate * jax.nn.sigmoid(gate) * up).astype(BF16)
        y = ALPHA * x + 0.5 * _dot(act, wd_ref[...])
        o_ref[rows, :] = _layer_norm(y, g_ref[...], b_ref[...])


def _ffn_ln(x2d, wg, wu, wd, g, b, *, tm, n_sub):
    t, d = x2d.shape

    def full(a):
        return pl.BlockSpec(a.shape, lambda i: (0,) * a.ndim, pipeline_mode=pl.Buffered(1))

    return pl.pallas_call(
        functools.partial(_ffn_ln_body, n_sub=n_sub),
        grid=(t // tm,),
        in_specs=[pl.BlockSpec((tm, d), lambda i: (i, 0))] + [full(a) for a in (wg, wu, wd, g, b)],
        out_specs=pl.BlockSpec((tm, d), lambda i: (i, 0)),
        out_shape=jax.ShapeDtypeStruct((t, d), F32),
        compiler_params=pltpu.CompilerParams(
            dimension_semantics=("parallel",), vmem_limit_bytes=VMEM_LIMIT),
        name="ffn_ln",
    )(x2d, wg, wu, wd, g, b)


def _proj_body(*refs):
    n_hc = D_MODEL // LANES
    h_refs = refs[:n_hc]
    (wcq_ref, wckv_ref, wkra_ref, wkrb_ref, gq_ref, gkv_ref,
     wqa_ref, wqb_ref, wka_ref, wv_ref, wdil_ref,
     cq_ref, sq_ref, ck_ref, sk_ref,
     q_ref, k_ref, v_ref, dil0_ref, dil1_ref, dil2_ref, hperm_ref) = refs[n_hc:]
    hb = jnp.concatenate([hc[...] for hc in h_refs], axis=1).astype(BF16)
    cqn = _rms_norm(_dot(hb, wcq_ref[...]), gq_ref[...]).astype(BF16)
    ckvn = _rms_norm(_dot(hb, wckv_ref[...]), gkv_ref[...]).astype(BF16)
    k_rope = _dot(hb, wkra_ref[...]) * ck_ref[...] + _dot(hb, wkrb_ref[...]) * sk_ref[...]
    ka = _dot(ckvn, wka_ref[...])
    for h in range(MLA_HEADS):
        sl = slice(h * LANES, (h + 1) * LANES)
        k_ref[0, h] = (ka[:, sl] + k_rope).astype(BF16)
    qa_t = _dot_tt(wqa_ref[...], cqn)
    qb_t = _dot_tt(wqb_ref[...], cqn)
    cq_t = cq_ref[...]
    sq_t = sq_ref[...]
    for h in range(MLA_HEADS):
        sl = slice(h * LANES, (h + 1) * LANES)
        q_ref[0, sl] = (qa_t[sl] * cq_t + qb_t[sl] * sq_t).astype(BF16)
    v_ref[0] = _dot_tt(wv_ref[...], ckvn).astype(BF16)

    dil_scale = DIL_HEAD_DIM ** -0.5
    tm = h_refs[0].shape[0]
    for g, (dil_ref, (_, dilation)) in enumerate(zip((dil0_ref, dil1_ref, dil2_ref), DIL_PAIRS)):
        rows_per = tm // dilation
        if dilation == 1:
            hp = hb
        else:
            for r in range(dilation):
                for c, hc in enumerate(h_refs):
                    hperm_ref[r * rows_per:(r + 1) * rows_per, c * LANES:(c + 1) * LANES] = hc[
                        pl.ds(r, rows_per, stride=dilation), :].astype(BF16)
            hp = hperm_ref[...]
        for part in range(3):
            c0 = (part * DIL_GROUPS + g) * DIL_OUT
            res = _dot(hp, wdil_ref[:, c0:c0 + DIL_OUT])
            if part == 0:
                res = res * dil_scale
            res = res.astype(BF16)
            for r in range(dilation):
                dil_ref[0, r, :, part * DIL_OUT:(part + 1) * DIL_OUT] = res[r * rows_per:(r + 1) * rows_per]


def _proj(h2d, weights, tables, *, batch, seq, tm):
    t, d = h2d.shape
    n_s = seq // tm

    def full(a):
        return pl.BlockSpec(a.shape, lambda i: (0,) * a.ndim)

    tab_spec = pl.BlockSpec((tm, LANES), lambda i: (i % n_s, 0))
    tab_t_spec = pl.BlockSpec((LANES, tm), lambda i: (0, i % n_s))
    head_map = lambda i: (i // n_s, 0, i % n_s, 0)
    feat_map = lambda i: (i // n_s, 0, i % n_s)
    dil_specs = [pl.BlockSpec((1, dl, tm // dl, 3 * DIL_OUT), head_map) for _, dl in DIL_PAIRS]
    dil_shapes = [jax.ShapeDtypeStruct((batch, dl, seq // dl, 3 * DIL_OUT), BF16) for _, dl in DIL_PAIRS]
    return pl.pallas_call(
        _proj_body,
        grid=(t // tm,),
        in_specs=[pl.BlockSpec((tm, LANES), functools.partial(lambda c, i: (i, c), c))
                  for c in range(d // LANES)]
        + [full(w) for w in weights] + [tab_t_spec] * 2 + [tab_spec] * 2,
        out_specs=[
            pl.BlockSpec((1, MLA_HEADS * LANES, tm), feat_map),
            pl.BlockSpec((1, MLA_HEADS, tm, LANES), head_map),
            pl.BlockSpec((1, MLA_OUT, tm), feat_map),
        ] + dil_specs,
        out_shape=[
            jax.ShapeDtypeStruct((batch, MLA_HEADS * LANES, seq), BF16),
            jax.ShapeDtypeStruct((batch, MLA_HEADS, seq, LANES), BF16),
            jax.ShapeDtypeStruct((batch, MLA_OUT, seq), BF16),
        ] + dil_shapes,
        scratch_shapes=[pltpu.VMEM((tm, d), BF16)],
        compiler_params=pltpu.CompilerParams(
            dimension_semantics=("parallel",), vmem_limit_bytes=VMEM_LIMIT),
        name="proj",
    )(*([h2d] * (d // LANES)), *weights, *tables)


def _mla_attn_body(q_ref, k_ref, vt_ref, o_ref, st_ref, m_ref, acc_ref, *, tq, tk):
    seq = k_ref.shape[2]
    n_tiles = seq // tq
    n_units = (seq // tk) * n_tiles
    heads = (0, 1)
    n_pieces = tk // MLA_PIECE

    def offsets(unit):
        if isinstance(unit, int):
            return (unit // n_tiles) * tk, (unit % n_tiles) * tq
        return (pl.multiple_of(lax.div(unit, n_tiles) * tk, tk),
                pl.multiple_of(lax.rem(unit, n_tiles) * tq, tq))

    def scores_piece(unit, buf, p, maxes):
        k_off, q_off = offsets(unit)
        rows = slice(p * MLA_PIECE, (p + 1) * MLA_PIECE)
        out = []
        for h in heads:
            st = _dot(k_ref[0, h, pl.ds(k_off + p * MLA_PIECE, MLA_PIECE), :],
                      q_ref[0, h * LANES:(h + 1) * LANES, pl.ds(q_off, tq)])
            st_ref[buf, h, rows] = st
            out.append(jnp.maximum(maxes[h], jnp.max(st, axis=0, keepdims=True)))
        return tuple(out)

    ones_rows = jnp.ones((MLA_DEN_ROWS, MLA_PIECE), BF16)

    def values_piece(k_off, buf, p, m_new, accs):
        rows = slice(p * MLA_PIECE, (p + 1) * MLA_PIECE)
        out = []
        for h in heads:
            pt = jnp.exp2(st_ref[buf, h, rows] - m_new[h]).astype(BF16)
            vtc = jnp.concatenate(
                [vt_ref[0, h * MLA_V:(h + 1) * MLA_V, pl.ds(k_off + p * MLA_PIECE, MLA_PIECE)], ones_rows],
                axis=0)
            out.append(accs[h] + _dot(vtc, pt))
        return tuple(out)

    neg_inf = jnp.full((1, tq), -jnp.inf, F32)

    def half(next_unit, next_buf, unit, buf, unit_max):
        k_off, q_off = offsets(unit)
        cols = pl.ds(q_off, tq)
        m_old = tuple(m_ref[h, :, cols] for h in heads)
        m_new = tuple(jnp.maximum(m_old[h], unit_max[h]) for h in heads)
        accs = tuple(jnp.exp2(m_old[h] - m_new[h]) * acc_ref[h, :, cols] for h in heads)
        next_max = (neg_inf, neg_inf)
        for p in range(n_pieces):
            if next_unit is not None:
                next_max = scores_piece(next_unit, next_buf, p, next_max)
            accs = values_piece(k_off, buf, p, m_new, accs)
        for h in heads:
            m_ref[h, :, cols] = m_new[h]
            acc_ref[h, :, cols] = accs[h]
        return next_max

    def group_step(jj, unit_max):
        for s in range(MLA_UNITS_PER_TRIP):
            u = MLA_UNITS_PER_TRIP * jj + s
            unit_max = half(u + 1, (s + 1) % 2, u, s % 2, unit_max)
        return unit_max

    m_ref[...] = jnp.full(m_ref.shape, -jnp.inf, F32)
    acc_ref[...] = jnp.zeros(acc_ref.shape, F32)
    unit_max = (neg_inf, neg_inf)
    for p in range(n_pieces):
        unit_max = scores_piece(0, 0, p, unit_max)
    unit_max = lax.fori_loop(0, n_units // MLA_UNITS_PER_TRIP - 1, group_step, unit_max)
    for u in range(n_units - MLA_UNITS_PER_TRIP, n_units):
        last = u == n_units - 1
        unit_max = half(None if last else u + 1, None if last else (u + 1) % 2, u, u % 2, unit_max)

    @pl.loop(0, n_tiles)
    def _(tile):
        cols = pl.ds(pl.multiple_of(tile * tq, tq), tq)
        out_t = jnp.concatenate(
            [acc_ref[h, :MLA_V, cols] / acc_ref[h, MLA_V:MLA_V + 1, cols] for h in heads], axis=0)
        o_ref[0, cols, :] = out_t.T.astype(o_ref.dtype)


def _mla_attn(q, k, v, *, tq, tk):
    batch, heads, seq, _ = k.shape
    assert MLA_UNITS_PER_TRIP % 2 == 0 and ((seq // tk) * (seq // tq)) % MLA_UNITS_PER_TRIP == 0
    return pl.pallas_call(
        functools.partial(_mla_attn_body, tq=tq, tk=tk),
        grid=(batch, heads // 2),
        in_specs=[
            pl.BlockSpec((1, 2 * LANES, seq), lambda b, hp: (b, hp, 0)),
            pl.BlockSpec((1, 2, seq, LANES), lambda b, hp: (b, hp, 0, 0)),
            pl.BlockSpec((1, 2 * MLA_V, seq), lambda b, hp: (b, hp, 0)),
        ],
        out_specs=pl.BlockSpec((1, seq, LANES), lambda b, hp: (b, 0, hp)),
        out_shape=jax.ShapeDtypeStruct((batch, seq, MLA_OUT), BF16),
        scratch_shapes=[pltpu.VMEM((2, 2, tk, tq), F32),
                        pltpu.VMEM((2, 1, seq), F32),
                        pltpu.VMEM((2, MLA_V + MLA_DEN_ROWS, seq), F32)],
        compiler_params=pltpu.CompilerParams(
            dimension_semantics=("parallel", "parallel"), vmem_limit_bytes=VMEM_LIMIT),
        name="mla_attn",
    )(q, k, v)


def _dil_attn_body(slopes_ref, q_ref, kl_ref, km_ref, kr_ref, vl_ref, vm_ref, vr_ref,
                   o_ref, lse_ref, kbuf, vbuf, *, group, dilation, tq, sub_len):
    i = pl.program_id(1)
    n_res = q_ref.shape[1]
    first_res = pl.program_id(2) * n_res
    kbuf[:, 0:DIL_HALO] = kl_ref[0]
    kbuf[:, DIL_HALO:DIL_HALO + tq] = km_ref[0]
    kbuf[:, DIL_HALO + tq:] = kr_ref[0]
    vbuf[:, 0:DIL_HALO] = vl_ref[0]
    vbuf[:, DIL_HALO:DIL_HALO + tq] = vm_ref[0]
    vbuf[:, DIL_HALO + tq:] = vr_ref[0]

    win = DIL_SUB + 2 * DIL_HALO
    row = lax.broadcasted_iota(jnp.int32, (DIL_SUB, win), 0)
    col = lax.broadcasted_iota(jnp.int32, (DIL_SUB, win), 1)
    steps = jnp.abs(col - DIL_HALO - row)
    in_band = steps <= DIL_HALF_SPAN
    dist = (steps * dilation).astype(F32)
    for sub in range(tq // DIL_SUB):
        key_pos = i * tq + (sub * DIL_SUB - DIL_HALO) + col
        valid = in_band & (key_pos >= 0) & (key_pos < sub_len)
        rows = slice(sub * DIL_SUB, (sub + 1) * DIL_SUB)
        wrows = slice(sub * DIL_SUB, sub * DIL_SUB + win)
        for rr in range(n_res):
            if dilation == 1:
                out_rows = pl.ds(sub * DIL_SUB, DIL_SUB)
            else:
                out_rows = pl.ds(sub * DIL_SUB * dilation + first_res + rr, DIL_SUB, stride=dilation)
            for h in range(DIL_HEADS_PER_GROUP):
                cols = slice(h * DIL_HEAD_DIM, (h + 1) * DIL_HEAD_DIM)
                slope = slopes_ref[group * DIL_HEADS_PER_GROUP + h]
                s = _dot_nt(q_ref[0, rr, rows, cols], kbuf[rr, wrows, cols]) - slope * dist
                s = jnp.where(valid, s, -jnp.inf)
                m = jnp.max(s, axis=1, keepdims=True)
                e = jnp.exp(s - m)
                den = jnp.sum(e, axis=1, keepdims=True)
                o_ref[0, h, out_rows, :] = _dot(e.astype(BF16), vbuf[rr, wrows, cols]) / den
                lse_ref[0, h, out_rows, :] = jnp.broadcast_to(m + jnp.log(den), (DIL_SUB, DIL_HEAD_DIM))


def _dil_attn(dil_g, slopes, *, group, dilation):
    batch, _, sub_len, _ = dil_g.shape
    seq = sub_len * dilation
    tq = min(512, sub_len, DIL_CHUNK_TOKENS // dilation)
    n_q = sub_len // tq
    n_res = min(dilation, DIL_ROWS_PER_STEP // tq)
    halo_per_tile = tq // DIL_HALO
    n_halo = sub_len // DIL_HALO
    q_col, k_col, v_col = 0, 1, 2

    def main(c):
        return pl.BlockSpec((1, n_res, tq, DIL_OUT), lambda b, i, r: (b, r, i, c))

    def left(c):
        return pl.BlockSpec(
            (1, n_res, DIL_HALO, DIL_OUT),
            lambda b, i, r: (b, r, jnp.maximum(i * halo_per_tile - 1, 0), c))

    def right(c):
        return pl.BlockSpec(
            (1, n_res, DIL_HALO, DIL_OUT),
            lambda b, i, r: (b, r, jnp.minimum((i + 1) * halo_per_tile, n_halo - 1), c))

    out_spec = pl.BlockSpec((1, DIL_HEADS_PER_GROUP, tq * dilation, DIL_HEAD_DIM),
                            lambda b, i, r: (b, 0, i, 0))
    out_shape = jax.ShapeDtypeStruct((batch, DIL_HEADS_PER_GROUP, seq, DIL_HEAD_DIM), F32)
    o, lse = pl.pallas_call(
        functools.partial(_dil_attn_body, group=group, dilation=dilation, tq=tq, sub_len=sub_len),
        grid=(batch, n_q, dilation // n_res),
        in_specs=[pl.BlockSpec(memory_space=pltpu.SMEM),
                  main(q_col), left(k_col), main(k_col), right(k_col),
                  left(v_col), main(v_col), right(v_col)],
        out_specs=[out_spec, out_spec],
        out_shape=[out_shape, out_shape],
        scratch_shapes=[pltpu.VMEM((n_res, tq + 2 * DIL_HALO, DIL_OUT), BF16),
                        pltpu.VMEM((n_res, tq + 2 * DIL_HALO, DIL_OUT), BF16)],
        compiler_params=pltpu.CompilerParams(
            dimension_semantics=("parallel", "parallel", "arbitrary"), vmem_limit_bytes=VMEM_LIMIT),
        name=f"dil_attn_g{group}",
    )(slopes, dil_g, dil_g, dil_g, dil_g, dil_g, dil_g, dil_g)
    return o, lse


def _mem_kv_body(mem_ref, w_ref, o_ref):
    o_ref[0] = _dot(mem_ref[0].astype(BF16), w_ref[...]).astype(BF16)


def _mem_kv(mem, w):
    batch, m_len, d = mem.shape
    n = w.shape[1]
    return pl.pallas_call(
        _mem_kv_body,
        grid=(batch,),
        in_specs=[pl.BlockSpec((1, m_len, d), lambda b: (b, 0, 0)),
                  pl.BlockSpec((d, n), lambda b: (0, 0))],
        out_specs=pl.BlockSpec((1, m_len, n), lambda b: (b, 0, 0)),
        out_shape=jax.ShapeDtypeStruct((batch, m_len, n), BF16),
        compiler_params=pltpu.CompilerParams(dimension_semantics=("parallel",)),
        name="mem_kv",
    )(mem, w)


def _merge_body(h_ref, omla_ref, od0_ref, od1_ref, od2_ref, ls0_ref, ls1_ref, ls2_ref, kvm_ref,
                wgate_ref, wmq_ref, wba_ref, wbb_ref, wbc_ref, wo_ref, g_ref, b_ref, out_ref, *, n_sub):
    sub_rows = h_ref.shape[0] // n_sub
    for sub in range(n_sub):
        rows = slice(sub * sub_rows, (sub + 1) * sub_rows)
        _merge_rows(rows, h_ref, omla_ref, od0_ref, od1_ref, od2_ref, ls0_ref, ls1_ref, ls2_ref, kvm_ref,
                    wgate_ref, wmq_ref, wba_ref, wbb_ref, wbc_ref, wo_ref, g_ref, b_ref, out_ref)


def _merge_rows(rows, h_ref, omla_ref, od0_ref, od1_ref, od2_ref, ls0_ref, ls1_ref, ls2_ref, kvm_ref,
                wgate_ref, wmq_ref, wba_ref, wbb_ref, wbc_ref, wo_ref, g_ref, b_ref, out_ref):
    h = h_ref[rows, :]
    hb = h.astype(BF16)

    def heads_to_lanes(ref):
        return jnp.concatenate([ref[0, hh, rows, :] for hh in range(DIL_HEADS_PER_GROUP)], axis=1)

    ls0, ls1, ls2 = heads_to_lanes(ls0_ref), heads_to_lanes(ls1_ref), heads_to_lanes(ls2_ref)
    mx = jnp.maximum(ls0, jnp.maximum(ls1, ls2))
    e0, e1, e2 = jnp.exp(ls0 - mx), jnp.exp(ls1 - mx), jnp.exp(ls2 - mx)
    o_dil = (e0 * heads_to_lanes(od0_ref) + e1 * heads_to_lanes(od1_ref)
             + e2 * heads_to_lanes(od2_ref)) / (e0 + e1 + e2)

    mq = (_dot(hb, wmq_ref[...]) * (MEM_HEAD_DIM ** -0.5)).astype(BF16)
    heads = []
    for hh in range(MEM_HEADS):
        kc = slice(hh * MEM_HEAD_DIM, (hh + 1) * MEM_HEAD_DIM)
        vc = slice(MEM_OUT + hh * MEM_HEAD_DIM, MEM_OUT + (hh + 1) * MEM_HEAD_DIM)
        s = _dot_nt(mq[:, kc], kvm_ref[0, :, kc])
        p = jnp.exp(s - jnp.max(s, axis=1, keepdims=True))
        heads.append(_dot(p.astype(BF16), kvm_ref[0, :, vc]) / jnp.sum(p, axis=1, keepdims=True))
    o_mem = jnp.concatenate(heads, axis=1)

    y_a = _dot(omla_ref[rows, :], wba_ref[...])
    y_b = _dot(o_dil.astype(BF16), wbb_ref[...])
    y_c = _dot(o_mem.astype(BF16), wbc_ref[...])
    d = h.shape[1]
    merged = (jax.nn.sigmoid(_dot(hb, wgate_ref[:, :d])) * y_a
              + jax.nn.sigmoid(_dot(hb, wgate_ref[:, d:2 * d])) * y_b
              + jax.nn.sigmoid(_dot(hb, wgate_ref[:, 2 * d:])) * y_c)
    mix = _dot(merged.astype(BF16), wo_ref[...])
    out_ref[rows, :] = _layer_norm(ALPHA * h + mix, g_ref[...], b_ref[...])


def _merge(h2d, o_mla, o_dil, lse_dil, kv_mem, weights, g, b, *, seq, tm, n_sub):
    t, d = h2d.shape
    n_s = seq // tm

    def rows(width):
        return pl.BlockSpec((tm, width), lambda i: (i, 0))

    def full(a):
        return pl.BlockSpec(a.shape, lambda i: (0,) * a.ndim, pipeline_mode=pl.Buffered(1))

    m_len, kv_cols = kv_mem.shape[1:]
    return pl.pallas_call(
        functools.partial(_merge_body, n_sub=n_sub),
        grid=(t // tm,),
        in_specs=[rows(d), rows(MLA_OUT)]
        + [pl.BlockSpec((1, DIL_HEADS_PER_GROUP, tm, DIL_HEAD_DIM),
                        lambda i: (i // n_s, 0, i % n_s, 0))] * 6
        + [pl.BlockSpec((1, m_len, kv_cols), lambda i: (i // n_s, 0, 0))]
        + [full(w) for w in weights] + [full(g), full(b)],
        out_specs=rows(d),
        out_shape=jax.ShapeDtypeStruct((t, d), F32),
        compiler_params=pltpu.CompilerParams(
            dimension_semantics=("parallel",), vmem_limit_bytes=VMEM_LIMIT),
        name="merge",
    )(h2d, o_mla, *o_dil, *lse_dil, kv_mem, *weights, g, b)


def _swap_halves(w):
    half = w.shape[-1] // 2
    return jnp.concatenate([w[..., half:], w[..., :half]], axis=-1)


def _prep_mla_weights(w_in, w_uq, w_ukv):
    d = w_in.shape[0]
    o_q, o_kv, o_kr = 0, MLA_Q_LORA, MLA_Q_LORA + MLA_KV_LORA
    w_cq = w_in[:, o_q:o_q + MLA_Q_LORA]
    w_ckv = w_in[:, o_kv:o_kv + MLA_KV_LORA]
    w_kr = w_in[:, o_kr:o_kr + MLA_ROPE]
    pad_tail = LANES - MLA_NOPE - MLA_ROPE
    z_nope = jnp.zeros((d, MLA_NOPE), F32)
    z_tail = jnp.zeros((d, pad_tail), F32)
    w_kra = jnp.concatenate([z_nope, w_kr, z_tail], axis=1)
    w_krb = jnp.concatenate([z_nope, _swap_halves(w_kr), z_tail], axis=1)

    uq = w_uq.reshape(MLA_Q_LORA, MLA_HEADS, MLA_NOPE + MLA_ROPE)
    uq_nope, uq_pe = uq[..., :MLA_NOPE], uq[..., MLA_NOPE:]
    zq_nope = jnp.zeros_like(uq_nope)
    zq_tail = jnp.zeros((MLA_Q_LORA, MLA_HEADS, pad_tail), F32)
    w_qa = jnp.concatenate([uq_nope, uq_pe, zq_tail], axis=-1).reshape(MLA_Q_LORA, MLA_HEADS * LANES)
    w_qb = jnp.concatenate([zq_nope, _swap_halves(uq_pe), zq_tail], axis=-1).reshape(
        MLA_Q_LORA, MLA_HEADS * LANES)

    ukv = w_ukv.reshape(MLA_KV_LORA, MLA_HEADS, MLA_NOPE + MLA_V)
    uk, uv = ukv[..., :MLA_NOPE], ukv[..., MLA_NOPE:]
    w_ka = jnp.concatenate(
        [uk, jnp.zeros((MLA_KV_LORA, MLA_HEADS, LANES - MLA_NOPE), F32)], axis=-1).reshape(
            MLA_KV_LORA, MLA_HEADS * LANES)
    w_v = uv.reshape(MLA_KV_LORA, MLA_HEADS * MLA_V)
    return [w.astype(BF16) for w in (w_cq, w_ckv, w_kra, w_krb)], [w.astype(BF16) for w in (w_qa, w_qb, w_ka, w_v)]


def _rope_tables(seq):
    pos = jnp.arange(seq, dtype=F32)
    inv = 1.0 / (ROPE_THETA ** (jnp.arange(0, MLA_ROPE, 2, dtype=F32) / MLA_ROPE))
    pad_tail = LANES - MLA_NOPE - MLA_ROPE
    q_scale = (MLA_NOPE + MLA_ROPE) ** -0.5 * LOG2_E
    ang = pos[:, None] * inv[None, :]
    cos_sin = (jnp.cos(ang), jnp.sin(ang))

    def tables(cos_sin, axis, lead_one):
        cos, sin = cos_sin
        shape = lambda n: (n, seq) if axis == 0 else (seq, n)
        lead = jnp.full(shape(MLA_NOPE), lead_one, F32)
        c = jnp.concatenate([lead, cos, cos, jnp.zeros(shape(pad_tail), F32)], axis=axis)
        s = jnp.concatenate([jnp.zeros(shape(MLA_NOPE), F32), -sin, sin, jnp.zeros(shape(pad_tail), F32)],
                            axis=axis)
        return c, s

    c_q, s_q = tables([t.T for t in cos_sin], 0, 1.0)
    c_k, s_k = tables(cos_sin, 1, 0.0)
    return c_q * q_scale, s_q * q_scale, c_k, s_k


def kernel(x, mem, w_in, mla_q_norm, mla_kv_norm, w_uq, w_ukv, w_mem_kv, w_br_mla, w_br_dil, w_br_mem, w_o,
           ffn1_w_gate, ffn1_w_up, ffn1_w_down, ffn2_w_gate, ffn2_w_up, ffn2_w_down,
           ln1_g, ln1_b, ln2_g, ln2_b, ln3_g, ln3_b):
    batch, seq, d = x.shape
    t = batch * seq
    tm_ffn = min(1024, t)
    tm_proj = min(512, seq)
    tm_merge = min(512, seq)
    tables = _rope_tables(seq)
    slopes = 2.0 ** (-8.0 * jnp.arange(1, DIL_HEADS + 1, dtype=F32) / DIL_HEADS)

    h = x.reshape(t, d)
    for l in range(DEPTH):
        bf = lambda w: w[l].astype(BF16)
        row = lambda v: v[l].reshape(1, -1)
        h = _ffn_ln(h, bf(ffn1_w_gate), bf(ffn1_w_up), bf(ffn1_w_down), row(ln1_g), row(ln1_b),
                    tm=tm_ffn, n_sub=FFN_SUB_TILES)

        w_in_l = w_in[l]
        o_dil_cols = MLA_Q_LORA + MLA_KV_LORA + MLA_ROPE
        o_memq = o_dil_cols + DIL_QKV
        o_gate = o_memq + MEM_OUT
        w_c, w_u = _prep_mla_weights(w_in_l, w_uq[l], w_ukv[l])
        proj_weights = w_c + [row(mla_q_norm), row(mla_kv_norm)] + w_u + [
            w_in_l[:, o_dil_cols:o_memq].astype(BF16)]
        q, k, v, *dil = _proj(h, proj_weights, tables, batch=batch, seq=seq, tm=tm_proj)

        o_mla = _mla_attn(q, k, v, tq=min(512, seq), tk=min(1024, seq)).reshape(t, MLA_OUT)
        dil_parts = [_dil_attn(dil[g], slopes, group=g, dilation=dl)
                     for g, (_, dl) in enumerate(DIL_PAIRS)]
        kv_mem = _mem_kv(mem, bf(w_mem_kv))

        merge_weights = [w_in_l[:, o_gate:].astype(BF16), w_in_l[:, o_memq:o_gate].astype(BF16),
                         bf(w_br_mla), bf(w_br_dil), bf(w_br_mem), bf(w_o)]
        h = _merge(h, o_mla, [p[0] for p in dil_parts], [p[1] for p in dil_parts], kv_mem,
                   merge_weights, row(ln2_g), row(ln2_b), seq=seq, tm=tm_merge, n_sub=2)

        h = _ffn_ln(h, bf(ffn2_w_gate), bf(ffn2_w_up), bf(ffn2_w_down), row(ln3_g), row(ln3_b),
                    tm=tm_ffn, n_sub=FFN_SUB_TILES)
    return h.reshape(batch, seq, d)
```

```python
import functools
import math

import jax
import jax.numpy as jnp
import numpy as np
from jax import lax
from jax.experimental import pallas as pl
from jax.experimental.pallas import tpu as pltpu

F32 = jnp.float32
BF16 = jnp.bfloat16

D_MODEL = 1024
DEPTH = 1
MLA_HEADS = 8
MLA_Q_LORA = 256
MLA_KV_LORA = 256
MLA_NOPE = 64
MLA_ROPE = 32
MLA_V = 64
ROPE_THETA = 10000.0
DIL_PAIRS = ((128, 1), (512, 4), (2048, 16))
DIL_GROUPS = 3
DIL_HEADS_PER_GROUP = 4
DIL_HEAD_DIM = 128
DIL_HEADS = DIL_GROUPS * DIL_HEADS_PER_GROUP
MEM_HEADS = 4
MEM_HEAD_DIM = 128
EPS = 1e-5
ALPHA = (2 * DEPTH) ** 0.25

LOG2_E = math.log2(math.e)
LANES = 128
DIL_OUT = DIL_HEADS_PER_GROUP * DIL_HEAD_DIM
DIL_QKV = 3 * DIL_HEADS * DIL_HEAD_DIM
MEM_OUT = MEM_HEADS * MEM_HEAD_DIM
MLA_OUT = MLA_HEADS * MLA_V
MLA_DEN_ROWS = 16
MLA_PIECE = 256
MLA_UNITS_PER_TRIP = 8
DIL_HALF_SPAN = 64
DIL_HALO = 64
DIL_SUB = 128
DIL_CHUNK_TOKENS = 2048
DIL_ROWS_PER_STEP = 1024
FFN_SUB_TILES = 4
VMEM_LIMIT = 56 * 1024 * 1024

assert all(w // 2 // d == DIL_HALF_SPAN for w, d in DIL_PAIRS)

_NT = (((1,), (1,)), ((), ()))


def _dot(a, b):
    return jnp.dot(a, b, preferred_element_type=F32)


def _dot_nt(a, b):
    return lax.dot_general(a, b, _NT, preferred_element_type=F32)


def _dot_tt(w, x):
    return lax.dot_general(w, x, (((0,), (1,)), ((), ())), preferred_element_type=F32)


def _layer_norm(y, g, b):
    mu = jnp.mean(y, axis=-1, keepdims=True)
    yc = y - mu
    var = jnp.mean(yc * yc, axis=-1, keepdims=True)
    return yc * lax.rsqrt(var + EPS) * g + b


def _rms_norm(y, g):
    return y * lax.rsqrt(jnp.mean(y * y, axis=-1, keepdims=True) + EPS) * g


def _ffn_ln_body(x_ref, wg_ref, wu_ref, wd_ref, g_ref, b_ref, o_ref, *, n_sub):
    sub_rows = x_ref.shape[0] // n_sub
    for sub in range(n_sub):
        rows = slice(sub * sub_rows, (sub + 1) * sub_rows)
        x = x_ref[rows, :]
        xb = x.astype(BF16)
        gate = _dot(xb, wg_ref[...])
        up = _dot(xb, wu_ref[...])
        act = (gate * jax.nn.sigmoid(gate) * up).astype(BF16)
        y = ALPHA * x + 0.5 * _dot(act, wd_ref[...])
        o_ref[rows, :] = _layer_norm(y, g_ref[...], b_ref[...])


def _ffn_ln(x2d, wg, wu, wd, g, b, *, tm, n_sub):
    t, d = x2d.shape

    def full(a):
        return pl.BlockSpec(a.shape, lambda i: (0,) * a.ndim, pipeline_mode=pl.Buffered(1))

    return pl.pallas_call(
        functools.partial(_ffn_ln_body, n_sub=n_sub),
        grid=(t // tm,),
        in_specs=[pl.BlockSpec((tm, d), lambda i: (i, 0))] + [full(a) for a in (wg, wu, wd, g, b)],
        out_specs=pl.BlockSpec((tm, d), lambda i: (i, 0)),
        out_shape=jax.ShapeDtypeStruct((t, d), F32),
        compiler_params=pltpu.CompilerParams(
            dimension_semantics=("parallel",), vmem_limit_bytes=VMEM_LIMIT),
        name="ffn_ln",
    )(x2d, wg, wu, wd, g, b)


def _proj_body(*refs):
    n_hc = D_MODEL // LANES
    h_refs = refs[:n_hc]
    (wcq_ref, wckv_ref, wkra_ref, wkrb_ref, gq_ref, gkv_ref,
     wqa_ref, wqb_ref, wka_ref, wv_ref, wdil_ref,
     cq_ref, sq_ref, ck_ref, sk_ref,
     q_ref, k_ref, v_ref, dil0_ref, dil1_ref, dil2_ref, hperm_ref) = refs[n_hc:]
    hb = jnp.concatenate([hc[...] for hc in h_refs], axis=1).astype(BF16)
    cqn = _rms_norm(_dot(hb, wcq_ref[...]), gq_ref[...]).astype(BF16)
    ckvn = _rms_norm(_dot(hb, wckv_ref[...]), gkv_ref[...]).astype(BF16)
    k_rope = _dot(hb, wkra_ref[...]) * ck_ref[...] + _dot(hb, wkrb_ref[...]) * sk_ref[...]
    ka = _dot(ckvn, wka_ref[...])
    for h in range(MLA_HEADS):
        sl = slice(h * LANES, (h + 1) * LANES)
        k_ref[0, h] = (ka[:, sl] + k_rope).astype(BF16)
    qa_t = _dot_tt(wqa_ref[...], cqn)
    qb_t = _dot_tt(wqb_ref[...], cqn)
    cq_t = cq_ref[...]
    sq_t = sq_ref[...]
    for h in range(MLA_HEADS):
        sl = slice(h * LANES, (h + 1) * LANES)
        q_ref[0, sl] = (qa_t[sl] * cq_t + qb_t[sl] * sq_t).astype(BF16)
    v_ref[0] = _dot_tt(wv_ref[...], ckvn).astype(BF16)

    dil_scale = DIL_HEAD_DIM ** -0.5
    tm = h_refs[0].shape[0]
    for g, (dil_ref, (_, dilation)) in enumerate(zip((dil0_ref, dil1_ref, dil2_ref), DIL_PAIRS)):
        rows_per = tm // dilation
        if dilation == 1:
            hp = hb
        else:
            for r in range(dilation):
                for c, hc in enumerate(h_refs):
                    hperm_ref[r * rows_per:(r + 1) * rows_per, c * LANES:(c + 1) * LANES] = hc[
                        pl.ds(r, rows_per, stride=dilation), :].astype(BF16)
            hp = hperm_ref[...]
        for part in range(3):
            c0 = (part * DIL_GROUPS + g) * DIL_OUT
            res = _dot(hp, wdil_ref[:, c0:c0 + DIL_OUT])
            if part == 0:
                res = res * dil_scale
            res = res.astype(BF16)
            for r in range(dilation):
                dil_ref[0, r, :, part * DIL_OUT:(part + 1) * DIL_OUT] = res[r * rows_per:(r + 1) * rows_per]


def _proj(h2d, weights, tables, *, batch, seq, tm):
    t, d = h2d.shape
    n_s = seq // tm

    def full(a):
        return pl.BlockSpec(a.shape, lambda i: (0,) * a.ndim)

    tab_spec = pl.BlockSpec((tm, LANES), lambda i: (i % n_s, 0))
    tab_t_spec = pl.BlockSpec((LANES, tm), lambda i: (0, i % n_s))
    head_map = lambda i: (i // n_s, 0, i % n_s, 0)
    feat_map = lambda i: (i // n_s, 0, i % n_s)
    dil_specs = [pl.BlockSpec((1, dl, tm // dl, 3 * DIL_OUT), head_map) for _, dl in DIL_PAIRS]
    dil_shapes = [jax.ShapeDtypeStruct((batch, dl, seq // dl, 3 * DIL_OUT), BF16) for _, dl in DIL_PAIRS]
    return pl.pallas_call(
        _proj_body,
        grid=(t // tm,),
        in_specs=[pl.BlockSpec((tm, LANES), functools.partial(lambda c, i: (i, c), c))
                  for c in range(d // LANES)]
        + [full(w) for w in weights] + [tab_t_spec] * 2 + [tab_spec] * 2,
        out_specs=[
            pl.BlockSpec((1, MLA_HEADS * LANES, tm), feat_map),
            pl.BlockSpec((1, MLA_HEADS, tm, LANES), head_map),
            pl.BlockSpec((1, MLA_OUT, tm), feat_map),
        ] + dil_specs,
        out_shape=[
            jax.ShapeDtypeStruct((batch, MLA_HEADS * LANES, seq), BF16),
            jax.ShapeDtypeStruct((batch, MLA_HEADS, seq, LANES), BF16),
            jax.ShapeDtypeStruct((batch, MLA_OUT, seq), BF16),
        ] + dil_shapes,
        scratch_shapes=[pltpu.VMEM((tm, d), BF16)],
        compiler_params=pltpu.CompilerParams(
            dimension_semantics=("parallel",), vmem_limit_bytes=VMEM_LIMIT),
        name="proj",
    )(*([h2d] * (d // LANES)), *weights, *tables)


def _mla_attn_body(q_ref, k_ref, vt_ref, o_ref, st_ref, m_ref, acc_ref, *, tq, tk):
    seq = k_ref.shape[2]
    n_tiles = seq // tq
    n_units = (seq // tk) * n_tiles
    heads = (0, 1)
    n_pieces = tk // MLA_PIECE

    def offsets(unit):
        if isinstance(unit, int):
            return (unit // n_tiles) * tk, (unit % n_tiles) * tq
        return (pl.multiple_of(lax.div(unit, n_tiles) * tk, tk),
                pl.multiple_of(lax.rem(unit, n_tiles) * tq, tq))

    def scores_piece(unit, buf, p, maxes):
        k_off, q_off = offsets(unit)
        rows = slice(p * MLA_PIECE, (p + 1) * MLA_PIECE)
        out = []
        for h in heads:
            st = _dot(k_ref[0, h, pl.ds(k_off + p * MLA_PIECE, MLA_PIECE), :],
                      q_ref[0, h * LANES:(h + 1) * LANES, pl.ds(q_off, tq)])
            st_ref[buf, h, rows] = st
            out.append(jnp.maximum(maxes[h], jnp.max(st, axis=0, keepdims=True)))
        return tuple(out)

    ones_rows = jnp.ones((MLA_DEN_ROWS, MLA_PIECE), BF16)

    def values_piece(k_off, buf, p, m_new, accs):
        rows = slice(p * MLA_PIECE, (p + 1) * MLA_PIECE)
        out = []
        for h in heads:
            pt = jnp.exp2(st_ref[buf, h, rows] - m_new[h]).astype(BF16)
            vtc = jnp.concatenate(
                [vt_ref[0, h * MLA_V:(h + 1) * MLA_V, pl.ds(k_off + p * MLA_PIECE, MLA_PIECE)], ones_rows],
                axis=0)
            out.append(accs[h] + _dot(vtc, pt))
        return tuple(out)

    neg_inf = jnp.full((1, tq), -jnp.inf, F32)

    def half(next_unit, next_buf, unit, buf, unit_max):
        k_off, q_off = offsets(unit)
        cols = pl.ds(q_off, tq)
        m_old = tuple(m_ref[h, :, cols] for h in heads)
        m_new = tuple(jnp.maximum(m_old[h], unit_max[h]) for h in heads)
        accs = tuple(jnp.exp2(m_old[h] - m_new[h]) * acc_ref[h, :, cols] for h in heads)
        next_max = (neg_inf, neg_inf)
        for p in range(n_pieces):
            if next_unit is not None:
                next_max = scores_piece(next_unit, next_buf, p, next_max)
            accs = values_piece(k_off, buf, p, m_new, accs)
        for h in heads:
            m_ref[h, :, cols] = m_new[h]
            acc_ref[h, :, cols] = accs[h]
        return next_max

    def group_step(jj, unit_max):
        for s in range(MLA_UNITS_PER_TRIP):
            u = MLA_UNITS_PER_TRIP * jj + s
            unit_max = half(u + 1, (s + 1) % 2, u, s % 2, unit_max)
        return unit_max

    m_ref[...] = jnp.full(m_ref.shape, -jnp.inf, F32)
    acc_ref[...] = jnp.zeros(acc_ref.shape, F32)
    unit_max = (neg_inf, neg_inf)
    for p in range(n_pieces):
        unit_max = scores_piece(0, 0, p, unit_max)
    unit_max = lax.fori_loop(0, n_units // MLA_UNITS_PER_TRIP - 1, group_step, unit_max)
    for u in range(n_units - MLA_UNITS_PER_TRIP, n_units):
        last = u == n_units - 1
        unit_max = half(None if last else u + 1, None if last else (u + 1) % 2, u, u % 2, unit_max)

    @pl.loop(0, n_tiles)
    def _(tile):
        cols = pl.ds(pl.multiple_of(tile * tq, tq), tq)
        out_t = jnp.concatenate(
            [acc_ref[h, :MLA_V, cols] / acc_ref[h, MLA_V:MLA_V + 1, cols] for h in heads], axis=0)
        o_ref[0, cols, :] = out_t.T.astype(o_ref.dtype)


def _mla_attn(q, k, v, *, tq, tk):
    batch, heads, seq, _ = k.shape
    assert MLA_UNITS_PER_TRIP % 2 == 0 and ((seq // tk) * (seq // tq)) % MLA_UNITS_PER_TRIP == 0
    return pl.pallas_call(
        functools.partial(_mla_attn_body, tq=tq, tk=tk),
        grid=(batch, heads // 2),
        in_specs=[
            pl.BlockSpec((1, 2 * LANES, seq), lambda b, hp: (b, hp, 0)),
            pl.BlockSpec((1, 2, seq, LANES), lambda b, hp: (b, hp, 0, 0)),
            pl.BlockSpec((1, 2 * MLA_V, seq), lambda b, hp: (b, hp, 0)),
        ],
        out_specs=pl.BlockSpec((1, seq, LANES), lambda b, hp: (b, 0, hp)),
        out_shape=jax.ShapeDtypeStruct((batch, seq, MLA_OUT), BF16),
        scratch_shapes=[pltpu.VMEM((2, 2, tk, tq), F32),
                        pltpu.VMEM((2, 1, seq), F32),
                        pltpu.VMEM((2, MLA_V + MLA_DEN_ROWS, seq), F32)],
        compiler_params=pltpu.CompilerParams(
            dimension_semantics=("parallel", "parallel"), vmem_limit_bytes=VMEM_LIMIT),
        name="mla_attn",
    )(q, k, v)


def _dil_attn_body(slopes_ref, q_ref, kl_ref, km_ref, kr_ref, vl_ref, vm_ref, vr_ref,
                   o_ref, lse_ref, kbuf, vbuf, *, group, dilation, tq, sub_len):
    i = pl.program_id(1)
    n_res = q_ref.shape[1]
    first_res = pl.program_id(2) * n_res
    kbuf[:, 0:DIL_HALO] = kl_ref[0]
    kbuf[:, DIL_HALO:DIL_HALO + tq] = km_ref[0]
    kbuf[:, DIL_HALO + tq:] = kr_ref[0]
    vbuf[:, 0:DIL_HALO] = vl_ref[0]
    vbuf[:, DIL_HALO:DIL_HALO + tq] = vm_ref[0]
    vbuf[:, DIL_HALO + tq:] = vr_ref[0]

    win = DIL_SUB + 2 * DIL_HALO
    row = lax.broadcasted_iota(jnp.int32, (DIL_SUB, win), 0)
    col = lax.broadcasted_iota(jnp.int32, (DIL_SUB, win), 1)
    steps = jnp.abs(col - DIL_HALO - row)
    in_band = steps <= DIL_HALF_SPAN
    dist = (steps * dilation).astype(F32)
    for sub in range(tq // DIL_SUB):
        key_pos = i * tq + (sub * DIL_SUB - DIL_HALO) + col
        valid = in_band & (key_pos >= 0) & (key_pos < sub_len)
        rows = slice(sub * DIL_SUB, (sub + 1) * DIL_SUB)
        wrows = slice(sub * DIL_SUB, sub * DIL_SUB + win)
        for rr in range(n_res):
            if dilation == 1:
                out_rows = pl.ds(sub * DIL_SUB, DIL_SUB)
            else:
                out_rows = pl.ds(sub * DIL_SUB * dilation + first_res + rr, DIL_SUB, stride=dilation)
            for h in range(DIL_HEADS_PER_GROUP):
                cols = slice(h * DIL_HEAD_DIM, (h + 1) * DIL_HEAD_DIM)
                slope = slopes_ref[group * DIL_HEADS_PER_GROUP + h]
                s = _dot_nt(q_ref[0, rr, rows, cols], kbuf[rr, wrows, cols]) - slope * dist
                s = jnp.where(valid, s, -jnp.inf)
                m = jnp.max(s, axis=1, keepdims=True)
                e = jnp.exp(s - m)
                den = jnp.sum(e, axis=1, keepdims=True)
                o_ref[0, h, out_rows, :] = _dot(e.astype(BF16), vbuf[rr, wrows, cols]) / den
                lse_ref[0, h, out_rows, :] = jnp.broadcast_to(m + jnp.log(den), (DIL_SUB, DIL_HEAD_DIM))


def _dil_attn(dil_g, slopes, *, group, dilation):
    batch, _, sub_len, _ = dil_g.shape
    seq = sub_len * dilation
    tq = min(DIL_ROWS_PER_STEP, sub_len, DIL_CHUNK_TOKENS // dilation)
    n_q = sub_len // tq
    n_res = min(dilation, DIL_ROWS_PER_STEP // tq)
    halo_per_tile = tq // DIL_HALO
    n_halo = sub_len // DIL_HALO
    q_col, k_col, v_col = 0, 1, 2

    def main(c):
        return pl.BlockSpec((1, n_res, tq, DIL_OUT), lambda b, i, r: (b, r, i, c))

    def left(c):
        return pl.BlockSpec(
            (1, n_res, DIL_HALO, DIL_OUT),
            lambda b, i, r: (b, r, jnp.maximum(i * halo_per_tile - 1, 0), c))

    def right(c):
        return pl.BlockSpec(
            (1, n_res, DIL_HALO, DIL_OUT),
            lambda b, i, r: (b, r, jnp.minimum((i + 1) * halo_per_tile, n_halo - 1), c))

    out_spec = pl.BlockSpec((1, DIL_HEADS_PER_GROUP, tq * dilation, DIL_HEAD_DIM),
                            lambda b, i, r: (b, 0, i, 0))
    out_shape = jax.ShapeDtypeStruct((batch, DIL_HEADS_PER_GROUP, seq, DIL_HEAD_DIM), F32)
    o, lse = pl.pallas_call(
        functools.partial(_dil_attn_body, group=group, dilation=dilation, tq=tq, sub_len=sub_len),
        grid=(batch, n_q, dilation // n_res),
        in_specs=[pl.BlockSpec(memory_space=pltpu.SMEM),
                  main(q_col), left(k_col), main(k_col), right(k_col),
                  left(v_col), main(v_col), right(v_col)],
        out_specs=[out_spec, out_spec],
        out_shape=[out_shape, out_shape],
        scratch_shapes=[pltpu.VMEM((n_res, tq + 2 * DIL_HALO, DIL_OUT), BF16),
                        pltpu.VMEM((n_res, tq + 2 * DIL_HALO, DIL_OUT), BF16)],
        compiler_params=pltpu.CompilerParams(
            dimension_semantics=("parallel", "parallel", "arbitrary"), vmem_limit_bytes=VMEM_LIMIT),
        name=f"dil_attn_g{group}",
    )(slopes, dil_g, dil_g, dil_g, dil_g, dil_g, dil_g, dil_g)
    return o, lse


def _mem_kv_body(mem_ref, w_ref, o_ref):
    o_ref[0] = _dot(mem_ref[0].astype(BF16), w_ref[...]).astype(BF16)


def _mem_kv(mem, w):
    batch, m_len, d = mem.shape
    n = w.shape[1]
    return pl.pallas_call(
        _mem_kv_body,
        grid=(batch,),
        in_specs=[pl.BlockSpec((1, m_len, d), lambda b: (b, 0, 0)),
                  pl.BlockSpec((d, n), lambda b: (0, 0))],
        out_specs=pl.BlockSpec((1, m_len, n), lambda b: (b, 0, 0)),
        out_shape=jax.ShapeDtypeStruct((batch, m_len, n), BF16),
        compiler_params=pltpu.CompilerParams(dimension_semantics=("parallel",)),
        name="mem_kv",
    )(mem, w)


def _merge_body(h_ref, omla_ref, od0_ref, od1_ref, od2_ref, ls0_ref, ls1_ref, ls2_ref, kvm_ref,
                wgate_ref, wmq_ref, wba_ref, wbb_ref, wbc_ref, wo_ref, g_ref, b_ref, out_ref, *, n_sub):
    sub_rows = h_ref.shape[0] // n_sub
    for sub in range(n_sub):
        rows = slice(sub * sub_rows, (sub + 1) * sub_rows)
        _merge_rows(rows, h_ref, omla_ref, od0_ref, od1_ref, od2_ref, ls0_ref, ls1_ref, ls2_ref, kvm_ref,
                    wgate_ref, wmq_ref, wba_ref, wbb_ref, wbc_ref, wo_ref, g_ref, b_ref, out_ref)


def _merge_rows(rows, h_ref, omla_ref, od0_ref, od1_ref, od2_ref, ls0_ref, ls1_ref, ls2_ref, kvm_ref,
                wgate_ref, wmq_ref, wba_ref, wbb_ref, wbc_ref, wo_ref, g_ref, b_ref, out_ref):
    h = h_ref[rows, :]
    hb = h.astype(BF16)

    def heads_to_lanes(ref):
        return jnp.concatenate([ref[0, hh, rows, :] for hh in range(DIL_HEADS_PER_GROUP)], axis=1)

    ls0, ls1, ls2 = heads_to_lanes(ls0_ref), heads_to_lanes(ls1_ref), heads_to_lanes(ls2_ref)
    mx = jnp.maximum(ls0, jnp.maximum(ls1, ls2))
    e0, e1, e2 = jnp.exp(ls0 - mx), jnp.exp(ls1 - mx), jnp.exp(ls2 - mx)
    o_dil = (e0 * heads_to_lanes(od0_ref) + e1 * heads_to_lanes(od1_ref)
             + e2 * heads_to_lanes(od2_ref)) / (e0 + e1 + e2)

    mq = (_dot(hb, wmq_ref[...]) * (MEM_HEAD_DIM ** -0.5)).astype(BF16)
    heads = []
    for hh in range(MEM_HEADS):
        kc = slice(hh * MEM_HEAD_DIM, (hh + 1) * MEM_HEAD_DIM)
        vc = slice(MEM_OUT + hh * MEM_HEAD_DIM, MEM_OUT + (hh + 1) * MEM_HEAD_DIM)
        s = _dot_nt(mq[:, kc], kvm_ref[0, :, kc])
        p = jnp.exp(s - jnp.max(s, axis=1, keepdims=True))
        heads.append(_dot(p.astype(BF16), kvm_ref[0, :, vc]) / jnp.sum(p, axis=1, keepdims=True))
    o_mem = jnp.concatenate(heads, axis=1)

    y_a = _dot(omla_ref[rows, :], wba_ref[...])
    y_b = _dot(o_dil.astype(BF16), wbb_ref[...])
    y_c = _dot(o_mem.astype(BF16), wbc_ref[...])
    d = h.shape[1]
    merged = (jax.nn.sigmoid(_dot(hb, wgate_ref[:, :d])) * y_a
              + jax.nn.sigmoid(_dot(hb, wgate_ref[:, d:2 * d])) * y_b
              + jax.nn.sigmoid(_dot(hb, wgate_ref[:, 2 * d:])) * y_c)
    mix = _dot(merged.astype(BF16), wo_ref[...])
    out_ref[rows, :] = _layer_norm(ALPHA * h + mix, g_ref[...], b_ref[...])


def _merge(h2d, o_mla, o_dil, lse_dil, kv_mem, weights, g, b, *, seq, tm, n_sub):
    t, d = h2d.shape
    n_s = seq // tm

    def rows(width):
        return pl.BlockSpec((tm, width), lambda i: (i, 0))

    def full(a):
        return pl.BlockSpec(a.shape, lambda i: (0,) * a.ndim, pipeline_mode=pl.Buffered(1))

    m_len, kv_cols = kv_mem.shape[1:]
    return pl.pallas_call(
        functools.partial(_merge_body, n_sub=n_sub),
        grid=(t // tm,),
        in_specs=[rows(d), rows(MLA_OUT)]
        + [pl.BlockSpec((1, DIL_HEADS_PER_GROUP, tm, DIL_HEAD_DIM),
                        lambda i: (i // n_s, 0, i % n_s, 0))] * 6
        + [pl.BlockSpec((1, m_len, kv_cols), lambda i: (i // n_s, 0, 0))]
        + [full(w) for w in weights] + [full(g), full(b)],
        out_specs=rows(d),
        out_shape=jax.ShapeDtypeStruct((t, d), F32),
        compiler_params=pltpu.CompilerParams(
            dimension_semantics=("parallel",), vmem_limit_bytes=VMEM_LIMIT),
        name="merge",
    )(h2d, o_mla, *o_dil, *lse_dil, kv_mem, *weights, g, b)


def _swap_halves(w):
    half = w.shape[-1] // 2
    return jnp.concatenate([w[..., half:], w[..., :half]], axis=-1)


def _prep_mla_weights(w_in, w_uq, w_ukv):
    d = w_in.shape[0]
    o_q, o_kv, o_kr = 0, MLA_Q_LORA, MLA_Q_LORA + MLA_KV_LORA
    w_cq = w_in[:, o_q:o_q + MLA_Q_LORA]
    w_ckv = w_in[:, o_kv:o_kv + MLA_KV_LORA]
    w_kr = w_in[:, o_kr:o_kr + MLA_ROPE]
    pad_tail = LANES - MLA_NOPE - MLA_ROPE
    z_nope = jnp.zeros((d, MLA_NOPE), F32)
    z_tail = jnp.zeros((d, pad_tail), F32)
    w_kra = jnp.concatenate([z_nope, w_kr, z_tail], axis=1)
    w_krb = jnp.concatenate([z_nope, _swap_halves(w_kr), z_tail], axis=1)

    uq = w_uq.reshape(MLA_Q_LORA, MLA_HEADS, MLA_NOPE + MLA_ROPE)
    uq_nope, uq_pe = uq[..., :MLA_NOPE], uq[..., MLA_NOPE:]
    zq_nope = jnp.zeros_like(uq_nope)
    zq_tail = jnp.zeros((MLA_Q_LORA, MLA_HEADS, pad_tail), F32)
    w_qa = jnp.concatenate([uq_nope, uq_pe, zq_tail], axis=-1).reshape(MLA_Q_LORA, MLA_HEADS * LANES)
    w_qb = jnp.concatenate([zq_nope, _swap_halves(uq_pe), zq_tail], axis=-1).reshape(
        MLA_Q_LORA, MLA_HEADS * LANES)

    ukv = w_ukv.reshape(MLA_KV_LORA, MLA_HEADS, MLA_NOPE + MLA_V)
    uk, uv = ukv[..., :MLA_NOPE], ukv[..., MLA_NOPE:]
    w_ka = jnp.concatenate(
        [uk, jnp.zeros((MLA_KV_LORA, MLA_HEADS, LANES - MLA_NOPE), F32)], axis=-1).reshape(
            MLA_KV_LORA, MLA_HEADS * LANES)
    w_v = uv.reshape(MLA_KV_LORA, MLA_HEADS * MLA_V)
    return [w.astype(BF16) for w in (w_cq, w_ckv, w_kra, w_krb)], [w.astype(BF16) for w in (w_qa, w_qb, w_ka, w_v)]


def _rope_tables(seq):
    pos = np.arange(seq, dtype=np.float64)
    inv = 1.0 / (ROPE_THETA ** (np.arange(0, MLA_ROPE, 2, dtype=np.float64) / MLA_ROPE))
    pad_tail = LANES - MLA_NOPE - MLA_ROPE
    q_scale = (MLA_NOPE + MLA_ROPE) ** -0.5 * LOG2_E
    ang = pos[:, None] * inv[None, :]
    cos, sin = np.cos(ang), np.sin(ang)

    def tables(lead_one):
        c = np.concatenate([np.full((seq, MLA_NOPE), lead_one), cos, cos, np.zeros((seq, pad_tail))], axis=1)
        s = np.concatenate([np.zeros((seq, MLA_NOPE)), -sin, sin, np.zeros((seq, pad_tail))], axis=1)
        return c, s

    c_q, s_q = tables(1.0)
    c_k, s_k = tables(0.0)
    return tuple(np.ascontiguousarray(a, dtype=np.float32)
                 for a in ((c_q * q_scale).T, (s_q * q_scale).T, c_k, s_k))


def kernel(x, mem, w_in, mla_q_norm, mla_kv_norm, w_uq, w_ukv, w_mem_kv, w_br_mla, w_br_dil, w_br_mem, w_o,
           ffn1_w_gate, ffn1_w_up, ffn1_w_down, ffn2_w_gate, ffn2_w_up, ffn2_w_down,
           ln1_g, ln1_b, ln2_g, ln2_b, ln3_g, ln3_b):
    batch, seq, d = x.shape
    t = batch * seq
    tm_ffn = min(1024, t)
    tm_proj = min(512, seq)
    tm_merge = min(512, seq)
    tables = _rope_tables(seq)
    slopes = 2.0 ** (-8.0 * jnp.arange(1, DIL_HEADS + 1, dtype=F32) / DIL_HEADS)

    h = x.reshape(t, d)
    for l in range(DEPTH):
        bf = lambda w: w[l].astype(BF16)
        row = lambda v: v[l].reshape(1, -1)
        h = _ffn_ln(h, bf(ffn1_w_gate), bf(ffn1_w_up), bf(ffn1_w_down), row(ln1_g), row(ln1_b),
                    tm=tm_ffn, n_sub=FFN_SUB_TILES)

        w_in_l = w_in[l]
        o_dil_cols = MLA_Q_LORA + MLA_KV_LORA + MLA_ROPE
        o_memq = o_dil_cols + DIL_QKV
        o_gate = o_memq + MEM_OUT
        w_c, w_u = _prep_mla_weights(w_in_l, w_uq[l], w_ukv[l])
        proj_weights = w_c + [row(mla_q_norm), row(mla_kv_norm)] + w_u + [
            w_in_l[:, o_dil_cols:o_memq].astype(BF16)]
        q, k, v, *dil = _proj(h, proj_weights, tables, batch=batch, seq=seq, tm=tm_proj)

        o_mla = _mla_attn(q, k, v, tq=min(512, seq), tk=min(1024, seq)).reshape(t, MLA_OUT)
        dil_parts = [_dil_attn(dil[g], slopes, group=g, dilation=dl)
                     for g, (_, dl) in enumerate(DIL_PAIRS)]
        kv_mem = _mem_kv(mem, bf(w_mem_kv))

        merge_weights = [w_in_l[:, o_gate:].astype(BF16), w_in_l[:, o_memq:o_gate].astype(BF16),
                         bf(w_br_mla), bf(w_br_dil), bf(w_br_mem), bf(w_o)]
        h = _merge(h, o_mla, [p[0] for p in dil_parts], [p[1] for p in dil_parts], kv_mem,
                   merge_weights, row(ln2_g), row(ln2_b), seq=seq, tm=tm_merge, n_sub=2)

        h = _ffn_ln(h, bf(ffn2_w_gate), bf(ffn2_w_up), bf(ffn2_w_down), row(ln3_g), row(ln3_b),
                    tm=tm_ffn, n_sub=FFN_SUB_TILES)
    return h.reshape(batch, seq, d)
```

```python
import functools
import math

import jax
import jax.numpy as jnp
import numpy as np
from jax import lax
from jax.experimental import pallas as pl
from jax.experimental.pallas import tpu as pltpu

F32 = jnp.float32
BF16 = jnp.bfloat16

D_MODEL = 1024
DEPTH = 1
MLA_HEADS = 8
MLA_Q_LORA = 256
MLA_KV_LORA = 256
MLA_NOPE = 64
MLA_ROPE = 32
MLA_V = 64
ROPE_THETA = 10000.0
DIL_PAIRS = ((128, 1), (512, 4), (2048, 16))
DIL_GROUPS = 3
DIL_HEADS_PER_GROUP = 4
DIL_HEAD_DIM = 128
DIL_HEADS = DIL_GROUPS * DIL_HEADS_PER_GROUP
MEM_HEADS = 4
MEM_HEAD_DIM = 128
EPS = 1e-5
ALPHA = (2 * DEPTH) ** 0.25

LOG2_E = math.log2(math.e)
LANES = 128
DIL_OUT = DIL_HEADS_PER_GROUP * DIL_HEAD_DIM
DIL_QKV = 3 * DIL_HEADS * DIL_HEAD_DIM
MEM_OUT = MEM_HEADS * MEM_HEAD_DIM
MLA_OUT = MLA_HEADS * MLA_V
MLA_DEN_ROWS = 16
MLA_PIECE = 256
MLA_UNITS_PER_TRIP = 8
W_MLA_COLS = MLA_Q_LORA + MLA_KV_LORA + MLA_ROPE
WP_DIL = 640
WP_MEMQ = 7168
WP_COLS = WP_MEMQ + MEM_OUT + 3 * D_MODEL
WP_PROJ_BLOCK = WP_COLS // 2
WP_MERGE_BLOCK = WP_COLS // 3
assert WP_DIL >= W_MLA_COLS and WP_PROJ_BLOCK >= WP_DIL + DIL_QKV and WP_MEMQ == 2 * WP_MERGE_BLOCK
DIL_HALF_SPAN = 64
DIL_HALO = 64
DIL_SUB = 128
DIL_CHUNK_TOKENS = 2048
DIL_ROWS_PER_STEP = 1024
FFN_SUB_TILES = 4
VMEM_LIMIT = 56 * 1024 * 1024

assert all(w // 2 // d == DIL_HALF_SPAN for w, d in DIL_PAIRS)

_NT = (((1,), (1,)), ((), ()))


def _dot(a, b):
    return jnp.dot(a, b, preferred_element_type=F32)


def _dot_nt(a, b):
    return lax.dot_general(a, b, _NT, preferred_element_type=F32)


def _dot_tt(w, x):
    return lax.dot_general(w, x, (((0,), (1,)), ((), ())), preferred_element_type=F32)


def _layer_norm(y, g, b):
    mu = jnp.mean(y, axis=-1, keepdims=True)
    yc = y - mu
    var = jnp.mean(yc * yc, axis=-1, keepdims=True)
    return yc * lax.rsqrt(var + EPS) * g + b


def _rms_norm(y, g):
    return y * lax.rsqrt(jnp.mean(y * y, axis=-1, keepdims=True) + EPS) * g


def _ffn_ln_body(x_ref, wg_ref, wu_ref, wd_ref, g_ref, b_ref, o_ref, *, n_sub):
    sub_rows = x_ref.shape[0] // n_sub
    for sub in range(n_sub):
        rows = slice(sub * sub_rows, (sub + 1) * sub_rows)
        x = x_ref[rows, :]
        xb = x.astype(BF16)
        gate = _dot(xb, wg_ref[...])
        up = _dot(xb, wu_ref[...])
        act = (gate * jax.nn.sigmoid(gate) * up).astype(BF16)
        y = ALPHA * x + 0.5 * _dot(act, wd_ref[...])
        o_ref[rows, :] = _layer_norm(y, g_ref[...], b_ref[...])


def _ffn_ln(x2d, wg, wu, wd, g, b, *, tm, n_sub):
    t, d = x2d.shape

    def full(a):
        return pl.BlockSpec(a.shape, lambda i: (0,) * a.ndim, pipeline_mode=pl.Buffered(1))

    return pl.pallas_call(
        functools.partial(_ffn_ln_body, n_sub=n_sub),
        grid=(t // tm,),
        in_specs=[pl.BlockSpec((tm, d), lambda i: (i, 0))] + [full(a) for a in (wg, wu, wd, g, b)],
        out_specs=pl.BlockSpec((tm, d), lambda i: (i, 0)),
        out_shape=jax.ShapeDtypeStruct((t, d), F32),
        compiler_params=pltpu.CompilerParams(
            dimension_semantics=("parallel",), vmem_limit_bytes=VMEM_LIMIT),
        name="ffn_ln",
    )(x2d, wg, wu, wd, g, b)


def _proj_body(*refs):
    n_hc = D_MODEL // LANES
    h_refs = refs[:n_hc]
    (win_ref, wkra_ref, wkrb_ref, gq_ref, gkv_ref,
     wqa_ref, wqb_ref, wka_ref, wv_ref,
     cq_ref, sq_ref, ck_ref, sk_ref,
     q_ref, k_ref, v_ref, dil0_ref, dil1_ref, dil2_ref, hperm_ref) = refs[n_hc:]
    hb = jnp.concatenate([hc[...] for hc in h_refs], axis=1).astype(BF16)
    cqn = _rms_norm(_dot(hb, win_ref[:, :MLA_Q_LORA]), gq_ref[...]).astype(BF16)
    ckvn = _rms_norm(_dot(hb, win_ref[:, MLA_Q_LORA:MLA_Q_LORA + MLA_KV_LORA]), gkv_ref[...]).astype(BF16)
    k_rope = _dot(hb, wkra_ref[...]) * ck_ref[...] + _dot(hb, wkrb_ref[...]) * sk_ref[...]
    ka = _dot(ckvn, wka_ref[...])
    for h in range(MLA_HEADS):
        sl = slice(h * LANES, (h + 1) * LANES)
        k_ref[0, h] = (ka[:, sl] + k_rope).astype(BF16)
    qa_t = _dot_tt(wqa_ref[...], cqn)
    qb_t = _dot_tt(wqb_ref[...], cqn)
    cq_t = cq_ref[...]
    sq_t = sq_ref[...]
    for h in range(MLA_HEADS):
        sl = slice(h * LANES, (h + 1) * LANES)
        q_ref[0, sl] = (qa_t[sl] * cq_t + qb_t[sl] * sq_t).astype(BF16)
    v_ref[0] = _dot_tt(wv_ref[...], ckvn).astype(BF16)

    dil_scale = DIL_HEAD_DIM ** -0.5
    tm = h_refs[0].shape[0]
    for g, (dil_ref, (_, dilation)) in enumerate(zip((dil0_ref, dil1_ref, dil2_ref), DIL_PAIRS)):
        rows_per = tm // dilation
        if dilation == 1:
            hp = hb
        else:
            for r in range(dilation):
                for c, hc in enumerate(h_refs):
                    hperm_ref[r * rows_per:(r + 1) * rows_per, c * LANES:(c + 1) * LANES] = hc[
                        pl.ds(r, rows_per, stride=dilation), :].astype(BF16)
            hp = hperm_ref[...]
        for part in range(3):
            c0 = WP_DIL + (part * DIL_GROUPS + g) * DIL_OUT
            res = _dot(hp, win_ref[:, c0:c0 + DIL_OUT])
            if part == 0:
                res = res * dil_scale
            res = res.astype(BF16)
            for r in range(dilation):
                dil_ref[0, r, :, part * DIL_OUT:(part + 1) * DIL_OUT] = res[r * rows_per:(r + 1) * rows_per]


def _proj(h2d, w_in_p, weights, tables, *, batch, seq, tm):
    t, d = h2d.shape
    n_s = seq // tm

    def full(a):
        return pl.BlockSpec(a.shape, lambda i: (0,) * a.ndim)

    w_in_spec = pl.BlockSpec((d, WP_PROJ_BLOCK), lambda i: (0, 0), pipeline_mode=pl.Buffered(1))

    tab_spec = pl.BlockSpec((tm, LANES), lambda i: (i % n_s, 0))
    tab_t_spec = pl.BlockSpec((LANES, tm), lambda i: (0, i % n_s))
    head_map = lambda i: (i // n_s, 0, i % n_s, 0)
    feat_map = lambda i: (i // n_s, 0, i % n_s)
    dil_specs = [pl.BlockSpec((1, dl, tm // dl, 3 * DIL_OUT), head_map) for _, dl in DIL_PAIRS]
    dil_shapes = [jax.ShapeDtypeStruct((batch, dl, seq // dl, 3 * DIL_OUT), BF16) for _, dl in DIL_PAIRS]
    return pl.pallas_call(
        _proj_body,
        grid=(t // tm,),
        in_specs=[pl.BlockSpec((tm, LANES), functools.partial(lambda c, i: (i, c), c))
                  for c in range(d // LANES)]
        + [w_in_spec] + [full(w) for w in weights] + [tab_t_spec] * 2 + [tab_spec] * 2,
        out_specs=[
            pl.BlockSpec((1, MLA_HEADS * LANES, tm), feat_map),
            pl.BlockSpec((1, MLA_HEADS, tm, LANES), head_map),
            pl.BlockSpec((1, MLA_OUT, tm), feat_map),
        ] + dil_specs,
        out_shape=[
            jax.ShapeDtypeStruct((batch, MLA_HEADS * LANES, seq), BF16),
            jax.ShapeDtypeStruct((batch, MLA_HEADS, seq, LANES), BF16),
            jax.ShapeDtypeStruct((batch, MLA_OUT, seq), BF16),
        ] + dil_shapes,
        scratch_shapes=[pltpu.VMEM((tm, d), BF16)],
        compiler_params=pltpu.CompilerParams(
            dimension_semantics=("parallel",), vmem_limit_bytes=VMEM_LIMIT),
        name="proj",
    )(*([h2d] * (d // LANES)), w_in_p, *weights, *tables)


def _mla_attn_body(q_ref, k_ref, vt_ref, o_ref, st_ref, m_ref, acc_ref, *, tq, tk):
    seq = k_ref.shape[2]
    n_tiles = seq // tq
    n_units = (seq // tk) * n_tiles
    heads = (0, 1)
    n_pieces = tk // MLA_PIECE

    def offsets(unit):
        if isinstance(unit, int):
            return (unit // n_tiles) * tk, (unit % n_tiles) * tq
        return (pl.multiple_of(lax.div(unit, n_tiles) * tk, tk),
                pl.multiple_of(lax.rem(unit, n_tiles) * tq, tq))

    def scores_piece(unit, buf, p, maxes):
        k_off, q_off = offsets(unit)
        rows = slice(p * MLA_PIECE, (p + 1) * MLA_PIECE)
        out = []
        for h in heads:
            st = _dot(k_ref[0, h, pl.ds(k_off + p * MLA_PIECE, MLA_PIECE), :],
                      q_ref[0, h * LANES:(h + 1) * LANES, pl.ds(q_off, tq)])
            st_ref[buf, h, rows] = st
            out.append(jnp.maximum(maxes[h], jnp.max(st, axis=0, keepdims=True)))
        return tuple(out)

    ones_rows = jnp.ones((MLA_DEN_ROWS, MLA_PIECE), BF16)

    def values_piece(k_off, buf, p, m_new, accs):
        rows = slice(p * MLA_PIECE, (p + 1) * MLA_PIECE)
        out = []
        for h in heads:
            pt = jnp.exp2(st_ref[buf, h, rows] - m_new[h]).astype(BF16)
            vtc = jnp.concatenate(
                [vt_ref[0, h * MLA_V:(h + 1) * MLA_V, pl.ds(k_off + p * MLA_PIECE, MLA_PIECE)], ones_rows],
                axis=0)
            out.append(accs[h] + _dot(vtc, pt))
        return tuple(out)

    neg_inf = jnp.full((1, tq), -jnp.inf, F32)

    def half(next_unit, next_buf, unit, buf, unit_max):
        k_off, q_off = offsets(unit)
        cols = pl.ds(q_off, tq)
        m_old = tuple(m_ref[h, :, cols] for h in heads)
        m_new = tuple(jnp.maximum(m_old[h], unit_max[h]) for h in heads)
        accs = tuple(jnp.exp2(m_old[h] - m_new[h]) * acc_ref[h, :, cols] for h in heads)
        next_max = (neg_inf, neg_inf)
        for p in range(n_pieces):
            if next_unit is not None:
                next_max = scores_piece(next_unit, next_buf, p, next_max)
            accs = values_piece(k_off, buf, p, m_new, accs)
        for h in heads:
            m_ref[h, :, cols] = m_new[h]
            acc_ref[h, :, cols] = accs[h]
        return next_max

    def group_step(jj, unit_max):
        for s in range(MLA_UNITS_PER_TRIP):
            u = MLA_UNITS_PER_TRIP * jj + s
            unit_max = half(u + 1, (s + 1) % 2, u, s % 2, unit_max)
        return unit_max

    m_ref[...] = jnp.full(m_ref.shape, -jnp.inf, F32)
    acc_ref[...] = jnp.zeros(acc_ref.shape, F32)
    unit_max = (neg_inf, neg_inf)
    for p in range(n_pieces):
        unit_max = scores_piece(0, 0, p, unit_max)
    unit_max = lax.fori_loop(0, n_units // MLA_UNITS_PER_TRIP - 1, group_step, unit_max)
    for u in range(n_units - MLA_UNITS_PER_TRIP, n_units):
        last = u == n_units - 1
        unit_max = half(None if last else u + 1, None if last else (u + 1) % 2, u, u % 2, unit_max)

    @pl.loop(0, n_tiles)
    def _(tile):
        cols = pl.ds(pl.multiple_of(tile * tq, tq), tq)
        out_t = jnp.concatenate(
            [acc_ref[h, :MLA_V, cols] / acc_ref[h, MLA_V:MLA_V + 1, cols] for h in heads], axis=0)
        o_ref[0, cols, :] = out_t.T.astype(o_ref.dtype)


def _mla_attn(q, k, v, *, tq, tk):
    batch, heads, seq, _ = k.shape
    assert MLA_UNITS_PER_TRIP % 2 == 0 and ((seq // tk) * (seq // tq)) % MLA_UNITS_PER_TRIP == 0
    return pl.pallas_call(
        functools.partial(_mla_attn_body, tq=tq, tk=tk),
        grid=(batch, heads // 2),
        in_specs=[
            pl.BlockSpec((1, 2 * LANES, seq), lambda b, hp: (b, hp, 0)),
            pl.BlockSpec((1, 2, seq, LANES), lambda b, hp: (b, hp, 0, 0)),
            pl.BlockSpec((1, 2 * MLA_V, seq), lambda b, hp: (b, hp, 0)),
        ],
        out_specs=pl.BlockSpec((1, seq, LANES), lambda b, hp: (b, 0, hp)),
        out_shape=jax.ShapeDtypeStruct((batch, seq, MLA_OUT), BF16),
        scratch_shapes=[pltpu.VMEM((2, 2, tk, tq), F32),
                        pltpu.VMEM((2, 1, seq), F32),
                        pltpu.VMEM((2, MLA_V + MLA_DEN_ROWS, seq), F32)],
        compiler_params=pltpu.CompilerParams(
            dimension_semantics=("parallel", "parallel"), vmem_limit_bytes=VMEM_LIMIT),
        name="mla_attn",
    )(q, k, v)


def _dil_attn_body(slopes_ref, q_ref, kl_ref, km_ref, kr_ref, vl_ref, vm_ref, vr_ref,
                   o_ref, lse_ref, kbuf, vbuf, *, group, dilation, tq, sub_len):
    i = pl.program_id(1)
    n_res = q_ref.shape[1]
    first_res = pl.program_id(2) * n_res
    kbuf[:, 0:DIL_HALO] = kl_ref[0]
    kbuf[:, DIL_HALO:DIL_HALO + tq] = km_ref[0]
    kbuf[:, DIL_HALO + tq:] = kr_ref[0]
    vbuf[:, 0:DIL_HALO] = vl_ref[0]
    vbuf[:, DIL_HALO:DIL_HALO + tq] = vm_ref[0]
    vbuf[:, DIL_HALO + tq:] = vr_ref[0]

    win = DIL_SUB + 2 * DIL_HALO
    row = lax.broadcasted_iota(jnp.int32, (DIL_SUB, win), 0)
    col = lax.broadcasted_iota(jnp.int32, (DIL_SUB, win), 1)
    steps = jnp.abs(col - DIL_HALO - row)
    in_band = steps <= DIL_HALF_SPAN
    dist = (steps * dilation).astype(F32)
    for sub in range(tq // DIL_SUB):
        key_pos = i * tq + (sub * DIL_SUB - DIL_HALO) + col
        valid = in_band & (key_pos >= 0) & (key_pos < sub_len)
        rows = slice(sub * DIL_SUB, (sub + 1) * DIL_SUB)
        wrows = slice(sub * DIL_SUB, sub * DIL_SUB + win)
        for rr in range(n_res):
            if dilation == 1:
                out_rows = pl.ds(sub * DIL_SUB, DIL_SUB)
            else:
                out_rows = pl.ds(sub * DIL_SUB * dilation + first_res + rr, DIL_SUB, stride=dilation)
            for h in range(DIL_HEADS_PER_GROUP):
                cols = slice(h * DIL_HEAD_DIM, (h + 1) * DIL_HEAD_DIM)
                slope = slopes_ref[group * DIL_HEADS_PER_GROUP + h]
                s = _dot_nt(q_ref[0, rr, rows, cols], kbuf[rr, wrows, cols]) - slope * dist
                s = jnp.where(valid, s, -jnp.inf)
                m = jnp.max(s, axis=1, keepdims=True)
                e = jnp.exp(s - m)
                den = jnp.sum(e, axis=1, keepdims=True)
                o_ref[0, h, out_rows, :] = _dot(e.astype(BF16), vbuf[rr, wrows, cols]) / den
                lse_ref[0, h, out_rows, :] = jnp.broadcast_to(m + jnp.log(den), (DIL_SUB, DIL_HEAD_DIM))


def _dil_attn(dil_g, slopes, *, group, dilation):
    batch, _, sub_len, _ = dil_g.shape
    seq = sub_len * dilation
    tq = min(DIL_ROWS_PER_STEP, sub_len, DIL_CHUNK_TOKENS // dilation)
    n_q = sub_len // tq
    n_res = min(dilation, DIL_ROWS_PER_STEP // tq)
    halo_per_tile = tq // DIL_HALO
    n_halo = sub_len // DIL_HALO
    q_col, k_col, v_col = 0, 1, 2

    def main(c):
        return pl.BlockSpec((1, n_res, tq, DIL_OUT), lambda b, i, r: (b, r, i, c))

    def left(c):
        return pl.BlockSpec(
            (1, n_res, DIL_HALO, DIL_OUT),
            lambda b, i, r: (b, r, jnp.maximum(i * halo_per_tile - 1, 0), c))

    def right(c):
        return pl.BlockSpec(
            (1, n_res, DIL_HALO, DIL_OUT),
            lambda b, i, r: (b, r, jnp.minimum((i + 1) * halo_per_tile, n_halo - 1), c))

    out_spec = pl.BlockSpec((1, DIL_HEADS_PER_GROUP, tq * dilation, DIL_HEAD_DIM),
                            lambda b, i, r: (b, 0, i, 0))
    out_shape = jax.ShapeDtypeStruct((batch, DIL_HEADS_PER_GROUP, seq, DIL_HEAD_DIM), F32)
    o, lse = pl.pallas_call(
        functools.partial(_dil_attn_body, group=group, dilation=dilation, tq=tq, sub_len=sub_len),
        grid=(batch, n_q, dilation // n_res),
        in_specs=[pl.BlockSpec(memory_space=pltpu.SMEM),
                  main(q_col), left(k_col), main(k_col), right(k_col),
                  left(v_col), main(v_col), right(v_col)],
        out_specs=[out_spec, out_spec],
        out_shape=[out_shape, out_shape],
        scratch_shapes=[pltpu.VMEM((n_res, tq + 2 * DIL_HALO, DIL_OUT), BF16),
                        pltpu.VMEM((n_res, tq + 2 * DIL_HALO, DIL_OUT), BF16)],
        compiler_params=pltpu.CompilerParams(
            dimension_semantics=("parallel", "parallel", "arbitrary"), vmem_limit_bytes=VMEM_LIMIT),
        name=f"dil_attn_g{group}",
    )(slopes, dil_g, dil_g, dil_g, dil_g, dil_g, dil_g, dil_g)
    return o, lse


def _mem_kv_body(mem_ref, w_ref, o_ref):
    o_ref[0] = _dot(mem_ref[0].astype(BF16), w_ref[...]).astype(BF16)


def _mem_kv(mem, w):
    batch, m_len, d = mem.shape
    n = w.shape[1]
    return pl.pallas_call(
        _mem_kv_body,
        grid=(batch,),
        in_specs=[pl.BlockSpec((1, m_len, d), lambda b: (b, 0, 0)),
                  pl.BlockSpec((d, n), lambda b: (0, 0))],
        out_specs=pl.BlockSpec((1, m_len, n), lambda b: (b, 0, 0)),
        out_shape=jax.ShapeDtypeStruct((batch, m_len, n), BF16),
        compiler_params=pltpu.CompilerParams(dimension_semantics=("parallel",)),
        name="mem_kv",
    )(mem, w)


def _merge_body(h_ref, omla_ref, od0_ref, od1_ref, od2_ref, ls0_ref, ls1_ref, ls2_ref, kvm_ref,
                wmg_ref, wba_ref, wbb_ref, wbc_ref, wo_ref, g_ref, b_ref, out_ref, *, n_sub):
    sub_rows = h_ref.shape[0] // n_sub
    for sub in range(n_sub):
        rows = slice(sub * sub_rows, (sub + 1) * sub_rows)
        _merge_rows(rows, h_ref, omla_ref, od0_ref, od1_ref, od2_ref, ls0_ref, ls1_ref, ls2_ref, kvm_ref,
                    wmg_ref, wba_ref, wbb_ref, wbc_ref, wo_ref, g_ref, b_ref, out_ref)


def _merge_rows(rows, h_ref, omla_ref, od0_ref, od1_ref, od2_ref, ls0_ref, ls1_ref, ls2_ref, kvm_ref,
                wmg_ref, wba_ref, wbb_ref, wbc_ref, wo_ref, g_ref, b_ref, out_ref):
    h = h_ref[rows, :]
    hb = h.astype(BF16)

    def heads_to_lanes(ref):
        return jnp.concatenate([ref[0, hh, rows, :] for hh in range(DIL_HEADS_PER_GROUP)], axis=1)

    ls0, ls1, ls2 = heads_to_lanes(ls0_ref), heads_to_lanes(ls1_ref), heads_to_lanes(ls2_ref)
    mx = jnp.maximum(ls0, jnp.maximum(ls1, ls2))
    e0, e1, e2 = jnp.exp(ls0 - mx), jnp.exp(ls1 - mx), jnp.exp(ls2 - mx)
    o_dil = (e0 * heads_to_lanes(od0_ref) + e1 * heads_to_lanes(od1_ref)
             + e2 * heads_to_lanes(od2_ref)) / (e0 + e1 + e2)

    mq = (_dot(hb, wmg_ref[:, :MEM_OUT]) * (MEM_HEAD_DIM ** -0.5)).astype(BF16)
    heads = []
    for hh in range(MEM_HEADS):
        kc = slice(hh * MEM_HEAD_DIM, (hh + 1) * MEM_HEAD_DIM)
        vc = slice(MEM_OUT + hh * MEM_HEAD_DIM, MEM_OUT + (hh + 1) * MEM_HEAD_DIM)
        s = _dot_nt(mq[:, kc], kvm_ref[0, :, kc])
        p = jnp.exp(s - jnp.max(s, axis=1, keepdims=True))
        heads.append(_dot(p.astype(BF16), kvm_ref[0, :, vc]) / jnp.sum(p, axis=1, keepdims=True))
    o_mem = jnp.concatenate(heads, axis=1)

    y_a = _dot(omla_ref[rows, :], wba_ref[...])
    y_b = _dot(o_dil.astype(BF16), wbb_ref[...])
    y_c = _dot(o_mem.astype(BF16), wbc_ref[...])
    d = h.shape[1]
    gate = lambda br: jax.nn.sigmoid(_dot(hb, wmg_ref[:, MEM_OUT + br * d:MEM_OUT + (br + 1) * d]))
    merged = gate(0) * y_a + gate(1) * y_b + gate(2) * y_c
    mix = _dot(merged.astype(BF16), wo_ref[...])
    out_ref[rows, :] = _layer_norm(ALPHA * h + mix, g_ref[...], b_ref[...])


def _merge(h2d, o_mla, o_dil, lse_dil, kv_mem, w_in_p, weights, g, b, *, seq, tm, n_sub):
    t, d = h2d.shape
    n_s = seq // tm
    w_in_spec = pl.BlockSpec((d, WP_MERGE_BLOCK), lambda i: (0, WP_MEMQ // WP_MERGE_BLOCK),
                             pipeline_mode=pl.Buffered(1))

    def rows(width):
        return pl.BlockSpec((tm, width), lambda i: (i, 0))

    def full(a):
        return pl.BlockSpec(a.shape, lambda i: (0,) * a.ndim, pipeline_mode=pl.Buffered(1))

    m_len, kv_cols = kv_mem.shape[1:]
    return pl.pallas_call(
        functools.partial(_merge_body, n_sub=n_sub),
        grid=(t // tm,),
        in_specs=[rows(d), rows(MLA_OUT)]
        + [pl.BlockSpec((1, DIL_HEADS_PER_GROUP, tm, DIL_HEAD_DIM),
                        lambda i: (i // n_s, 0, i % n_s, 0))] * 6
        + [pl.BlockSpec((1, m_len, kv_cols), lambda i: (i // n_s, 0, 0))]
        + [w_in_spec] + [full(w) for w in weights] + [full(g), full(b)],
        out_specs=rows(d),
        out_shape=jax.ShapeDtypeStruct((t, d), F32),
        compiler_params=pltpu.CompilerParams(
            dimension_semantics=("parallel",), vmem_limit_bytes=VMEM_LIMIT),
        name="merge",
    )(h2d, o_mla, *o_dil, *lse_dil, kv_mem, w_in_p, *weights, g, b)


def _swap_halves(w):
    half = w.shape[-1] // 2
    return jnp.concatenate([w[..., half:], w[..., :half]], axis=-1)


def _pad_w_in(w_in):
    d = w_in.shape[0]
    pieces = [w_in[:, :W_MLA_COLS], jnp.zeros((d, WP_DIL - W_MLA_COLS), w_in.dtype),
              w_in[:, W_MLA_COLS:W_MLA_COLS + DIL_QKV],
              jnp.zeros((d, WP_MEMQ - WP_DIL - DIL_QKV), w_in.dtype),
              w_in[:, W_MLA_COLS + DIL_QKV:]]
    return jnp.concatenate(pieces, axis=1).astype(BF16)


def _prep_mla_weights(w_in, w_uq, w_ukv):
    d = w_in.shape[0]
    o_kr = MLA_Q_LORA + MLA_KV_LORA
    w_kr = w_in[:, o_kr:o_kr + MLA_ROPE]
    pad_tail = LANES - MLA_NOPE - MLA_ROPE
    z_nope = jnp.zeros((d, MLA_NOPE), F32)
    z_tail = jnp.zeros((d, pad_tail), F32)
    w_kra = jnp.concatenate([z_nope, w_kr, z_tail], axis=1)
    w_krb = jnp.concatenate([z_nope, _swap_halves(w_kr), z_tail], axis=1)

    uq = w_uq.reshape(MLA_Q_LORA, MLA_HEADS, MLA_NOPE + MLA_ROPE)
    uq_nope, uq_pe = uq[..., :MLA_NOPE], uq[..., MLA_NOPE:]
    zq_nope = jnp.zeros_like(uq_nope)
    zq_tail = jnp.zeros((MLA_Q_LORA, MLA_HEADS, pad_tail), F32)
    w_qa = jnp.concatenate([uq_nope, uq_pe, zq_tail], axis=-1).reshape(MLA_Q_LORA, MLA_HEADS * LANES)
    w_qb = jnp.concatenate([zq_nope, _swap_halves(uq_pe), zq_tail], axis=-1).reshape(
        MLA_Q_LORA, MLA_HEADS * LANES)

    ukv = w_ukv.reshape(MLA_KV_LORA, MLA_HEADS, MLA_NOPE + MLA_V)
    uk, uv = ukv[..., :MLA_NOPE], ukv[..., MLA_NOPE:]
    w_ka = jnp.concatenate(
        [uk, jnp.zeros((MLA_KV_LORA, MLA_HEADS, LANES - MLA_NOPE), F32)], axis=-1).reshape(
            MLA_KV_LORA, MLA_HEADS * LANES)
    w_v = uv.reshape(MLA_KV_LORA, MLA_HEADS * MLA_V)
    return [w.astype(BF16) for w in (w_kra, w_krb)], [w.astype(BF16) for w in (w_qa, w_qb, w_ka, w_v)]


def _rope_tables(seq):
    pos = np.arange(seq, dtype=np.float64)
    inv = 1.0 / (ROPE_THETA ** (np.arange(0, MLA_ROPE, 2, dtype=np.float64) / MLA_ROPE))
    pad_tail = LANES - MLA_NOPE - MLA_ROPE
    q_scale = (MLA_NOPE + MLA_ROPE) ** -0.5 * LOG2_E
    ang = pos[:, None] * inv[None, :]
    cos, sin = np.cos(ang), np.sin(ang)

    def tables(lead_one):
        c = np.concatenate([np.full((seq, MLA_NOPE), lead_one), cos, cos, np.zeros((seq, pad_tail))], axis=1)
        s = np.concatenate([np.zeros((seq, MLA_NOPE)), -sin, sin, np.zeros((seq, pad_tail))], axis=1)
        return c, s

    c_q, s_q = tables(1.0)
    c_k, s_k = tables(0.0)
    return tuple(np.ascontiguousarray(a, dtype=np.float32)
                 for a in ((c_q * q_scale).T, (s_q * q_scale).T, c_k, s_k))


def kernel(x, mem, w_in, mla_q_norm, mla_kv_norm, w_uq, w_ukv, w_mem_kv, w_br_mla, w_br_dil, w_br_mem, w_o,
           ffn1_w_gate, ffn1_w_up, ffn1_w_down, ffn2_w_gate, ffn2_w_up, ffn2_w_down,
           ln1_g, ln1_b, ln2_g, ln2_b, ln3_g, ln3_b):
    batch, seq, d = x.shape
    t = batch * seq
    tm_ffn = min(1024, t)
    tm_proj = min(512, seq)
    tm_merge = min(512, seq)
    tables = _rope_tables(seq)
    slopes = 2.0 ** (-8.0 * jnp.arange(1, DIL_HEADS + 1, dtype=F32) / DIL_HEADS)

    h = x.reshape(t, d)
    for l in range(DEPTH):
        bf = lambda w: w[l].astype(BF16)
        row = lambda v: v[l].reshape(1, -1)
        h = _ffn_ln(h, bf(ffn1_w_gate), bf(ffn1_w_up), bf(ffn1_w_down), row(ln1_g), row(ln1_b),
                    tm=tm_ffn, n_sub=FFN_SUB_TILES)

        w_in_p = _pad_w_in(w_in[l])
        w_kr, w_u = _prep_mla_weights(w_in[l], w_uq[l], w_ukv[l])
        proj_weights = w_kr + [row(mla_q_norm), row(mla_kv_norm)] + w_u
        q, k, v, *dil = _proj(h, w_in_p, proj_weights, tables, batch=batch, seq=seq, tm=tm_proj)

        o_mla = _mla_attn(q, k, v, tq=min(512, seq), tk=min(1024, seq)).reshape(t, MLA_OUT)
        dil_parts = [_dil_attn(dil[g], slopes, group=g, dilation=dl)
                     for g, (_, dl) in enumerate(DIL_PAIRS)]
        kv_mem = _mem_kv(mem, bf(w_mem_kv))

        merge_weights = [bf(w_br_mla), bf(w_br_dil), bf(w_br_mem), bf(w_o)]
        h = _merge(h, o_mla, [p[0] for p in dil_parts], [p[1] for p in dil_parts], kv_mem, w_in_p,
                   merge_weights, row(ln2_g), row(ln2_b), seq=seq, tm=tm_merge, n_sub=2)

        h = _ffn_ln(h, bf(ffn2_w_gate), bf(ffn2_w_up), bf(ffn2_w_down), row(ln3_g), row(ln3_b),
                    tm=tm_ffn, n_sub=FFN_SUB_TILES)
    return h.reshape(batch, seq, d)
```

```python
import functools
import math

import jax
import jax.numpy as jnp
import numpy as np
from jax import lax
from jax.experimental import pallas as pl
from jax.experimental.pallas import tpu as pltpu

F32 = jnp.float32
BF16 = jnp.bfloat16

D_MODEL = 1024
DEPTH = 1
MLA_HEADS = 8
MLA_Q_LORA = 256
MLA_KV_LORA = 256
MLA_NOPE = 64
MLA_ROPE = 32
MLA_V = 64
ROPE_THETA = 10000.0
DIL_PAIRS = ((128, 1), (512, 4), (2048, 16))
DIL_GROUPS = 3
DIL_HEADS_PER_GROUP = 4
DIL_HEAD_DIM = 128
DIL_HEADS = DIL_GROUPS * DIL_HEADS_PER_GROUP
MEM_HEADS = 4
MEM_HEAD_DIM = 128
EPS = 1e-5
ALPHA = (2 * DEPTH) ** 0.25

LOG2_E = math.log2(math.e)
LANES = 128
DIL_OUT = DIL_HEADS_PER_GROUP * DIL_HEAD_DIM
DIL_QKV = 3 * DIL_HEADS * DIL_HEAD_DIM
MEM_OUT = MEM_HEADS * MEM_HEAD_DIM
MLA_OUT = MLA_HEADS * MLA_V
MLA_DEN_ROWS = 16
MLA_PIECE = 256
MLA_UNITS_PER_TRIP = 8
DIL_HALF_SPAN = 64
DIL_HALO = 64
DIL_SUB = 128
DIL_CHUNK_TOKENS = 2048
DIL_ROWS_PER_STEP = 2048
FFN_SUB_TILES = 4
VMEM_LIMIT = 56 * 1024 * 1024

assert all(w // 2 // d == DIL_HALF_SPAN for w, d in DIL_PAIRS)

_NT = (((1,), (1,)), ((), ()))


def _dot(a, b):
    return jnp.dot(a, b, preferred_element_type=F32)


def _dot_nt(a, b):
    return lax.dot_general(a, b, _NT, preferred_element_type=F32)


def _dot_tt(w, x):
    return lax.dot_general(w, x, (((0,), (1,)), ((), ())), preferred_element_type=F32)


def _layer_norm(y, g, b):
    mu = jnp.mean(y, axis=-1, keepdims=True)
    yc = y - mu
    var = jnp.mean(yc * yc, axis=-1, keepdims=True)
    return yc * lax.rsqrt(var + EPS) * g + b


def _rms_norm(y, g):
    return y * lax.rsqrt(jnp.mean(y * y, axis=-1, keepdims=True) + EPS) * g


def _ffn_ln_body(x_ref, wg_ref, wu_ref, wd_ref, g_ref, b_ref, o_ref, *, n_sub):
    sub_rows = x_ref.shape[0] // n_sub
    for sub in range(n_sub):
        rows = slice(sub * sub_rows, (sub + 1) * sub_rows)
        x = x_ref[rows, :]
        xb = x.astype(BF16)
        gate = _dot(xb, wg_ref[...])
        up = _dot(xb, wu_ref[...])
        act = (gate * jax.nn.sigmoid(gate) * up).astype(BF16)
        y = ALPHA * x + 0.5 * _dot(act, wd_ref[...])
        o_ref[rows, :] = _layer_norm(y, g_ref[...], b_ref[...])


def _ffn_ln(x2d, wg, wu, wd, g, b, *, tm, n_sub):
    t, d = x2d.shape

    def full(a):
        return pl.BlockSpec(a.shape, lambda i: (0,) * a.ndim, pipeline_mode=pl.Buffered(1))

    return pl.pallas_call(
        functools.partial(_ffn_ln_body, n_sub=n_sub),
        grid=(t // tm,),
        in_specs=[pl.BlockSpec((tm, d), lambda i: (i, 0))] + [full(a) for a in (wg, wu, wd, g, b)],
        out_specs=pl.BlockSpec((tm, d), lambda i: (i, 0)),
        out_shape=jax.ShapeDtypeStruct((t, d), F32),
        compiler_params=pltpu.CompilerParams(
            dimension_semantics=("parallel",), vmem_limit_bytes=VMEM_LIMIT),
        name="ffn_ln",
    )(x2d, wg, wu, wd, g, b)


def _proj_body(*refs):
    n_hc = D_MODEL // LANES
    h_refs = refs[:n_hc]
    (wcq_ref, wckv_ref, wkra_ref, wkrb_ref, gq_ref, gkv_ref,
     wqa_ref, wqb_ref, wka_ref, wv_ref, wdil_ref,
     cq_ref, sq_ref, ck_ref, sk_ref,
     q_ref, k_ref, v_ref, dil0_ref, dil1_ref, dil2_ref, hperm_ref) = refs[n_hc:]
    hb = jnp.concatenate([hc[...] for hc in h_refs], axis=1).astype(BF16)
    cqn = _rms_norm(_dot(hb, wcq_ref[...]), gq_ref[...]).astype(BF16)
    ckvn = _rms_norm(_dot(hb, wckv_ref[...]), gkv_ref[...]).astype(BF16)
    k_rope = _dot(hb, wkra_ref[...]) * ck_ref[...] + _dot(hb, wkrb_ref[...]) * sk_ref[...]
    ka = _dot(ckvn, wka_ref[...])
    for h in range(MLA_HEADS):
        sl = slice(h * LANES, (h + 1) * LANES)
        k_ref[0, h] = (ka[:, sl] + k_rope).astype(BF16)
    qa_t = _dot_tt(wqa_ref[...], cqn)
    qb_t = _dot_tt(wqb_ref[...], cqn)
    cq_t = cq_ref[...]
    sq_t = sq_ref[...]
    for h in range(MLA_HEADS):
        sl = slice(h * LANES, (h + 1) * LANES)
        q_ref[0, sl] = (qa_t[sl] * cq_t + qb_t[sl] * sq_t).astype(BF16)
    v_ref[0] = _dot_tt(wv_ref[...], ckvn).astype(BF16)

    dil_scale = DIL_HEAD_DIM ** -0.5
    tm = h_refs[0].shape[0]
    for g, (dil_ref, (_, dilation)) in enumerate(zip((dil0_ref, dil1_ref, dil2_ref), DIL_PAIRS)):
        rows_per = tm // dilation
        if dilation == 1:
            hp = hb
        else:
            for r in range(dilation):
                for c, hc in enumerate(h_refs):
                    hperm_ref[r * rows_per:(r + 1) * rows_per, c * LANES:(c + 1) * LANES] = hc[
                        pl.ds(r, rows_per, stride=dilation), :].astype(BF16)
            hp = hperm_ref[...]
        for part in range(3):
            c0 = (part * DIL_GROUPS + g) * DIL_OUT
            res = _dot(hp, wdil_ref[:, c0:c0 + DIL_OUT])
            if part == 0:
                res = res * dil_scale
            res = res.astype(BF16)
            for r in range(dilation):
                dil_ref[0, r, :, part * DIL_OUT:(part + 1) * DIL_OUT] = res[r * rows_per:(r + 1) * rows_per]


def _proj(h2d, weights, tables, *, batch, seq, tm):
    t, d = h2d.shape
    n_s = seq // tm

    def full(a):
        return pl.BlockSpec(a.shape, lambda i: (0,) * a.ndim)

    tab_spec = pl.BlockSpec((tm, LANES), lambda i: (i % n_s, 0))
    tab_t_spec = pl.BlockSpec((LANES, tm), lambda i: (0, i % n_s))
    head_map = lambda i: (i // n_s, 0, i % n_s, 0)
    feat_map = lambda i: (i // n_s, 0, i % n_s)
    dil_specs = [pl.BlockSpec((1, dl, tm // dl, 3 * DIL_OUT), head_map) for _, dl in DIL_PAIRS]
    dil_shapes = [jax.ShapeDtypeStruct((batch, dl, seq // dl, 3 * DIL_OUT), BF16) for _, dl in DIL_PAIRS]
    return pl.pallas_call(
        _proj_body,
        grid=(t // tm,),
        in_specs=[pl.BlockSpec((tm, LANES), functools.partial(lambda c, i: (i, c), c))
                  for c in range(d // LANES)]
        + [full(w) for w in weights] + [tab_t_spec] * 2 + [tab_spec] * 2,
        out_specs=[
            pl.BlockSpec((1, MLA_HEADS * LANES, tm), feat_map),
            pl.BlockSpec((1, MLA_HEADS, tm, LANES), head_map),
            pl.BlockSpec((1, MLA_OUT, tm), feat_map),
        ] + dil_specs,
        out_shape=[
            jax.ShapeDtypeStruct((batch, MLA_HEADS * LANES, seq), BF16),
            jax.ShapeDtypeStruct((batch, MLA_HEADS, seq, LANES), BF16),
            jax.ShapeDtypeStruct((batch, MLA_OUT, seq), BF16),
        ] + dil_shapes,
        scratch_shapes=[pltpu.VMEM((tm, d), BF16)],
        compiler_params=pltpu.CompilerParams(
            dimension_semantics=("parallel",), vmem_limit_bytes=VMEM_LIMIT),
        name="proj",
    )(*([h2d] * (d // LANES)), *weights, *tables)


def _mla_attn_body(q_ref, k_ref, vt_ref, o_ref, st_ref, m_ref, acc_ref, *, tq, tk):
    seq = k_ref.shape[2]
    n_tiles = seq // tq
    n_units = (seq // tk) * n_tiles
    heads = (0, 1)
    n_pieces = tk // MLA_PIECE

    def offsets(unit):
        if isinstance(unit, int):
            return (unit // n_tiles) * tk, (unit % n_tiles) * tq
        return (pl.multiple_of(lax.div(unit, n_tiles) * tk, tk),
                pl.multiple_of(lax.rem(unit, n_tiles) * tq, tq))

    def scores_piece(unit, buf, p, maxes):
        k_off, q_off = offsets(unit)
        rows = slice(p * MLA_PIECE, (p + 1) * MLA_PIECE)
        out = []
        for h in heads:
            st = _dot(k_ref[0, h, pl.ds(k_off + p * MLA_PIECE, MLA_PIECE), :],
                      q_ref[0, h * LANES:(h + 1) * LANES, pl.ds(q_off, tq)])
            st_ref[buf, h, rows] = st
            out.append(jnp.maximum(maxes[h], jnp.max(st, axis=0, keepdims=True)))
        return tuple(out)

    ones_rows = jnp.ones((MLA_DEN_ROWS, MLA_PIECE), BF16)

    def values_piece(k_off, buf, p, m_new, accs):
        rows = slice(p * MLA_PIECE, (p + 1) * MLA_PIECE)
        out = []
        for h in heads:
            pt = jnp.exp2(st_ref[buf, h, rows] - m_new[h]).astype(BF16)
            vtc = jnp.concatenate(
                [vt_ref[0, h * MLA_V:(h + 1) * MLA_V, pl.ds(k_off + p * MLA_PIECE, MLA_PIECE)], ones_rows],
                axis=0)
            out.append(accs[h] + _dot(vtc, pt))
        return tuple(out)

    neg_inf = jnp.full((1, tq), -jnp.inf, F32)

    def half(next_unit, next_buf, unit, buf, unit_max):
        k_off, q_off = offsets(unit)
        cols = pl.ds(q_off, tq)
        m_old = tuple(m_ref[h, :, cols] for h in heads)
        m_new = tuple(jnp.maximum(m_old[h], unit_max[h]) for h in heads)
        accs = tuple(jnp.exp2(m_old[h] - m_new[h]) * acc_ref[h, :, cols] for h in heads)
        next_max = (neg_inf, neg_inf)
        for p in range(n_pieces):
            if next_unit is not None:
                next_max = scores_piece(next_unit, next_buf, p, next_max)
            accs = values_piece(k_off, buf, p, m_new, accs)
        for h in heads:
            m_ref[h, :, cols] = m_new[h]
            acc_ref[h, :, cols] = accs[h]
        return next_max

    def group_step(jj, unit_max):
        for s in range(MLA_UNITS_PER_TRIP):
            u = MLA_UNITS_PER_TRIP * jj + s
            unit_max = half(u + 1, (s + 1) % 2, u, s % 2, unit_max)
        return unit_max

    m_ref[...] = jnp.full(m_ref.shape, -jnp.inf, F32)
    acc_ref[...] = jnp.zeros(acc_ref.shape, F32)
    unit_max = (neg_inf, neg_inf)
    for p in range(n_pieces):
        unit_max = scores_piece(0, 0, p, unit_max)
    unit_max = lax.fori_loop(0, n_units // MLA_UNITS_PER_TRIP - 1, group_step, unit_max)
    for u in range(n_units - MLA_UNITS_PER_TRIP, n_units):
        last = u == n_units - 1
        unit_max = half(None if last else u + 1, None if last else (u + 1) % 2, u, u % 2, unit_max)

    @pl.loop(0, n_tiles)
    def _(tile):
        cols = pl.ds(pl.multiple_of(tile * tq, tq), tq)
        out_t = jnp.concatenate(
            [acc_ref[h, :MLA_V, cols] / acc_ref[h, MLA_V:MLA_V + 1, cols] for h in heads], axis=0)
        o_ref[0, cols, :] = out_t.T.astype(o_ref.dtype)


def _mla_attn(q, k, v, *, tq, tk):
    batch, heads, seq, _ = k.shape
    assert MLA_UNITS_PER_TRIP % 2 == 0 and ((seq // tk) * (seq // tq)) % MLA_UNITS_PER_TRIP == 0
    return pl.pallas_call(
        functools.partial(_mla_attn_body, tq=tq, tk=tk),
        grid=(batch, heads // 2),
        in_specs=[
            pl.BlockSpec((1, 2 * LANES, seq), lambda b, hp: (b, hp, 0)),
            pl.BlockSpec((1, 2, seq, LANES), lambda b, hp: (b, hp, 0, 0)),
            pl.BlockSpec((1, 2 * MLA_V, seq), lambda b, hp: (b, hp, 0)),
        ],
        out_specs=pl.BlockSpec((1, seq, LANES), lambda b, hp: (b, 0, hp)),
        out_shape=jax.ShapeDtypeStruct((batch, seq, MLA_OUT), BF16),
        scratch_shapes=[pltpu.VMEM((2, 2, tk, tq), F32),
                        pltpu.VMEM((2, 1, seq), F32),
                        pltpu.VMEM((2, MLA_V + MLA_DEN_ROWS, seq), F32)],
        compiler_params=pltpu.CompilerParams(
            dimension_semantics=("parallel", "parallel"), vmem_limit_bytes=VMEM_LIMIT),
        name="mla_attn",
    )(q, k, v)


def _dil_attn_body(slopes_ref, q_ref, kl_ref, km_ref, kr_ref, vl_ref, vm_ref, vr_ref,
                   o_ref, lse_ref, kbuf, vbuf, *, group, dilation, tq, sub_len):
    i = pl.program_id(1)
    n_res = q_ref.shape[1]
    first_res = pl.program_id(2) * n_res
    kbuf[:, 0:DIL_HALO] = kl_ref[0]
    kbuf[:, DIL_HALO:DIL_HALO + tq] = km_ref[0]
    kbuf[:, DIL_HALO + tq:] = kr_ref[0]
    vbuf[:, 0:DIL_HALO] = vl_ref[0]
    vbuf[:, DIL_HALO:DIL_HALO + tq] = vm_ref[0]
    vbuf[:, DIL_HALO + tq:] = vr_ref[0]

    win = DIL_SUB + 2 * DIL_HALO
    row = lax.broadcasted_iota(jnp.int32, (DIL_SUB, win), 0)
    col = lax.broadcasted_iota(jnp.int32, (DIL_SUB, win), 1)
    steps = jnp.abs(col - DIL_HALO - row)
    in_band = steps <= DIL_HALF_SPAN
    dist = (steps * dilation).astype(F32)
    for sub in range(tq // DIL_SUB):
        key_pos = i * tq + (sub * DIL_SUB - DIL_HALO) + col
        valid = in_band & (key_pos >= 0) & (key_pos < sub_len)
        rows = slice(sub * DIL_SUB, (sub + 1) * DIL_SUB)
        wrows = slice(sub * DIL_SUB, sub * DIL_SUB + win)
        for rr in range(n_res):
            if dilation == 1:
                out_rows = pl.ds(sub * DIL_SUB, DIL_SUB)
            else:
                out_rows = pl.ds(sub * DIL_SUB * dilation + first_res + rr, DIL_SUB, stride=dilation)
            for h in range(DIL_HEADS_PER_GROUP):
                cols = slice(h * DIL_HEAD_DIM, (h + 1) * DIL_HEAD_DIM)
                slope = slopes_ref[group * DIL_HEADS_PER_GROUP + h]
                s = _dot_nt(q_ref[0, rr, rows, cols], kbuf[rr, wrows, cols]) - slope * dist
                s = jnp.where(valid, s, -jnp.inf)
                m = jnp.max(s, axis=1, keepdims=True)
                e = jnp.exp(s - m)
                den = jnp.sum(e, axis=1, keepdims=True)
                o_ref[0, h, out_rows, :] = _dot(e.astype(BF16), vbuf[rr, wrows, cols]) / den
                lse_ref[0, h, out_rows, :] = jnp.broadcast_to(m + jnp.log(den), (DIL_SUB, DIL_HEAD_DIM))


def _dil_attn(dil_g, slopes, *, group, dilation):
    batch, _, sub_len, _ = dil_g.shape
    seq = sub_len * dilation
    tq = min(DIL_ROWS_PER_STEP, sub_len, DIL_CHUNK_TOKENS // dilation)
    n_q = sub_len // tq
    n_res = min(dilation, DIL_ROWS_PER_STEP // tq)
    halo_per_tile = tq // DIL_HALO
    n_halo = sub_len // DIL_HALO
    q_col, k_col, v_col = 0, 1, 2

    def main(c):
        return pl.BlockSpec((1, n_res, tq, DIL_OUT), lambda b, i, r: (b, r, i, c))

    def left(c):
        return pl.BlockSpec(
            (1, n_res, DIL_HALO, DIL_OUT),
            lambda b, i, r: (b, r, jnp.maximum(i * halo_per_tile - 1, 0), c))

    def right(c):
        return pl.BlockSpec(
            (1, n_res, DIL_HALO, DIL_OUT),
            lambda b, i, r: (b, r, jnp.minimum((i + 1) * halo_per_tile, n_halo - 1), c))

    out_spec = pl.BlockSpec((1, DIL_HEADS_PER_GROUP, tq * dilation, DIL_HEAD_DIM),
                            lambda b, i, r: (b, 0, i, 0))
    out_shape = jax.ShapeDtypeStruct((batch, DIL_HEADS_PER_GROUP, seq, DIL_HEAD_DIM), F32)
    o, lse = pl.pallas_call(
        functools.partial(_dil_attn_body, group=group, dilation=dilation, tq=tq, sub_len=sub_len),
        grid=(batch, n_q, dilation // n_res),
        in_specs=[pl.BlockSpec(memory_space=pltpu.SMEM),
                  main(q_col), left(k_col), main(k_col), right(k_col),
                  left(v_col), main(v_col), right(v_col)],
        out_specs=[out_spec, out_spec],
        out_shape=[out_shape, out_shape],
        scratch_shapes=[pltpu.VMEM((n_res, tq + 2 * DIL_HALO, DIL_OUT), BF16),
                        pltpu.VMEM((n_res, tq + 2 * DIL_HALO, DIL_OUT), BF16)],
        compiler_params=pltpu.CompilerParams(
            dimension_semantics=("parallel", "parallel", "arbitrary"), vmem_limit_bytes=VMEM_LIMIT),
        name=f"dil_attn_g{group}",
    )(slopes, dil_g, dil_g, dil_g, dil_g, dil_g, dil_g, dil_g)
    return o, lse


def _mem_kv_body(mem_ref, w_ref, o_ref):
    o_ref[0] = _dot(mem_ref[0].astype(BF16), w_ref[...]).astype(BF16)


def _mem_kv(mem, w):
    batch, m_len, d = mem.shape
    n = w.shape[1]
    return pl.pallas_call(
        _mem_kv_body,
        grid=(batch,),
        in_specs=[pl.BlockSpec((1, m_len, d), lambda b: (b, 0, 0)),
                  pl.BlockSpec((d, n), lambda b: (0, 0))],
        out_specs=pl.BlockSpec((1, m_len, n), lambda b: (b, 0, 0)),
        out_shape=jax.ShapeDtypeStruct((batch, m_len, n), BF16),
        compiler_params=pltpu.CompilerParams(dimension_semantics=("parallel",)),
        name="mem_kv",
    )(mem, w)


def _merge_body(h_ref, omla_ref, od0_ref, od1_ref, od2_ref, ls0_ref, ls1_ref, ls2_ref, kvm_ref,
                wgate_ref, wmq_ref, wba_ref, wbb_ref, wbc_ref, wo_ref, g_ref, b_ref, out_ref, *, n_sub):
    sub_rows = h_ref.shape[0] // n_sub
    for sub in range(n_sub):
        rows = slice(sub * sub_rows, (sub + 1) * sub_rows)
        _merge_rows(rows, h_ref, omla_ref, od0_ref, od1_ref, od2_ref, ls0_ref, ls1_ref, ls2_ref, kvm_ref,
                    wgate_ref, wmq_ref, wba_ref, wbb_ref, wbc_ref, wo_ref, g_ref, b_ref, out_ref)


def _merge_rows(rows, h_ref, omla_ref, od0_ref, od1_ref, od2_ref, ls0_ref, ls1_ref, ls2_ref, kvm_ref,
                wgate_ref, wmq_ref, wba_ref, wbb_ref, wbc_ref, wo_ref, g_ref, b_ref, out_ref):
    h = h_ref[rows, :]
    hb = h.astype(BF16)

    def heads_to_lanes(ref):
        return jnp.concatenate([ref[0, hh, rows, :] for hh in range(DIL_HEADS_PER_GROUP)], axis=1)

    ls0, ls1, ls2 = heads_to_lanes(ls0_ref), heads_to_lanes(ls1_ref), heads_to_lanes(ls2_ref)
    mx = jnp.maximum(ls0, jnp.maximum(ls1, ls2))
    e0, e1, e2 = jnp.exp(ls0 - mx), jnp.exp(ls1 - mx), jnp.exp(ls2 - mx)
    o_dil = (e0 * heads_to_lanes(od0_ref) + e1 * heads_to_lanes(od1_ref)
             + e2 * heads_to_lanes(od2_ref)) / (e0 + e1 + e2)

    mq = (_dot(hb, wmq_ref[...]) * (MEM_HEAD_DIM ** -0.5)).astype(BF16)
    heads = []
    for hh in range(MEM_HEADS):
        kc = slice(hh * MEM_HEAD_DIM, (hh + 1) * MEM_HEAD_DIM)
        vc = slice(MEM_OUT + hh * MEM_HEAD_DIM, MEM_OUT + (hh + 1) * MEM_HEAD_DIM)
        s = _dot_nt(mq[:, kc], kvm_ref[0, :, kc])
        p = jnp.exp(s - jnp.max(s, axis=1, keepdims=True))
        heads.append(_dot(p.astype(BF16), kvm_ref[0, :, vc]) / jnp.sum(p, axis=1, keepdims=True))
    o_mem = jnp.concatenate(heads, axis=1)

    y_a = _dot(omla_ref[rows, :], wba_ref[...])
    y_b = _dot(o_dil.astype(BF16), wbb_ref[...])
    y_c = _dot(o_mem.astype(BF16), wbc_ref[...])
    d = h.shape[1]
    merged = (jax.nn.sigmoid(_dot(hb, wgate_ref[:, :d])) * y_a
              + jax.nn.sigmoid(_dot(hb, wgate_ref[:, d:2 * d])) * y_b
              + jax.nn.sigmoid(_dot(hb, wgate_ref[:, 2 * d:])) * y_c)
    mix = _dot(merged.astype(BF16), wo_ref[...])
    out_ref[rows, :] = _layer_norm(ALPHA * h + mix, g_ref[...], b_ref[...])


def _merge(h2d, o_mla, o_dil, lse_dil, kv_mem, weights, g, b, *, seq, tm, n_sub):
    t, d = h2d.shape
    n_s = seq // tm

    def rows(width):
        return pl.BlockSpec((tm, width), lambda i: (i, 0))

    def full(a):
        return pl.BlockSpec(a.shape, lambda i: (0,) * a.ndim, pipeline_mode=pl.Buffered(1))

    m_len, kv_cols = kv_mem.shape[1:]
    return pl.pallas_call(
        functools.partial(_merge_body, n_sub=n_sub),
        grid=(t // tm,),
        in_specs=[rows(d), rows(MLA_OUT)]
        + [pl.BlockSpec((1, DIL_HEADS_PER_GROUP, tm, DIL_HEAD_DIM),
                        lambda i: (i // n_s, 0, i % n_s, 0))] * 6
        + [pl.BlockSpec((1, m_len, kv_cols), lambda i: (i // n_s, 0, 0))]
        + [full(w) for w in weights] + [full(g), full(b)],
        out_specs=rows(d),
        out_shape=jax.ShapeDtypeStruct((t, d), F32),
        compiler_params=pltpu.CompilerParams(
            dimension_semantics=("parallel",), vmem_limit_bytes=VMEM_LIMIT),
        name="merge",
    )(h2d, o_mla, *o_dil, *lse_dil, kv_mem, *weights, g, b)


def _swap_halves(w):
    half = w.shape[-1] // 2
    return jnp.concatenate([w[..., half:], w[..., :half]], axis=-1)


def _prep_mla_weights(w_in, w_uq, w_ukv):
    d = w_in.shape[0]
    o_q, o_kv, o_kr = 0, MLA_Q_LORA, MLA_Q_LORA + MLA_KV_LORA
    w_cq = w_in[:, o_q:o_q + MLA_Q_LORA]
    w_ckv = w_in[:, o_kv:o_kv + MLA_KV_LORA]
    w_kr = w_in[:, o_kr:o_kr + MLA_ROPE]
    pad_tail = LANES - MLA_NOPE - MLA_ROPE
    z_nope = jnp.zeros((d, MLA_NOPE), F32)
    z_tail = jnp.zeros((d, pad_tail), F32)
    w_kra = jnp.concatenate([z_nope, w_kr, z_tail], axis=1)
    w_krb = jnp.concatenate([z_nope, _swap_halves(w_kr), z_tail], axis=1)

    uq = w_uq.reshape(MLA_Q_LORA, MLA_HEADS, MLA_NOPE + MLA_ROPE)
    uq_nope, uq_pe = uq[..., :MLA_NOPE], uq[..., MLA_NOPE:]
    zq_nope = jnp.zeros_like(uq_nope)
    zq_tail = jnp.zeros((MLA_Q_LORA, MLA_HEADS, pad_tail), F32)
    w_qa = jnp.concatenate([uq_nope, uq_pe, zq_tail], axis=-1).reshape(MLA_Q_LORA, MLA_HEADS * LANES)
    w_qb = jnp.concatenate([zq_nope, _swap_halves(uq_pe), zq_tail], axis=-1).reshape(
        MLA_Q_LORA, MLA_HEADS * LANES)

    ukv = w_ukv.reshape(MLA_KV_LORA, MLA_HEADS, MLA_NOPE + MLA_V)
    uk, uv = ukv[..., :MLA_NOPE], ukv[..., MLA_NOPE:]
    w_ka = jnp.concatenate(
        [uk, jnp.zeros((MLA_KV_LORA, MLA_HEADS, LANES - MLA_NOPE), F32)], axis=-1).reshape(
            MLA_KV_LORA, MLA_HEADS * LANES)
    w_v = uv.reshape(MLA_KV_LORA, MLA_HEADS * MLA_V)
    return [w.astype(BF16) for w in (w_cq, w_ckv, w_kra, w_krb)], [w.astype(BF16) for w in (w_qa, w_qb, w_ka, w_v)]


def _rope_tables(seq):
    pos = np.arange(seq, dtype=np.float64)
    inv = 1.0 / (ROPE_THETA ** (np.arange(0, MLA_ROPE, 2, dtype=np.float64) / MLA_ROPE))
    pad_tail = LANES - MLA_NOPE - MLA_ROPE
    q_scale = (MLA_NOPE + MLA_ROPE) ** -0.5 * LOG2_E
    ang = pos[:, None] * inv[None, :]
    cos, sin = np.cos(ang), np.sin(ang)

    def tables(lead_one):
        c = np.concatenate([np.full((seq, MLA_NOPE), lead_one), cos, cos, np.zeros((seq, pad_tail))], axis=1)
        s = np.concatenate([np.zeros((seq, MLA_NOPE)), -sin, sin, np.zeros((seq, pad_tail))], axis=1)
        return c, s

    c_q, s_q = tables(1.0)
    c_k, s_k = tables(0.0)
    return tuple(np.ascontiguousarray(a, dtype=np.float32)
                 for a in ((c_q * q_scale).T, (s_q * q_scale).T, c_k, s_k))


def kernel(x, mem, w_in, mla_q_norm, mla_kv_norm, w_uq, w_ukv, w_mem_kv, w_br_mla, w_br_dil, w_br_mem, w_o,
           ffn1_w_gate, ffn1_w_up, ffn1_w_down, ffn2_w_gate, ffn2_w_up, ffn2_w_down,
           ln1_g, ln1_b, ln2_g, ln2_b, ln3_g, ln3_b):
    batch, seq, d = x.shape
    t = batch * seq
    tm_ffn = min(1024, t)
    tm_proj = min(512, seq)
    tm_merge = min(512, seq)
    tables = _rope_tables(seq)
    slopes = 2.0 ** (-8.0 * jnp.arange(1, DIL_HEADS + 1, dtype=F32) / DIL_HEADS)

    h = x.reshape(t, d)
    for l in range(DEPTH):
        bf = lambda w: w[l].astype(BF16)
        row = lambda v: v[l].reshape(1, -1)
        h = _ffn_ln(h, bf(ffn1_w_gate), bf(ffn1_w_up), bf(ffn1_w_down), row(ln1_g), row(ln1_b),
                    tm=tm_ffn, n_sub=FFN_SUB_TILES)

        w_in_l = w_in[l]
        o_dil_cols = MLA_Q_LORA + MLA_KV_LORA + MLA_ROPE
        o_memq = o_dil_cols + DIL_QKV
        o_gate = o_memq + MEM_OUT
        w_c, w_u = _prep_mla_weights(w_in_l, w_uq[l], w_ukv[l])
        proj_weights = w_c + [row(mla_q_norm), row(mla_kv_norm)] + w_u + [
            w_in_l[:, o_dil_cols:o_memq].astype(BF16)]
        q, k, v, *dil = _proj(h, proj_weights, tables, batch=batch, seq=seq, tm=tm_proj)

        o_mla = _mla_attn(q, k, v, tq=min(512, seq), tk=min(1024, seq)).reshape(t, MLA_OUT)
        dil_parts = [_dil_attn(dil[g], slopes, group=g, dilation=dl)
                     for g, (_, dl) in enumerate(DIL_PAIRS)]
        kv_mem = _mem_kv(mem, bf(w_mem_kv))

        merge_weights = [w_in_l[:, o_gate:].astype(BF16), w_in_l[:, o_memq:o_gate].astype(BF16),
                         bf(w_br_mla), bf(w_br_dil), bf(w_br_mem), bf(w_o)]
        h = _merge(h, o_mla, [p[0] for p in dil_parts], [p[1] for p in dil_parts], kv_mem,
                   merge_weights, row(ln2_g), row(ln2_b), seq=seq, tm=tm_merge, n_sub=2)

        h = _ffn_ln(h, bf(ffn2_w_gate), bf(ffn2_w_up), bf(ffn2_w_down), row(ln3_g), row(ln3_b),
                    tm=tm_ffn, n_sub=FFN_SUB_TILES)
    return h.reshape(batch, seq, d)
```

```python
import functools
import math

import jax
import jax.numpy as jnp
import numpy as np
from jax import lax
from jax.experimental import pallas as pl
from jax.experimental.pallas import tpu as pltpu

F32 = jnp.float32
BF16 = jnp.bfloat16

D_MODEL = 1024
DEPTH = 1
MLA_HEADS = 8
MLA_Q_LORA = 256
MLA_KV_LORA = 256
MLA_NOPE = 64
MLA_ROPE = 32
MLA_V = 64
ROPE_THETA = 10000.0
DIL_PAIRS = ((128, 1), (512, 4), (2048, 16))
DIL_GROUPS = 3
DIL_HEADS_PER_GROUP = 4
DIL_HEAD_DIM = 128
DIL_HEADS = DIL_GROUPS * DIL_HEADS_PER_GROUP
MEM_HEADS = 4
MEM_HEAD_DIM = 128
EPS = 1e-5
ALPHA = (2 * DEPTH) ** 0.25

LOG2_E = math.log2(math.e)
LANES = 128
DIL_OUT = DIL_HEADS_PER_GROUP * DIL_HEAD_DIM
DIL_QKV = 3 * DIL_HEADS * DIL_HEAD_DIM
MEM_OUT = MEM_HEADS * MEM_HEAD_DIM
MLA_OUT = MLA_HEADS * MLA_V
MLA_DEN_ROWS = 16
MLA_PIECE = 256
MLA_UNITS_PER_TRIP = 16
DIL_HALF_SPAN = 64
DIL_HALO = 64
DIL_SUB = 128
DIL_CHUNK_TOKENS = 2048
DIL_ROWS_PER_STEP = 2048
FFN_SUB_TILES = 4
VMEM_LIMIT = 56 * 1024 * 1024

assert all(w // 2 // d == DIL_HALF_SPAN for w, d in DIL_PAIRS)

_NT = (((1,), (1,)), ((), ()))


def _dot(a, b):
    return jnp.dot(a, b, preferred_element_type=F32)


def _dot_nt(a, b):
    return lax.dot_general(a, b, _NT, preferred_element_type=F32)


def _dot_tt(w, x):
    return lax.dot_general(w, x, (((0,), (1,)), ((), ())), preferred_element_type=F32)


def _layer_norm(y, g, b):
    mu = jnp.mean(y, axis=-1, keepdims=True)
    yc = y - mu
    var = jnp.mean(yc * yc, axis=-1, keepdims=True)
    return yc * lax.rsqrt(var + EPS) * g + b


def _rms_norm(y, g):
    return y * lax.rsqrt(jnp.mean(y * y, axis=-1, keepdims=True) + EPS) * g


def _ffn_ln_body(x_ref, wg_ref, wu_ref, wd_ref, g_ref, b_ref, o_ref, *, n_sub):
    sub_rows = x_ref.shape[0] // n_sub
    for sub in range(n_sub):
        rows = slice(sub * sub_rows, (sub + 1) * sub_rows)
        x = x_ref[rows, :]
        xb = x.astype(BF16)
        gate = _dot(xb, wg_ref[...])
        up = _dot(xb, wu_ref[...])
        act = (gate * jax.nn.sigmoid(gate) * up).astype(BF16)
        y = ALPHA * x + 0.5 * _dot(act, wd_ref[...])
        o_ref[rows, :] = _layer_norm(y, g_ref[...], b_ref[...])


def _ffn_ln(x2d, wg, wu, wd, g, b, *, tm, n_sub):
    t, d = x2d.shape

    def full(a):
        return pl.BlockSpec(a.shape, lambda i: (0,) * a.ndim, pipeline_mode=pl.Buffered(1))

    return pl.pallas_call(
        functools.partial(_ffn_ln_body, n_sub=n_sub),
        grid=(t // tm,),
        in_specs=[pl.BlockSpec((tm, d), lambda i: (i, 0))] + [full(a) for a in (wg, wu, wd, g, b)],
        out_specs=pl.BlockSpec((tm, d), lambda i: (i, 0)),
        out_shape=jax.ShapeDtypeStruct((t, d), F32),
        compiler_params=pltpu.CompilerParams(
            dimension_semantics=("parallel",), vmem_limit_bytes=VMEM_LIMIT),
        name="ffn_ln",
    )(x2d, wg, wu, wd, g, b)


def _proj_body(*refs):
    n_hc = D_MODEL // LANES
    h_refs = refs[:n_hc]
    (wcq_ref, wckv_ref, wkra_ref, wkrb_ref, gq_ref, gkv_ref,
     wqa_ref, wqb_ref, wka_ref, wv_ref, wdil_ref,
     cq_ref, sq_ref, ck_ref, sk_ref,
     q_ref, k_ref, v_ref, dil0_ref, dil1_ref, dil2_ref, hperm_ref) = refs[n_hc:]
    hb = jnp.concatenate([hc[...] for hc in h_refs], axis=1).astype(BF16)
    cqn = _rms_norm(_dot(hb, wcq_ref[...]), gq_ref[...]).astype(BF16)
    ckvn = _rms_norm(_dot(hb, wckv_ref[...]), gkv_ref[...]).astype(BF16)
    k_rope = _dot(hb, wkra_ref[...]) * ck_ref[...] + _dot(hb, wkrb_ref[...]) * sk_ref[...]
    ka = _dot(ckvn, wka_ref[...])
    for h in range(MLA_HEADS):
        sl = slice(h * LANES, (h + 1) * LANES)
        k_ref[0, h] = (ka[:, sl] + k_rope).astype(BF16)
    qa_t = _dot_tt(wqa_ref[...], cqn)
    qb_t = _dot_tt(wqb_ref[...], cqn)
    cq_t = cq_ref[...]
    sq_t = sq_ref[...]
    for h in range(MLA_HEADS):
        sl = slice(h * LANES, (h + 1) * LANES)
        q_ref[0, sl] = (qa_t[sl] * cq_t + qb_t[sl] * sq_t).astype(BF16)
    v_ref[0] = _dot_tt(wv_ref[...], ckvn).astype(BF16)

    dil_scale = DIL_HEAD_DIM ** -0.5
    tm = h_refs[0].shape[0]
    for g, (dil_ref, (_, dilation)) in enumerate(zip((dil0_ref, dil1_ref, dil2_ref), DIL_PAIRS)):
        rows_per = tm // dilation
        if dilation == 1:
            hp = hb
        else:
            for r in range(dilation):
                for c, hc in enumerate(h_refs):
                    hperm_ref[r * rows_per:(r + 1) * rows_per, c * LANES:(c + 1) * LANES] = hc[
                        pl.ds(r, rows_per, stride=dilation), :].astype(BF16)
            hp = hperm_ref[...]
        for part in range(3):
            c0 = (part * DIL_GROUPS + g) * DIL_OUT
            res = _dot(hp, wdil_ref[:, c0:c0 + DIL_OUT])
            if part == 0:
                res = res * dil_scale
            res = res.astype(BF16)
            for r in range(dilation):
                dil_ref[0, r, :, part * DIL_OUT:(part + 1) * DIL_OUT] = res[r * rows_per:(r + 1) * rows_per]


def _proj(h2d, weights, tables, *, batch, seq, tm):
    t, d = h2d.shape
    n_s = seq // tm

    def full(a):
        return pl.BlockSpec(a.shape, lambda i: (0,) * a.ndim)

    tab_spec = pl.BlockSpec((tm, LANES), lambda i: (i % n_s, 0))
    tab_t_spec = pl.BlockSpec((LANES, tm), lambda i: (0, i % n_s))
    head_map = lambda i: (i // n_s, 0, i % n_s, 0)
    feat_map = lambda i: (i // n_s, 0, i % n_s)
    dil_specs = [pl.BlockSpec((1, dl, tm // dl, 3 * DIL_OUT), head_map) for _, dl in DIL_PAIRS]
    dil_shapes = [jax.ShapeDtypeStruct((batch, dl, seq // dl, 3 * DIL_OUT), BF16) for _, dl in DIL_PAIRS]
    return pl.pallas_call(
        _proj_body,
        grid=(t // tm,),
        in_specs=[pl.BlockSpec((tm, LANES), functools.partial(lambda c, i: (i, c), c))
                  for c in range(d // LANES)]
        + [full(w) for w in weights] + [tab_t_spec] * 2 + [tab_spec] * 2,
        out_specs=[
            pl.BlockSpec((1, MLA_HEADS * LANES, tm), feat_map),
            pl.BlockSpec((1, MLA_HEADS, tm, LANES), head_map),
            pl.BlockSpec((1, MLA_OUT, tm), feat_map),
        ] + dil_specs,
        out_shape=[
            jax.ShapeDtypeStruct((batch, MLA_HEADS * LANES, seq), BF16),
            jax.ShapeDtypeStruct((batch, MLA_HEADS, seq, LANES), BF16),
            jax.ShapeDtypeStruct((batch, MLA_OUT, seq), BF16),
        ] + dil_shapes,
        scratch_shapes=[pltpu.VMEM((tm, d), BF16)],
        compiler_params=pltpu.CompilerParams(
            dimension_semantics=("parallel",), vmem_limit_bytes=VMEM_LIMIT),
        name="proj",
    )(*([h2d] * (d // LANES)), *weights, *tables)


def _mla_attn_body(q_ref, k_ref, vt_ref, o_ref, st_ref, m_ref, acc_ref, *, tq, tk):
    seq = k_ref.shape[2]
    n_tiles = seq // tq
    n_units = (seq // tk) * n_tiles
    heads = (0, 1)
    n_pieces = tk // MLA_PIECE

    def offsets(unit):
        if isinstance(unit, int):
            return (unit // n_tiles) * tk, (unit % n_tiles) * tq
        return (pl.multiple_of(lax.div(unit, n_tiles) * tk, tk),
                pl.multiple_of(lax.rem(unit, n_tiles) * tq, tq))

    def scores_piece(unit, buf, p, maxes):
        k_off, q_off = offsets(unit)
        rows = slice(p * MLA_PIECE, (p + 1) * MLA_PIECE)
        out = []
        for h in heads:
            st = _dot(k_ref[0, h, pl.ds(k_off + p * MLA_PIECE, MLA_PIECE), :],
                      q_ref[0, h * LANES:(h + 1) * LANES, pl.ds(q_off, tq)])
            st_ref[buf, h, rows] = st
            out.append(jnp.maximum(maxes[h], jnp.max(st, axis=0, keepdims=True)))
        return tuple(out)

    ones_rows = jnp.ones((MLA_DEN_ROWS, MLA_PIECE), BF16)

    def values_piece(k_off, buf, p, m_new, accs):
        rows = slice(p * MLA_PIECE, (p + 1) * MLA_PIECE)
        out = []
        for h in heads:
            pt = jnp.exp2(st_ref[buf, h, rows] - m_new[h]).astype(BF16)
            vtc = jnp.concatenate(
                [vt_ref[0, h * MLA_V:(h + 1) * MLA_V, pl.ds(k_off + p * MLA_PIECE, MLA_PIECE)], ones_rows],
                axis=0)
            out.append(accs[h] + _dot(vtc, pt))
        return tuple(out)

    neg_inf = jnp.full((1, tq), -jnp.inf, F32)

    def half(next_unit, next_buf, unit, buf, unit_max):
        k_off, q_off = offsets(unit)
        cols = pl.ds(q_off, tq)
        m_old = tuple(m_ref[h, :, cols] for h in heads)
        m_new = tuple(jnp.maximum(m_old[h], unit_max[h]) for h in heads)
        accs = tuple(jnp.exp2(m_old[h] - m_new[h]) * acc_ref[h, :, cols] for h in heads)
        next_max = (neg_inf, neg_inf)
        for p in range(n_pieces):
            if next_unit is not None:
                next_max = scores_piece(next_unit, next_buf, p, next_max)
            accs = values_piece(k_off, buf, p, m_new, accs)
        for h in heads:
            m_ref[h, :, cols] = m_new[h]
            acc_ref[h, :, cols] = accs[h]
        return next_max

    def group_step(jj, unit_max):
        for s in range(MLA_UNITS_PER_TRIP):
            u = MLA_UNITS_PER_TRIP * jj + s
            unit_max = half(u + 1, (s + 1) % 2, u, s % 2, unit_max)
        return unit_max

    m_ref[...] = jnp.full(m_ref.shape, -jnp.inf, F32)
    acc_ref[...] = jnp.zeros(acc_ref.shape, F32)
    unit_max = (neg_inf, neg_inf)
    for p in range(n_pieces):
        unit_max = scores_piece(0, 0, p, unit_max)
    unit_max = lax.fori_loop(0, n_units // MLA_UNITS_PER_TRIP - 1, group_step, unit_max)
    for u in range(n_units - MLA_UNITS_PER_TRIP, n_units):
        last = u == n_units - 1
        unit_max = half(None if last else u + 1, None if last else (u + 1) % 2, u, u % 2, unit_max)

    @pl.loop(0, n_tiles)
    def _(tile):
        cols = pl.ds(pl.multiple_of(tile * tq, tq), tq)
        out_t = jnp.concatenate(
            [acc_ref[h, :MLA_V, cols] / acc_ref[h, MLA_V:MLA_V + 1, cols] for h in heads], axis=0)
        o_ref[0, cols, :] = out_t.T.astype(o_ref.dtype)


def _mla_attn(q, k, v, *, tq, tk):
    batch, heads, seq, _ = k.shape
    assert MLA_UNITS_PER_TRIP % 2 == 0 and ((seq // tk) * (seq // tq)) % MLA_UNITS_PER_TRIP == 0
    return pl.pallas_call(
        functools.partial(_mla_attn_body, tq=tq, tk=tk),
        grid=(batch, heads // 2),
        in_specs=[
            pl.BlockSpec((1, 2 * LANES, seq), lambda b, hp: (b, hp, 0)),
            pl.BlockSpec((1, 2, seq, LANES), lambda b, hp: (b, hp, 0, 0)),
            pl.BlockSpec((1, 2 * MLA_V, seq), lambda b, hp: (b, hp, 0)),
        ],
        out_specs=pl.BlockSpec((1, seq, LANES), lambda b, hp: (b, 0, hp)),
        out_shape=jax.ShapeDtypeStruct((batch, seq, MLA_OUT), BF16),
        scratch_shapes=[pltpu.VMEM((2, 2, tk, tq), F32),
                        pltpu.VMEM((2, 1, seq), F32),
                        pltpu.VMEM((2, MLA_V + MLA_DEN_ROWS, seq), F32)],
        compiler_params=pltpu.CompilerParams(
            dimension_semantics=("parallel", "parallel"), vmem_limit_bytes=VMEM_LIMIT),
        name="mla_attn",
    )(q, k, v)


def _dil_attn_body(slopes_ref, q_ref, kl_ref, km_ref, kr_ref, vl_ref, vm_ref, vr_ref,
                   o_ref, lse_ref, kbuf, vbuf, *, group, dilation, tq, sub_len):
    i = pl.program_id(1)
    n_res = q_ref.shape[1]
    first_res = pl.program_id(2) * n_res
    kbuf[:, 0:DIL_HALO] = kl_ref[0]
    kbuf[:, DIL_HALO:DIL_HALO + tq] = km_ref[0]
    kbuf[:, DIL_HALO + tq:] = kr_ref[0]
    vbuf[:, 0:DIL_HALO] = vl_ref[0]
    vbuf[:, DIL_HALO:DIL_HALO + tq] = vm_ref[0]
    vbuf[:, DIL_HALO + tq:] = vr_ref[0]

    win = DIL_SUB + 2 * DIL_HALO
    row = lax.broadcasted_iota(jnp.int32, (DIL_SUB, win), 0)
    col = lax.broadcasted_iota(jnp.int32, (DIL_SUB, win), 1)
    steps = jnp.abs(col - DIL_HALO - row)
    in_band = steps <= DIL_HALF_SPAN
    dist = (steps * dilation).astype(F32)
    for sub in range(tq // DIL_SUB):
        key_pos = i * tq + (sub * DIL_SUB - DIL_HALO) + col
        valid = in_band & (key_pos >= 0) & (key_pos < sub_len)
        rows = slice(sub * DIL_SUB, (sub + 1) * DIL_SUB)
        wrows = slice(sub * DIL_SUB, sub * DIL_SUB + win)
        for rr in range(n_res):
            if dilation == 1:
                out_rows = pl.ds(sub * DIL_SUB, DIL_SUB)
            else:
                out_rows = pl.ds(sub * DIL_SUB * dilation + first_res + rr, DIL_SUB, stride=dilation)
            for h in range(DIL_HEADS_PER_GROUP):
                cols = slice(h * DIL_HEAD_DIM, (h + 1) * DIL_HEAD_DIM)
                slope = slopes_ref[group * DIL_HEADS_PER_GROUP + h]
                s = _dot_nt(q_ref[0, rr, rows, cols], kbuf[rr, wrows, cols]) - slope * dist
                s = jnp.where(valid, s, -jnp.inf)
                m = jnp.max(s, axis=1, keepdims=True)
                e = jnp.exp(s - m)
                den = jnp.sum(e, axis=1, keepdims=True)
                o_ref[0, h, out_rows, :] = _dot(e.astype(BF16), vbuf[rr, wrows, cols]) / den
                lse_ref[0, h, out_rows, :] = jnp.broadcast_to(m + jnp.log(den), (DIL_SUB, DIL_HEAD_DIM))


def _dil_attn(dil_g, slopes, *, group, dilation):
    batch, _, sub_len, _ = dil_g.shape
    seq = sub_len * dilation
    tq = min(DIL_ROWS_PER_STEP, sub_len, DIL_CHUNK_TOKENS // dilation)
    n_q = sub_len // tq
    n_res = min(dilation, DIL_ROWS_PER_STEP // tq)
    halo_per_tile = tq // DIL_HALO
    n_halo = sub_len // DIL_HALO
    q_col, k_col, v_col = 0, 1, 2

    def main(c):
        return pl.BlockSpec((1, n_res, tq, DIL_OUT), lambda b, i, r: (b, r, i, c))

    def left(c):
        return pl.BlockSpec(
            (1, n_res, DIL_HALO, DIL_OUT),
            lambda b, i, r: (b, r, jnp.maximum(i * halo_per_tile - 1, 0), c))

    def right(c):
        return pl.BlockSpec(
            (1, n_res, DIL_HALO, DIL_OUT),
            lambda b, i, r: (b, r, jnp.minimum((i + 1) * halo_per_tile, n_halo - 1), c))

    out_spec = pl.BlockSpec((1, DIL_HEADS_PER_GROUP, tq * dilation, DIL_HEAD_DIM),
                            lambda b, i, r: (b, 0, i, 0))
    out_shape = jax.ShapeDtypeStruct((batch, DIL_HEADS_PER_GROUP, seq, DIL_HEAD_DIM), F32)
    o, lse = pl.pallas_call(
        functools.partial(_dil_attn_body, group=group, dilation=dilation, tq=tq, sub_len=sub_len),
        grid=(batch, n_q, dilation // n_res),
        in_specs=[pl.BlockSpec(memory_space=pltpu.SMEM),
                  main(q_col), left(k_col), main(k_col), right(k_col),
                  left(v_col), main(v_col), right(v_col)],
        out_specs=[out_spec, out_spec],
        out_shape=[out_shape, out_shape],
        scratch_shapes=[pltpu.VMEM((n_res, tq + 2 * DIL_HALO, DIL_OUT), BF16),
                        pltpu.VMEM((n_res, tq + 2 * DIL_HALO, DIL_OUT), BF16)],
        compiler_params=pltpu.CompilerParams(
            dimension_semantics=("parallel", "parallel", "arbitrary"), vmem_limit_bytes=VMEM_LIMIT),
        name=f"dil_attn_g{group}",
    )(slopes, dil_g, dil_g, dil_g, dil_g, dil_g, dil_g, dil_g)
    return o, lse


def _mem_kv_body(mem_ref, w_ref, o_ref):
    o_ref[0] = _dot(mem_ref[0].astype(BF16), w_ref[...]).astype(BF16)


def _mem_kv(mem, w):
    batch, m_len, d = mem.shape
    n = w.shape[1]
    return pl.pallas_call(
        _mem_kv_body,
        grid=(batch,),
        in_specs=[pl.BlockSpec((1, m_len, d), lambda b: (b, 0, 0)),
                  pl.BlockSpec((d, n), lambda b: (0, 0))],
        out_specs=pl.BlockSpec((1, m_len, n), lambda b: (b, 0, 0)),
        out_shape=jax.ShapeDtypeStruct((batch, m_len, n), BF16),
        compiler_params=pltpu.CompilerParams(dimension_semantics=("parallel",)),
        name="mem_kv",
    )(mem, w)


def _merge_body(h_ref, omla_ref, od0_ref, od1_ref, od2_ref, ls0_ref, ls1_ref, ls2_ref, kvm_ref,
                wgate_ref, wmq_ref, wba_ref, wbb_ref, wbc_ref, wo_ref, g_ref, b_ref, out_ref, *, n_sub):
    sub_rows = h_ref.shape[0] // n_sub
    for sub in range(n_sub):
        rows = slice(sub * sub_rows, (sub + 1) * sub_rows)
        _merge_rows(rows, h_ref, omla_ref, od0_ref, od1_ref, od2_ref, ls0_ref, ls1_ref, ls2_ref, kvm_ref,
                    wgate_ref, wmq_ref, wba_ref, wbb_ref, wbc_ref, wo_ref, g_ref, b_ref, out_ref)


def _merge_rows(rows, h_ref, omla_ref, od0_ref, od1_ref, od2_ref, ls0_ref, ls1_ref, ls2_ref, kvm_ref,
                wgate_ref, wmq_ref, wba_ref, wbb_ref, wbc_ref, wo_ref, g_ref, b_ref, out_ref):
    h = h_ref[rows, :]
    hb = h.astype(BF16)

    def heads_to_lanes(ref):
        return jnp.concatenate([ref[0, hh, rows, :] for hh in range(DIL_HEADS_PER_GROUP)], axis=1)

    ls0, ls1, ls2 = heads_to_lanes(ls0_ref), heads_to_lanes(ls1_ref), heads_to_lanes(ls2_ref)
    mx = jnp.maximum(ls0, jnp.maximum(ls1, ls2))
    e0, e1, e2 = jnp.exp(ls0 - mx), jnp.exp(ls1 - mx), jnp.exp(ls2 - mx)
    o_dil = (e0 * heads_to_lanes(od0_ref) + e1 * heads_to_lanes(od1_ref)
             + e2 * heads_to_lanes(od2_ref)) / (e0 + e1 + e2)

    mq = (_dot(hb, wmq_ref[...]) * (MEM_HEAD_DIM ** -0.5)).astype(BF16)
    heads = []
    for hh in range(MEM_HEADS):
        kc = slice(hh * MEM_HEAD_DIM, (hh + 1) * MEM_HEAD_DIM)
        vc = slice(MEM_OUT + hh * MEM_HEAD_DIM, MEM_OUT + (hh + 1) * MEM_HEAD_DIM)
        s = _dot_nt(mq[:, kc], kvm_ref[0, :, kc])
        p = jnp.exp(s - jnp.max(s, axis=1, keepdims=True))
        heads.append(_dot(p.astype(BF16), kvm_ref[0, :, vc]) / jnp.sum(p, axis=1, keepdims=True))
    o_mem = jnp.concatenate(heads, axis=1)

    y_a = _dot(omla_ref[rows, :], wba_ref[...])
    y_b = _dot(o_dil.astype(BF16), wbb_ref[...])
    y_c = _dot(o_mem.astype(BF16), wbc_ref[...])
    d = h.shape[1]
    merged = (jax.nn.sigmoid(_dot(hb, wgate_ref[:, :d])) * y_a
              + jax.nn.sigmoid(_dot(hb, wgate_ref[:, d:2 * d])) * y_b
              + jax.nn.sigmoid(_dot(hb, wgate_ref[:, 2 * d:])) * y_c)
    mix = _dot(merged.astype(BF16), wo_ref[...])
    out_ref[rows, :] = _layer_norm(ALPHA * h + mix, g_ref[...], b_ref[...])


def _merge(h2d, o_mla, o_dil, lse_dil, kv_mem, weights, g, b, *, seq, tm, n_sub):
    t, d = h2d.shape
    n_s = seq // tm

    def rows(width):
        return pl.BlockSpec((tm, width), lambda i: (i, 0))

    def full(a):
        return pl.BlockSpec(a.shape, lambda i: (0,) * a.ndim, pipeline_mode=pl.Buffered(1))

    m_len, kv_cols = kv_mem.shape[1:]
    return pl.pallas_call(
        functools.partial(_merge_body, n_sub=n_sub),
        grid=(t // tm,),
        in_specs=[rows(d), rows(MLA_OUT)]
        + [pl.BlockSpec((1, DIL_HEADS_PER_GROUP, tm, DIL_HEAD_DIM),
                        lambda i: (i // n_s, 0, i % n_s, 0))] * 6
        + [pl.BlockSpec((1, m_len, kv_cols), lambda i: (i // n_s, 0, 0))]
        + [full(w) for w in weights] + [full(g), full(b)],
        out_specs=rows(d),
        out_shape=jax.ShapeDtypeStruct((t, d), F32),
        compiler_params=pltpu.CompilerParams(
            dimension_semantics=("parallel",), vmem_limit_bytes=VMEM_LIMIT),
        name="merge",
    )(h2d, o_mla, *o_dil, *lse_dil, kv_mem, *weights, g, b)


def _swap_halves(w):
    half = w.shape[-1] // 2
    return jnp.concatenate([w[..., half:], w[..., :half]], axis=-1)


def _prep_mla_weights(w_in, w_uq, w_ukv):
    d = w_in.shape[0]
    o_q, o_kv, o_kr = 0, MLA_Q_LORA, MLA_Q_LORA + MLA_KV_LORA
    w_cq = w_in[:, o_q:o_q + MLA_Q_LORA]
    w_ckv = w_in[:, o_kv:o_kv + MLA_KV_LORA]
    w_kr = w_in[:, o_kr:o_kr + MLA_ROPE]
    pad_tail = LANES - MLA_NOPE - MLA_ROPE
    z_nope = jnp.zeros((d, MLA_NOPE), F32)
    z_tail = jnp.zeros((d, pad_tail), F32)
    w_kra = jnp.concatenate([z_nope, w_kr, z_tail], axis=1)
    w_krb = jnp.concatenate([z_nope, _swap_halves(w_kr), z_tail], axis=1)

    uq = w_uq.reshape(MLA_Q_LORA, MLA_HEADS, MLA_NOPE + MLA_ROPE)
    uq_nope, uq_pe = uq[..., :MLA_NOPE], uq[..., MLA_NOPE:]
    zq_nope = jnp.zeros_like(uq_nope)
    zq_tail = jnp.zeros((MLA_Q_LORA, MLA_HEADS, pad_tail), F32)
    w_qa = jnp.concatenate([uq_nope, uq_pe, zq_tail], axis=-1).reshape(MLA_Q_LORA, MLA_HEADS * LANES)
    w_qb = jnp.concatenate([zq_nope, _swap_halves(uq_pe), zq_tail], axis=-1).reshape(
        MLA_Q_LORA, MLA_HEADS * LANES)

    ukv = w_ukv.reshape(MLA_KV_LORA, MLA_HEADS, MLA_NOPE + MLA_V)
    uk, uv = ukv[..., :MLA_NOPE], ukv[..., MLA_NOPE:]
    w_ka = jnp.concatenate(
        [uk, jnp.zeros((MLA_KV_LORA, MLA_HEADS, LANES - MLA_NOPE), F32)], axis=-1).reshape(
            MLA_KV_LORA, MLA_HEADS * LANES)
    w_v = uv.reshape(MLA_KV_LORA, MLA_HEADS * MLA_V)
    return [w.astype(BF16) for w in (w_cq, w_ckv, w_kra, w_krb)], [w.astype(BF16) for w in (w_qa, w_qb, w_ka, w_v)]


def _rope_tables(seq):
    pos = np.arange(seq, dtype=np.float64)
    inv = 1.0 / (ROPE_THETA ** (np.arange(0, MLA_ROPE, 2, dtype=np.float64) / MLA_ROPE))
    pad_tail = LANES - MLA_NOPE - MLA_ROPE
    q_scale = (MLA_NOPE + MLA_ROPE) ** -0.5 * LOG2_E
    ang = pos[:, None] * inv[None, :]
    cos, sin = np.cos(ang), np.sin(ang)

    def tables(lead_one):
        c = np.concatenate([np.full((seq, MLA_NOPE), lead_one), cos, cos, np.zeros((seq, pad_tail))], axis=1)
        s = np.concatenate([np.zeros((seq, MLA_NOPE)), -sin, sin, np.zeros((seq, pad_tail))], axis=1)
        return c, s

    c_q, s_q = tables(1.0)
    c_k, s_k = tables(0.0)
    return tuple(np.ascontiguousarray(a, dtype=np.float32)
                 for a in ((c_q * q_scale).T, (s_q * q_scale).T, c_k, s_k))


def kernel(x, mem, w_in, mla_q_norm, mla_kv_norm, w_uq, w_ukv, w_mem_kv, w_br_mla, w_br_dil, w_br_mem, w_o,
           ffn1_w_gate, ffn1_w_up, ffn1_w_down, ffn2_w_gate, ffn2_w_up, ffn2_w_down,
           ln1_g, ln1_b, ln2_g, ln2_b, ln3_g, ln3_b):
    batch, seq, d = x.shape
    t = batch * seq
    tm_ffn = min(1024, t)
    tm_proj = min(512, seq)
    tm_merge = min(512, seq)
    tables = _rope_tables(seq)
    slopes = 2.0 ** (-8.0 * jnp.arange(1, DIL_HEADS + 1, dtype=F32) / DIL_HEADS)

    h = x.reshape(t, d)
    for l in range(DEPTH):
        bf = lambda w: w[l].astype(BF16)
        row = lambda v: v[l].reshape(1, -1)
        h = _ffn_ln(h, bf(ffn1_w_gate), bf(ffn1_w_up), bf(ffn1_w_down), row(ln1_g), row(ln1_b),
                    tm=tm_ffn, n_sub=FFN_SUB_TILES)

        w_in_l = w_in[l]
        o_dil_cols = MLA_Q_LORA + MLA_KV_LORA + MLA_ROPE
        o_memq = o_dil_cols + DIL_QKV
        o_gate = o_memq + MEM_OUT
        w_c, w_u = _prep_mla_weights(w_in_l, w_uq[l], w_ukv[l])
        proj_weights = w_c + [row(mla_q_norm), row(mla_kv_norm)] + w_u + [
            w_in_l[:, o_dil_cols:o_memq].astype(BF16)]
        q, k, v, *dil = _proj(h, proj_weights, tables, batch=batch, seq=seq, tm=tm_proj)

        o_mla = _mla_attn(q, k, v, tq=min(512, seq), tk=min(1024, seq)).reshape(t, MLA_OUT)
        dil_parts = [_dil_attn(dil[g], slopes, group=g, dilation=dl)
                     for g, (_, dl) in enumerate(DIL_PAIRS)]
        kv_mem = _mem_kv(mem, bf(w_mem_kv))

        merge_weights = [w_in_l[:, o_gate:].astype(BF16), w_in_l[:, o_memq:o_gate].astype(BF16),
                         bf(w_br_mla), bf(w_br_dil), bf(w_br_mem), bf(w_o)]
        h = _merge(h, o_mla, [p[0] for p in dil_parts], [p[1] for p in dil_parts], kv_mem,
                   merge_weights, row(ln2_g), row(ln2_b), seq=seq, tm=tm_merge, n_sub=2)

        h = _ffn_ln(h, bf(ffn2_w_gate), bf(ffn2_w_up), bf(ffn2_w_down), row(ln3_g), row(ln3_b),
                    tm=tm_ffn, n_sub=FFN_SUB_TILES)
    return h.reshape(batch, seq, d)
```

```python
import functools
import math

import jax
import jax.numpy as jnp
import numpy as np
from jax import lax
from jax.experimental import pallas as pl
from jax.experimental.pallas import tpu as pltpu

F32 = jnp.float32
BF16 = jnp.bfloat16

D_MODEL = 1024
DEPTH = 1
MLA_HEADS = 8
MLA_Q_LORA = 256
MLA_KV_LORA = 256
MLA_NOPE = 64
MLA_ROPE = 32
MLA_V = 64
ROPE_THETA = 10000.0
DIL_PAIRS = ((128, 1), (512, 4), (2048, 16))
DIL_GROUPS = 3
DIL_HEADS_PER_GROUP = 4
DIL_HEAD_DIM = 128
DIL_HEADS = DIL_GROUPS * DIL_HEADS_PER_GROUP
MEM_HEADS = 4
MEM_HEAD_DIM = 128
EPS = 1e-5
ALPHA = (2 * DEPTH) ** 0.25

LOG2_E = math.log2(math.e)
LANES = 128
DIL_OUT = DIL_HEADS_PER_GROUP * DIL_HEAD_DIM
DIL_QKV = 3 * DIL_HEADS * DIL_HEAD_DIM
MEM_OUT = MEM_HEADS * MEM_HEAD_DIM
MLA_OUT = MLA_HEADS * MLA_V
MLA_DEN_ROWS = 16
MLA_PIECE = 256
MLA_UNITS_PER_TRIP = 8
DIL_HALF_SPAN = 64
DIL_HALO = 64
DIL_SUB = 128
DIL_CHUNK_TOKENS = 2048
DIL_ROWS_PER_STEP = 2048
FFN_SUB_TILES = 4
WCOPY_BLOCK = 512
VMEM_LIMIT = 56 * 1024 * 1024

assert all(w // 2 // d == DIL_HALF_SPAN for w, d in DIL_PAIRS)

_NT = (((1,), (1,)), ((), ()))


def _dot(a, b):
    return jnp.dot(a, b, preferred_element_type=F32)


def _dot_nt(a, b):
    return lax.dot_general(a, b, _NT, preferred_element_type=F32)


def _dot_tt(w, x):
    return lax.dot_general(w, x, (((0,), (1,)), ((), ())), preferred_element_type=F32)


def _layer_norm(y, g, b):
    mu = jnp.mean(y, axis=-1, keepdims=True)
    yc = y - mu
    var = jnp.mean(yc * yc, axis=-1, keepdims=True)
    return yc * lax.rsqrt(var + EPS) * g + b


def _rms_norm(y, g):
    return y * lax.rsqrt(jnp.mean(y * y, axis=-1, keepdims=True) + EPS) * g


def _ffn_ln_body(x_ref, wg_ref, wu_ref, wd_ref, g_ref, b_ref, o_ref, *, n_sub):
    sub_rows = x_ref.shape[0] // n_sub
    for sub in range(n_sub):
        rows = slice(sub * sub_rows, (sub + 1) * sub_rows)
        x = x_ref[rows, :]
        xb = x.astype(BF16)
        gate = _dot(xb, wg_ref[...])
        up = _dot(xb, wu_ref[...])
        act = (gate * jax.nn.sigmoid(gate) * up).astype(BF16)
        y = ALPHA * x + 0.5 * _dot(act, wd_ref[...])
        o_ref[rows, :] = _layer_norm(y, g_ref[...], b_ref[...])


def _ffn_ln(x2d, wg, wu, wd, g, b, *, tm, n_sub):
    t, d = x2d.shape

    def full(a):
        return pl.BlockSpec(a.shape, lambda i: (0,) * a.ndim, pipeline_mode=pl.Buffered(1))

    return pl.pallas_call(
        functools.partial(_ffn_ln_body, n_sub=n_sub),
        grid=(t // tm,),
        in_specs=[pl.BlockSpec((tm, d), lambda i: (i, 0))] + [full(a) for a in (wg, wu, wd, g, b)],
        out_specs=pl.BlockSpec((tm, d), lambda i: (i, 0)),
        out_shape=jax.ShapeDtypeStruct((t, d), F32),
        compiler_params=pltpu.CompilerParams(
            dimension_semantics=("parallel",), vmem_limit_bytes=VMEM_LIMIT),
        name="ffn_ln",
    )(x2d, wg, wu, wd, g, b)


def _proj_body(*refs):
    n_hc = D_MODEL // LANES
    h_refs = refs[:n_hc]
    (wcq_ref, wckv_ref, wkra_ref, wkrb_ref, gq_ref, gkv_ref,
     wqa_ref, wqb_ref, wka_ref, wv_ref, wdil_ref,
     cq_ref, sq_ref, ck_ref, sk_ref,
     q_ref, k_ref, v_ref, dil0_ref, dil1_ref, dil2_ref, hperm_ref) = refs[n_hc:]
    hb = jnp.concatenate([hc[...] for hc in h_refs], axis=1).astype(BF16)
    cqn = _rms_norm(_dot(hb, wcq_ref[...]), gq_ref[...]).astype(BF16)
    ckvn = _rms_norm(_dot(hb, wckv_ref[...]), gkv_ref[...]).astype(BF16)
    k_rope = _dot(hb, wkra_ref[...]) * ck_ref[...] + _dot(hb, wkrb_ref[...]) * sk_ref[...]
    ka = _dot(ckvn, wka_ref[...])
    for h in range(MLA_HEADS):
        sl = slice(h * LANES, (h + 1) * LANES)
        k_ref[0, h] = (ka[:, sl] + k_rope).astype(BF16)
    qa_t = _dot_tt(wqa_ref[...], cqn)
    qb_t = _dot_tt(wqb_ref[...], cqn)
    cq_t = cq_ref[...]
    sq_t = sq_ref[...]
    for h in range(MLA_HEADS):
        sl = slice(h * LANES, (h + 1) * LANES)
        q_ref[0, sl] = (qa_t[sl] * cq_t + qb_t[sl] * sq_t).astype(BF16)
    v_ref[0] = _dot_tt(wv_ref[...], ckvn).astype(BF16)

    dil_scale = DIL_HEAD_DIM ** -0.5
    tm = h_refs[0].shape[0]
    for g, (dil_ref, (_, dilation)) in enumerate(zip((dil0_ref, dil1_ref, dil2_ref), DIL_PAIRS)):
        rows_per = tm // dilation
        if dilation == 1:
            hp = hb
        else:
            for r in range(dilation):
                for c, hc in enumerate(h_refs):
                    hperm_ref[r * rows_per:(r + 1) * rows_per, c * LANES:(c + 1) * LANES] = hc[
                        pl.ds(r, rows_per, stride=dilation), :].astype(BF16)
            hp = hperm_ref[...]
        for part in range(3):
            c0 = (part * DIL_GROUPS + g) * DIL_OUT
            res = _dot(hp, wdil_ref[:, c0:c0 + DIL_OUT])
            if part == 0:
                res = res * dil_scale
            res = res.astype(BF16)
            for r in range(dilation):
                dil_ref[0, r, :, part * DIL_OUT:(part + 1) * DIL_OUT] = res[r * rows_per:(r + 1) * rows_per]


def _proj(h2d, weights, tables, *, batch, seq, tm):
    t, d = h2d.shape
    n_s = seq // tm

    def full(a):
        return pl.BlockSpec(a.shape, lambda i: (0,) * a.ndim)

    tab_spec = pl.BlockSpec((tm, LANES), lambda i: (i % n_s, 0))
    tab_t_spec = pl.BlockSpec((LANES, tm), lambda i: (0, i % n_s))
    head_map = lambda i: (i // n_s, 0, i % n_s, 0)
    feat_map = lambda i: (i // n_s, 0, i % n_s)
    dil_specs = [pl.BlockSpec((1, dl, tm // dl, 3 * DIL_OUT), head_map) for _, dl in DIL_PAIRS]
    dil_shapes = [jax.ShapeDtypeStruct((batch, dl, seq // dl, 3 * DIL_OUT), BF16) for _, dl in DIL_PAIRS]
    return pl.pallas_call(
        _proj_body,
        grid=(t // tm,),
        in_specs=[pl.BlockSpec((tm, LANES), functools.partial(lambda c, i: (i, c), c))
                  for c in range(d // LANES)]
        + [full(w) for w in weights] + [tab_t_spec] * 2 + [tab_spec] * 2,
        out_specs=[
            pl.BlockSpec((1, MLA_HEADS * LANES, tm), feat_map),
            pl.BlockSpec((1, MLA_HEADS, tm, LANES), head_map),
            pl.BlockSpec((1, MLA_OUT, tm), feat_map),
        ] + dil_specs,
        out_shape=[
            jax.ShapeDtypeStruct((batch, MLA_HEADS * LANES, seq), BF16),
            jax.ShapeDtypeStruct((batch, MLA_HEADS, seq, LANES), BF16),
            jax.ShapeDtypeStruct((batch, MLA_OUT, seq), BF16),
        ] + dil_shapes,
        scratch_shapes=[pltpu.VMEM((tm, d), BF16)],
        compiler_params=pltpu.CompilerParams(
            dimension_semantics=("parallel",), vmem_limit_bytes=VMEM_LIMIT),
        name="proj",
    )(*([h2d] * (d // LANES)), *weights, *tables)


def _mla_attn_body(q_ref, k_ref, vt_ref, o_ref, st_ref, m_ref, acc_ref, *, tq, tk):
    seq = k_ref.shape[2]
    n_tiles = seq // tq
    n_units = (seq // tk) * n_tiles
    heads = (0, 1)
    n_pieces = tk // MLA_PIECE

    def offsets(unit):
        if isinstance(unit, int):
            return (unit // n_tiles) * tk, (unit % n_tiles) * tq
        return (pl.multiple_of(lax.div(unit, n_tiles) * tk, tk),
                pl.multiple_of(lax.rem(unit, n_tiles) * tq, tq))

    def scores_piece(unit, buf, p, maxes):
        k_off, q_off = offsets(unit)
        rows = slice(p * MLA_PIECE, (p + 1) * MLA_PIECE)
        out = []
        for h in heads:
            st = _dot(k_ref[0, h, pl.ds(k_off + p * MLA_PIECE, MLA_PIECE), :],
                      q_ref[0, h * LANES:(h + 1) * LANES, pl.ds(q_off, tq)])
            st_ref[buf, h, rows] = st
            out.append(jnp.maximum(maxes[h], jnp.max(st, axis=0, keepdims=True)))
        return tuple(out)

    ones_rows = jnp.ones((MLA_DEN_ROWS, MLA_PIECE), BF16)

    def values_piece(k_off, buf, p, m_new, accs):
        rows = slice(p * MLA_PIECE, (p + 1) * MLA_PIECE)
        out = []
        for h in heads:
            pt = jnp.exp2(st_ref[buf, h, rows] - m_new[h]).astype(BF16)
            vtc = jnp.concatenate(
                [vt_ref[0, h * MLA_V:(h + 1) * MLA_V, pl.ds(k_off + p * MLA_PIECE, MLA_PIECE)], ones_rows],
                axis=0)
            out.append(accs[h] + _dot(vtc, pt))
        return tuple(out)

    neg_inf = jnp.full((1, tq), -jnp.inf, F32)

    def half(next_unit, next_buf, unit, buf, unit_max):
        k_off, q_off = offsets(unit)
        cols = pl.ds(q_off, tq)
        m_old = tuple(m_ref[h, :, cols] for h in heads)
        m_new = tuple(jnp.maximum(m_old[h], unit_max[h]) for h in heads)
        accs = tuple(jnp.exp2(m_old[h] - m_new[h]) * acc_ref[h, :, cols] for h in heads)
        next_max = (neg_inf, neg_inf)
        for p in range(n_pieces):
            if next_unit is not None:
                next_max = scores_piece(next_unit, next_buf, p, next_max)
            accs = values_piece(k_off, buf, p, m_new, accs)
        for h in heads:
            m_ref[h, :, cols] = m_new[h]
            acc_ref[h, :, cols] = accs[h]
        return next_max

    def group_step(jj, unit_max):
        for s in range(MLA_UNITS_PER_TRIP):
            u = MLA_UNITS_PER_TRIP * jj + s
            unit_max = half(u + 1, (s + 1) % 2, u, s % 2, unit_max)
        return unit_max

    m_ref[...] = jnp.full(m_ref.shape, -jnp.inf, F32)
    acc_ref[...] = jnp.zeros(acc_ref.shape, F32)
    unit_max = (neg_inf, neg_inf)
    for p in range(n_pieces):
        unit_max = scores_piece(0, 0, p, unit_max)
    unit_max = lax.fori_loop(0, n_units // MLA_UNITS_PER_TRIP - 1, group_step, unit_max)
    for u in range(n_units - MLA_UNITS_PER_TRIP, n_units):
        last = u == n_units - 1
        unit_max = half(None if last else u + 1, None if last else (u + 1) % 2, u, u % 2, unit_max)

    @pl.loop(0, n_tiles)
    def _(tile):
        cols = pl.ds(pl.multiple_of(tile * tq, tq), tq)
        out_t = jnp.concatenate(
            [acc_ref[h, :MLA_V, cols] / acc_ref[h, MLA_V:MLA_V + 1, cols] for h in heads], axis=0)
        o_ref[0, cols, :] = out_t.T.astype(o_ref.dtype)


def _mla_attn(q, k, v, *, tq, tk):
    batch, heads, seq, _ = k.shape
    assert MLA_UNITS_PER_TRIP % 2 == 0 and ((seq // tk) * (seq // tq)) % MLA_UNITS_PER_TRIP == 0
    return pl.pallas_call(
        functools.partial(_mla_attn_body, tq=tq, tk=tk),
        grid=(batch, heads // 2),
        in_specs=[
            pl.BlockSpec((1, 2 * LANES, seq), lambda b, hp: (b, hp, 0)),
            pl.BlockSpec((1, 2, seq, LANES), lambda b, hp: (b, hp, 0, 0)),
            pl.BlockSpec((1, 2 * MLA_V, seq), lambda b, hp: (b, hp, 0)),
        ],
        out_specs=pl.BlockSpec((1, seq, LANES), lambda b, hp: (b, 0, hp)),
        out_shape=jax.ShapeDtypeStruct((batch, seq, MLA_OUT), BF16),
        scratch_shapes=[pltpu.VMEM((2, 2, tk, tq), F32),
                        pltpu.VMEM((2, 1, seq), F32),
                        pltpu.VMEM((2, MLA_V + MLA_DEN_ROWS, seq), F32)],
        compiler_params=pltpu.CompilerParams(
            dimension_semantics=("parallel", "parallel"), vmem_limit_bytes=VMEM_LIMIT),
        name="mla_attn",
    )(q, k, v)


def _dil_attn_body(slopes_ref, q_ref, kl_ref, km_ref, kr_ref, vl_ref, vm_ref, vr_ref,
                   o_ref, lse_ref, kbuf, vbuf, *, group, dilation, tq, sub_len):
    i = pl.program_id(1)
    n_res = q_ref.shape[1]
    first_res = pl.program_id(2) * n_res
    kbuf[:, 0:DIL_HALO] = kl_ref[0]
    kbuf[:, DIL_HALO:DIL_HALO + tq] = km_ref[0]
    kbuf[:, DIL_HALO + tq:] = kr_ref[0]
    vbuf[:, 0:DIL_HALO] = vl_ref[0]
    vbuf[:, DIL_HALO:DIL_HALO + tq] = vm_ref[0]
    vbuf[:, DIL_HALO + tq:] = vr_ref[0]

    win = DIL_SUB + 2 * DIL_HALO
    row = lax.broadcasted_iota(jnp.int32, (DIL_SUB, win), 0)
    col = lax.broadcasted_iota(jnp.int32, (DIL_SUB, win), 1)
    steps = jnp.abs(col - DIL_HALO - row)
    in_band = steps <= DIL_HALF_SPAN
    dist = (steps * dilation).astype(F32)
    for sub in range(tq // DIL_SUB):
        key_pos = i * tq + (sub * DIL_SUB - DIL_HALO) + col
        valid = in_band & (key_pos >= 0) & (key_pos < sub_len)
        rows = slice(sub * DIL_SUB, (sub + 1) * DIL_SUB)
        wrows = slice(sub * DIL_SUB, sub * DIL_SUB + win)
        for rr in range(n_res):
            if dilation == 1:
                out_rows = pl.ds(sub * DIL_SUB, DIL_SUB)
            else:
                out_rows = pl.ds(sub * DIL_SUB * dilation + first_res + rr, DIL_SUB, stride=dilation)
            for h in range(DIL_HEADS_PER_GROUP):
                cols = slice(h * DIL_HEAD_DIM, (h + 1) * DIL_HEAD_DIM)
                slope = slopes_ref[group * DIL_HEADS_PER_GROUP + h]
                s = _dot_nt(q_ref[0, rr, rows, cols], kbuf[rr, wrows, cols]) - slope * dist
                s = jnp.where(valid, s, -jnp.inf)
                m = jnp.max(s, axis=1, keepdims=True)
                e = jnp.exp(s - m)
                den = jnp.sum(e, axis=1, keepdims=True)
                o_ref[0, h, out_rows, :] = _dot(e.astype(BF16), vbuf[rr, wrows, cols]) / den
                lse_ref[0, h, out_rows, :] = jnp.broadcast_to(m + jnp.log(den), (DIL_SUB, DIL_HEAD_DIM))


def _dil_attn(dil_g, slopes, *, group, dilation):
    batch, _, sub_len, _ = dil_g.shape
    seq = sub_len * dilation
    tq = min(DIL_ROWS_PER_STEP, sub_len, DIL_CHUNK_TOKENS // dilation)
    n_q = sub_len // tq
    n_res = min(dilation, DIL_ROWS_PER_STEP // tq)
    halo_per_tile = tq // DIL_HALO
    n_halo = sub_len // DIL_HALO
    q_col, k_col, v_col = 0, 1, 2

    def main(c):
        return pl.BlockSpec((1, n_res, tq, DIL_OUT), lambda b, i, r: (b, r, i, c))

    def left(c):
        return pl.BlockSpec(
            (1, n_res, DIL_HALO, DIL_OUT),
            lambda b, i, r: (b, r, jnp.maximum(i * halo_per_tile - 1, 0), c))

    def right(c):
        return pl.BlockSpec(
            (1, n_res, DIL_HALO, DIL_OUT),
            lambda b, i, r: (b, r, jnp.minimum((i + 1) * halo_per_tile, n_halo - 1), c))

    out_spec = pl.BlockSpec((1, DIL_HEADS_PER_GROUP, tq * dilation, DIL_HEAD_DIM),
                            lambda b, i, r: (b, 0, i, 0))
    out_shape = jax.ShapeDtypeStruct((batch, DIL_HEADS_PER_GROUP, seq, DIL_HEAD_DIM), F32)
    o, lse = pl.pallas_call(
        functools.partial(_dil_attn_body, group=group, dilation=dilation, tq=tq, sub_len=sub_len),
        grid=(batch, n_q, dilation // n_res),
        in_specs=[pl.BlockSpec(memory_space=pltpu.SMEM),
                  main(q_col), left(k_col), main(k_col), right(k_col),
                  left(v_col), main(v_col), right(v_col)],
        out_specs=[out_spec, out_spec],
        out_shape=[out_shape, out_shape],
        scratch_shapes=[pltpu.VMEM((n_res, tq + 2 * DIL_HALO, DIL_OUT), BF16),
                        pltpu.VMEM((n_res, tq + 2 * DIL_HALO, DIL_OUT), BF16)],
        compiler_params=pltpu.CompilerParams(
            dimension_semantics=("parallel", "parallel", "arbitrary"), vmem_limit_bytes=VMEM_LIMIT),
        name=f"dil_attn_g{group}",
    )(slopes, dil_g, dil_g, dil_g, dil_g, dil_g, dil_g, dil_g)
    return o, lse


def _mem_kv_body(mem_ref, w_ref, o_ref):
    o_ref[0] = _dot(mem_ref[0].astype(BF16), w_ref[...]).astype(BF16)


def _mem_kv(mem, w):
    batch, m_len, d = mem.shape
    n = w.shape[1]
    return pl.pallas_call(
        _mem_kv_body,
        grid=(batch,),
        in_specs=[pl.BlockSpec((1, m_len, d), lambda b: (b, 0, 0)),
                  pl.BlockSpec((d, n), lambda b: (0, 0))],
        out_specs=pl.BlockSpec((1, m_len, n), lambda b: (b, 0, 0)),
        out_shape=jax.ShapeDtypeStruct((batch, m_len, n), BF16),
        compiler_params=pltpu.CompilerParams(dimension_semantics=("parallel",)),
        name="mem_kv",
    )(mem, w)


def _merge_body(h_ref, omla_ref, od0_ref, od1_ref, od2_ref, ls0_ref, ls1_ref, ls2_ref, kvm_ref,
                wmg_ref, wba_ref, wbb_ref, wbc_ref, wo_ref, g_ref, b_ref, out_ref, *, n_sub):
    sub_rows = h_ref.shape[0] // n_sub
    for sub in range(n_sub):
        rows = slice(sub * sub_rows, (sub + 1) * sub_rows)
        _merge_rows(rows, h_ref, omla_ref, od0_ref, od1_ref, od2_ref, ls0_ref, ls1_ref, ls2_ref, kvm_ref,
                    wmg_ref, wba_ref, wbb_ref, wbc_ref, wo_ref, g_ref, b_ref, out_ref)


def _merge_rows(rows, h_ref, omla_ref, od0_ref, od1_ref, od2_ref, ls0_ref, ls1_ref, ls2_ref, kvm_ref,
                wmg_ref, wba_ref, wbb_ref, wbc_ref, wo_ref, g_ref, b_ref, out_ref):
    h = h_ref[rows, :]
    hb = h.astype(BF16)

    def heads_to_lanes(ref):
        return jnp.concatenate([ref[0, hh, rows, :] for hh in range(DIL_HEADS_PER_GROUP)], axis=1)

    ls0, ls1, ls2 = heads_to_lanes(ls0_ref), heads_to_lanes(ls1_ref), heads_to_lanes(ls2_ref)
    mx = jnp.maximum(ls0, jnp.maximum(ls1, ls2))
    e0, e1, e2 = jnp.exp(ls0 - mx), jnp.exp(ls1 - mx), jnp.exp(ls2 - mx)
    o_dil = (e0 * heads_to_lanes(od0_ref) + e1 * heads_to_lanes(od1_ref)
             + e2 * heads_to_lanes(od2_ref)) / (e0 + e1 + e2)

    mq = (_dot(hb, wmg_ref[:, :MEM_OUT]) * (MEM_HEAD_DIM ** -0.5)).astype(BF16)
    heads = []
    for hh in range(MEM_HEADS):
        kc = slice(hh * MEM_HEAD_DIM, (hh + 1) * MEM_HEAD_DIM)
        vc = slice(MEM_OUT + hh * MEM_HEAD_DIM, MEM_OUT + (hh + 1) * MEM_HEAD_DIM)
        s = _dot_nt(mq[:, kc], kvm_ref[0, :, kc])
        p = jnp.exp(s - jnp.max(s, axis=1, keepdims=True))
        heads.append(_dot(p.astype(BF16), kvm_ref[0, :, vc]) / jnp.sum(p, axis=1, keepdims=True))
    o_mem = jnp.concatenate(heads, axis=1)

    y_a = _dot(omla_ref[rows, :], wba_ref[...])
    y_b = _dot(o_dil.astype(BF16), wbb_ref[...])
    y_c = _dot(o_mem.astype(BF16), wbc_ref[...])
    d = h.shape[1]
    gate = lambda br: jax.nn.sigmoid(_dot(hb, wmg_ref[:, MEM_OUT + br * d:MEM_OUT + (br + 1) * d]))
    merged = gate(0) * y_a + gate(1) * y_b + gate(2) * y_c
    mix = _dot(merged.astype(BF16), wo_ref[...])
    out_ref[rows, :] = _layer_norm(ALPHA * h + mix, g_ref[...], b_ref[...])


def _merge(h2d, o_mla, o_dil, lse_dil, kv_mem, weights, g, b, *, seq, tm, n_sub):
    t, d = h2d.shape
    n_s = seq // tm

    def rows(width):
        return pl.BlockSpec((tm, width), lambda i: (i, 0))

    def full(a):
        return pl.BlockSpec(a.shape, lambda i: (0,) * a.ndim, pipeline_mode=pl.Buffered(1))

    m_len, kv_cols = kv_mem.shape[1:]
    return pl.pallas_call(
        functools.partial(_merge_body, n_sub=n_sub),
        grid=(t // tm,),
        in_specs=[rows(d), rows(MLA_OUT)]
        + [pl.BlockSpec((1, DIL_HEADS_PER_GROUP, tm, DIL_HEAD_DIM),
                        lambda i: (i // n_s, 0, i % n_s, 0))] * 6
        + [pl.BlockSpec((1, m_len, kv_cols), lambda i: (i // n_s, 0, 0))]
        + [full(w) for w in weights] + [full(g), full(b)],
        out_specs=rows(d),
        out_shape=jax.ShapeDtypeStruct((t, d), F32),
        compiler_params=pltpu.CompilerParams(
            dimension_semantics=("parallel",), vmem_limit_bytes=VMEM_LIMIT),
        name="merge",
    )(h2d, o_mla, *o_dil, *lse_dil, kv_mem, *weights, g, b)


def _cols_to_bf16_body(main_ref, halo_ref, o_ref, *, lane_shift):
    x = jnp.concatenate([main_ref[...], halo_ref[...]], axis=1)
    o_ref[...] = x[:, lane_shift:lane_shift + o_ref.shape[1]].astype(BF16)


def _cols_to_bf16(w, start, width):
    d = w.shape[0]
    lane_shift = start % LANES
    tiles_per_step = WCOPY_BLOCK // LANES
    first_tile = start // LANES
    assert first_tile % tiles_per_step == 0 and width % WCOPY_BLOCK == 0
    first_block = first_tile // tiles_per_step
    return pl.pallas_call(
        functools.partial(_cols_to_bf16_body, lane_shift=lane_shift),
        grid=(width // WCOPY_BLOCK,),
        in_specs=[pl.BlockSpec((d, WCOPY_BLOCK), lambda j: (0, first_block + j)),
                  pl.BlockSpec((d, LANES), lambda j: (0, first_tile + (j + 1) * tiles_per_step))],
        out_specs=pl.BlockSpec((d, WCOPY_BLOCK), lambda j: (0, j)),
        out_shape=jax.ShapeDtypeStruct((d, width), BF16),
        compiler_params=pltpu.CompilerParams(dimension_semantics=("parallel",)),
        name="cols_to_bf16",
    )(w, w)
def _swap_halves(w):
    half = w.shape[-1] // 2
    return jnp.concatenate([w[..., half:], w[..., :half]], axis=-1)


def _prep_mla_weights(w_in, w_uq, w_ukv):
    d = w_in.shape[0]
    o_q, o_kv, o_kr = 0, MLA_Q_LORA, MLA_Q_LORA + MLA_KV_LORA
    w_cq = w_in[:, o_q:o_q + MLA_Q_LORA]
    w_ckv = w_in[:, o_kv:o_kv + MLA_KV_LORA]
    w_kr = w_in[:, o_kr:o_kr + MLA_ROPE]
    pad_tail = LANES - MLA_NOPE - MLA_ROPE
    z_nope = jnp.zeros((d, MLA_NOPE), F32)
    z_tail = jnp.zeros((d, pad_tail), F32)
    w_kra = jnp.concatenate([z_nope, w_kr, z_tail], axis=1)
    w_krb = jnp.concatenate([z_nope, _swap_halves(w_kr), z_tail], axis=1)

    uq = w_uq.reshape(MLA_Q_LORA, MLA_HEADS, MLA_NOPE + MLA_ROPE)
    uq_nope, uq_pe = uq[..., :MLA_NOPE], uq[..., MLA_NOPE:]
    zq_nope = jnp.zeros_like(uq_nope)
    zq_tail = jnp.zeros((MLA_Q_LORA, MLA_HEADS, pad_tail), F32)
    w_qa = jnp.concatenate([uq_nope, uq_pe, zq_tail], axis=-1).reshape(MLA_Q_LORA, MLA_HEADS * LANES)
    w_qb = jnp.concatenate([zq_nope, _swap_halves(uq_pe), zq_tail], axis=-1).reshape(
        MLA_Q_LORA, MLA_HEADS * LANES)

    ukv = w_ukv.reshape(MLA_KV_LORA, MLA_HEADS, MLA_NOPE + MLA_V)
    uk, uv = ukv[..., :MLA_NOPE], ukv[..., MLA_NOPE:]
    w_ka = jnp.concatenate(
        [uk, jnp.zeros((MLA_KV_LORA, MLA_HEADS, LANES - MLA_NOPE), F32)], axis=-1).reshape(
            MLA_KV_LORA, MLA_HEADS * LANES)
    w_v = uv.reshape(MLA_KV_LORA, MLA_HEADS * MLA_V)
    return [w.astype(BF16) for w in (w_cq, w_ckv, w_kra, w_krb)], [w.astype(BF16) for w in (w_qa, w_qb, w_ka, w_v)]


def _rope_tables(seq):
    pos = np.arange(seq, dtype=np.float64)
    inv = 1.0 / (ROPE_THETA ** (np.arange(0, MLA_ROPE, 2, dtype=np.float64) / MLA_ROPE))
    pad_tail = LANES - MLA_NOPE - MLA_ROPE
    q_scale = (MLA_NOPE + MLA_ROPE) ** -0.5 * LOG2_E
    ang = pos[:, None] * inv[None, :]
    cos, sin = np.cos(ang), np.sin(ang)

    def tables(lead_one):
        c = np.concatenate([np.full((seq, MLA_NOPE), lead_one), cos, cos, np.zeros((seq, pad_tail))], axis=1)
        s = np.concatenate([np.zeros((seq, MLA_NOPE)), -sin, sin, np.zeros((seq, pad_tail))], axis=1)
        return c, s

    c_q, s_q = tables(1.0)
    c_k, s_k = tables(0.0)
    return tuple(np.ascontiguousarray(a, dtype=np.float32)
                 for a in ((c_q * q_scale).T, (s_q * q_scale).T, c_k, s_k))


def kernel(x, mem, w_in, mla_q_norm, mla_kv_norm, w_uq, w_ukv, w_mem_kv, w_br_mla, w_br_dil, w_br_mem, w_o,
           ffn1_w_gate, ffn1_w_up, ffn1_w_down, ffn2_w_gate, ffn2_w_up, ffn2_w_down,
           ln1_g, ln1_b, ln2_g, ln2_b, ln3_g, ln3_b):
    batch, seq, d = x.shape
    t = batch * seq
    tm_ffn = min(1024, t)
    tm_proj = min(512, seq)
    tm_merge = min(512, seq)
    tables = _rope_tables(seq)
    slopes = 2.0 ** (-8.0 * jnp.arange(1, DIL_HEADS + 1, dtype=F32) / DIL_HEADS)

    h = x.reshape(t, d)
    for l in range(DEPTH):
        bf = lambda w: w[l].astype(BF16)
        row = lambda v: v[l].reshape(1, -1)
        h = _ffn_ln(h, bf(ffn1_w_gate), bf(ffn1_w_up), bf(ffn1_w_down), row(ln1_g), row(ln1_b),
                    tm=tm_ffn, n_sub=FFN_SUB_TILES)

        w_in_l = w_in[l]
        o_dil_cols = MLA_Q_LORA + MLA_KV_LORA + MLA_ROPE
        o_memq = o_dil_cols + DIL_QKV
        o_gate = o_memq + MEM_OUT
        w_c, w_u = _prep_mla_weights(w_in_l, w_uq[l], w_ukv[l])
        proj_weights = w_c + [row(mla_q_norm), row(mla_kv_norm)] + w_u + [
            _cols_to_bf16(w_in_l, o_dil_cols, DIL_QKV)]
        q, k, v, *dil = _proj(h, proj_weights, tables, batch=batch, seq=seq, tm=tm_proj)

        o_mla = _mla_attn(q, k, v, tq=min(512, seq), tk=min(1024, seq)).reshape(t, MLA_OUT)
        dil_parts = [_dil_attn(dil[g], slopes, group=g, dilation=dl)
                     for g, (_, dl) in enumerate(DIL_PAIRS)]
        kv_mem = _mem_kv(mem, bf(w_mem_kv))

        merge_weights = [_cols_to_bf16(w_in_l, o_memq, MEM_OUT + 3 * d),
                         bf(w_br_mla), bf(w_br_dil), bf(w_br_mem), bf(w_o)]
        h = _merge(h, o_mla, [p[0] for p in dil_parts], [p[1] for p in dil_parts], kv_mem,
                   merge_weights, row(ln2_g), row(ln2_b), seq=seq, tm=tm_merge, n_sub=2)

        h = _ffn_ln(h, bf(ffn2_w_gate), bf(ffn2_w_up), bf(ffn2_w_down), row(ln3_g), row(ln3_b),
                    tm=tm_ffn, n_sub=FFN_SUB_TILES)
    return h.reshape(batch, seq, d)
```

```python
import functools
import math

import jax
import jax.numpy as jnp
import numpy as np
from jax import lax
from jax.experimental import pallas as pl
from jax.experimental.pallas import tpu as pltpu

F32 = jnp.float32
BF16 = jnp.bfloat16

D_MODEL = 1024
DEPTH = 1
MLA_HEADS = 8
MLA_Q_LORA = 256
MLA_KV_LORA = 256
MLA_NOPE = 64
MLA_ROPE = 32
MLA_V = 64
ROPE_THETA = 10000.0
DIL_PAIRS = ((128, 1), (512, 4), (2048, 16))
DIL_GROUPS = 3
DIL_HEADS_PER_GROUP = 4
DIL_HEAD_DIM = 128
DIL_HEADS = DIL_GROUPS * DIL_HEADS_PER_GROUP
MEM_HEADS = 4
MEM_HEAD_DIM = 128
EPS = 1e-5
ALPHA = (2 * DEPTH) ** 0.25

LOG2_E = math.log2(math.e)
LANES = 128
DIL_OUT = DIL_HEADS_PER_GROUP * DIL_HEAD_DIM
DIL_QKV = 3 * DIL_HEADS * DIL_HEAD_DIM
MEM_OUT = MEM_HEADS * MEM_HEAD_DIM
MLA_OUT = MLA_HEADS * MLA_V
MLA_DEN_ROWS = 16
MLA_PIECE = 256
MLA_UNITS_PER_TRIP = 16
DIL_HALF_SPAN = 64
DIL_HALO = 64
DIL_SUB = 128
DIL_CHUNK_TOKENS = 2048
DIL_ROWS_PER_STEP = 2048
FFN_SUB_TILES = 4
VMEM_LIMIT = 56 * 1024 * 1024

assert all(w // 2 // d == DIL_HALF_SPAN for w, d in DIL_PAIRS)

_NT = (((1,), (1,)), ((), ()))


def _dot(a, b):
    return jnp.dot(a, b, preferred_element_type=F32)


def _dot_nt(a, b):
    return lax.dot_general(a, b, _NT, preferred_element_type=F32)


def _dot_tt(w, x):
    return lax.dot_general(w, x, (((0,), (1,)), ((), ())), preferred_element_type=F32)


def _layer_norm(y, g, b):
    mu = jnp.mean(y, axis=-1, keepdims=True)
    yc = y - mu
    var = jnp.mean(yc * yc, axis=-1, keepdims=True)
    return yc * lax.rsqrt(var + EPS) * g + b


def _rms_norm(y, g):
    return y * lax.rsqrt(jnp.mean(y * y, axis=-1, keepdims=True) + EPS) * g


def _ffn_ln_body(x_ref, wg_ref, wu_ref, wd_ref, g_ref, b_ref, o_ref, *, n_sub):
    sub_rows = x_ref.shape[0] // n_sub
    for sub in range(n_sub):
        rows = slice(sub * sub_rows, (sub + 1) * sub_rows)
        x = x_ref[rows, :]
        xb = x.astype(BF16)
        gate = _dot(xb, wg_ref[...])
        up = _dot(xb, wu_ref[...])
        act = (gate * jax.nn.sigmoid(gate) * up).astype(BF16)
        y = ALPHA * x + 0.5 * _dot(act, wd_ref[...])
        o_ref[rows, :] = _layer_norm(y, g_ref[...], b_ref[...])


def _ffn_ln(x2d, wg, wu, wd, g, b, *, tm, n_sub):
    t, d = x2d.shape

    def full(a):
        return pl.BlockSpec(a.shape, lambda i: (0,) * a.ndim, pipeline_mode=pl.Buffered(1))

    return pl.pallas_call(
        functools.partial(_ffn_ln_body, n_sub=n_sub),
        grid=(t // tm,),
        in_specs=[pl.BlockSpec((tm, d), lambda i: (i, 0))] + [full(a) for a in (wg, wu, wd, g, b)],
        out_specs=pl.BlockSpec((tm, d), lambda i: (i, 0)),
        out_shape=jax.ShapeDtypeStruct((t, d), F32),
        compiler_params=pltpu.CompilerParams(
            dimension_semantics=("parallel",), vmem_limit_bytes=VMEM_LIMIT),
        name="ffn_ln",
    )(x2d, wg, wu, wd, g, b)


def _proj_body(*refs):
    n_hc = D_MODEL // LANES
    h_refs = refs[:n_hc]
    (wcq_ref, wckv_ref, wkra_ref, wkrb_ref, gq_ref, gkv_ref,
     wqa_ref, wqb_ref, wka_ref, wv_ref, wdil_ref,
     cq_ref, sq_ref, ck_ref, sk_ref,
     q_ref, k_ref, v_ref, dil0_ref, dil1_ref, dil2_ref, hperm_ref) = refs[n_hc:]
    hb = jnp.concatenate([hc[...] for hc in h_refs], axis=1).astype(BF16)
    cqn = _rms_norm(_dot(hb, wcq_ref[...]), gq_ref[...]).astype(BF16)
    ckvn = _rms_norm(_dot(hb, wckv_ref[...]), gkv_ref[...]).astype(BF16)
    k_rope = _dot(hb, wkra_ref[...]) * ck_ref[...] + _dot(hb, wkrb_ref[...]) * sk_ref[...]
    ka = _dot(ckvn, wka_ref[...])
    for h in range(MLA_HEADS):
        sl = slice(h * LANES, (h + 1) * LANES)
        k_ref[0, h] = (ka[:, sl] + k_rope).astype(BF16)
    qa_t = _dot_tt(wqa_ref[...], cqn)
    qb_t = _dot_tt(wqb_ref[...], cqn)
    cq_t = cq_ref[...]
    sq_t = sq_ref[...]
    for h in range(MLA_HEADS):
        sl = slice(h * LANES, (h + 1) * LANES)
        q_ref[0, sl] = (qa_t[sl] * cq_t + qb_t[sl] * sq_t).astype(BF16)
    v_ref[0] = _dot_tt(wv_ref[...], ckvn).astype(BF16)

    dil_scale = DIL_HEAD_DIM ** -0.5
    tm = h_refs[0].shape[0]
    for g, (dil_ref, (_, dilation)) in enumerate(zip((dil0_ref, dil1_ref, dil2_ref), DIL_PAIRS)):
        rows_per = tm // dilation
        if dilation == 1:
            hp = hb
        else:
            for r in range(dilation):
                for c, hc in enumerate(h_refs):
                    hperm_ref[r * rows_per:(r + 1) * rows_per, c * LANES:(c + 1) * LANES] = hc[
                        pl.ds(r, rows_per, stride=dilation), :].astype(BF16)
            hp = hperm_ref[...]
        for part in range(3):
            c0 = (part * DIL_GROUPS + g) * DIL_OUT
            res = _dot(hp, wdil_ref[:, c0:c0 + DIL_OUT])
            if part == 0:
                res = res * dil_scale
            res = res.astype(BF16)
            for r in range(dilation):
                dil_ref[0, r, :, part * DIL_OUT:(part + 1) * DIL_OUT] = res[r * rows_per:(r + 1) * rows_per]


def _proj(h2d, weights, tables, *, batch, seq, tm):
    t, d = h2d.shape
    n_s = seq // tm

    def full(a):
        return pl.BlockSpec(a.shape, lambda i: (0,) * a.ndim)

    tab_spec = pl.BlockSpec((tm, LANES), lambda i: (i % n_s, 0))
    tab_t_spec = pl.BlockSpec((LANES, tm), lambda i: (0, i % n_s))
    head_map = lambda i: (i // n_s, 0, i % n_s, 0)
    feat_map = lambda i: (i // n_s, 0, i % n_s)
    dil_specs = [pl.BlockSpec((1, dl, tm // dl, 3 * DIL_OUT), head_map) for _, dl in DIL_PAIRS]
    dil_shapes = [jax.ShapeDtypeStruct((batch, dl, seq // dl, 3 * DIL_OUT), BF16) for _, dl in DIL_PAIRS]
    return pl.pallas_call(
        _proj_body,
        grid=(t // tm,),
        in_specs=[pl.BlockSpec((tm, LANES), functools.partial(lambda c, i: (i, c), c))
                  for c in range(d // LANES)]
        + [full(w) for w in weights] + [tab_t_spec] * 2 + [tab_spec] * 2,
        out_specs=[
            pl.BlockSpec((1, MLA_HEADS * LANES, tm), feat_map),
            pl.BlockSpec((1, MLA_HEADS, tm, LANES), head_map),
            pl.BlockSpec((1, MLA_OUT, tm), feat_map),
        ] + dil_specs,
        out_shape=[
            jax.ShapeDtypeStruct((batch, MLA_HEADS * LANES, seq), BF16),
            jax.ShapeDtypeStruct((batch, MLA_HEADS, seq, LANES), BF16),
            jax.ShapeDtypeStruct((batch, MLA_OUT, seq), BF16),
        ] + dil_shapes,
        scratch_shapes=[pltpu.VMEM((tm, d), BF16)],
        compiler_params=pltpu.CompilerParams(
            dimension_semantics=("parallel",), vmem_limit_bytes=VMEM_LIMIT),
        name="proj",
    )(*([h2d] * (d // LANES)), *weights, *tables)


def _mla_attn_body(q_ref, k_ref, vt_ref, o_ref, st_ref, m_ref, acc_ref, *, tq, tk):
    seq = k_ref.shape[2]
    n_tiles = seq // tq
    n_units = (seq // tk) * n_tiles
    heads = (0, 1)
    n_pieces = tk // MLA_PIECE

    def offsets(unit):
        if isinstance(unit, int):
            return (unit // n_tiles) * tk, (unit % n_tiles) * tq
        return (pl.multiple_of(lax.div(unit, n_tiles) * tk, tk),
                pl.multiple_of(lax.rem(unit, n_tiles) * tq, tq))

    def scores_piece(unit, buf, p, maxes):
        k_off, q_off = offsets(unit)
        rows = slice(p * MLA_PIECE, (p + 1) * MLA_PIECE)
        out = []
        for h in heads:
            st = _dot(k_ref[0, h, pl.ds(k_off + p * MLA_PIECE, MLA_PIECE), :],
                      q_ref[0, h * LANES:(h + 1) * LANES, pl.ds(q_off, tq)])
            st_ref[buf, h, rows] = st
            out.append(jnp.maximum(maxes[h], jnp.max(st, axis=0, keepdims=True)))
        return tuple(out)

    ones_rows = jnp.ones((MLA_DEN_ROWS, MLA_PIECE), BF16)

    def values_piece(k_off, buf, p, m_new, accs):
        rows = slice(p * MLA_PIECE, (p + 1) * MLA_PIECE)
        out = []
        for h in heads:
            pt = jnp.exp2(st_ref[buf, h, rows] - m_new[h]).astype(BF16)
            vtc = jnp.concatenate(
                [vt_ref[0, h * MLA_V:(h + 1) * MLA_V, pl.ds(k_off + p * MLA_PIECE, MLA_PIECE)], ones_rows],
                axis=0)
            out.append(accs[h] + _dot(vtc, pt))
        return tuple(out)

    neg_inf = jnp.full((1, tq), -jnp.inf, F32)

    def half(next_unit, next_buf, unit, buf, unit_max):
        k_off, q_off = offsets(unit)
        cols = pl.ds(q_off, tq)
        m_old = tuple(m_ref[h, :, cols] for h in heads)
        m_new = tuple(jnp.maximum(m_old[h], unit_max[h]) for h in heads)
        accs = tuple(jnp.exp2(m_old[h] - m_new[h]) * acc_ref[h, :, cols] for h in heads)
        next_max = (neg_inf, neg_inf)
        for p in range(n_pieces):
            if next_unit is not None:
                next_max = scores_piece(next_unit, next_buf, p, next_max)
            accs = values_piece(k_off, buf, p, m_new, accs)
        for h in heads:
            m_ref[h, :, cols] = m_new[h]
            acc_ref[h, :, cols] = accs[h]
        return next_max

    def group_step(jj, unit_max):
        for s in range(MLA_UNITS_PER_TRIP):
            u = MLA_UNITS_PER_TRIP * jj + s
            unit_max = half(u + 1, (s + 1) % 2, u, s % 2, unit_max)
        return unit_max

    m_ref[...] = jnp.full(m_ref.shape, -jnp.inf, F32)
    acc_ref[...] = jnp.zeros(acc_ref.shape, F32)
    unit_max = (neg_inf, neg_inf)
    for p in range(n_pieces):
        unit_max = scores_piece(0, 0, p, unit_max)
    unit_max = lax.fori_loop(0, n_units // MLA_UNITS_PER_TRIP - 1, group_step, unit_max)
    for u in range(n_units - MLA_UNITS_PER_TRIP, n_units):
        last = u == n_units - 1
        unit_max = half(None if last else u + 1, None if last else (u + 1) % 2, u, u % 2, unit_max)

    @pl.loop(0, n_tiles)
    def _(tile):
        cols = pl.ds(pl.multiple_of(tile * tq, tq), tq)
        out_t = jnp.concatenate(
            [acc_ref[h, :MLA_V, cols] / acc_ref[h, MLA_V:MLA_V + 1, cols] for h in heads], axis=0)
        o_ref[0, cols, :] = out_t.T.astype(o_ref.dtype)


def _mla_attn(q, k, v, *, tq, tk):
    batch, heads, seq, _ = k.shape
    assert MLA_UNITS_PER_TRIP % 2 == 0 and ((seq // tk) * (seq // tq)) % MLA_UNITS_PER_TRIP == 0
    return pl.pallas_call(
        functools.partial(_mla_attn_body, tq=tq, tk=tk),
        grid=(batch, heads // 2),
        in_specs=[
            pl.BlockSpec((1, 2 * LANES, seq), lambda b, hp: (b, hp, 0)),
            pl.BlockSpec((1, 2, seq, LANES), lambda b, hp: (b, hp, 0, 0)),
            pl.BlockSpec((1, 2 * MLA_V, seq), lambda b, hp: (b, hp, 0)),
        ],
        out_specs=pl.BlockSpec((1, seq, LANES), lambda b, hp: (b, 0, hp)),
        out_shape=jax.ShapeDtypeStruct((batch, seq, MLA_OUT), BF16),
        scratch_shapes=[pltpu.VMEM((2, 2, tk, tq), F32),
                        pltpu.VMEM((2, 1, seq), F32),
                        pltpu.VMEM((2, MLA_V + MLA_DEN_ROWS, seq), F32)],
        compiler_params=pltpu.CompilerParams(
            dimension_semantics=("parallel", "parallel"), vmem_limit_bytes=VMEM_LIMIT),
        name="mla_attn",
    )(q, k, v)


def _dil_attn_body(slopes_ref, q_ref, kl_ref, km_ref, kr_ref, vl_ref, vm_ref, vr_ref,
                   o_ref, lse_ref, *, group, dilation, tq, sub_len):
    i = pl.program_id(1)
    n_res = q_ref.shape[1]
    first_res = pl.program_id(2) * n_res
    n_sub = tq // DIL_SUB

    def window(left_ref, main_ref, right_ref, rr, sub, cols):
        lo, hi = sub * DIL_SUB - DIL_HALO, (sub + 1) * DIL_SUB + DIL_HALO
        parts = []
        if lo < 0:
            parts.append(left_ref[0, rr, :, cols])
        parts.append(main_ref[0, rr, max(lo, 0):min(hi, tq), cols])
        if hi > tq:
            parts.append(right_ref[0, rr, :, cols])
        return parts[0] if len(parts) == 1 else jnp.concatenate(parts, axis=0)

    win = DIL_SUB + 2 * DIL_HALO
    row = lax.broadcasted_iota(jnp.int32, (DIL_SUB, win), 0)
    col = lax.broadcasted_iota(jnp.int32, (DIL_SUB, win), 1)
    steps = jnp.abs(col - DIL_HALO - row)
    in_band = steps <= DIL_HALF_SPAN
    dist = (steps * dilation).astype(F32)
    for sub in range(n_sub):
        key_pos = i * tq + (sub * DIL_SUB - DIL_HALO) + col
        valid = in_band & (key_pos >= 0) & (key_pos < sub_len)
        rows = slice(sub * DIL_SUB, (sub + 1) * DIL_SUB)
        for rr in range(n_res):
            if dilation == 1:
                out_rows = pl.ds(sub * DIL_SUB, DIL_SUB)
            else:
                out_rows = pl.ds(sub * DIL_SUB * dilation + first_res + rr, DIL_SUB, stride=dilation)
            for h in range(DIL_HEADS_PER_GROUP):
                cols = slice(h * DIL_HEAD_DIM, (h + 1) * DIL_HEAD_DIM)
                slope = slopes_ref[group * DIL_HEADS_PER_GROUP + h]
                s = _dot_nt(q_ref[0, rr, rows, cols], window(kl_ref, km_ref, kr_ref, rr, sub, cols)) - slope * dist
                s = jnp.where(valid, s, -jnp.inf)
                m = jnp.max(s, axis=1, keepdims=True)
                e = jnp.exp(s - m)
                den = jnp.sum(e, axis=1, keepdims=True)
                o_ref[0, h, out_rows, :] = _dot(
                    e.astype(BF16), window(vl_ref, vm_ref, vr_ref, rr, sub, cols)) / den
                lse_ref[0, h, out_rows, :] = jnp.broadcast_to(m + jnp.log(den), (DIL_SUB, DIL_HEAD_DIM))


def _dil_attn(dil_g, slopes, *, group, dilation):
    batch, _, sub_len, _ = dil_g.shape
    seq = sub_len * dilation
    tq = min(DIL_ROWS_PER_STEP, sub_len, DIL_CHUNK_TOKENS // dilation)
    n_q = sub_len // tq
    n_res = min(dilation, DIL_ROWS_PER_STEP // tq)
    halo_per_tile = tq // DIL_HALO
    n_halo = sub_len // DIL_HALO
    q_col, k_col, v_col = 0, 1, 2

    def main(c):
        return pl.BlockSpec((1, n_res, tq, DIL_OUT), lambda b, i, r: (b, r, i, c))

    def left(c):
        return pl.BlockSpec(
            (1, n_res, DIL_HALO, DIL_OUT),
            lambda b, i, r: (b, r, jnp.maximum(i * halo_per_tile - 1, 0), c))

    def right(c):
        return pl.BlockSpec(
            (1, n_res, DIL_HALO, DIL_OUT),
            lambda b, i, r: (b, r, jnp.minimum((i + 1) * halo_per_tile, n_halo - 1), c))

    out_spec = pl.BlockSpec((1, DIL_HEADS_PER_GROUP, tq * dilation, DIL_HEAD_DIM),
                            lambda b, i, r: (b, 0, i, 0))
    out_shape = jax.ShapeDtypeStruct((batch, DIL_HEADS_PER_GROUP, seq, DIL_HEAD_DIM), F32)
    o, lse = pl.pallas_call(
        functools.partial(_dil_attn_body, group=group, dilation=dilation, tq=tq, sub_len=sub_len),
        grid=(batch, n_q, dilation // n_res),
        in_specs=[pl.BlockSpec(memory_space=pltpu.SMEM),
                  main(q_col), left(k_col), main(k_col), right(k_col),
                  left(v_col), main(v_col), right(v_col)],
        out_specs=[out_spec, out_spec],
        out_shape=[out_shape, out_shape],
        compiler_params=pltpu.CompilerParams(
            dimension_semantics=("parallel", "parallel", "arbitrary"), vmem_limit_bytes=VMEM_LIMIT),
        name=f"dil_attn_g{group}",
    )(slopes, dil_g, dil_g, dil_g, dil_g, dil_g, dil_g, dil_g)
    return o, lse


def _mem_kv_body(mem_ref, w_ref, o_ref):
    o_ref[0] = _dot(mem_ref[0].astype(BF16), w_ref[...]).astype(BF16)


def _mem_kv(mem, w):
    batch, m_len, d = mem.shape
    n = w.shape[1]
    return pl.pallas_call(
        _mem_kv_body,
        grid=(batch,),
        in_specs=[pl.BlockSpec((1, m_len, d), lambda b: (b, 0, 0)),
                  pl.BlockSpec((d, n), lambda b: (0, 0))],
        out_specs=pl.BlockSpec((1, m_len, n), lambda b: (b, 0, 0)),
        out_shape=jax.ShapeDtypeStruct((batch, m_len, n), BF16),
        compiler_params=pltpu.CompilerParams(dimension_semantics=("parallel",)),
        name="mem_kv",
    )(mem, w)


def _merge_body(h_ref, omla_ref, od0_ref, od1_ref, od2_ref, ls0_ref, ls1_ref, ls2_ref, kvm_ref,
                wgate_ref, wmq_ref, wba_ref, wbb_ref, wbc_ref, wo_ref, g_ref, b_ref, out_ref, *, n_sub):
    sub_rows = h_ref.shape[0] // n_sub
    for sub in range(n_sub):
        rows = slice(sub * sub_rows, (sub + 1) * sub_rows)
        _merge_rows(rows, h_ref, omla_ref, od0_ref, od1_ref, od2_ref, ls0_ref, ls1_ref, ls2_ref, kvm_ref,
                    wgate_ref, wmq_ref, wba_ref, wbb_ref, wbc_ref, wo_ref, g_ref, b_ref, out_ref)


def _merge_rows(rows, h_ref, omla_ref, od0_ref, od1_ref, od2_ref, ls0_ref, ls1_ref, ls2_ref, kvm_ref,
                wgate_ref, wmq_ref, wba_ref, wbb_ref, wbc_ref, wo_ref, g_ref, b_ref, out_ref):
    h = h_ref[rows, :]
    hb = h.astype(BF16)

    def heads_to_lanes(ref):
        return jnp.concatenate([ref[0, hh, rows, :] for hh in range(DIL_HEADS_PER_GROUP)], axis=1)

    ls0, ls1, ls2 = heads_to_lanes(ls0_ref), heads_to_lanes(ls1_ref), heads_to_lanes(ls2_ref)
    mx = jnp.maximum(ls0, jnp.maximum(ls1, ls2))
    e0, e1, e2 = jnp.exp(ls0 - mx), jnp.exp(ls1 - mx), jnp.exp(ls2 - mx)
    o_dil = (e0 * heads_to_lanes(od0_ref) + e1 * heads_to_lanes(od1_ref)
             + e2 * heads_to_lanes(od2_ref)) / (e0 + e1 + e2)

    mq = (_dot(hb, wmq_ref[...]) * (MEM_HEAD_DIM ** -0.5)).astype(BF16)
    heads = []
    for hh in range(MEM_HEADS):
        kc = slice(hh * MEM_HEAD_DIM, (hh + 1) * MEM_HEAD_DIM)
        vc = slice(MEM_OUT + hh * MEM_HEAD_DIM, MEM_OUT + (hh + 1) * MEM_HEAD_DIM)
        s = _dot_nt(mq[:, kc], kvm_ref[0, :, kc])
        p = jnp.exp(s - jnp.max(s, axis=1, keepdims=True))
        heads.append(_dot(p.astype(BF16), kvm_ref[0, :, vc]) / jnp.sum(p, axis=1, keepdims=True))
    o_mem = jnp.concatenate(heads, axis=1)

    y_a = _dot(omla_ref[rows, :], wba_ref[...])
    y_b = _dot(o_dil.astype(BF16), wbb_ref[...])
    y_c = _dot(o_mem.astype(BF16), wbc_ref[...])
    d = h.shape[1]
    merged = (jax.nn.sigmoid(_dot(hb, wgate_ref[:, :d])) * y_a
              + jax.nn.sigmoid(_dot(hb, wgate_ref[:, d:2 * d])) * y_b
              + jax.nn.sigmoid(_dot(hb, wgate_ref[:, 2 * d:])) * y_c)
    mix = _dot(merged.astype(BF16), wo_ref[...])
    out_ref[rows, :] = _layer_norm(ALPHA * h + mix, g_ref[...], b_ref[...])


def _merge(h2d, o_mla, o_dil, lse_dil, kv_mem, weights, g, b, *, seq, tm, n_sub):
    t, d = h2d.shape
    n_s = seq // tm

    def rows(width):
        return pl.BlockSpec((tm, width), lambda i: (i, 0))

    def full(a):
        return pl.BlockSpec(a.shape, lambda i: (0,) * a.ndim, pipeline_mode=pl.Buffered(1))

    m_len, kv_cols = kv_mem.shape[1:]
    return pl.pallas_call(
        functools.partial(_merge_body, n_sub=n_sub),
        grid=(t // tm,),
        in_specs=[rows(d), rows(MLA_OUT)]
        + [pl.BlockSpec((1, DIL_HEADS_PER_GROUP, tm, DIL_HEAD_DIM),
                        lambda i: (i // n_s, 0, i % n_s, 0))] * 6
        + [pl.BlockSpec((1, m_len, kv_cols), lambda i: (i // n_s, 0, 0))]
        + [full(w) for w in weights] + [full(g), full(b)],
        out_specs=rows(d),
        out_shape=jax.ShapeDtypeStruct((t, d), F32),
        compiler_params=pltpu.CompilerParams(
            dimension_semantics=("parallel",), vmem_limit_bytes=VMEM_LIMIT),
        name="merge",
    )(h2d, o_mla, *o_dil, *lse_dil, kv_mem, *weights, g, b)


def _swap_halves(w):
    half = w.shape[-1] // 2
    return jnp.concatenate([w[..., half:], w[..., :half]], axis=-1)


def _prep_mla_weights(w_in, w_uq, w_ukv):
    d = w_in.shape[0]
    o_q, o_kv, o_kr = 0, MLA_Q_LORA, MLA_Q_LORA + MLA_KV_LORA
    w_cq = w_in[:, o_q:o_q + MLA_Q_LORA]
    w_ckv = w_in[:, o_kv:o_kv + MLA_KV_LORA]
    w_kr = w_in[:, o_kr:o_kr + MLA_ROPE]
    pad_tail = LANES - MLA_NOPE - MLA_ROPE
    z_nope = jnp.zeros((d, MLA_NOPE), F32)
    z_tail = jnp.zeros((d, pad_tail), F32)
    w_kra = jnp.concatenate([z_nope, w_kr, z_tail], axis=1)
    w_krb = jnp.concatenate([z_nope, _swap_halves(w_kr), z_tail], axis=1)

    uq = w_uq.reshape(MLA_Q_LORA, MLA_HEADS, MLA_NOPE + MLA_ROPE)
    uq_nope, uq_pe = uq[..., :MLA_NOPE], uq[..., MLA_NOPE:]
    zq_nope = jnp.zeros_like(uq_nope)
    zq_tail = jnp.zeros((MLA_Q_LORA, MLA_HEADS, pad_tail), F32)
    w_qa = jnp.concatenate([uq_nope, uq_pe, zq_tail], axis=-1).reshape(MLA_Q_LORA, MLA_HEADS * LANES)
    w_qb = jnp.concatenate([zq_nope, _swap_halves(uq_pe), zq_tail], axis=-1).reshape(
        MLA_Q_LORA, MLA_HEADS * LANES)

    ukv = w_ukv.reshape(MLA_KV_LORA, MLA_HEADS, MLA_NOPE + MLA_V)
    uk, uv = ukv[..., :MLA_NOPE], ukv[..., MLA_NOPE:]
    w_ka = jnp.concatenate(
        [uk, jnp.zeros((MLA_KV_LORA, MLA_HEADS, LANES - MLA_NOPE), F32)], axis=-1).reshape(
            MLA_KV_LORA, MLA_HEADS * LANES)
    w_v = uv.reshape(MLA_KV_LORA, MLA_HEADS * MLA_V)
    return [w.astype(BF16) for w in (w_cq, w_ckv, w_kra, w_krb)], [w.astype(BF16) for w in (w_qa, w_qb, w_ka, w_v)]


def _rope_tables(seq):
    pos = np.arange(seq, dtype=np.float64)
    inv = 1.0 / (ROPE_THETA ** (np.arange(0, MLA_ROPE, 2, dtype=np.float64) / MLA_ROPE))
    pad_tail = LANES - MLA_NOPE - MLA_ROPE
    q_scale = (MLA_NOPE + MLA_ROPE) ** -0.5 * LOG2_E
    ang = pos[:, None] * inv[None, :]
    cos, sin = np.cos(ang), np.sin(ang)

    def tables(lead_one):
        c = np.concatenate([np.full((seq, MLA_NOPE), lead_one), cos, cos, np.zeros((seq, pad_tail))], axis=1)
        s = np.concatenate([np.zeros((seq, MLA_NOPE)), -sin, sin, np.zeros((seq, pad_tail))], axis=1)
        return c, s

    c_q, s_q = tables(1.0)
    c_k, s_k = tables(0.0)
    return tuple(np.ascontiguousarray(a, dtype=np.float32)
                 for a in ((c_q * q_scale).T, (s_q * q_scale).T, c_k, s_k))


def kernel(x, mem, w_in, mla_q_norm, mla_kv_norm, w_uq, w_ukv, w_mem_kv, w_br_mla, w_br_dil, w_br_mem, w_o,
           ffn1_w_gate, ffn1_w_up, ffn1_w_down, ffn2_w_gate, ffn2_w_up, ffn2_w_down,
           ln1_g, ln1_b, ln2_g, ln2_b, ln3_g, ln3_b):
    batch, seq, d = x.shape
    t = batch * seq
    tm_ffn = min(1024, t)
    tm_proj = min(512, seq)
    tm_merge = min(512, seq)
    tables = _rope_tables(seq)
    slopes = 2.0 ** (-8.0 * jnp.arange(1, DIL_HEADS + 1, dtype=F32) / DIL_HEADS)

    h = x.reshape(t, d)
    for l in range(DEPTH):
        bf = lambda w: w[l].astype(BF16)
        row = lambda v: v[l].reshape(1, -1)
        h = _ffn_ln(h, bf(ffn1_w_gate), bf(ffn1_w_up), bf(ffn1_w_down), row(ln1_g), row(ln1_b),
                    tm=tm_ffn, n_sub=FFN_SUB_TILES)

        w_in_l = w_in[l]
        o_dil_cols = MLA_Q_LORA + MLA_KV_LORA + MLA_ROPE
        o_memq = o_dil_cols + DIL_QKV
        o_gate = o_memq + MEM_OUT
        w_c, w_u = _prep_mla_weights(w_in_l, w_uq[l], w_ukv[l])
        proj_weights = w_c + [row(mla_q_norm), row(mla_kv_norm)] + w_u + [
            w_in_l[:, o_dil_cols:o_memq].astype(BF16)]
        q, k, v, *dil = _proj(h, proj_weights, tables, batch=batch, seq=seq, tm=tm_proj)

        o_mla = _mla_attn(q, k, v, tq=min(512, seq), tk=min(1024, seq)).reshape(t, MLA_OUT)
        dil_parts = [_dil_attn(dil[g], slopes, group=g, dilation=dl)
                     for g, (_, dl) in enumerate(DIL_PAIRS)]
        kv_mem = _mem_kv(mem, bf(w_mem_kv))

        merge_weights = [w_in_l[:, o_gate:].astype(BF16), w_in_l[:, o_memq:o_gate].astype(BF16),
                         bf(w_br_mla), bf(w_br_dil), bf(w_br_mem), bf(w_o)]
        h = _merge(h, o_mla, [p[0] for p in dil_parts], [p[1] for p in dil_parts], kv_mem,
                   merge_weights, row(ln2_g), row(ln2_b), seq=seq, tm=tm_merge, n_sub=2)

        h = _ffn_ln(h, bf(ffn2_w_gate), bf(ffn2_w_up), bf(ffn2_w_down), row(ln3_g), row(ln3_b),
                    tm=tm_ffn, n_sub=FFN_SUB_TILES)
    return h.reshape(batch, seq, d)
```

```python
import functools
import math

import jax
import jax.numpy as jnp
import numpy as np
from jax import lax
from jax.experimental import pallas as pl
from jax.experimental.pallas import tpu as pltpu

F32 = jnp.float32
BF16 = jnp.bfloat16

D_MODEL = 1024
DEPTH = 1
MLA_HEADS = 8
MLA_Q_LORA = 256
MLA_KV_LORA = 256
MLA_NOPE = 64
MLA_ROPE = 32
MLA_V = 64
ROPE_THETA = 10000.0
DIL_PAIRS = ((128, 1), (512, 4), (2048, 16))
DIL_GROUPS = 3
DIL_HEADS_PER_GROUP = 4
DIL_HEAD_DIM = 128
DIL_HEADS = DIL_GROUPS * DIL_HEADS_PER_GROUP
MEM_HEADS = 4
MEM_HEAD_DIM = 128
EPS = 1e-5
ALPHA = (2 * DEPTH) ** 0.25

LOG2_E = math.log2(math.e)
LN_2 = math.log(2.0)
LANES = 128
DIL_OUT = DIL_HEADS_PER_GROUP * DIL_HEAD_DIM
DIL_QKV = 3 * DIL_HEADS * DIL_HEAD_DIM
MEM_OUT = MEM_HEADS * MEM_HEAD_DIM
MLA_OUT = MLA_HEADS * MLA_V
MLA_DEN_ROWS = 16
MLA_PIECE = 256
MLA_UNITS_PER_TRIP = 16
DIL_HALF_SPAN = 64
DIL_HALO = 64
DIL_SUB = 128
DIL_CHUNK_TOKENS = 2048
DIL_ROWS_PER_STEP = 2048
FFN_SUB_TILES = 4
VMEM_LIMIT = 56 * 1024 * 1024

assert all(w // 2 // d == DIL_HALF_SPAN for w, d in DIL_PAIRS)

_NT = (((1,), (1,)), ((), ()))


def _dot(a, b):
    return jnp.dot(a, b, preferred_element_type=F32)


def _dot_nt(a, b):
    return lax.dot_general(a, b, _NT, preferred_element_type=F32)


def _dot_tt(w, x):
    return lax.dot_general(w, x, (((0,), (1,)), ((), ())), preferred_element_type=F32)


def _layer_norm(y, g, b):
    mu = jnp.mean(y, axis=-1, keepdims=True)
    yc = y - mu
    var = jnp.mean(yc * yc, axis=-1, keepdims=True)
    return yc * lax.rsqrt(var + EPS) * g + b


def _rms_norm(y, g):
    return y * lax.rsqrt(jnp.mean(y * y, axis=-1, keepdims=True) + EPS) * g


def _ffn_ln_body(x_ref, wg_ref, wu_ref, wd_ref, g_ref, b_ref, o_ref, *, n_sub):
    sub_rows = x_ref.shape[0] // n_sub
    for sub in range(n_sub):
        rows = slice(sub * sub_rows, (sub + 1) * sub_rows)
        x = x_ref[rows, :]
        xb = x.astype(BF16)
        gate = _dot(xb, wg_ref[...])
        up = _dot(xb, wu_ref[...])
        act = (gate * jax.nn.sigmoid(gate) * up).astype(BF16)
        y = ALPHA * x + 0.5 * _dot(act, wd_ref[...])
        o_ref[rows, :] = _layer_norm(y, g_ref[...], b_ref[...])


def _ffn_ln(x2d, wg, wu, wd, g, b, *, tm, n_sub):
    t, d = x2d.shape

    def full(a):
        return pl.BlockSpec(a.shape, lambda i: (0,) * a.ndim, pipeline_mode=pl.Buffered(1))

    return pl.pallas_call(
        functools.partial(_ffn_ln_body, n_sub=n_sub),
        grid=(t // tm,),
        in_specs=[pl.BlockSpec((tm, d), lambda i: (i, 0))] + [full(a) for a in (wg, wu, wd, g, b)],
        out_specs=pl.BlockSpec((tm, d), lambda i: (i, 0)),
        out_shape=jax.ShapeDtypeStruct((t, d), F32),
        compiler_params=pltpu.CompilerParams(
            dimension_semantics=("parallel",), vmem_limit_bytes=VMEM_LIMIT),
        name="ffn_ln",
    )(x2d, wg, wu, wd, g, b)


def _proj_body(*refs):
    n_hc = D_MODEL // LANES
    h_refs = refs[:n_hc]
    (wcq_ref, wckv_ref, wkra_ref, wkrb_ref, gq_ref, gkv_ref,
     wqa_ref, wqb_ref, wka_ref, wv_ref, wdil_ref,
     cq_ref, sq_ref, ck_ref, sk_ref,
     q_ref, k_ref, v_ref, dil0_ref, dil1_ref, dil2_ref, hperm_ref) = refs[n_hc:]
    hb = jnp.concatenate([hc[...] for hc in h_refs], axis=1).astype(BF16)
    cqn = _rms_norm(_dot(hb, wcq_ref[...]), gq_ref[...]).astype(BF16)
    ckvn = _rms_norm(_dot(hb, wckv_ref[...]), gkv_ref[...]).astype(BF16)
    k_rope = _dot(hb, wkra_ref[...]) * ck_ref[...] + _dot(hb, wkrb_ref[...]) * sk_ref[...]
    ka = _dot(ckvn, wka_ref[...])
    for h in range(MLA_HEADS):
        sl = slice(h * LANES, (h + 1) * LANES)
        k_ref[0, h] = (ka[:, sl] + k_rope).astype(BF16)
    qa_t = _dot_tt(wqa_ref[...], cqn)
    qb_t = _dot_tt(wqb_ref[...], cqn)
    cq_t = cq_ref[...]
    sq_t = sq_ref[...]
    for h in range(MLA_HEADS):
        sl = slice(h * LANES, (h + 1) * LANES)
        q_ref[0, sl] = (qa_t[sl] * cq_t + qb_t[sl] * sq_t).astype(BF16)
    v_ref[0] = _dot_tt(wv_ref[...], ckvn).astype(BF16)

    dil_scale = DIL_HEAD_DIM ** -0.5 * LOG2_E
    tm = h_refs[0].shape[0]
    for g, (dil_ref, (_, dilation)) in enumerate(zip((dil0_ref, dil1_ref, dil2_ref), DIL_PAIRS)):
        rows_per = tm // dilation
        if dilation == 1:
            hp = hb
        else:
            for r in range(dilation):
                for c, hc in enumerate(h_refs):
                    hperm_ref[r * rows_per:(r + 1) * rows_per, c * LANES:(c + 1) * LANES] = hc[
                        pl.ds(r, rows_per, stride=dilation), :].astype(BF16)
            hp = hperm_ref[...]
        for part in range(3):
            c0 = (part * DIL_GROUPS + g) * DIL_OUT
            res = _dot(hp, wdil_ref[:, c0:c0 + DIL_OUT])
            if part == 0:
                res = res * dil_scale
            res = res.astype(BF16)
            for r in range(dilation):
                dil_ref[0, r, :, part * DIL_OUT:(part + 1) * DIL_OUT] = res[r * rows_per:(r + 1) * rows_per]


def _proj(h2d, weights, tables, *, batch, seq, tm):
    t, d = h2d.shape
    n_s = seq // tm

    def full(a):
        return pl.BlockSpec(a.shape, lambda i: (0,) * a.ndim)

    tab_spec = pl.BlockSpec((tm, LANES), lambda i: (i % n_s, 0))
    tab_t_spec = pl.BlockSpec((LANES, tm), lambda i: (0, i % n_s))
    head_map = lambda i: (i // n_s, 0, i % n_s, 0)
    feat_map = lambda i: (i // n_s, 0, i % n_s)
    dil_specs = [pl.BlockSpec((1, dl, tm // dl, 3 * DIL_OUT), head_map) for _, dl in DIL_PAIRS]
    dil_shapes = [jax.ShapeDtypeStruct((batch, dl, seq // dl, 3 * DIL_OUT), BF16) for _, dl in DIL_PAIRS]
    return pl.pallas_call(
        _proj_body,
        grid=(t // tm,),
        in_specs=[pl.BlockSpec((tm, LANES), functools.partial(lambda c, i: (i, c), c))
                  for c in range(d // LANES)]
        + [full(w) for w in weights] + [tab_t_spec] * 2 + [tab_spec] * 2,
        out_specs=[
            pl.BlockSpec((1, MLA_HEADS * LANES, tm), feat_map),
            pl.BlockSpec((1, MLA_HEADS, tm, LANES), head_map),
            pl.BlockSpec((1, MLA_OUT, tm), feat_map),
        ] + dil_specs,
        out_shape=[
            jax.ShapeDtypeStruct((batch, MLA_HEADS * LANES, seq), BF16),
            jax.ShapeDtypeStruct((batch, MLA_HEADS, seq, LANES), BF16),
            jax.ShapeDtypeStruct((batch, MLA_OUT, seq), BF16),
        ] + dil_shapes,
        scratch_shapes=[pltpu.VMEM((tm, d), BF16)],
        compiler_params=pltpu.CompilerParams(
            dimension_semantics=("parallel",), vmem_limit_bytes=VMEM_LIMIT),
        name="proj",
    )(*([h2d] * (d // LANES)), *weights, *tables)


def _mla_attn_body(q_ref, k_ref, vt_ref, o_ref, st_ref, m_ref, acc_ref, *, tq, tk):
    seq = k_ref.shape[2]
    n_tiles = seq // tq
    n_units = (seq // tk) * n_tiles
    heads = (0, 1)
    n_pieces = tk // MLA_PIECE

    def offsets(unit):
        if isinstance(unit, int):
            return (unit // n_tiles) * tk, (unit % n_tiles) * tq
        return (pl.multiple_of(lax.div(unit, n_tiles) * tk, tk),
                pl.multiple_of(lax.rem(unit, n_tiles) * tq, tq))

    def scores_piece(unit, buf, p, maxes):
        k_off, q_off = offsets(unit)
        rows = slice(p * MLA_PIECE, (p + 1) * MLA_PIECE)
        out = []
        for h in heads:
            st = _dot(k_ref[0, h, pl.ds(k_off + p * MLA_PIECE, MLA_PIECE), :],
                      q_ref[0, h * LANES:(h + 1) * LANES, pl.ds(q_off, tq)])
            st_ref[buf, h, rows] = st
            out.append(jnp.maximum(maxes[h], jnp.max(st, axis=0, keepdims=True)))
        return tuple(out)

    ones_rows = jnp.ones((MLA_DEN_ROWS, MLA_PIECE), BF16)

    def values_piece(k_off, buf, p, m_new, accs):
        rows = slice(p * MLA_PIECE, (p + 1) * MLA_PIECE)
        out = []
        for h in heads:
            pt = jnp.exp2(st_ref[buf, h, rows] - m_new[h]).astype(BF16)
            vtc = jnp.concatenate(
                [vt_ref[0, h * MLA_V:(h + 1) * MLA_V, pl.ds(k_off + p * MLA_PIECE, MLA_PIECE)], ones_rows],
                axis=0)
            out.append(accs[h] + _dot(vtc, pt))
        return tuple(out)

    neg_inf = jnp.full((1, tq), -jnp.inf, F32)

    def half(next_unit, next_buf, unit, buf, unit_max):
        k_off, q_off = offsets(unit)
        cols = pl.ds(q_off, tq)
        m_old = tuple(m_ref[h, :, cols] for h in heads)
        m_new = tuple(jnp.maximum(m_old[h], unit_max[h]) for h in heads)
        accs = tuple(jnp.exp2(m_old[h] - m_new[h]) * acc_ref[h, :, cols] for h in heads)
        next_max = (neg_inf, neg_inf)
        for p in range(n_pieces):
            if next_unit is not None:
                next_max = scores_piece(next_unit, next_buf, p, next_max)
            accs = values_piece(k_off, buf, p, m_new, accs)
        for h in heads:
            m_ref[h, :, cols] = m_new[h]
            acc_ref[h, :, cols] = accs[h]
        return next_max

    def group_step(jj, unit_max):
        for s in range(MLA_UNITS_PER_TRIP):
            u = MLA_UNITS_PER_TRIP * jj + s
            unit_max = half(u + 1, (s + 1) % 2, u, s % 2, unit_max)
        return unit_max

    m_ref[...] = jnp.full(m_ref.shape, -jnp.inf, F32)
    acc_ref[...] = jnp.zeros(acc_ref.shape, F32)
    unit_max = (neg_inf, neg_inf)
    for p in range(n_pieces):
        unit_max = scores_piece(0, 0, p, unit_max)
    unit_max = lax.fori_loop(0, n_units // MLA_UNITS_PER_TRIP - 1, group_step, unit_max)
    for u in range(n_units - MLA_UNITS_PER_TRIP, n_units):
        last = u == n_units - 1
        unit_max = half(None if last else u + 1, None if last else (u + 1) % 2, u, u % 2, unit_max)

    @pl.loop(0, n_tiles)
    def _(tile):
        cols = pl.ds(pl.multiple_of(tile * tq, tq), tq)
        out_t = jnp.concatenate(
            [acc_ref[h, :MLA_V, cols] / acc_ref[h, MLA_V:MLA_V + 1, cols] for h in heads], axis=0)
        o_ref[0, cols, :] = out_t.T.astype(o_ref.dtype)


def _mla_attn(q, k, v, *, tq, tk):
    batch, heads, seq, _ = k.shape
    assert MLA_UNITS_PER_TRIP % 2 == 0 and ((seq // tk) * (seq // tq)) % MLA_UNITS_PER_TRIP == 0
    return pl.pallas_call(
        functools.partial(_mla_attn_body, tq=tq, tk=tk),
        grid=(batch, heads // 2),
        in_specs=[
            pl.BlockSpec((1, 2 * LANES, seq), lambda b, hp: (b, hp, 0)),
            pl.BlockSpec((1, 2, seq, LANES), lambda b, hp: (b, hp, 0, 0)),
            pl.BlockSpec((1, 2 * MLA_V, seq), lambda b, hp: (b, hp, 0)),
        ],
        out_specs=pl.BlockSpec((1, seq, LANES), lambda b, hp: (b, 0, hp)),
        out_shape=jax.ShapeDtypeStruct((batch, seq, MLA_OUT), BF16),
        scratch_shapes=[pltpu.VMEM((2, 2, tk, tq), F32),
                        pltpu.VMEM((2, 1, seq), F32),
                        pltpu.VMEM((2, MLA_V + MLA_DEN_ROWS, seq), F32)],
        compiler_params=pltpu.CompilerParams(
            dimension_semantics=("parallel", "parallel"), vmem_limit_bytes=VMEM_LIMIT),
        name="mla_attn",
    )(q, k, v)


def _dil_attn_body(slopes_ref, q_ref, kl_ref, km_ref, kr_ref, vl_ref, vm_ref, vr_ref,
                   o_ref, lse_ref, *, group, dilation, tq, sub_len):
    i = pl.program_id(1)
    n_res = q_ref.shape[1]
    first_res = pl.program_id(2) * n_res
    n_sub = tq // DIL_SUB

    def window(left_ref, main_ref, right_ref, rr, sub, cols):
        lo, hi = sub * DIL_SUB - DIL_HALO, (sub + 1) * DIL_SUB + DIL_HALO
        parts = []
        if lo < 0:
            parts.append(left_ref[0, rr, :, cols])
        parts.append(main_ref[0, rr, max(lo, 0):min(hi, tq), cols])
        if hi > tq:
            parts.append(right_ref[0, rr, :, cols])
        return parts[0] if len(parts) == 1 else jnp.concatenate(parts, axis=0)

    win = DIL_SUB + 2 * DIL_HALO
    row = lax.broadcasted_iota(jnp.int32, (DIL_SUB, win), 0)
    col = lax.broadcasted_iota(jnp.int32, (DIL_SUB, win), 1)
    steps = jnp.abs(col - DIL_HALO - row)
    in_band = steps <= DIL_HALF_SPAN
    dist = (steps * dilation).astype(F32)
    bias = [jnp.where(in_band, (-LOG2_E * slopes_ref[group * DIL_HEADS_PER_GROUP + h]) * dist, -jnp.inf)
            for h in range(DIL_HEADS_PER_GROUP)]
    for sub in range(n_sub):
        key_pos = i * tq + (sub * DIL_SUB - DIL_HALO) + col
        in_range = None
        if sub == 0:
            in_range = key_pos >= 0
        if sub == n_sub - 1:
            below = key_pos < sub_len
            in_range = below if in_range is None else in_range & below
        rows = slice(sub * DIL_SUB, (sub + 1) * DIL_SUB)
        for rr in range(n_res):
            if dilation == 1:
                out_rows = pl.ds(sub * DIL_SUB, DIL_SUB)
            else:
                out_rows = pl.ds(sub * DIL_SUB * dilation + first_res + rr, DIL_SUB, stride=dilation)
            for h in range(DIL_HEADS_PER_GROUP):
                cols = slice(h * DIL_HEAD_DIM, (h + 1) * DIL_HEAD_DIM)
                s = _dot_nt(q_ref[0, rr, rows, cols], window(kl_ref, km_ref, kr_ref, rr, sub, cols)) + bias[h]
                if in_range is not None:
                    s = jnp.where(in_range, s, -jnp.inf)
                m = jnp.max(s, axis=1, keepdims=True)
                e = jnp.exp2(s - m)
                den = jnp.sum(e, axis=1, keepdims=True)
                o_ref[0, h, out_rows, :] = _dot(
                    e.astype(BF16), window(vl_ref, vm_ref, vr_ref, rr, sub, cols)) / den
                lse_ref[0, h, out_rows, :] = jnp.broadcast_to(
                    (m + jnp.log2(den)) * LN_2, (DIL_SUB, DIL_HEAD_DIM))


def _dil_attn(dil_g, slopes, *, group, dilation):
    batch, _, sub_len, _ = dil_g.shape
    seq = sub_len * dilation
    tq = min(DIL_ROWS_PER_STEP, sub_len, DIL_CHUNK_TOKENS // dilation)
    n_q = sub_len // tq
    n_res = min(dilation, DIL_ROWS_PER_STEP // tq)
    halo_per_tile = tq // DIL_HALO
    n_halo = sub_len // DIL_HALO
    q_col, k_col, v_col = 0, 1, 2

    def main(c):
        return pl.BlockSpec((1, n_res, tq, DIL_OUT), lambda b, i, r: (b, r, i, c))

    def left(c):
        return pl.BlockSpec(
            (1, n_res, DIL_HALO, DIL_OUT),
            lambda b, i, r: (b, r, jnp.maximum(i * halo_per_tile - 1, 0), c))

    def right(c):
        return pl.BlockSpec(
            (1, n_res, DIL_HALO, DIL_OUT),
            lambda b, i, r: (b, r, jnp.minimum((i + 1) * halo_per_tile, n_halo - 1), c))

    out_spec = pl.BlockSpec((1, DIL_HEADS_PER_GROUP, tq * dilation, DIL_HEAD_DIM),
                            lambda b, i, r: (b, 0, i, 0))
    out_shape = jax.ShapeDtypeStruct((batch, DIL_HEADS_PER_GROUP, seq, DIL_HEAD_DIM), F32)
    o, lse = pl.pallas_call(
        functools.partial(_dil_attn_body, group=group, dilation=dilation, tq=tq, sub_len=sub_len),
        grid=(batch, n_q, dilation // n_res),
        in_specs=[pl.BlockSpec(memory_space=pltpu.SMEM),
                  main(q_col), left(k_col), main(k_col), right(k_col),
                  left(v_col), main(v_col), right(v_col)],
        out_specs=[out_spec, out_spec],
        out_shape=[out_shape, out_shape],
        compiler_params=pltpu.CompilerParams(
            dimension_semantics=("parallel", "parallel", "arbitrary"), vmem_limit_bytes=VMEM_LIMIT),
        name=f"dil_attn_g{group}",
    )(slopes, dil_g, dil_g, dil_g, dil_g, dil_g, dil_g, dil_g)
    return o, lse


def _mem_kv_body(mem_ref, w_ref, o_ref):
    o_ref[0] = _dot(mem_ref[0].astype(BF16), w_ref[...]).astype(BF16)


def _mem_kv(mem, w):
    batch, m_len, d = mem.shape
    n = w.shape[1]
    return pl.pallas_call(
        _mem_kv_body,
        grid=(batch,),
        in_specs=[pl.BlockSpec((1, m_len, d), lambda b: (b, 0, 0)),
                  pl.BlockSpec((d, n), lambda b: (0, 0))],
        out_specs=pl.BlockSpec((1, m_len, n), lambda b: (b, 0, 0)),
        out_shape=jax.ShapeDtypeStruct((batch, m_len, n), BF16),
        compiler_params=pltpu.CompilerParams(dimension_semantics=("parallel",)),
        name="mem_kv",
    )(mem, w)


def _merge_body(h_ref, omla_ref, od0_ref, od1_ref, od2_ref, ls0_ref, ls1_ref, ls2_ref, kvm_ref,
                wgate_ref, wmq_ref, wba_ref, wbb_ref, wbc_ref, wo_ref, g_ref, b_ref, out_ref, *, n_sub):
    sub_rows = h_ref.shape[0] // n_sub
    for sub in range(n_sub):
        rows = slice(sub * sub_rows, (sub + 1) * sub_rows)
        _merge_rows(rows, h_ref, omla_ref, od0_ref, od1_ref, od2_ref, ls0_ref, ls1_ref, ls2_ref, kvm_ref,
                    wgate_ref, wmq_ref, wba_ref, wbb_ref, wbc_ref, wo_ref, g_ref, b_ref, out_ref)


def _merge_rows(rows, h_ref, omla_ref, od0_ref, od1_ref, od2_ref, ls0_ref, ls1_ref, ls2_ref, kvm_ref,
                wgate_ref, wmq_ref, wba_ref, wbb_ref, wbc_ref, wo_ref, g_ref, b_ref, out_ref):
    h = h_ref[rows, :]
    hb = h.astype(BF16)

    def heads_to_lanes(ref):
        return jnp.concatenate([ref[0, hh, rows, :] for hh in range(DIL_HEADS_PER_GROUP)], axis=1)

    ls0, ls1, ls2 = heads_to_lanes(ls0_ref), heads_to_lanes(ls1_ref), heads_to_lanes(ls2_ref)
    mx = jnp.maximum(ls0, jnp.maximum(ls1, ls2))
    e0, e1, e2 = jnp.exp(ls0 - mx), jnp.exp(ls1 - mx), jnp.exp(ls2 - mx)
    o_dil = (e0 * heads_to_lanes(od0_ref) + e1 * heads_to_lanes(od1_ref)
             + e2 * heads_to_lanes(od2_ref)) / (e0 + e1 + e2)

    mq = (_dot(hb, wmq_ref[...]) * (MEM_HEAD_DIM ** -0.5)).astype(BF16)
    heads = []
    for hh in range(MEM_HEADS):
        kc = slice(hh * MEM_HEAD_DIM, (hh + 1) * MEM_HEAD_DIM)
        vc = slice(MEM_OUT + hh * MEM_HEAD_DIM, MEM_OUT + (hh + 1) * MEM_HEAD_DIM)
        s = _dot_nt(mq[:, kc], kvm_ref[0, :, kc])
        p = jnp.exp(s - jnp.max(s, axis=1, keepdims=True))
        heads.append(_dot(p.astype(BF16), kvm_ref[0, :, vc]) / jnp.sum(p, axis=1, keepdims=True))
    o_mem = jnp.concatenate(heads, axis=1)

    y_a = _dot(omla_ref[rows, :], wba_ref[...])
    y_b = _dot(o_dil.astype(BF16), wbb_ref[...])
    y_c = _dot(o_mem.astype(BF16), wbc_ref[...])
    d = h.shape[1]
    merged = (jax.nn.sigmoid(_dot(hb, wgate_ref[:, :d])) * y_a
              + jax.nn.sigmoid(_dot(hb, wgate_ref[:, d:2 * d])) * y_b
              + jax.nn.sigmoid(_dot(hb, wgate_ref[:, 2 * d:])) * y_c)
    mix = _dot(merged.astype(BF16), wo_ref[...])
    out_ref[rows, :] = _layer_norm(ALPHA * h + mix, g_ref[...], b_ref[...])


def _merge(h2d, o_mla, o_dil, lse_dil, kv_mem, weights, g, b, *, seq, tm, n_sub):
    t, d = h2d.shape
    n_s = seq // tm

    def rows(width):
        return pl.BlockSpec((tm, width), lambda i: (i, 0))

    def full(a):
        return pl.BlockSpec(a.shape, lambda i: (0,) * a.ndim, pipeline_mode=pl.Buffered(1))

    m_len, kv_cols = kv_mem.shape[1:]
    return pl.pallas_call(
        functools.partial(_merge_body, n_sub=n_sub),
        grid=(t // tm,),
        in_specs=[rows(d), rows(MLA_OUT)]
        + [pl.BlockSpec((1, DIL_HEADS_PER_GROUP, tm, DIL_HEAD_DIM),
                        lambda i: (i // n_s, 0, i % n_s, 0))] * 6
        + [pl.BlockSpec((1, m_len, kv_cols), lambda i: (i // n_s, 0, 0))]
        + [full(w) for w in weights] + [full(g), full(b)],
        out_specs=rows(d),
        out_shape=jax.ShapeDtypeStruct((t, d), F32),
        compiler_params=pltpu.CompilerParams(
            dimension_semantics=("parallel",), vmem_limit_bytes=VMEM_LIMIT),
        name="merge",
    )(h2d, o_mla, *o_dil, *lse_dil, kv_mem, *weights, g, b)


def _swap_halves(w):
    half = w.shape[-1] // 2
    return jnp.concatenate([w[..., half:], w[..., :half]], axis=-1)


def _prep_mla_weights(w_in, w_uq, w_ukv):
    d = w_in.shape[0]
    o_q, o_kv, o_kr = 0, MLA_Q_LORA, MLA_Q_LORA + MLA_KV_LORA
    w_cq = w_in[:, o_q:o_q + MLA_Q_LORA]
    w_ckv = w_in[:, o_kv:o_kv + MLA_KV_LORA]
    w_kr = w_in[:, o_kr:o_kr + MLA_ROPE]
    pad_tail = LANES - MLA_NOPE - MLA_ROPE
    z_nope = jnp.zeros((d, MLA_NOPE), F32)
    z_tail = jnp.zeros((d, pad_tail), F32)
    w_kra = jnp.concatenate([z_nope, w_kr, z_tail], axis=1)
    w_krb = jnp.concatenate([z_nope, _swap_halves(w_kr), z_tail], axis=1)

    uq = w_uq.reshape(MLA_Q_LORA, MLA_HEADS, MLA_NOPE + MLA_ROPE)
    uq_nope, uq_pe = uq[..., :MLA_NOPE], uq[..., MLA_NOPE:]
    zq_nope = jnp.zeros_like(uq_nope)
    zq_tail = jnp.zeros((MLA_Q_LORA, MLA_HEADS, pad_tail), F32)
    w_qa = jnp.concatenate([uq_nope, uq_pe, zq_tail], axis=-1).reshape(MLA_Q_LORA, MLA_HEADS * LANES)
    w_qb = jnp.concatenate([zq_nope, _swap_halves(uq_pe), zq_tail], axis=-1).reshape(
        MLA_Q_LORA, MLA_HEADS * LANES)

    ukv = w_ukv.reshape(MLA_KV_LORA, MLA_HEADS, MLA_NOPE + MLA_V)
    uk, uv = ukv[..., :MLA_NOPE], ukv[..., MLA_NOPE:]
    w_ka = jnp.concatenate(
        [uk, jnp.zeros((MLA_KV_LORA, MLA_HEADS, LANES - MLA_NOPE), F32)], axis=-1).reshape(
            MLA_KV_LORA, MLA_HEADS * LANES)
    w_v = uv.reshape(MLA_KV_LORA, MLA_HEADS * MLA_V)
    return [w.astype(BF16) for w in (w_cq, w_ckv, w_kra, w_krb)], [w.astype(BF16) for w in (w_qa, w_qb, w_ka, w_v)]


def _rope_tables(seq):
    pos = np.arange(seq, dtype=np.float64)
    inv = 1.0 / (ROPE_THETA ** (np.arange(0, MLA_ROPE, 2, dtype=np.float64) / MLA_ROPE))
    pad_tail = LANES - MLA_NOPE - MLA_ROPE
    q_scale = (MLA_NOPE + MLA_ROPE) ** -0.5 * LOG2_E
    ang = pos[:, None] * inv[None, :]
    cos, sin = np.cos(ang), np.sin(ang)

    def tables(lead_one):
        c = np.concatenate([np.full((seq, MLA_NOPE), lead_one), cos, cos, np.zeros((seq, pad_tail))], axis=1)
        s = np.concatenate([np.zeros((seq, MLA_NOPE)), -sin, sin, np.zeros((seq, pad_tail))], axis=1)
        return c, s

    c_q, s_q = tables(1.0)
    c_k, s_k = tables(0.0)
    return tuple(np.ascontiguousarray(a, dtype=np.float32)
                 for a in ((c_q * q_scale).T, (s_q * q_scale).T, c_k, s_k))


def kernel(x, mem, w_in, mla_q_norm, mla_kv_norm, w_uq, w_ukv, w_mem_kv, w_br_mla, w_br_dil, w_br_mem, w_o,
           ffn1_w_gate, ffn1_w_up, ffn1_w_down, ffn2_w_gate, ffn2_w_up, ffn2_w_down,
           ln1_g, ln1_b, ln2_g, ln2_b, ln3_g, ln3_b):
    batch, seq, d = x.shape
    t = batch * seq
    tm_ffn = min(1024, t)
    tm_proj = min(512, seq)
    tm_merge = min(512, seq)
    tables = _rope_tables(seq)
    slopes = 2.0 ** (-8.0 * jnp.arange(1, DIL_HEADS + 1, dtype=F32) / DIL_HEADS)

    h = x.reshape(t, d)
    for l in range(DEPTH):
        bf = lambda w: w[l].astype(BF16)
        row = lambda v: v[l].reshape(1, -1)
        h = _ffn_ln(h, bf(ffn1_w_gate), bf(ffn1_w_up), bf(ffn1_w_down), row(ln1_g), row(ln1_b),
                    tm=tm_ffn, n_sub=FFN_SUB_TILES)

        w_in_l = w_in[l]
        o_dil_cols = MLA_Q_LORA + MLA_KV_LORA + MLA_ROPE
        o_memq = o_dil_cols + DIL_QKV
        o_gate = o_memq + MEM_OUT
        w_c, w_u = _prep_mla_weights(w_in_l, w_uq[l], w_ukv[l])
        proj_weights = w_c + [row(mla_q_norm), row(mla_kv_norm)] + w_u + [
            w_in_l[:, o_dil_cols:o_memq].astype(BF16)]
        q, k, v, *dil = _proj(h, proj_weights, tables, batch=batch, seq=seq, tm=tm_proj)

        o_mla = _mla_attn(q, k, v, tq=min(512, seq), tk=min(1024, seq)).reshape(t, MLA_OUT)
        dil_parts = [_dil_attn(dil[g], slopes, group=g, dilation=dl)
                     for g, (_, dl) in enumerate(DIL_PAIRS)]
        kv_mem = _mem_kv(mem, bf(w_mem_kv))

        merge_weights = [w_in_l[:, o_gate:].astype(BF16), w_in_l[:, o_memq:o_gate].astype(BF16),
                         bf(w_br_mla), bf(w_br_dil), bf(w_br_mem), bf(w_o)]
        h = _merge(h, o_mla, [p[0] for p in dil_parts], [p[1] for p in dil_parts], kv_mem,
                   merge_weights, row(ln2_g), row(ln2_b), seq=seq, tm=tm_merge, n_sub=2)

        h = _ffn_ln(h, bf(ffn2_w_gate), bf(ffn2_w_up), bf(ffn2_w_down), row(ln3_g), row(ln3_b),
                    tm=tm_ffn, n_sub=FFN_SUB_TILES)
    return h.reshape(batch, seq, d)
```

```python
import functools
import math

import jax
import jax.numpy as jnp
import numpy as np
from jax import lax
from jax.experimental import pallas as pl
from jax.experimental.pallas import tpu as pltpu

F32 = jnp.float32
BF16 = jnp.bfloat16

D_MODEL = 1024
DEPTH = 1
MLA_HEADS = 8
MLA_Q_LORA = 256
MLA_KV_LORA = 256
MLA_NOPE = 64
MLA_ROPE = 32
MLA_V = 64
ROPE_THETA = 10000.0
DIL_PAIRS = ((128, 1), (512, 4), (2048, 16))
DIL_GROUPS = 3
DIL_HEADS_PER_GROUP = 4
DIL_HEAD_DIM = 128
DIL_HEADS = DIL_GROUPS * DIL_HEADS_PER_GROUP
MEM_HEADS = 4
MEM_HEAD_DIM = 128
EPS = 1e-5
ALPHA = (2 * DEPTH) ** 0.25

LOG2_E = math.log2(math.e)
LN_2 = math.log(2.0)
LANES = 128
DIL_OUT = DIL_HEADS_PER_GROUP * DIL_HEAD_DIM
DIL_QKV = 3 * DIL_HEADS * DIL_HEAD_DIM
MEM_OUT = MEM_HEADS * MEM_HEAD_DIM
MLA_OUT = MLA_HEADS * MLA_V
MLA_DEN_ROWS = 16
MLA_PIECE = 256
MLA_UNITS_PER_TRIP = 16
DIL_HALF_SPAN = 64
DIL_HALO = 64
DIL_SUB = 128
DIL_CHUNK_TOKENS = 2048
DIL_ROWS_PER_STEP = 2048
FFN_SUB_TILES = 4
VMEM_LIMIT = 56 * 1024 * 1024

assert all(w // 2 // d == DIL_HALF_SPAN for w, d in DIL_PAIRS)

_NT = (((1,), (1,)), ((), ()))


def _dot(a, b):
    return jnp.dot(a, b, preferred_element_type=F32)


def _dot_nt(a, b):
    return lax.dot_general(a, b, _NT, preferred_element_type=F32)


def _dot_tt(w, x):
    return lax.dot_general(w, x, (((0,), (1,)), ((), ())), preferred_element_type=F32)


def _layer_norm(y, g, b):
    mu = jnp.mean(y, axis=-1, keepdims=True)
    yc = y - mu
    var = jnp.mean(yc * yc, axis=-1, keepdims=True)
    return yc * lax.rsqrt(var + EPS) * g + b


def _rms_norm(y, g):
    return y * lax.rsqrt(jnp.mean(y * y, axis=-1, keepdims=True) + EPS) * g


def _ffn_ln_body(x_ref, wg_ref, wu_ref, wd_ref, g_ref, b_ref, o_ref, *, n_sub):
    sub_rows = x_ref.shape[0] // n_sub
    for sub in range(n_sub):
        rows = slice(sub * sub_rows, (sub + 1) * sub_rows)
        x = x_ref[rows, :]
        xb = x.astype(BF16)
        gate = _dot(xb, wg_ref[...])
        up = _dot(xb, wu_ref[...])
        act = (gate * jax.nn.sigmoid(gate) * up).astype(BF16)
        y = ALPHA * x + 0.5 * _dot(act, wd_ref[...])
        o_ref[rows, :] = _layer_norm(y, g_ref[...], b_ref[...])


def _ffn_ln(x2d, wg, wu, wd, g, b, *, tm, n_sub):
    t, d = x2d.shape

    def full(a):
        return pl.BlockSpec(a.shape, lambda i: (0,) * a.ndim, pipeline_mode=pl.Buffered(1))

    return pl.pallas_call(
        functools.partial(_ffn_ln_body, n_sub=n_sub),
        grid=(t // tm,),
        in_specs=[pl.BlockSpec((tm, d), lambda i: (i, 0))] + [full(a) for a in (wg, wu, wd, g, b)],
        out_specs=pl.BlockSpec((tm, d), lambda i: (i, 0)),
        out_shape=jax.ShapeDtypeStruct((t, d), F32),
        compiler_params=pltpu.CompilerParams(
            dimension_semantics=("parallel",), vmem_limit_bytes=VMEM_LIMIT),
        name="ffn_ln",
    )(x2d, wg, wu, wd, g, b)


def _proj_body(*refs):
    n_hc = D_MODEL // LANES
    h_refs = refs[:n_hc]
    (wcq_ref, wckv_ref, wkra_ref, wkrb_ref, gq_ref, gkv_ref,
     wqa_ref, wqb_ref, wka_ref, wv_ref, wdil_ref,
     cq_ref, sq_ref, ck_ref, sk_ref,
     q_ref, k_ref, v_ref, dil0_ref, dil1_ref, dil2_ref, hperm_ref) = refs[n_hc:]
    hb = jnp.concatenate([hc[...] for hc in h_refs], axis=1).astype(BF16)
    cqn = _rms_norm(_dot(hb, wcq_ref[...]), gq_ref[...]).astype(BF16)
    ckvn = _rms_norm(_dot(hb, wckv_ref[...]), gkv_ref[...]).astype(BF16)
    k_rope = _dot(hb, wkra_ref[...]) * ck_ref[...] + _dot(hb, wkrb_ref[...]) * sk_ref[...]
    ka = _dot(ckvn, wka_ref[...])
    for h in range(MLA_HEADS):
        sl = slice(h * LANES, (h + 1) * LANES)
        k_ref[0, h] = (ka[:, sl] + k_rope).astype(BF16)
    qa_t = _dot_tt(wqa_ref[...], cqn)
    qb_t = _dot_tt(wqb_ref[...], cqn)
    cq_t = cq_ref[...]
    sq_t = sq_ref[...]
    for h in range(MLA_HEADS):
        sl = slice(h * LANES, (h + 1) * LANES)
        q_ref[0, sl] = (qa_t[sl] * cq_t + qb_t[sl] * sq_t).astype(BF16)
    v_ref[0] = _dot_tt(wv_ref[...], ckvn).astype(BF16)

    dil_scale = DIL_HEAD_DIM ** -0.5 * LOG2_E
    tm = h_refs[0].shape[0]
    for g, (dil_ref, (_, dilation)) in enumerate(zip((dil0_ref, dil1_ref, dil2_ref), DIL_PAIRS)):
        rows_per = tm // dilation
        if dilation == 1:
            hp = hb
        else:
            for r in range(dilation):
                for c, hc in enumerate(h_refs):
                    hperm_ref[r * rows_per:(r + 1) * rows_per, c * LANES:(c + 1) * LANES] = hc[
                        pl.ds(r, rows_per, stride=dilation), :].astype(BF16)
            hp = hperm_ref[...]
        for part in range(3):
            c0 = (part * DIL_GROUPS + g) * DIL_OUT
            res = _dot(hp, wdil_ref[:, c0:c0 + DIL_OUT])
            if part == 0:
                res = res * dil_scale
            res = res.astype(BF16)
            for r in range(dilation):
                dil_ref[0, r, :, part * DIL_OUT:(part + 1) * DIL_OUT] = res[r * rows_per:(r + 1) * rows_per]


def _proj(h2d, weights, tables, *, batch, seq, tm):
    t, d = h2d.shape
    n_s = seq // tm

    def full(a):
        return pl.BlockSpec(a.shape, lambda i: (0,) * a.ndim)

    tab_spec = pl.BlockSpec((tm, LANES), lambda i: (i % n_s, 0))
    tab_t_spec = pl.BlockSpec((LANES, tm), lambda i: (0, i % n_s))
    head_map = lambda i: (i // n_s, 0, i % n_s, 0)
    feat_map = lambda i: (i // n_s, 0, i % n_s)
    dil_specs = [pl.BlockSpec((1, dl, tm // dl, 3 * DIL_OUT), head_map) for _, dl in DIL_PAIRS]
    dil_shapes = [jax.ShapeDtypeStruct((batch, dl, seq // dl, 3 * DIL_OUT), BF16) for _, dl in DIL_PAIRS]
    return pl.pallas_call(
        _proj_body,
        grid=(t // tm,),
        in_specs=[pl.BlockSpec((tm, LANES), functools.partial(lambda c, i: (i, c), c))
                  for c in range(d // LANES)]
        + [full(w) for w in weights] + [tab_t_spec] * 2 + [tab_spec] * 2,
        out_specs=[
            pl.BlockSpec((1, MLA_HEADS * LANES, tm), feat_map),
            pl.BlockSpec((1, MLA_HEADS, tm, LANES), head_map),
            pl.BlockSpec((1, MLA_OUT, tm), feat_map),
        ] + dil_specs,
        out_shape=[
            jax.ShapeDtypeStruct((batch, MLA_HEADS * LANES, seq), BF16),
            jax.ShapeDtypeStruct((batch, MLA_HEADS, seq, LANES), BF16),
            jax.ShapeDtypeStruct((batch, MLA_OUT, seq), BF16),
        ] + dil_shapes,
        scratch_shapes=[pltpu.VMEM((tm, d), BF16)],
        compiler_params=pltpu.CompilerParams(
            dimension_semantics=("parallel",), vmem_limit_bytes=VMEM_LIMIT),
        name="proj",
    )(*([h2d] * (d // LANES)), *weights, *tables)


def _mla_attn_body(q_ref, k_ref, vt_ref, o_ref, st_ref, m_ref, acc_ref, *, tq, tk):
    seq = k_ref.shape[2]
    n_tiles = seq // tq
    n_units = (seq // tk) * n_tiles
    heads = (0, 1)
    n_pieces = tk // MLA_PIECE

    def offsets(unit):
        if isinstance(unit, int):
            return (unit // n_tiles) * tk, (unit % n_tiles) * tq
        return (pl.multiple_of(lax.div(unit, n_tiles) * tk, tk),
                pl.multiple_of(lax.rem(unit, n_tiles) * tq, tq))

    def scores_piece(unit, buf, p, maxes):
        k_off, q_off = offsets(unit)
        rows = slice(p * MLA_PIECE, (p + 1) * MLA_PIECE)
        out = []
        for h in heads:
            st = _dot(k_ref[0, h, pl.ds(k_off + p * MLA_PIECE, MLA_PIECE), :],
                      q_ref[0, h * LANES:(h + 1) * LANES, pl.ds(q_off, tq)])
            st_ref[buf, h, rows] = st
            out.append(jnp.maximum(maxes[h], jnp.max(st, axis=0, keepdims=True)))
        return tuple(out)

    ones_rows = jnp.ones((MLA_DEN_ROWS, MLA_PIECE), BF16)

    def values_piece(k_off, buf, p, m_new, accs):
        rows = slice(p * MLA_PIECE, (p + 1) * MLA_PIECE)
        out = []
        for h in heads:
            pt = jnp.exp2(st_ref[buf, h, rows] - m_new[h]).astype(BF16)
            vtc = jnp.concatenate(
                [vt_ref[0, h * MLA_V:(h + 1) * MLA_V, pl.ds(k_off + p * MLA_PIECE, MLA_PIECE)], ones_rows],
                axis=0)
            out.append(accs[h] + _dot(vtc, pt))
        return tuple(out)

    neg_inf = jnp.full((1, tq), -jnp.inf, F32)

    def half(next_unit, next_buf, unit, buf, unit_max):
        k_off, q_off = offsets(unit)
        cols = pl.ds(q_off, tq)
        m_old = tuple(m_ref[h, :, cols] for h in heads)
        m_new = tuple(jnp.maximum(m_old[h], unit_max[h]) for h in heads)
        accs = tuple(jnp.exp2(m_old[h] - m_new[h]) * acc_ref[h, :, cols] for h in heads)
        next_max = (neg_inf, neg_inf)
        for p in range(n_pieces):
            if next_unit is not None:
                next_max = scores_piece(next_unit, next_buf, p, next_max)
            accs = values_piece(k_off, buf, p, m_new, accs)
        for h in heads:
            m_ref[h, :, cols] = m_new[h]
            acc_ref[h, :, cols] = accs[h]
        return next_max

    def group_step(jj, unit_max):
        for s in range(MLA_UNITS_PER_TRIP):
            u = MLA_UNITS_PER_TRIP * jj + s
            unit_max = half(u + 1, (s + 1) % 2, u, s % 2, unit_max)
        return unit_max

    m_ref[...] = jnp.full(m_ref.shape, -jnp.inf, F32)
    acc_ref[...] = jnp.zeros(acc_ref.shape, F32)
    unit_max = (neg_inf, neg_inf)
    for p in range(n_pieces):
        unit_max = scores_piece(0, 0, p, unit_max)
    unit_max = lax.fori_loop(0, n_units // MLA_UNITS_PER_TRIP - 1, group_step, unit_max)
    for u in range(n_units - MLA_UNITS_PER_TRIP, n_units):
        last = u == n_units - 1
        unit_max = half(None if last else u + 1, None if last else (u + 1) % 2, u, u % 2, unit_max)
        if u >= n_units - n_tiles:
            cols = slice((u % n_tiles) * tq, (u % n_tiles + 1) * tq)
            out_t = jnp.concatenate(
                [acc_ref[h, :MLA_V, cols] / acc_ref[h, MLA_V:MLA_V + 1, cols] for h in heads], axis=0)
            o_ref[0, cols, :] = out_t.T.astype(o_ref.dtype)


def _mla_attn(q, k, v, *, tq, tk):
    batch, heads, seq, _ = k.shape
    assert MLA_UNITS_PER_TRIP % 2 == 0 and ((seq // tk) * (seq // tq)) % MLA_UNITS_PER_TRIP == 0
    assert seq // tq <= MLA_UNITS_PER_TRIP
    return pl.pallas_call(
        functools.partial(_mla_attn_body, tq=tq, tk=tk),
        grid=(batch, heads // 2),
        in_specs=[
            pl.BlockSpec((1, 2 * LANES, seq), lambda b, hp: (b, hp, 0)),
            pl.BlockSpec((1, 2, seq, LANES), lambda b, hp: (b, hp, 0, 0)),
            pl.BlockSpec((1, 2 * MLA_V, seq), lambda b, hp: (b, hp, 0)),
        ],
        out_specs=pl.BlockSpec((1, seq, LANES), lambda b, hp: (b, 0, hp)),
        out_shape=jax.ShapeDtypeStruct((batch, seq, MLA_OUT), BF16),
        scratch_shapes=[pltpu.VMEM((2, 2, tk, tq), F32),
                        pltpu.VMEM((2, 1, seq), F32),
                        pltpu.VMEM((2, MLA_V + MLA_DEN_ROWS, seq), F32)],
        compiler_params=pltpu.CompilerParams(
            dimension_semantics=("parallel", "parallel"), vmem_limit_bytes=VMEM_LIMIT),
        name="mla_attn",
    )(q, k, v)


def _dil_attn_body(slopes_ref, q_ref, kl_ref, km_ref, kr_ref, vl_ref, vm_ref, vr_ref,
                   o_ref, lse_ref, *, group, dilation, tq, sub_len):
    i = pl.program_id(1)
    n_res = q_ref.shape[1]
    first_res = pl.program_id(2) * n_res
    n_sub = tq // DIL_SUB

    def window(left_ref, main_ref, right_ref, rr, sub, cols):
        lo, hi = sub * DIL_SUB - DIL_HALO, (sub + 1) * DIL_SUB + DIL_HALO
        parts = []
        if lo < 0:
            parts.append(left_ref[0, rr, :, cols])
        parts.append(main_ref[0, rr, max(lo, 0):min(hi, tq), cols])
        if hi > tq:
            parts.append(right_ref[0, rr, :, cols])
        return parts[0] if len(parts) == 1 else jnp.concatenate(parts, axis=0)

    win = DIL_SUB + 2 * DIL_HALO
    row = lax.broadcasted_iota(jnp.int32, (DIL_SUB, win), 0)
    col = lax.broadcasted_iota(jnp.int32, (DIL_SUB, win), 1)
    steps = jnp.abs(col - DIL_HALO - row)
    in_band = steps <= DIL_HALF_SPAN
    dist = (steps * dilation).astype(F32)
    bias = [jnp.where(in_band, (-LOG2_E * slopes_ref[group * DIL_HEADS_PER_GROUP + h]) * dist, -jnp.inf)
            for h in range(DIL_HEADS_PER_GROUP)]
    for sub in range(n_sub):
        key_pos = i * tq + (sub * DIL_SUB - DIL_HALO) + col
        in_range = None
        if sub == 0:
            in_range = key_pos >= 0
        if sub == n_sub - 1:
            below = key_pos < sub_len
            in_range = below if in_range is None else in_range & below
        rows = slice(sub * DIL_SUB, (sub + 1) * DIL_SUB)
        for rr in range(n_res):
            if dilation == 1:
                out_rows = pl.ds(sub * DIL_SUB, DIL_SUB)
            else:
                out_rows = pl.ds(sub * DIL_SUB * dilation + first_res + rr, DIL_SUB, stride=dilation)
            for h in range(DIL_HEADS_PER_GROUP):
                cols = slice(h * DIL_HEAD_DIM, (h + 1) * DIL_HEAD_DIM)
                s = _dot_nt(q_ref[0, rr, rows, cols], window(kl_ref, km_ref, kr_ref, rr, sub, cols)) + bias[h]
                if in_range is not None:
                    s = jnp.where(in_range, s, -jnp.inf)
                m = jnp.max(s, axis=1, keepdims=True)
                e = jnp.exp2(s - m)
                den = jnp.sum(e, axis=1, keepdims=True)
                o_ref[0, h, out_rows, :] = _dot(
                    e.astype(BF16), window(vl_ref, vm_ref, vr_ref, rr, sub, cols)) / den
                lse_ref[0, h, out_rows, :] = jnp.broadcast_to(
                    (m + jnp.log2(den)) * LN_2, (DIL_SUB, DIL_HEAD_DIM))


def _dil_attn(dil_g, slopes, *, group, dilation):
    batch, _, sub_len, _ = dil_g.shape
    seq = sub_len * dilation
    tq = min(DIL_ROWS_PER_STEP, sub_len, DIL_CHUNK_TOKENS // dilation)
    n_q = sub_len // tq
    n_res = min(dilation, DIL_ROWS_PER_STEP // tq)
    halo_per_tile = tq // DIL_HALO
    n_halo = sub_len // DIL_HALO
    q_col, k_col, v_col = 0, 1, 2

    def main(c):
        return pl.BlockSpec((1, n_res, tq, DIL_OUT), lambda b, i, r: (b, r, i, c))

    def left(c):
        return pl.BlockSpec(
            (1, n_res, DIL_HALO, DIL_OUT),
            lambda b, i, r: (b, r, jnp.maximum(i * halo_per_tile - 1, 0), c))

    def right(c):
        return pl.BlockSpec(
            (1, n_res, DIL_HALO, DIL_OUT),
            lambda b, i, r: (b, r, jnp.minimum((i + 1) * halo_per_tile, n_halo - 1), c))

    out_spec = pl.BlockSpec((1, DIL_HEADS_PER_GROUP, tq * dilation, DIL_HEAD_DIM),
                            lambda b, i, r: (b, 0, i, 0))
    out_shape = jax.ShapeDtypeStruct((batch, DIL_HEADS_PER_GROUP, seq, DIL_HEAD_DIM), F32)
    o, lse = pl.pallas_call(
        functools.partial(_dil_attn_body, group=group, dilation=dilation, tq=tq, sub_len=sub_len),
        grid=(batch, n_q, dilation // n_res),
        in_specs=[pl.BlockSpec(memory_space=pltpu.SMEM),
                  main(q_col), left(k_col), main(k_col), right(k_col),
                  left(v_col), main(v_col), right(v_col)],
        out_specs=[out_spec, out_spec],
        out_shape=[out_shape, out_shape],
        compiler_params=pltpu.CompilerParams(
            dimension_semantics=("parallel", "parallel", "arbitrary"), vmem_limit_bytes=VMEM_LIMIT),
        name=f"dil_attn_g{group}",
    )(slopes, dil_g, dil_g, dil_g, dil_g, dil_g, dil_g, dil_g)
    return o, lse


def _mem_kv_body(mem_ref, w_ref, o_ref):
    o_ref[0] = _dot(mem_ref[0].astype(BF16), w_ref[...]).astype(BF16)


def _mem_kv(mem, w):
    batch, m_len, d = mem.shape
    n = w.shape[1]
    return pl.pallas_call(
        _mem_kv_body,
        grid=(batch,),
        in_specs=[pl.BlockSpec((1, m_len, d), lambda b: (b, 0, 0)),
                  pl.BlockSpec((d, n), lambda b: (0, 0))],
        out_specs=pl.BlockSpec((1, m_len, n), lambda b: (b, 0, 0)),
        out_shape=jax.ShapeDtypeStruct((batch, m_len, n), BF16),
        compiler_params=pltpu.CompilerParams(dimension_semantics=("parallel",)),
        name="mem_kv",
    )(mem, w)


def _merge_body(h_ref, omla_ref, od0_ref, od1_ref, od2_ref, ls0_ref, ls1_ref, ls2_ref, kvm_ref,
                wgate_ref, wmq_ref, wba_ref, wbb_ref, wbc_ref, wo_ref, g_ref, b_ref, out_ref, *, n_sub):
    sub_rows = h_ref.shape[0] // n_sub
    for sub in range(n_sub):
        rows = slice(sub * sub_rows, (sub + 1) * sub_rows)
        _merge_rows(rows, h_ref, omla_ref, od0_ref, od1_ref, od2_ref, ls0_ref, ls1_ref, ls2_ref, kvm_ref,
                    wgate_ref, wmq_ref, wba_ref, wbb_ref, wbc_ref, wo_ref, g_ref, b_ref, out_ref)


def _merge_rows(rows, h_ref, omla_ref, od0_ref, od1_ref, od2_ref, ls0_ref, ls1_ref, ls2_ref, kvm_ref,
                wgate_ref, wmq_ref, wba_ref, wbb_ref, wbc_ref, wo_ref, g_ref, b_ref, out_ref):
    h = h_ref[rows, :]
    hb = h.astype(BF16)

    def heads_to_lanes(ref):
        return jnp.concatenate([ref[0, hh, rows, :] for hh in range(DIL_HEADS_PER_GROUP)], axis=1)

    ls0, ls1, ls2 = heads_to_lanes(ls0_ref), heads_to_lanes(ls1_ref), heads_to_lanes(ls2_ref)
    mx = jnp.maximum(ls0, jnp.maximum(ls1, ls2))
    e0, e1, e2 = jnp.exp(ls0 - mx), jnp.exp(ls1 - mx), jnp.exp(ls2 - mx)
    o_dil = (e0 * heads_to_lanes(od0_ref) + e1 * heads_to_lanes(od1_ref)
             + e2 * heads_to_lanes(od2_ref)) / (e0 + e1 + e2)

    mq = (_dot(hb, wmq_ref[...]) * (MEM_HEAD_DIM ** -0.5)).astype(BF16)
    heads = []
    for hh in range(MEM_HEADS):
        kc = slice(hh * MEM_HEAD_DIM, (hh + 1) * MEM_HEAD_DIM)
        vc = slice(MEM_OUT + hh * MEM_HEAD_DIM, MEM_OUT + (hh + 1) * MEM_HEAD_DIM)
        s = _dot_nt(mq[:, kc], kvm_ref[0, :, kc])
        p = jnp.exp(s - jnp.max(s, axis=1, keepdims=True))
        heads.append(_dot(p.astype(BF16), kvm_ref[0, :, vc]) / jnp.sum(p, axis=1, keepdims=True))
    o_mem = jnp.concatenate(heads, axis=1)

    y_a = _dot(omla_ref[rows, :], wba_ref[...])
    y_b = _dot(o_dil.astype(BF16), wbb_ref[...])
    y_c = _dot(o_mem.astype(BF16), wbc_ref[...])
    d = h.shape[1]
    merged = (jax.nn.sigmoid(_dot(hb, wgate_ref[:, :d])) * y_a
              + jax.nn.sigmoid(_dot(hb, wgate_ref[:, d:2 * d])) * y_b
              + jax.nn.sigmoid(_dot(hb, wgate_ref[:, 2 * d:])) * y_c)
    mix = _dot(merged.astype(BF16), wo_ref[...])
    out_ref[rows, :] = _layer_norm(ALPHA * h + mix, g_ref[...], b_ref[...])


def _merge(h2d, o_mla, o_dil, lse_dil, kv_mem, weights, g, b, *, seq, tm, n_sub):
    t, d = h2d.shape
    n_s = seq // tm

    def rows(width):
        return pl.BlockSpec((tm, width), lambda i: (i, 0))

    def full(a):
        return pl.BlockSpec(a.shape, lambda i: (0,) * a.ndim, pipeline_mode=pl.Buffered(1))

    m_len, kv_cols = kv_mem.shape[1:]
    return pl.pallas_call(
        functools.partial(_merge_body, n_sub=n_sub),
        grid=(t // tm,),
        in_specs=[rows(d), rows(MLA_OUT)]
        + [pl.BlockSpec((1, DIL_HEADS_PER_GROUP, tm, DIL_HEAD_DIM),
                        lambda i: (i // n_s, 0, i % n_s, 0))] * 6
        + [pl.BlockSpec((1, m_len, kv_cols), lambda i: (i // n_s, 0, 0))]
        + [full(w) for w in weights] + [full(g), full(b)],
        out_specs=rows(d),
        out_shape=jax.ShapeDtypeStruct((t, d), F32),
        compiler_params=pltpu.CompilerParams(
            dimension_semantics=("parallel",), vmem_limit_bytes=VMEM_LIMIT),
        name="merge",
    )(h2d, o_mla, *o_dil, *lse_dil, kv_mem, *weights, g, b)


def _swap_halves(w):
    half = w.shape[-1] // 2
    return jnp.concatenate([w[..., half:], w[..., :half]], axis=-1)


def _prep_mla_weights(w_in, w_uq, w_ukv):
    d = w_in.shape[0]
    o_q, o_kv, o_kr = 0, MLA_Q_LORA, MLA_Q_LORA + MLA_KV_LORA
    w_cq = w_in[:, o_q:o_q + MLA_Q_LORA]
    w_ckv = w_in[:, o_kv:o_kv + MLA_KV_LORA]
    w_kr = w_in[:, o_kr:o_kr + MLA_ROPE]
    pad_tail = LANES - MLA_NOPE - MLA_ROPE
    z_nope = jnp.zeros((d, MLA_NOPE), F32)
    z_tail = jnp.zeros((d, pad_tail), F32)
    w_kra = jnp.concatenate([z_nope, w_kr, z_tail], axis=1)
    w_krb = jnp.concatenate([z_nope, _swap_halves(w_kr), z_tail], axis=1)

    uq = w_uq.reshape(MLA_Q_LORA, MLA_HEADS, MLA_NOPE + MLA_ROPE)
    uq_nope, uq_pe = uq[..., :MLA_NOPE], uq[..., MLA_NOPE:]
    zq_nope = jnp.zeros_like(uq_nope)
    zq_tail = jnp.zeros((MLA_Q_LORA, MLA_HEADS, pad_tail), F32)
    w_qa = jnp.concatenate([uq_nope, uq_pe, zq_tail], axis=-1).reshape(MLA_Q_LORA, MLA_HEADS * LANES)
    w_qb = jnp.concatenate([zq_nope, _swap_halves(uq_pe), zq_tail], axis=-1).reshape(
        MLA_Q_LORA, MLA_HEADS * LANES)

    ukv = w_ukv.reshape(MLA_KV_LORA, MLA_HEADS, MLA_NOPE + MLA_V)
    uk, uv = ukv[..., :MLA_NOPE], ukv[..., MLA_NOPE:]
    w_ka = jnp.concatenate(
        [uk, jnp.zeros((MLA_KV_LORA, MLA_HEADS, LANES - MLA_NOPE), F32)], axis=-1).reshape(
            MLA_KV_LORA, MLA_HEADS * LANES)
    w_v = uv.reshape(MLA_KV_LORA, MLA_HEADS * MLA_V)
    return [w.astype(BF16) for w in (w_cq, w_ckv, w_kra, w_krb)], [w.astype(BF16) for w in (w_qa, w_qb, w_ka, w_v)]


def _rope_tables(seq):
    pos = np.arange(seq, dtype=np.float64)
    inv = 1.0 / (ROPE_THETA ** (np.arange(0, MLA_ROPE, 2, dtype=np.float64) / MLA_ROPE))
    pad_tail = LANES - MLA_NOPE - MLA_ROPE
    q_scale = (MLA_NOPE + MLA_ROPE) ** -0.5 * LOG2_E
    ang = pos[:, None] * inv[None, :]
    cos, sin = np.cos(ang), np.sin(ang)

    def tables(lead_one):
        c = np.concatenate([np.full((seq, MLA_NOPE), lead_one), cos, cos, np.zeros((seq, pad_tail))], axis=1)
        s = np.concatenate([np.zeros((seq, MLA_NOPE)), -sin, sin, np.zeros((seq, pad_tail))], axis=1)
        return c, s

    c_q, s_q = tables(1.0)
    c_k, s_k = tables(0.0)
    return tuple(np.ascontiguousarray(a, dtype=np.float32)
                 for a in ((c_q * q_scale).T, (s_q * q_scale).T, c_k, s_k))


def kernel(x, mem, w_in, mla_q_norm, mla_kv_norm, w_uq, w_ukv, w_mem_kv, w_br_mla, w_br_dil, w_br_mem, w_o,
           ffn1_w_gate, ffn1_w_up, ffn1_w_down, ffn2_w_gate, ffn2_w_up, ffn2_w_down,
           ln1_g, ln1_b, ln2_g, ln2_b, ln3_g, ln3_b):
    batch, seq, d = x.shape
    t = batch * seq
    tm_ffn = min(1024, t)
    tm_proj = min(512, seq)
    tm_merge = min(512, seq)
    tables = _rope_tables(seq)
    slopes = 2.0 ** (-8.0 * jnp.arange(1, DIL_HEADS + 1, dtype=F32) / DIL_HEADS)

    h = x.reshape(t, d)
    for l in range(DEPTH):
        bf = lambda w: w[l].astype(BF16)
        row = lambda v: v[l].reshape(1, -1)
        h = _ffn_ln(h, bf(ffn1_w_gate), bf(ffn1_w_up), bf(ffn1_w_down), row(ln1_g), row(ln1_b),
                    tm=tm_ffn, n_sub=FFN_SUB_TILES)

        w_in_l = w_in[l]
        o_dil_cols = MLA_Q_LORA + MLA_KV_LORA + MLA_ROPE
        o_memq = o_dil_cols + DIL_QKV
        o_gate = o_memq + MEM_OUT
        w_c, w_u = _prep_mla_weights(w_in_l, w_uq[l], w_ukv[l])
        proj_weights = w_c + [row(mla_q_norm), row(mla_kv_norm)] + w_u + [
            w_in_l[:, o_dil_cols:o_memq].astype(BF16)]
        q, k, v, *dil = _proj(h, proj_weights, tables, batch=batch, seq=seq, tm=tm_proj)

        o_mla = _mla_attn(q, k, v, tq=min(512, seq), tk=min(1024, seq)).reshape(t, MLA_OUT)
        dil_parts = [_dil_attn(dil[g], slopes, group=g, dilation=dl)
                     for g, (_, dl) in enumerate(DIL_PAIRS)]
        kv_mem = _mem_kv(mem, bf(w_mem_kv))

        merge_weights = [w_in_l[:, o_gate:].astype(BF16), w_in_l[:, o_memq:o_gate].astype(BF16),
                         bf(w_br_mla), bf(w_br_dil), bf(w_br_mem), bf(w_o)]
        h = _merge(h, o_mla, [p[0] for p in dil_parts], [p[1] for p in dil_parts], kv_mem,
                   merge_weights, row(ln2_g), row(ln2_b), seq=seq, tm=tm_merge, n_sub=2)

        h = _ffn_ln(h, bf(ffn2_w_gate), bf(ffn2_w_up), bf(ffn2_w_down), row(ln3_g), row(ln3_b),
                    tm=tm_ffn, n_sub=FFN_SUB_TILES)
    return h.reshape(batch, seq, d)
```

```python
import functools
import math

import jax
import jax.numpy as jnp
import numpy as np
from jax import lax
from jax.experimental import pallas as pl
from jax.experimental.pallas import tpu as pltpu

F32 = jnp.float32
BF16 = jnp.bfloat16

D_MODEL = 1024
DEPTH = 1
MLA_HEADS = 8
MLA_Q_LORA = 256
MLA_KV_LORA = 256
MLA_NOPE = 64
MLA_ROPE = 32
MLA_V = 64
ROPE_THETA = 10000.0
DIL_PAIRS = ((128, 1), (512, 4), (2048, 16))
DIL_GROUPS = 3
DIL_HEADS_PER_GROUP = 4
DIL_HEAD_DIM = 128
DIL_HEADS = DIL_GROUPS * DIL_HEADS_PER_GROUP
MEM_HEADS = 4
MEM_HEAD_DIM = 128
EPS = 1e-5
ALPHA = (2 * DEPTH) ** 0.25

LOG2_E = math.log2(math.e)
LN_2 = math.log(2.0)
LANES = 128
DIL_OUT = DIL_HEADS_PER_GROUP * DIL_HEAD_DIM
DIL_QKV = 3 * DIL_HEADS * DIL_HEAD_DIM
MEM_OUT = MEM_HEADS * MEM_HEAD_DIM
MLA_OUT = MLA_HEADS * MLA_V
MLA_DEN_ROWS = 16
MLA_PIECE = 256
MLA_UNITS_PER_TRIP = 16
DIL_HALF_SPAN = 64
DIL_HALO = 64
DIL_SUB = 128
DIL_CHUNK_TOKENS = 2048
DIL_ROWS_PER_STEP = 2048
FFN_SUB_TILES = 4
VMEM_LIMIT = 56 * 1024 * 1024

assert all(w // 2 // d == DIL_HALF_SPAN for w, d in DIL_PAIRS)

_NT = (((1,), (1,)), ((), ()))


def _dot(a, b):
    return jnp.dot(a, b, preferred_element_type=F32)


def _dot_nt(a, b):
    return lax.dot_general(a, b, _NT, preferred_element_type=F32)


def _dot_tt(w, x):
    return lax.dot_general(w, x, (((0,), (1,)), ((), ())), preferred_element_type=F32)


def _layer_norm(y, g, b):
    mu = jnp.mean(y, axis=-1, keepdims=True)
    yc = y - mu
    var = jnp.mean(yc * yc, axis=-1, keepdims=True)
    return yc * lax.rsqrt(var + EPS) * g + b


def _rms_norm(y, g):
    return y * lax.rsqrt(jnp.mean(y * y, axis=-1, keepdims=True) + EPS) * g


def _ffn_ln_body(x_ref, wg_ref, wu_ref, wd_ref, g_ref, b_ref, o_ref, *, n_sub):
    sub_rows = x_ref.shape[0] // n_sub
    for sub in range(n_sub):
        rows = slice(sub * sub_rows, (sub + 1) * sub_rows)
        x = x_ref[rows, :]
        xb = x.astype(BF16)
        gate = _dot(xb, wg_ref[...])
        up = _dot(xb, wu_ref[...])
        act = (gate * jax.nn.sigmoid(gate) * up).astype(BF16)
        y = ALPHA * x + 0.5 * _dot(act, wd_ref[...])
        o_ref[rows, :] = _layer_norm(y, g_ref[...], b_ref[...])


def _ffn_ln(x2d, wg, wu, wd, g, b, *, tm, n_sub):
    t, d = x2d.shape

    def full(a):
        return pl.BlockSpec(a.shape, lambda i: (0,) * a.ndim, pipeline_mode=pl.Buffered(1))

    return pl.pallas_call(
        functools.partial(_ffn_ln_body, n_sub=n_sub),
        grid=(t // tm,),
        in_specs=[pl.BlockSpec((tm, d), lambda i: (i, 0))] + [full(a) for a in (wg, wu, wd, g, b)],
        out_specs=pl.BlockSpec((tm, d), lambda i: (i, 0)),
        out_shape=jax.ShapeDtypeStruct((t, d), F32),
        compiler_params=pltpu.CompilerParams(
            dimension_semantics=("parallel",), vmem_limit_bytes=VMEM_LIMIT),
        name="ffn_ln",
    )(x2d, wg, wu, wd, g, b)


def _proj_body(*refs):
    n_hc = D_MODEL // LANES
    h_refs = refs[:n_hc]
    (wcq_ref, wckv_ref, wkra_ref, wkrb_ref, gq_ref, gkv_ref,
     wqa_ref, wqb_ref, wka_ref, wv_ref, wdil_ref,
     cq_ref, sq_ref, ck_ref, sk_ref,
     q_ref, k_ref, v_ref, dil0_ref, dil1_ref, dil2_ref, hperm_ref) = refs[n_hc:]
    hb = jnp.concatenate([hc[...] for hc in h_refs], axis=1).astype(BF16)
    cqn = _rms_norm(_dot(hb, wcq_ref[...]), gq_ref[...]).astype(BF16)
    ckvn = _rms_norm(_dot(hb, wckv_ref[...]), gkv_ref[...]).astype(BF16)
    k_rope = _dot(hb, wkra_ref[...]) * ck_ref[...] + _dot(hb, wkrb_ref[...]) * sk_ref[...]
    ka = _dot(ckvn, wka_ref[...])
    for h in range(MLA_HEADS):
        sl = slice(h * LANES, (h + 1) * LANES)
        k_ref[0, h] = (ka[:, sl] + k_rope).astype(BF16)
    qa_t = _dot_tt(wqa_ref[...], cqn)
    qb_t = _dot_tt(wqb_ref[...], cqn)
    cq_t = cq_ref[...]
    sq_t = sq_ref[...]
    for h in range(MLA_HEADS):
        sl = slice(h * LANES, (h + 1) * LANES)
        q_ref[0, sl] = (qa_t[sl] * cq_t + qb_t[sl] * sq_t).astype(BF16)
    v_ref[0] = _dot_tt(wv_ref[...], ckvn).astype(BF16)

    dil_scale = DIL_HEAD_DIM ** -0.5 * LOG2_E
    tm = h_refs[0].shape[0]
    for g, (dil_ref, (_, dilation)) in enumerate(zip((dil0_ref, dil1_ref, dil2_ref), DIL_PAIRS)):
        rows_per = tm // dilation
        if dilation == 1:
            hp = hb
        else:
            for r in range(dilation):
                for c, hc in enumerate(h_refs):
                    hperm_ref[r * rows_per:(r + 1) * rows_per, c * LANES:(c + 1) * LANES] = hc[
                        pl.ds(r, rows_per, stride=dilation), :].astype(BF16)
            hp = hperm_ref[...]
        for part in range(3):
            c0 = (part * DIL_GROUPS + g) * DIL_OUT
            res = _dot(hp, wdil_ref[:, c0:c0 + DIL_OUT])
            if part == 0:
                res = res * dil_scale
            res = res.astype(BF16)
            for r in range(dilation):
                dil_ref[0, r, :, part * DIL_OUT:(part + 1) * DIL_OUT] = res[r * rows_per:(r + 1) * rows_per]


def _proj(h2d, weights, tables, *, batch, seq, tm):
    t, d = h2d.shape
    n_s = seq // tm

    def full(a):
        return pl.BlockSpec(a.shape, lambda i: (0,) * a.ndim)

    tab_spec = pl.BlockSpec((tm, LANES), lambda i: (i % n_s, 0))
    tab_t_spec = pl.BlockSpec((LANES, tm), lambda i: (0, i % n_s))
    head_map = lambda i: (i // n_s, 0, i % n_s, 0)
    feat_map = lambda i: (i // n_s, 0, i % n_s)
    dil_specs = [pl.BlockSpec((1, dl, tm // dl, 3 * DIL_OUT), head_map) for _, dl in DIL_PAIRS]
    dil_shapes = [jax.ShapeDtypeStruct((batch, dl, seq // dl, 3 * DIL_OUT), BF16) for _, dl in DIL_PAIRS]
    return pl.pallas_call(
        _proj_body,
        grid=(t // tm,),
        in_specs=[pl.BlockSpec((tm, LANES), functools.partial(lambda c, i: (i, c), c))
                  for c in range(d // LANES)]
        + [full(w) for w in weights] + [tab_t_spec] * 2 + [tab_spec] * 2,
        out_specs=[
            pl.BlockSpec((1, MLA_HEADS * LANES, tm), feat_map),
            pl.BlockSpec((1, MLA_HEADS, tm, LANES), head_map),
            pl.BlockSpec((1, MLA_OUT, tm), feat_map),
        ] + dil_specs,
        out_shape=[
            jax.ShapeDtypeStruct((batch, MLA_HEADS * LANES, seq), BF16),
            jax.ShapeDtypeStruct((batch, MLA_HEADS, seq, LANES), BF16),
            jax.ShapeDtypeStruct((batch, MLA_OUT, seq), BF16),
        ] + dil_shapes,
        scratch_shapes=[pltpu.VMEM((tm, d), BF16)],
        compiler_params=pltpu.CompilerParams(
            dimension_semantics=("parallel",), vmem_limit_bytes=VMEM_LIMIT),
        name="proj",
    )(*([h2d] * (d // LANES)), *weights, *tables)


def _mla_attn_body(q_ref, k_ref, vt_ref, o_ref, st_ref, m_ref, acc_ref, *, tq, tk):
    seq = k_ref.shape[2]
    n_tiles = seq // tq
    n_units = (seq // tk) * n_tiles
    heads = (0, 1)
    n_pieces = tk // MLA_PIECE

    def offsets(unit):
        if isinstance(unit, int):
            return (unit // n_tiles) * tk, (unit % n_tiles) * tq
        return (pl.multiple_of(lax.div(unit, n_tiles) * tk, tk),
                pl.multiple_of(lax.rem(unit, n_tiles) * tq, tq))

    def scores_piece(unit, buf, p, maxes):
        k_off, q_off = offsets(unit)
        rows = slice(p * MLA_PIECE, (p + 1) * MLA_PIECE)
        out = []
        for h in heads:
            st = _dot(k_ref[0, h, pl.ds(k_off + p * MLA_PIECE, MLA_PIECE), :],
                      q_ref[0, h * LANES:(h + 1) * LANES, pl.ds(q_off, tq)])
            st_ref[buf, h, rows] = st
            out.append(jnp.maximum(maxes[h], jnp.max(st, axis=0, keepdims=True)))
        return tuple(out)

    ones_rows = jnp.ones((MLA_DEN_ROWS, MLA_PIECE), BF16)

    def values_piece(k_off, buf, p, m_new, accs):
        rows = slice(p * MLA_PIECE, (p + 1) * MLA_PIECE)
        out = []
        for h in heads:
            pt = jnp.exp2(st_ref[buf, h, rows] - m_new[h]).astype(BF16)
            vtc = jnp.concatenate(
                [vt_ref[0, h * MLA_V:(h + 1) * MLA_V, pl.ds(k_off + p * MLA_PIECE, MLA_PIECE)], ones_rows],
                axis=0)
            out.append(accs[h] + _dot(vtc, pt))
        return tuple(out)

    neg_inf = jnp.full((1, tq), -jnp.inf, F32)

    def half(next_unit, next_buf, unit, buf, unit_max, first_visit=False):
        k_off, q_off = offsets(unit)
        cols = pl.ds(q_off, tq)
        if first_visit:
            m_new = unit_max
            accs = tuple(jnp.zeros((MLA_V + MLA_DEN_ROWS, tq), F32) for _ in heads)
        else:
            m_old = tuple(m_ref[h, :, cols] for h in heads)
            m_new = tuple(jnp.maximum(m_old[h], unit_max[h]) for h in heads)
            accs = tuple(jnp.exp2(m_old[h] - m_new[h]) * acc_ref[h, :, cols] for h in heads)
        next_max = (neg_inf, neg_inf)
        for p in range(n_pieces):
            if next_unit is not None:
                next_max = scores_piece(next_unit, next_buf, p, next_max)
            accs = values_piece(k_off, buf, p, m_new, accs)
        for h in heads:
            m_ref[h, :, cols] = m_new[h]
            acc_ref[h, :, cols] = accs[h]
        return next_max

    def group_step(jj, unit_max):
        for s in range(MLA_UNITS_PER_TRIP):
            u = MLA_UNITS_PER_TRIP * jj + s
            unit_max = half(u + 1, (s + 1) % 2, u, s % 2, unit_max)
        return unit_max

    unit_max = (neg_inf, neg_inf)
    for p in range(n_pieces):
        unit_max = scores_piece(0, 0, p, unit_max)
    for u in range(MLA_UNITS_PER_TRIP):
        unit_max = half(u + 1, (u + 1) % 2, u, u % 2, unit_max, first_visit=u < n_tiles)
    unit_max = lax.fori_loop(1, n_units // MLA_UNITS_PER_TRIP - 1, group_step, unit_max)
    for u in range(n_units - MLA_UNITS_PER_TRIP, n_units):
        last = u == n_units - 1
        unit_max = half(None if last else u + 1, None if last else (u + 1) % 2, u, u % 2, unit_max)
        if u >= n_units - n_tiles:
            cols = slice((u % n_tiles) * tq, (u % n_tiles + 1) * tq)
            out_t = jnp.concatenate(
                [acc_ref[h, :MLA_V, cols] / acc_ref[h, MLA_V:MLA_V + 1, cols] for h in heads], axis=0)
            o_ref[0, cols, :] = out_t.T.astype(o_ref.dtype)


def _mla_attn(q, k, v, *, tq, tk):
    batch, heads, seq, _ = k.shape
    assert MLA_UNITS_PER_TRIP % 2 == 0 and ((seq // tk) * (seq // tq)) % MLA_UNITS_PER_TRIP == 0
    assert seq // tq <= MLA_UNITS_PER_TRIP and (seq // tk) * (seq // tq) >= 2 * MLA_UNITS_PER_TRIP
    return pl.pallas_call(
        functools.partial(_mla_attn_body, tq=tq, tk=tk),
        grid=(batch, heads // 2),
        in_specs=[
            pl.BlockSpec((1, 2 * LANES, seq), lambda b, hp: (b, hp, 0)),
            pl.BlockSpec((1, 2, seq, LANES), lambda b, hp: (b, hp, 0, 0)),
            pl.BlockSpec((1, 2 * MLA_V, seq), lambda b, hp: (b, hp, 0)),
        ],
        out_specs=pl.BlockSpec((1, seq, LANES), lambda b, hp: (b, 0, hp)),
        out_shape=jax.ShapeDtypeStruct((batch, seq, MLA_OUT), BF16),
        scratch_shapes=[pltpu.VMEM((2, 2, tk, tq), F32),
                        pltpu.VMEM((2, 1, seq), F32),
                        pltpu.VMEM((2, MLA_V + MLA_DEN_ROWS, seq), F32)],
        compiler_params=pltpu.CompilerParams(
            dimension_semantics=("parallel", "parallel"), vmem_limit_bytes=VMEM_LIMIT),
        name="mla_attn",
    )(q, k, v)


def _dil_attn_body(slopes_ref, q_ref, kl_ref, km_ref, kr_ref, vl_ref, vm_ref, vr_ref,
                   o_ref, lse_ref, *, group, dilation, tq, sub_len):
    i = pl.program_id(1)
    n_res = q_ref.shape[1]
    first_res = pl.program_id(2) * n_res
    n_sub = tq // DIL_SUB

    def window(left_ref, main_ref, right_ref, rr, sub, cols):
        lo, hi = sub * DIL_SUB - DIL_HALO, (sub + 1) * DIL_SUB + DIL_HALO
        parts = []
        if lo < 0:
            parts.append(left_ref[0, rr, :, cols])
        parts.append(main_ref[0, rr, max(lo, 0):min(hi, tq), cols])
        if hi > tq:
            parts.append(right_ref[0, rr, :, cols])
        return parts[0] if len(parts) == 1 else jnp.concatenate(parts, axis=0)

    win = DIL_SUB + 2 * DIL_HALO
    row = lax.broadcasted_iota(jnp.int32, (DIL_SUB, win), 0)
    col = lax.broadcasted_iota(jnp.int32, (DIL_SUB, win), 1)
    steps = jnp.abs(col - DIL_HALO - row)
    in_band = steps <= DIL_HALF_SPAN
    dist = (steps * dilation).astype(F32)
    bias = [jnp.where(in_band, (-LOG2_E * slopes_ref[group * DIL_HEADS_PER_GROUP + h]) * dist, -jnp.inf)
            for h in range(DIL_HEADS_PER_GROUP)]
    for sub in range(n_sub):
        key_pos = i * tq + (sub * DIL_SUB - DIL_HALO) + col
        in_range = None
        if sub == 0:
            in_range = key_pos >= 0
        if sub == n_sub - 1:
            below = key_pos < sub_len
            in_range = below if in_range is None else in_range & below
        rows = slice(sub * DIL_SUB, (sub + 1) * DIL_SUB)
        for rr in range(n_res):
            if dilation == 1:
                out_rows = pl.ds(sub * DIL_SUB, DIL_SUB)
            else:
                out_rows = pl.ds(sub * DIL_SUB * dilation + first_res + rr, DIL_SUB, stride=dilation)
            for h in range(DIL_HEADS_PER_GROUP):
                cols = slice(h * DIL_HEAD_DIM, (h + 1) * DIL_HEAD_DIM)
                s = _dot_nt(q_ref[0, rr, rows, cols], window(kl_ref, km_ref, kr_ref, rr, sub, cols)) + bias[h]
                if in_range is not None:
                    s = jnp.where(in_range, s, -jnp.inf)
                m = jnp.max(s, axis=1, keepdims=True)
                e = jnp.exp2(s - m)
                den = jnp.sum(e, axis=1, keepdims=True)
                o_ref[0, h, out_rows, :] = _dot(
                    e.astype(BF16), window(vl_ref, vm_ref, vr_ref, rr, sub, cols)) / den
                lse_ref[0, h, out_rows, :] = jnp.broadcast_to(
                    (m + jnp.log2(den)) * LN_2, (DIL_SUB, DIL_HEAD_DIM))


def _dil_attn(dil_g, slopes, *, group, dilation):
    batch, _, sub_len, _ = dil_g.shape
    seq = sub_len * dilation
    tq = min(DIL_ROWS_PER_STEP, sub_len, DIL_CHUNK_TOKENS // dilation)
    n_q = sub_len // tq
    n_res = min(dilation, DIL_ROWS_PER_STEP // tq)
    halo_per_tile = tq // DIL_HALO
    n_halo = sub_len // DIL_HALO
    q_col, k_col, v_col = 0, 1, 2

    def main(c):
        return pl.BlockSpec((1, n_res, tq, DIL_OUT), lambda b, i, r: (b, r, i, c))

    def left(c):
        return pl.BlockSpec(
            (1, n_res, DIL_HALO, DIL_OUT),
            lambda b, i, r: (b, r, jnp.maximum(i * halo_per_tile - 1, 0), c))

    def right(c):
        return pl.BlockSpec(
            (1, n_res, DIL_HALO, DIL_OUT),
            lambda b, i, r: (b, r, jnp.minimum((i + 1) * halo_per_tile, n_halo - 1), c))

    out_spec = pl.BlockSpec((1, DIL_HEADS_PER_GROUP, tq * dilation, DIL_HEAD_DIM),
                            lambda b, i, r: (b, 0, i, 0))
    out_shape = jax.ShapeDtypeStruct((batch, DIL_HEADS_PER_GROUP, seq, DIL_HEAD_DIM), F32)
    o, lse = pl.pallas_call(
        functools.partial(_dil_attn_body, group=group, dilation=dilation, tq=tq, sub_len=sub_len),
        grid=(batch, n_q, dilation // n_res),
        in_specs=[pl.BlockSpec(memory_space=pltpu.SMEM),
                  main(q_col), left(k_col), main(k_col), right(k_col),
                  left(v_col), main(v_col), right(v_col)],
        out_specs=[out_spec, out_spec],
        out_shape=[out_shape, out_shape],
        compiler_params=pltpu.CompilerParams(
            dimension_semantics=("parallel", "parallel", "arbitrary"), vmem_limit_bytes=VMEM_LIMIT),
        name=f"dil_attn_g{group}",
    )(slopes, dil_g, dil_g, dil_g, dil_g, dil_g, dil_g, dil_g)
    return o, lse


def _mem_kv_body(mem_ref, w_ref, o_ref):
    o_ref[0] = _dot(mem_ref[0].astype(BF16), w_ref[...]).astype(BF16)


def _mem_kv(mem, w):
    batch, m_len, d = mem.shape
    n = w.shape[1]
    return pl.pallas_call(
        _mem_kv_body,
        grid=(batch,),
        in_specs=[pl.BlockSpec((1, m_len, d), lambda b: (b, 0, 0)),
                  pl.BlockSpec((d, n), lambda b: (0, 0))],
        out_specs=pl.BlockSpec((1, m_len, n), lambda b: (b, 0, 0)),
        out_shape=jax.ShapeDtypeStruct((batch, m_len, n), BF16),
        compiler_params=pltpu.CompilerParams(dimension_semantics=("parallel",)),
        name="mem_kv",
    )(mem, w)


def _merge_body(h_ref, omla_ref, od0_ref, od1_ref, od2_ref, ls0_ref, ls1_ref, ls2_ref, kvm_ref,
                wgate_ref, wmq_ref, wba_ref, wbb_ref, wbc_ref, wo_ref, g_ref, b_ref, out_ref, *, n_sub):
    sub_rows = h_ref.shape[0] // n_sub
    for sub in range(n_sub):
        rows = slice(sub * sub_rows, (sub + 1) * sub_rows)
        _merge_rows(rows, h_ref, omla_ref, od0_ref, od1_ref, od2_ref, ls0_ref, ls1_ref, ls2_ref, kvm_ref,
                    wgate_ref, wmq_ref, wba_ref, wbb_ref, wbc_ref, wo_ref, g_ref, b_ref, out_ref)


def _merge_rows(rows, h_ref, omla_ref, od0_ref, od1_ref, od2_ref, ls0_ref, ls1_ref, ls2_ref, kvm_ref,
                wgate_ref, wmq_ref, wba_ref, wbb_ref, wbc_ref, wo_ref, g_ref, b_ref, out_ref):
    h = h_ref[rows, :]
    hb = h.astype(BF16)

    def heads_to_lanes(ref):
        return jnp.concatenate([ref[0, hh, rows, :] for hh in range(DIL_HEADS_PER_GROUP)], axis=1)

    ls0, ls1, ls2 = heads_to_lanes(ls0_ref), heads_to_lanes(ls1_ref), heads_to_lanes(ls2_ref)
    mx = jnp.maximum(ls0, jnp.maximum(ls1, ls2))
    e0, e1, e2 = jnp.exp(ls0 - mx), jnp.exp(ls1 - mx), jnp.exp(ls2 - mx)
    o_dil = (e0 * heads_to_lanes(od0_ref) + e1 * heads_to_lanes(od1_ref)
             + e2 * heads_to_lanes(od2_ref)) / (e0 + e1 + e2)

    mq = (_dot(hb, wmq_ref[...]) * (MEM_HEAD_DIM ** -0.5)).astype(BF16)
    heads = []
    for hh in range(MEM_HEADS):
        kc = slice(hh * MEM_HEAD_DIM, (hh + 1) * MEM_HEAD_DIM)
        vc = slice(MEM_OUT + hh * MEM_HEAD_DIM, MEM_OUT + (hh + 1) * MEM_HEAD_DIM)
        s = _dot_nt(mq[:, kc], kvm_ref[0, :, kc])
        p = jnp.exp(s - jnp.max(s, axis=1, keepdims=True))
        heads.append(_dot(p.astype(BF16), kvm_ref[0, :, vc]) / jnp.sum(p, axis=1, keepdims=True))
    o_mem = jnp.concatenate(heads, axis=1)

    y_a = _dot(omla_ref[rows, :], wba_ref[...])
    y_b = _dot(o_dil.astype(BF16), wbb_ref[...])
    y_c = _dot(o_mem.astype(BF16), wbc_ref[...])
    d = h.shape[1]
    merged = (jax.nn.sigmoid(_dot(hb, wgate_ref[:, :d])) * y_a
              + jax.nn.sigmoid(_dot(hb, wgate_ref[:, d:2 * d])) * y_b
              + jax.nn.sigmoid(_dot(hb, wgate_ref[:, 2 * d:])) * y_c)
    mix = _dot(merged.astype(BF16), wo_ref[...])
    out_ref[rows, :] = _layer_norm(ALPHA * h + mix, g_ref[...], b_ref[...])


def _merge(h2d, o_mla, o_dil, lse_dil, kv_mem, weights, g, b, *, seq, tm, n_sub):
    t, d = h2d.shape
    n_s = seq // tm

    def rows(width):
        return pl.BlockSpec((tm, width), lambda i: (i, 0))

    def full(a):
        return pl.BlockSpec(a.shape, lambda i: (0,) * a.ndim, pipeline_mode=pl.Buffered(1))

    m_len, kv_cols = kv_mem.shape[1:]
    return pl.pallas_call(
        functools.partial(_merge_body, n_sub=n_sub),
        grid=(t // tm,),
        in_specs=[rows(d), rows(MLA_OUT)]
        + [pl.BlockSpec((1, DIL_HEADS_PER_GROUP, tm, DIL_HEAD_DIM),
                        lambda i: (i // n_s, 0, i % n_s, 0))] * 6
        + [pl.BlockSpec((1, m_len, kv_cols), lambda i: (i // n_s, 0, 0))]
        + [full(w) for w in weights] + [full(g), full(b)],
        out_specs=rows(d),
        out_shape=jax.ShapeDtypeStruct((t, d), F32),
        compiler_params=pltpu.CompilerParams(
            dimension_semantics=("parallel",), vmem_limit_bytes=VMEM_LIMIT),
        name="merge",
    )(h2d, o_mla, *o_dil, *lse_dil, kv_mem, *weights, g, b)


def _swap_halves(w):
    half = w.shape[-1] // 2
    return jnp.concatenate([w[..., half:], w[..., :half]], axis=-1)


def _prep_mla_weights(w_in, w_uq, w_ukv):
    d = w_in.shape[0]
    o_q, o_kv, o_kr = 0, MLA_Q_LORA, MLA_Q_LORA + MLA_KV_LORA
    w_cq = w_in[:, o_q:o_q + MLA_Q_LORA]
    w_ckv = w_in[:, o_kv:o_kv + MLA_KV_LORA]
    w_kr = w_in[:, o_kr:o_kr + MLA_ROPE]
    pad_tail = LANES - MLA_NOPE - MLA_ROPE
    z_nope = jnp.zeros((d, MLA_NOPE), F32)
    z_tail = jnp.zeros((d, pad_tail), F32)
    w_kra = jnp.concatenate([z_nope, w_kr, z_tail], axis=1)
    w_krb = jnp.concatenate([z_nope, _swap_halves(w_kr), z_tail], axis=1)

    uq = w_uq.reshape(MLA_Q_LORA, MLA_HEADS, MLA_NOPE + MLA_ROPE)
    uq_nope, uq_pe = uq[..., :MLA_NOPE], uq[..., MLA_NOPE:]
    zq_nope = jnp.zeros_like(uq_nope)
    zq_tail = jnp.zeros((MLA_Q_LORA, MLA_HEADS, pad_tail), F32)
    w_qa = jnp.concatenate([uq_nope, uq_pe, zq_tail], axis=-1).reshape(MLA_Q_LORA, MLA_HEADS * LANES)
    w_qb = jnp.concatenate([zq_nope, _swap_halves(uq_pe), zq_tail], axis=-1).reshape(
        MLA_Q_LORA, MLA_HEADS * LANES)

    ukv = w_ukv.reshape(MLA_KV_LORA, MLA_HEADS, MLA_NOPE + MLA_V)
    uk, uv = ukv[..., :MLA_NOPE], ukv[..., MLA_NOPE:]
    w_ka = jnp.concatenate(
        [uk, jnp.zeros((MLA_KV_LORA, MLA_HEADS, LANES - MLA_NOPE), F32)], axis=-1).reshape(
            MLA_KV_LORA, MLA_HEADS * LANES)
    w_v = uv.reshape(MLA_KV_LORA, MLA_HEADS * MLA_V)
    return [w.astype(BF16) for w in (w_cq, w_ckv, w_kra, w_krb)], [w.astype(BF16) for w in (w_qa, w_qb, w_ka, w_v)]


def _rope_tables(seq):
    pos = np.arange(seq, dtype=np.float64)
    inv = 1.0 / (ROPE_THETA ** (np.arange(0, MLA_ROPE, 2, dtype=np.float64) / MLA_ROPE))
    pad_tail = LANES - MLA_NOPE - MLA_ROPE
    q_scale = (MLA_NOPE + MLA_ROPE) ** -0.5 * LOG2_E
    ang = pos[:, None] * inv[None, :]
    cos, sin = np.cos(ang), np.sin(ang)

    def tables(lead_one):
        c = np.concatenate([np.full((seq, MLA_NOPE), lead_one), cos, cos, np.zeros((seq, pad_tail))], axis=1)
        s = np.concatenate([np.zeros((seq, MLA_NOPE)), -sin, sin, np.zeros((seq, pad_tail))], axis=1)
        return c, s

    c_q, s_q = tables(1.0)
    c_k, s_k = tables(0.0)
    return tuple(np.ascontiguousarray(a, dtype=np.float32)
                 for a in ((c_q * q_scale).T, (s_q * q_scale).T, c_k, s_k))


def kernel(x, mem, w_in, mla_q_norm, mla_kv_norm, w_uq, w_ukv, w_mem_kv, w_br_mla, w_br_dil, w_br_mem, w_o,
           ffn1_w_gate, ffn1_w_up, ffn1_w_down, ffn2_w_gate, ffn2_w_up, ffn2_w_down,
           ln1_g, ln1_b, ln2_g, ln2_b, ln3_g, ln3_b):
    batch, seq, d = x.shape
    t = batch * seq
    tm_ffn = min(1024, t)
    tm_proj = min(512, seq)
    tm_merge = min(512, seq)
    tables = _rope_tables(seq)
    slopes = 2.0 ** (-8.0 * jnp.arange(1, DIL_HEADS + 1, dtype=F32) / DIL_HEADS)

    h = x.reshape(t, d)
    for l in range(DEPTH):
        bf = lambda w: w[l].astype(BF16)
        row = lambda v: v[l].reshape(1, -1)
        h = _ffn_ln(h, bf(ffn1_w_gate), bf(ffn1_w_up), bf(ffn1_w_down), row(ln1_g), row(ln1_b),
                    tm=tm_ffn, n_sub=FFN_SUB_TILES)

        w_in_l = w_in[l]
        o_dil_cols = MLA_Q_LORA + MLA_KV_LORA + MLA_ROPE
        o_memq = o_dil_cols + DIL_QKV
        o_gate = o_memq + MEM_OUT
        w_c, w_u = _prep_mla_weights(w_in_l, w_uq[l], w_ukv[l])
        proj_weights = w_c + [row(mla_q_norm), row(mla_kv_norm)] + w_u + [
            w_in_l[:, o_dil_cols:o_memq].astype(BF16)]
        q, k, v, *dil = _proj(h, proj_weights, tables, batch=batch, seq=seq, tm=tm_proj)

        o_mla = _mla_attn(q, k, v, tq=min(512, seq), tk=min(1024, seq)).reshape(t, MLA_OUT)
        dil_parts = [_dil_attn(dil[g], slopes, group=g, dilation=dl)
                     for g, (_, dl) in enumerate(DIL_PAIRS)]
        kv_mem = _mem_kv(mem, bf(w_mem_kv))

        merge_weights = [w_in_l[:, o_gate:].astype(BF16), w_in_l[:, o_memq:o_gate].astype(BF16),
                         bf(w_br_mla), bf(w_br_dil), bf(w_br_mem), bf(w_o)]
        h = _merge(h, o_mla, [p[0] for p in dil_parts], [p[1] for p in dil_parts], kv_mem,
                   merge_weights, row(ln2_g), row(ln2_b), seq=seq, tm=tm_merge, n_sub=2)

        h = _ffn_ln(h, bf(ffn2_w_gate), bf(ffn2_w_up), bf(ffn2_w_down), row(ln3_g), row(ln3_b),
                    tm=tm_ffn, n_sub=FFN_SUB_TILES)
    return h.reshape(batch, seq, d)
```
